```python
import math
import jax, jax.numpy as jnp
from jax import lax
import numpy as np

D_MODEL = 1024
BATCH = 16
SEQ = 4096
DEPTH = 1

ATTN_HEADS = 4
ATTN_HEAD_DIM = 64
ATTN_V_DIM = 2 * ATTN_HEAD_DIM
ATTN_WIDTH = ATTN_HEADS * ATTN_V_DIM
QK_COLS = ATTN_HEADS * 2 * ATTN_HEAD_DIM
Q_BLOCK = 128
CONV_CHANNELS = D_MODEL // 2
CONV_WIDTH = 31
N_BRANCHES = 2
IN_COLS = 2 * QK_COLS + ATTN_WIDTH + 2 * CONV_CHANNELS + N_BRANCHES * D_MODEL
N_GROUPS = 4
EXPERTS_PER_GROUP = 8
N_EXPERTS = N_GROUPS * EXPERTS_PER_GROUP
EXPERT_FF = D_MODEL // 2
TOP_K = 2
MOE_BLOCK = 128
PLE_DIM = 256
EPS = 1e-6

kernel_name = "hybrid_diffattn_conformer_hmoe_block"


def rmsnorm(x, g):
    x32 = x.astype(jnp.float32)
    y = x32 * lax.rsqrt(jnp.mean(x32 * x32, axis=-1, keepdims=True) + EPS)
    return (y * g.astype(jnp.float32)).astype(x.dtype)


def layernorm(x, g, b):
    x32 = x.astype(jnp.float32)
    mu = jnp.mean(x32, axis=-1, keepdims=True)
    var = jnp.mean(jnp.square(x32 - mu), axis=-1, keepdims=True)
    y = (x32 - mu) * lax.rsqrt(var + EPS)
    return (y * g.astype(jnp.float32) + b.astype(jnp.float32)).astype(x.dtype)


def diff_attention(q, k, v, lam):
    B, H, _, S, d = q.shape
    nqb = S // Q_BLOCK
    scale = 1.0 / math.sqrt(d)
    q_blocks = jnp.moveaxis(q.reshape(B, H, 2, nqb, Q_BLOCK, d), 3, 0)
    kpos = jnp.arange(S)

    def one_block(args):
        qb, bi = args
        qpos = bi * Q_BLOCK + jnp.arange(Q_BLOCK)
        mask = kpos[None, :] <= qpos[:, None]
        s = jnp.einsum('bhmqd,bhmkd->bhmqk', qb, k).astype(jnp.float32) * scale
        s = jnp.where(mask, s, -jnp.inf)
        probs = jax.nn.softmax(s, axis=-1)
        a = probs[:, :, 0] - lam * probs[:, :, 1]
        return jnp.einsum('bhqk,bhkv->bhqv', a.astype(v.dtype), v)

    out = lax.map(one_block, (q_blocks, jnp.arange(nqb)))
    return jnp.transpose(out, (1, 0, 3, 2, 4)).reshape(B, S, H, v.shape[-1])


def hierarchical_moe(t, w_rg, b_rg, w_re, b_re, w1, w3, w2):
    N, D = t.shape
    g_logits = (t @ w_rg + b_rg).astype(jnp.float32)
    g_probs = jax.nn.softmax(g_logits, axis=-1)
    g_idx = jnp.argmax(g_logits, axis=-1)
    g_gate = jnp.take_along_axis(g_probs, g_idx[:, None], axis=-1)
    e_logits = (t @ w_re + b_re).astype(jnp.float32).reshape(N, N_GROUPS, EXPERTS_PER_GROUP)
    e_logits = jnp.take_along_axis(e_logits, g_idx[:, None, None], axis=1)[:, 0]
    e_probs = jax.nn.softmax(e_logits, axis=-1)
    top_p, top_e = lax.top_k(e_probs, TOP_K)
    top_p = top_p / jnp.sum(top_p, axis=-1, keepdims=True)
    weights = g_gate * top_p
    expert_id = g_idx[:, None] * EXPERTS_PER_GROUP + top_e

    A = N * TOP_K
    flat_e = expert_id.reshape(-1)
    flat_t = jnp.repeat(jnp.arange(N), TOP_K)
    flat_w = weights.reshape(-1).astype(t.dtype)
    order = jnp.argsort(flat_e)
    sorted_e = flat_e[order]
    counts = jnp.bincount(flat_e, length=N_EXPERTS)
    padded = ((counts + MOE_BLOCK - 1) // MOE_BLOCK) * MOE_BLOCK
    start = jnp.cumsum(counts) - counts
    pend = jnp.cumsum(padded)
    pstart = pend - padded
    dest = pstart[sorted_e] + (jnp.arange(A) - start[sorted_e])
    R = ((A + MOE_BLOCK - 1) // MOE_BLOCK) * MOE_BLOCK + N_EXPERTS * MOE_BLOCK
    nb = R // MOE_BLOCK
    row_tok = jnp.full((R,), N, dtype=jnp.int32).at[dest].set(flat_t[order].astype(jnp.int32))
    row_w = jnp.zeros((R,), t.dtype).at[dest].set(flat_w[order])
    block_e = jnp.minimum(jnp.searchsorted(pend, jnp.arange(nb) * MOE_BLOCK, side='right'),
                          N_EXPERTS - 1)
    t_pad = jnp.concatenate([t, jnp.zeros((1, D), t.dtype)], axis=0)
    xs = t_pad[row_tok].reshape(nb, MOE_BLOCK, D)

    def expert_block(args):
        xb, e = args
        hdn = jax.nn.silu(xb @ w1[e]) * (xb @ w3[e])
        return hdn @ w2[e]

    ys = lax.map(expert_block, (xs, block_e)).reshape(R, D)
    out = jnp.zeros((N + 1, D), ys.dtype).at[row_tok].add(ys * row_w[:, None])
    return out[:N]


def setup_inputs(seed: int = 0) -> dict:
    key = jax.random.key(seed)
    ks = jax.random.split(key, 32)
    f32 = jnp.float32
    L, D, C, d = DEPTH, D_MODEL, CONV_CHANNELS, ATTN_HEAD_DIM

    def nrm(k, shape, scale):
        return jax.random.normal(k, shape, f32) * scale

    def gain(k, shape):
        return 1.0 + 0.02 * jax.random.normal(k, shape, f32)

    return {
        "x": nrm(ks[0], (BATCH, SEQ, D), 1.0),
        "p": nrm(ks[1], (L, BATCH, SEQ, PLE_DIM), 1.0),
        "norm_mix": gain(ks[2], (L, D)),
        "w_in": nrm(ks[3], (L, D, IN_COLS), D ** -0.5),
        "b_conv_in": nrm(ks[4], (L, 2 * C), 0.02),
        "b_gate": nrm(ks[5], (L, N_BRANCHES * D), 0.02),
        "q_norm": gain(ks[6], (L, 2, d)),
        "k_norm": gain(ks[7], (L, 2, d)),
        "lambda_q1": nrm(ks[8], (L, d), 0.1),
        "lambda_k1": nrm(ks[9], (L, d), 0.1),
        "lambda_q2": nrm(ks[10], (L, d), 0.1),
        "lambda_k2": nrm(ks[11], (L, d), 0.1),
        "subln": gain(ks[12], (L, ATTN_V_DIM)),
        "w_attn_out": nrm(ks[13], (L, ATTN_WIDTH, D), ATTN_WIDTH ** -0.5),
        "conv_w": nrm(ks[14], (L, CONV_WIDTH, C), CONV_WIDTH ** -0.5),
        "conv_b": nrm(ks[15], (L, C), 0.02),
        "conv_ln_g": gain(ks[16], (L, C)),
        "conv_ln_b": nrm(ks[17], (L, C), 0.02),
        "w_conv_out": nrm(ks[18], (L, C, D), C ** -0.5),
        "b_conv_out": nrm(ks[19], (L, D), 0.02),
        "w_o": nrm(ks[20], (L, D, D), D ** -0.5),
        "norm_ffn": gain(ks[21], (L, D)),
        "w_router_group": nrm(ks[22], (L, D, N_GROUPS), D ** -0.5),
        "b_router_group": nrm(ks[23], (L, N_GROUPS), 0.01),
        "w_router_expert": nrm(ks[24], (L, D, N_EXPERTS), D ** -0.5),
        "b_router_expert": nrm(ks[25], (L, N_EXPERTS), 0.01),
        "w1": nrm(ks[26], (L, N_EXPERTS, D, EXPERT_FF), D ** -0.5),
        "w3": nrm(ks[27], (L, N_EXPERTS, D, EXPERT_FF), D ** -0.5),
        "w2": nrm(ks[28], (L, N_EXPERTS, EXPERT_FF, D), EXPERT_FF ** -0.5),
        "norm_ple": gain(ks[29], (L, D)),
        "w_ple_gate": nrm(ks[30], (L, D, D), D ** -0.5),
        "b_ple_gate": nrm(ks[31], (L, D), 0.02),
        "w_ple_proj": nrm(jax.random.fold_in(key, 99), (L, PLE_DIM, D), PLE_DIM ** -0.5),
    }


def reference(x, p, norm_mix, w_in, b_conv_in, b_gate, q_norm, k_norm,
              lambda_q1, lambda_k1, lambda_q2, lambda_k2, subln, w_attn_out,
              conv_w, conv_b, conv_ln_g, conv_ln_b, w_conv_out, b_conv_out, w_o,
              norm_ffn, w_router_group, b_router_group, w_router_expert, b_router_expert,
              w1, w3, w2, norm_ple, w_ple_gate, b_ple_gate, w_ple_proj):
    B, S, D = x.shape
    H, d, C = ATTN_HEADS, ATTN_HEAD_DIM, CONV_CHANNELS
    h = x
    for i in range(DEPTH):
        lambda_init = 0.8 - 0.6 * math.exp(-0.3 * i)
        u = rmsnorm(h, norm_mix[i])
        proj = u @ w_in[i]
        q, k, v, c, g = jnp.split(
            proj, np.cumsum([QK_COLS, QK_COLS, ATTN_WIDTH, 2 * C]).tolist(), axis=-1)

        q = rmsnorm(q.reshape(B, S, H, 2, d), q_norm[i])
        k = rmsnorm(k.reshape(B, S, H, 2, d), k_norm[i])
        q = jnp.transpose(q, (0, 2, 3, 1, 4))
        k = jnp.transpose(k, (0, 2, 3, 1, 4))
        v = jnp.transpose(v.reshape(B, S, H, ATTN_V_DIM), (0, 2, 1, 3))
        lam = (jnp.exp(jnp.sum(lambda_q1[i].astype(jnp.float32) * lambda_k1[i].astype(jnp.float32)))
               - jnp.exp(jnp.sum(lambda_q2[i].astype(jnp.float32) * lambda_k2[i].astype(jnp.float32)))
               + lambda_init)
        attn = diff_attention(q, k, v, lam)
        attn = rmsnorm(attn, subln[i]) * (1.0 - lambda_init)
        y_a = attn.reshape(B, S, ATTN_WIDTH) @ w_attn_out[i]

        c = c + b_conv_in[i]
        c = c[..., :C] * jax.nn.sigmoid(c[..., C:])
        c = lax.conv_general_dilated(
            c, conv_w[i][:, None, :], window_strides=(1,), padding=[(CONV_WIDTH - 1, 0)],
            dimension_numbers=('NWC', 'WIO', 'NWC'), feature_group_count=C) + conv_b[i]
        c = jax.nn.silu(layernorm(c, conv_ln_g[i], conv_ln_b[i]))
        y_b = c @ w_conv_out[i] + b_conv_out[i]

        gates = jax.nn.sigmoid(g + b_gate[i])
        merged = gates[..., :D] * y_a + gates[..., D:] * y_b
        h = h + merged @ w_o[i]

        t = rmsnorm(h, norm_ffn[i]).reshape(B * S, D)
        moe = hierarchical_moe(t, w_router_group[i], b_router_group[i],
                               w_router_expert[i], b_router_expert[i], w1[i], w3[i], w2[i])
        h = h + moe.reshape(B, S, D)

        ple_gate = jax.nn.sigmoid(rmsnorm(h, norm_ple[i]) @ w_ple_gate[i] + b_ple_gate[i])
        h = h + ple_gate * (p[i] @ w_ple_proj[i])
    return h
```

```python
import functools
import math

import numpy as np
import jax
import jax.numpy as jnp
from jax import lax
from jax.experimental import pallas as pl
from jax.experimental.pallas import tpu as pltpu

ATTN_HEADS = 4
HEAD_DIM = 64
V_DIM = 2 * HEAD_DIM
CONV_WIDTH = 31
N_GROUPS = 4
EXPERTS_PER_GROUP = 8
N_EXPERTS = N_GROUPS * EXPERTS_PER_GROUP
PAIRS_PER_GROUP = EXPERTS_PER_GROUP * (EXPERTS_PER_GROUP - 1) // 2
N_CLASSES = N_GROUPS * PAIRS_PER_GROUP
EPS = 1e-6
LAMBDA_INIT = 0.8 - 0.6 * math.exp(-0.3 * 0)

LANES = 128
CONV_HALO = 32
NEG_BIG = -1e30

TM_INPROJ = 512
ATTN_TQ = 256
TM_MIX = 512
CONV_ROWS = 64
TS_SCATTER = 512
BM_MOE = 256
TM_FINAL = 512
VMEM_LIMIT = 56 * 1024 * 1024

_BF = jnp.bfloat16
_F32 = jnp.float32


def _const_spec(shape):
    nd = len(shape)
    return pl.BlockSpec(shape, lambda *_: (0,) * nd)


def _sigmoid(x):
    return 1.0 / (1.0 + jnp.exp(-x))


def _inproj_kernel(x_ref, nm_ref, w_ref, bci_ref, bg_ref, q_ref, k_ref, v_ref, c_ref, g_ref):
    x = x_ref[...]
    u = x * lax.rsqrt(jnp.mean(x * x, axis=-1, keepdims=True) + EPS) * nm_ref[...]
    ub = u.astype(_BF)
    qk = q_ref.shape[1]
    aw = v_ref.shape[1]
    cc = c_ref.shape[1]
    o0, o1, o2, o3 = qk, 2 * qk, 2 * qk + aw, 2 * qk + aw + 2 * cc
    q_ref[...] = jnp.dot(ub, w_ref[:, 0:o0], preferred_element_type=_F32).astype(_BF)
    k_ref[...] = jnp.dot(ub, w_ref[:, o0:o1], preferred_element_type=_F32).astype(_BF)
    v_ref[...] = jnp.dot(ub, w_ref[:, o1:o2], preferred_element_type=_F32).astype(_BF)
    c = jnp.dot(ub, w_ref[:, o2:o3], preferred_element_type=_F32) + bci_ref[...]
    c_ref[...] = (c[:, :cc] * _sigmoid(c[:, cc:])).astype(_BF)
    g = jnp.dot(ub, w_ref[:, o3:], preferred_element_type=_F32) + bg_ref[...]
    g_ref[...] = _sigmoid(g).astype(_BF)


def _inproj(x2, norm_mix, w_in_bf, b_conv_in, b_gate, qk, aw, cc):
    n, d = x2.shape
    tm = TM_INPROJ
    ng = b_gate.shape[-1]
    return pl.pallas_call(
        _inproj_kernel,
        grid=(n // tm,),
        in_specs=[
            pl.BlockSpec((tm, d), lambda i: (i, 0)),
            _const_spec((1, d)),
            _const_spec(w_in_bf.shape),
            _const_spec((1, 2 * cc)),
            _const_spec((1, ng)),
        ],
        out_specs=[
            pl.BlockSpec((tm, qk), lambda i: (i, 0)),
            pl.BlockSpec((tm, qk), lambda i: (i, 0)),
            pl.BlockSpec((tm, aw), lambda i: (i, 0)),
            pl.BlockSpec((tm, cc), lambda i: (i, 0)),
            pl.BlockSpec((tm, ng), lambda i: (i, 0)),
        ],
        out_shape=[
            jax.ShapeDtypeStruct((n, qk), _BF),
            jax.ShapeDtypeStruct((n, qk), _BF),
            jax.ShapeDtypeStruct((n, aw), _BF),
            jax.ShapeDtypeStruct((n, cc), _BF),
            jax.ShapeDtypeStruct((n, ng), _BF),
        ],
        compiler_params=pltpu.CompilerParams(
            dimension_semantics=("arbitrary",), vmem_limit_bytes=VMEM_LIMIT),
        name="inproj",
    )(x2, norm_mix.reshape(1, d), w_in_bf, b_conv_in.reshape(1, -1), b_gate.reshape(1, -1))


def _half_rmsnorm(x, gain_row, lo_mask):
    x2 = x * x
    s_lo = jnp.sum(jnp.where(lo_mask, x2, 0.0), axis=-1, keepdims=True)
    s_hi = jnp.sum(jnp.where(lo_mask, 0.0, x2), axis=-1, keepdims=True)
    r = jnp.where(lo_mask, lax.rsqrt(s_lo * (1.0 / HEAD_DIM) + EPS), lax.rsqrt(s_hi * (1.0 / HEAD_DIM) + EPS))
    return x * r * gain_row


def _attn_kernel(lam_ref, q_ref, k_ref, v_ref, qn_ref, kn_ref, sub_ref, o_ref, kn_s, qs_s, m_s, l_s, acc_s):
    s_len = q_ref.shape[1]
    tq = ATTN_TQ
    nq = s_len // tq
    lane = lax.broadcasted_iota(jnp.int32, (1, V_DIM), 1)
    lo_mask = lane < HEAD_DIM

    lv = lam_ref[...]
    d1 = jnp.sum(lv[0:1, :] * lv[1:2, :], axis=-1, keepdims=True)
    d2 = jnp.sum(lv[2:3, :] * lv[3:4, :], axis=-1, keepdims=True)
    lam = jnp.exp(d1) - jnp.exp(d2) + LAMBDA_INIT

    kgain = kn_ref[...]
    qgain = qn_ref[...] * (1.0 / math.sqrt(HEAD_DIM))

    def norm_k(i, carry):
        r0 = pl.multiple_of(i * tq, tq)
        kk = k_ref[0, pl.ds(r0, tq), :].astype(_F32)
        kn_s[pl.ds(r0, tq), :] = _half_rmsnorm(kk, kgain, lo_mask).astype(_BF)
        return carry

    lax.fori_loop(0, nq, norm_k, 0)

    def kv_step(j, masked):
        c0 = pl.multiple_of(j * tq, tq)
        kc = kn_s[pl.ds(c0, tq), :]
        vc = v_ref[0, pl.ds(c0, tq), :]
        s = lax.dot_general(qs_s[...], kc, (((1,), (1,)), ((), ())), preferred_element_type=_F32)
        if masked:
            row_in_blk = lax.broadcasted_iota(jnp.int32, (2 * tq, tq), 0) & (tq - 1)
            col_in_blk = lax.broadcasted_iota(jnp.int32, (2 * tq, tq), 1)
            s = jnp.where(col_in_blk <= row_in_blk, s, NEG_BIG)
        m_old = m_s[...]
        m_new = jnp.maximum(m_old, jnp.max(s, axis=-1, keepdims=True))
        alpha = jnp.exp(m_old - m_new)
        p = jnp.exp(s - m_new)
        l_s[...] = alpha * l_s[...] + jnp.sum(p, axis=-1, keepdims=True)
        acc_s[...] = alpha * acc_s[...] + jnp.dot(p.astype(_BF), vc, preferred_element_type=_F32)
        m_s[...] = m_new

    def q_block(qi, carry):
        r0 = pl.multiple_of(qi * tq, tq)
        qq = _half_rmsnorm(q_ref[0, pl.ds(r0, tq), :].astype(_F32), qgain, lo_mask)
        qs_s[0:tq, :] = jnp.where(lo_mask, qq, 0.0).astype(_BF)
        qs_s[tq:2 * tq, :] = jnp.where(lo_mask, 0.0, qq).astype(_BF)
        m_s[...] = jnp.full(m_s.shape, NEG_BIG, _F32)
        l_s[...] = jnp.zeros(l_s.shape, _F32)
        acc_s[...] = jnp.zeros(acc_s.shape, _F32)

        def full_step(j, c):
            kv_step(j, False)
            return c

        lax.fori_loop(0, qi, full_step, 0)
        kv_step(qi, True)

        o = acc_s[...] / l_s[...]
        a = o[0:tq, :] - lam * o[tq:2 * tq, :]
        a = a * lax.rsqrt(jnp.mean(a * a, axis=-1, keepdims=True) + EPS) * sub_ref[...] * (1.0 - LAMBDA_INIT)
        o_ref[0, pl.ds(r0, tq), :] = a.astype(_BF)
        return carry

    lax.fori_loop(0, nq, q_block, 0)


def _attention(lam_vecs, q3, k3, v3, q_norm, k_norm, subln):
    b, s, _ = q3.shape
    tq = ATTN_TQ
    slab = lambda bi, hi: (bi, 0, hi)
    return pl.pallas_call(
        _attn_kernel,
        grid=(b, ATTN_HEADS),
        in_specs=[
            _const_spec((4, HEAD_DIM)),
            pl.BlockSpec((1, s, V_DIM), slab),
            pl.BlockSpec((1, s, V_DIM), slab),
            pl.BlockSpec((1, s, V_DIM), slab),
            _const_spec((1, V_DIM)),
            _const_spec((1, V_DIM)),
            _const_spec((1, V_DIM)),
        ],
        out_specs=pl.BlockSpec((1, s, V_DIM), slab),
        out_shape=jax.ShapeDtypeStruct((b, s, ATTN_HEADS * V_DIM), _BF),
        scratch_shapes=[
            pltpu.VMEM((s, V_DIM), _BF),
            pltpu.VMEM((2 * tq, V_DIM), _BF),
            pltpu.VMEM((2 * tq, 1), _F32),
            pltpu.VMEM((2 * tq, 1), _F32),
            pltpu.VMEM((2 * tq, V_DIM), _F32),
        ],
        compiler_params=pltpu.CompilerParams(
            dimension_semantics=("arbitrary", "arbitrary"), vmem_limit_bytes=VMEM_LIMIT),
        name="diffattn",
    )(lam_vecs, q3, k3, v3, q_norm.reshape(1, V_DIM), k_norm.reshape(1, V_DIM), subln.reshape(1, V_DIM))


def _mix_kernel(x_ref, a_ref, c_ref, ch_ref, g_ref, wa_ref, wc_ref, wo_ref, cw_ref, cb_ref, lng_ref, lnb_ref,
                bco_ref, nf_ref, wr_ref, br_ref,
                h_ref, t_ref, meta_ref, cnt_ref, ext_s, cv_s, run_s, *, tiles_per_seq):
    i = pl.program_id(0)
    tm, d = x_ref.shape
    cc = c_ref.shape[1]

    @pl.when(i == 0)
    def _():
        run_s[...] = jnp.zeros(run_s.shape, _F32)

    first = (i % tiles_per_seq) == 0
    halo = ch_ref[...].astype(_F32)
    ext_s[0:CONV_HALO, :] = jnp.where(first, 0.0, halo)
    ext_s[CONV_HALO:CONV_HALO + tm, :] = c_ref[...].astype(_F32)
    off = CONV_HALO - (CONV_WIDTH - 1)

    for r0 in range(0, tm, CONV_ROWS):
        acc = jnp.zeros((CONV_ROWS, cc), _F32)
        for j in range(CONV_WIDTH):
            acc = acc + cw_ref[j:j + 1, :] * ext_s[r0 + off + j:r0 + off + j + CONV_ROWS, :]
        cv_s[r0:r0 + CONV_ROWS, :] = acc

    cv = cv_s[...] + cb_ref[...]
    mu = jnp.mean(cv, axis=-1, keepdims=True)
    xc = cv - mu
    var = jnp.mean(xc * xc, axis=-1, keepdims=True)
    ln = xc * lax.rsqrt(var + EPS) * lng_ref[...] + lnb_ref[...]
    act = ln * _sigmoid(ln)
    y_b = jnp.dot(act.astype(_BF), wc_ref[...], preferred_element_type=_F32) + bco_ref[...]
    y_a = jnp.dot(a_ref[...], wa_ref[...], preferred_element_type=_F32)

    merged = g_ref[:, 0:d].astype(_F32) * y_a + g_ref[:, d:2 * d].astype(_F32) * y_b
    h = x_ref[...] + jnp.dot(merged.astype(_BF), wo_ref[...], preferred_element_type=_F32)
    h_ref[...] = h

    t = h * lax.rsqrt(jnp.mean(h * h, axis=-1, keepdims=True) + EPS) * nf_ref[...]
    t_ref[:, 0:d] = t

    lg = jnp.dot(t, wr_ref[...], preferred_element_type=_F32, precision=lax.Precision.HIGHEST) + br_ref[...]
    lane = lax.broadcasted_iota(jnp.int32, (tm, LANES), 1)
    gl = jnp.where(lane < N_GROUPS, lg, NEG_BIG)
    gmax = jnp.max(gl, axis=-1, keepdims=True)
    gidx = jnp.min(jnp.where(gl == gmax, lane, LANES), axis=-1, keepdims=True)
    gsum = jnp.sum(jnp.where(lane < N_GROUPS, jnp.exp(gl - gmax), 0.0), axis=-1, keepdims=True)
    g_gate = 1.0 / gsum
    e_lo = N_GROUPS + EXPERTS_PER_GROUP * gidx
    el = jnp.where((lane >= e_lo) & (lane < e_lo + EXPERTS_PER_GROUP), lg, NEG_BIG)
    m1 = jnp.max(el, axis=-1, keepdims=True)
    i1 = jnp.min(jnp.where(el == m1, lane, LANES), axis=-1, keepdims=True)
    el2 = jnp.where(lane == i1, NEG_BIG, el)
    m2 = jnp.max(el2, axis=-1, keepdims=True)
    i2 = jnp.min(jnp.where(el2 == m2, lane, LANES), axis=-1, keepdims=True)
    e21 = jnp.exp(m2 - m1)
    w_top1 = g_gate * (1.0 / (1.0 + e21))
    w_top2 = g_gate * (e21 / (1.0 + e21))
    a1 = i1 - e_lo
    a2 = i2 - e_lo
    lo = jnp.minimum(a1, a2)
    hi = jnp.maximum(a1, a2)
    w_lo = jnp.where(a1 < a2, w_top1, w_top2)
    w_hi = jnp.where(a1 < a2, w_top2, w_top1)
    pidx = ((lo * (2 * EXPERTS_PER_GROUP - 1 - lo)) >> 1) + (hi - lo - 1)
    cls = gidx * PAIRS_PER_GROUP + pidx
    t_ref[:, d:d + LANES] = jnp.where(lane == 0, w_lo, jnp.where(lane == 1, w_hi, 0.0))

    onehot = lane == cls
    oh_bf = jnp.where(onehot, 1.0, 0.0).astype(_BF)
    rr = lax.broadcasted_iota(jnp.int32, (tm, tm), 0)
    ccol = lax.broadcasted_iota(jnp.int32, (tm, tm), 1)
    tri = jnp.where(ccol < rr, 1.0, 0.0).astype(_BF)
    before = jnp.dot(tri, oh_bf, preferred_element_type=_F32) + run_s[...]
    rank = jnp.sum(jnp.where(onehot, before, 0.0), axis=-1, keepdims=True).astype(jnp.int32)
    run_new = run_s[...] + jnp.sum(jnp.where(onehot, 1.0, 0.0), axis=0, keepdims=True)
    run_s[...] = run_new
    cnt_ref[...] = run_new
    meta_ref[...] = jnp.where(lane == 0, cls, jnp.where(lane == 1, rank, 0))


def _mix(x2, attn2, cglu2, gates2, wa_bf, wc_bf, wo_bf, conv_w, conv_b, ln_g, ln_b, b_conv_out, norm_ffn,
         w_router, b_router, seq_len):
    n, d = x2.shape
    tm = TM_MIX
    aw = attn2.shape[1]
    cc = cglu2.shape[1]
    tiles_per_seq = seq_len // tm
    halo_blocks = tm // CONV_HALO
    row = lambda i: (i, 0)
    kern = functools.partial(_mix_kernel, tiles_per_seq=tiles_per_seq)
    return pl.pallas_call(
        kern,
        grid=(n // tm,),
        in_specs=[
            pl.BlockSpec((tm, d), row),
            pl.BlockSpec((tm, aw), row),
            pl.BlockSpec((tm, cc), row),
            pl.BlockSpec((CONV_HALO, cc), lambda i: (jnp.maximum(i * halo_blocks - 1, 0), 0)),
            pl.BlockSpec((tm, 2 * d), row),
            _const_spec(wa_bf.shape),
            _const_spec(wc_bf.shape),
            _const_spec(wo_bf.shape),
            _const_spec((CONV_HALO, cc)),
            _const_spec((1, cc)),
            _const_spec((1, cc)),
            _const_spec((1, cc)),
            _const_spec((1, d)),
            _const_spec((1, d)),
            _const_spec((d, LANES)),
            _const_spec((1, LANES)),
        ],
        out_specs=[
            pl.BlockSpec((tm, d), row),
            pl.BlockSpec((tm, d + LANES), row),
            pl.BlockSpec((tm, LANES), row),
            _const_spec((1, LANES)),
        ],
        out_shape=[
            jax.ShapeDtypeStruct((n, d), _F32),
            jax.ShapeDtypeStruct((n, d + LANES), _F32),
            jax.ShapeDtypeStruct((n, LANES), jnp.int32),
            jax.ShapeDtypeStruct((1, LANES), _F32),
        ],
        scratch_shapes=[
            pltpu.VMEM((CONV_HALO + tm, cc), _F32),
            pltpu.VMEM((tm, cc), _F32),
            pltpu.VMEM((1, LANES), _F32),
        ],
        compiler_params=pltpu.CompilerParams(
            dimension_semantics=("arbitrary",), vmem_limit_bytes=VMEM_LIMIT),
        name="mix_router",
    )(x2, attn2, cglu2, cglu2, gates2, wa_bf, wc_bf, wo_bf, conv_w, conv_b.reshape(1, cc), ln_g.reshape(1, cc),
      ln_b.reshape(1, cc), b_conv_out.reshape(1, d), norm_ffn.reshape(1, d), w_router, b_router)


def _scatter_kernel(dest_ref, t_ref, init_ref, xs_ref, sem):
    del init_ref
    ts = t_ref.shape[0]

    def row_copy(r):
        return pltpu.make_async_copy(t_ref.at[pl.ds(r, 1)], xs_ref.at[pl.ds(dest_ref[0, 0, r], 1)], sem)

    def start(r, c):
        row_copy(r).start()
        return c

    def wait(r, c):
        row_copy(r).wait()
        return c

    lax.fori_loop(0, ts, start, 0)
    lax.fori_loop(0, ts, wait, 0)


def _scatter_rows(dest, t_aug, xs_init):
    n, w = t_aug.shape
    ts = TS_SCATTER
    return pl.pallas_call(
        _scatter_kernel,
        grid=(n // ts,),
        in_specs=[
            pl.BlockSpec((1, 1, ts), lambda i: (i, 0, 0), memory_space=pltpu.SMEM),
            pl.BlockSpec((ts, w), lambda i: (i, 0)),
            pl.BlockSpec(memory_space=pl.ANY),
        ],
        out_specs=pl.BlockSpec(memory_space=pl.ANY),
        out_shape=jax.ShapeDtypeStruct(xs_init.shape, xs_init.dtype),
        scratch_shapes=[pltpu.SemaphoreType.DMA],
        input_output_aliases={2: 0},
        compiler_params=pltpu.CompilerParams(
            dimension_semantics=("arbitrary",), vmem_limit_bytes=VMEM_LIMIT),
        name="scatter_rows",
    )(dest.reshape(n // ts, 1, ts), t_aug, xs_init)


def _moe_kernel(ea_ref, eb_ref, valid_ref, xs_ref, w1a, w3a, w2a, w1b, w3b, w2b, y_ref):
    j = pl.program_id(0)
    d = y_ref.shape[1]

    @pl.when(valid_ref[j] == 0)
    def _():
        y_ref[...] = jnp.zeros(y_ref.shape, y_ref.dtype)

    @pl.when(valid_ref[j] != 0)
    def _():
        xb = xs_ref[:, 0:d].astype(_BF)
        wts = xs_ref[:, d:d + LANES]

        def expert(w1, w3, w2):
            h1 = jnp.dot(xb, w1[0], preferred_element_type=_F32)
            h3 = jnp.dot(xb, w3[0], preferred_element_type=_F32)
            hdn = h1 * _sigmoid(h1) * h3
            return jnp.dot(hdn.astype(_BF), w2[0], preferred_element_type=_F32)

        y_ref[...] = wts[:, 0:1] * expert(w1a, w3a, w2a) + wts[:, 1:2] * expert(w1b, w3b, w2b)


def _moe(ea, eb, valid, xs, w1_bf, w3_bf, w2_bf):
    r, w = xs.shape
    d = w - LANES
    ff = w1_bf.shape[2]
    bm = BM_MOE
    nb = r // bm
    grid_spec = pltpu.PrefetchScalarGridSpec(
        num_scalar_prefetch=3,
        grid=(nb,),
        in_specs=[
            pl.BlockSpec((bm, w), lambda j, ea, eb, va: (j, 0)),
            pl.BlockSpec((1, d, ff), lambda j, ea, eb, va: (ea[j], 0, 0)),
            pl.BlockSpec((1, d, ff), lambda j, ea, eb, va: (ea[j], 0, 0)),
            pl.BlockSpec((1, ff, d), lambda j, ea, eb, va: (ea[j], 0, 0)),
            pl.BlockSpec((1, d, ff), lambda j, ea, eb, va: (eb[j], 0, 0)),
            pl.BlockSpec((1, d, ff), lambda j, ea, eb, va: (eb[j], 0, 0)),
            pl.BlockSpec((1, ff, d), lambda j, ea, eb, va: (eb[j], 0, 0)),
        ],
        out_specs=pl.BlockSpec((bm, d), lambda j, ea, eb, va: (j, 0)),
    )
    return pl.pallas_call(
        _moe_kernel,
        grid_spec=grid_spec,
        out_shape=jax.ShapeDtypeStruct((r, d), _F32),
        compiler_params=pltpu.CompilerParams(
            dimension_semantics=("arbitrary",), vmem_limit_bytes=VMEM_LIMIT),
        name="moe_experts",
    )(ea, eb, valid, xs, w1_bf, w3_bf, w2_bf, w1_bf, w3_bf, w2_bf)


def _final_kernel(dest_ref, h_ref, p_ref, ys_ref, np_ref, wg_ref, bg_ref, wp_ref, o_ref, y_s, sem):
    tm = h_ref.shape[0]

    def row_copy(r):
        return pltpu.make_async_copy(ys_ref.at[pl.ds(dest_ref[0, 0, r], 1)], y_s.at[pl.ds(r, 1)], sem)

    def start(r, c):
        row_copy(r).start()
        return c

    def wait(r, c):
        row_copy(r).wait()
        return c

    lax.fori_loop(0, tm, start, 0)
    proj = jnp.dot(p_ref[...].astype(_BF), wp_ref[...], preferred_element_type=_F32)
    lax.fori_loop(0, tm, wait, 0)

    h = h_ref[...] + y_s[...]
    u = h * lax.rsqrt(jnp.mean(h * h, axis=-1, keepdims=True) + EPS) * np_ref[...]
    gate = _sigmoid(jnp.dot(u.astype(_BF), wg_ref[...], preferred_element_type=_F32) + bg_ref[...])
    o_ref[...] = h + gate * proj


def _final(dest, h1, p2, ys, norm_ple, wg_bf, b_ple_gate, wp_bf):
    n, d = h1.shape
    tm = TM_FINAL
    pd = p2.shape[1]
    row = lambda i: (i, 0)
    return pl.pallas_call(
        _final_kernel,
        grid=(n // tm,),
        in_specs=[
            pl.BlockSpec((1, 1, tm), lambda i: (i, 0, 0), memory_space=pltpu.SMEM),
            pl.BlockSpec((tm, d), row),
            pl.BlockSpec((tm, pd), row),
            pl.BlockSpec(memory_space=pl.ANY),
            _const_spec((1, d)),
            _const_spec(wg_bf.shape),
            _const_spec((1, d)),
            _const_spec(wp_bf.shape),
        ],
        out_specs=pl.BlockSpec((tm, d), row),
        out_shape=jax.ShapeDtypeStruct((n, d), _F32),
        scratch_shapes=[pltpu.VMEM((tm, d), _F32), pltpu.SemaphoreType.DMA],
        compiler_params=pltpu.CompilerParams(
            dimension_semantics=("arbitrary",), vmem_limit_bytes=VMEM_LIMIT),
        name="final_ple",
    )(dest.reshape(n // tm, 1, tm), h1, p2, ys, norm_ple.reshape(1, d), wg_bf, b_ple_gate.reshape(1, d), wp_bf)


def _class_tables():
    ea, eb = [], []
    for g in range(N_GROUPS):
        for lo in range(EXPERTS_PER_GROUP):
            for hi in range(lo + 1, EXPERTS_PER_GROUP):
                ea.append(g * EXPERTS_PER_GROUP + lo)
                eb.append(g * EXPERTS_PER_GROUP + hi)
    return np.asarray(ea, np.int32), np.asarray(eb, np.int32)


_CLASS_EA, _CLASS_EB = _class_tables()


def kernel(x, p, norm_mix, w_in, b_conv_in, b_gate, q_norm, k_norm, lambda_q1, lambda_k1, lambda_q2, lambda_k2, subln, w_attn_out, conv_w, conv_b, conv_ln_g, conv_ln_b, w_conv_out, b_conv_out, w_o, norm_ffn, w_router_group, b_router_group, w_router_expert, b_router_expert, w1, w3, w2, norm_ple, w_ple_gate, b_ple_gate, w_ple_proj):
    b, s, d = x.shape
    n = b * s
    depth = w_in.shape[0]
    qk = ATTN_HEADS * 2 * HEAD_DIM
    aw = ATTN_HEADS * V_DIM
    cc = conv_w.shape[-1]
    assert depth == 1 and s % TM_MIX == 0 and s % ATTN_TQ == 0 and n % TM_INPROJ == 0
    assert n % TS_SCATTER == 0 and n % TM_FINAL == 0 and n % BM_MOE == 0
    i = 0
    h = x.reshape(n, d)

    q2, k2, v2, cglu2, gates2 = _inproj(h, norm_mix[i], w_in[i].astype(_BF), b_conv_in[i], b_gate[i], qk, aw, cc)

    lam_vecs = jnp.stack([lambda_q1[i], lambda_k1[i], lambda_q2[i], lambda_k2[i]]).astype(_F32)
    attn = _attention(lam_vecs, q2.reshape(b, s, qk), k2.reshape(b, s, qk), v2.reshape(b, s, aw),
                      q_norm[i], k_norm[i], subln[i])

    w_router = jnp.concatenate([w_router_group[i], w_router_expert[i]], axis=1)
    w_router = jnp.pad(w_router, ((0, 0), (0, LANES - w_router.shape[1])))
    b_router = jnp.concatenate([b_router_group[i], b_router_expert[i]])
    b_router = jnp.pad(b_router, (0, LANES - b_router.shape[0])).reshape(1, LANES)
    conv_w_pad = jnp.pad(conv_w[i], ((0, CONV_HALO - CONV_WIDTH), (0, 0)))
    h1, t_aug, meta, cnt = _mix(h, attn.reshape(n, aw), cglu2, gates2, w_attn_out[i].astype(_BF),
                                w_conv_out[i].astype(_BF), w_o[i].astype(_BF), conv_w_pad, conv_b[i],
                                conv_ln_g[i], conv_ln_b[i], b_conv_out[i], norm_ffn[i], w_router, b_router, s)

    bm = BM_MOE
    nb = n // bm + N_CLASSES
    cls = meta[:, 0]
    rank = meta[:, 1]
    counts = cnt[0, :N_CLASSES].astype(jnp.int32)
    padded = ((counts + bm - 1) // bm) * bm
    pend = jnp.cumsum(padded)
    pstart = pend - padded
    dest = (pstart[cls] + rank).astype(jnp.int32)
    blk = jnp.arange(nb, dtype=jnp.int32) * bm
    valid = blk < pend[-1]
    last_cls = jnp.searchsorted(pend, pend[-1] - 1, side='right')
    bcls = jnp.where(valid, jnp.searchsorted(pend, blk, side='right'), last_cls)
    bcls = jnp.minimum(bcls, N_CLASSES - 1).astype(jnp.int32)
    ea = jnp.asarray(_CLASS_EA)[bcls]
    eb = jnp.asarray(_CLASS_EB)[bcls]

    xs = _scatter_rows(dest, t_aug, jnp.zeros((nb * bm, d + LANES), _F32))
    ys = _moe(ea, eb, valid.astype(jnp.int32), xs, w1[i].astype(_BF), w3[i].astype(_BF), w2[i].astype(_BF))

    out = _final(dest, h1, p[i].reshape(n, -1), ys, norm_ple[i], w_ple_gate[i].astype(_BF), b_ple_gate[i],
                 w_ple_proj[i].astype(_BF))
    return out.reshape(b, s, d)
```

```python
import functools
import math

import numpy as np
import jax
import jax.numpy as jnp
from jax import lax
from jax.experimental import pallas as pl
from jax.experimental.pallas import tpu as pltpu

ATTN_HEADS = 4
HEAD_DIM = 64
V_DIM = 2 * HEAD_DIM
CONV_WIDTH = 31
N_GROUPS = 4
EXPERTS_PER_GROUP = 8
N_EXPERTS = N_GROUPS * EXPERTS_PER_GROUP
PAIRS_PER_GROUP = EXPERTS_PER_GROUP * (EXPERTS_PER_GROUP - 1) // 2
N_CLASSES = N_GROUPS * PAIRS_PER_GROUP
EPS = 1e-6
LAMBDA_INIT = 0.8 - 0.6 * math.exp(-0.3 * 0)

LANES = 128
SUBLANES = 8
CONV_HALO = 32
NEG_BIG = -1e30
RANK_BITS = 17
DMA_UNROLL = 8

TM_INPROJ = 512
ATTN_TQ = 512
ATTN_TK = 256
TM_MIX = 512
CONV_ROWS = 64
TS_SCATTER = 512
BM_MOE = 256
TM_FINAL = 512
VMEM_LIMIT = 56 * 1024 * 1024

_BF = jnp.bfloat16
_F32 = jnp.float32


def _const_spec(shape):
    nd = len(shape)
    return pl.BlockSpec(shape, lambda *_: (0,) * nd)


def _sigmoid(x):
    return 1.0 / (1.0 + jnp.exp(-x))


def _inproj_kernel(x_ref, nm_ref, w_ref, bci_ref, bg_ref, q_ref, k_ref, v_ref, c_ref, g_ref):
    x = x_ref[...]
    u = x * lax.rsqrt(jnp.mean(x * x, axis=-1, keepdims=True) + EPS) * nm_ref[...]
    ub = u.astype(_BF)
    qk = q_ref.shape[1]
    aw = v_ref.shape[1]
    cc = c_ref.shape[1]
    o0, o1, o2, o3 = qk, 2 * qk, 2 * qk + aw, 2 * qk + aw + 2 * cc
    q_ref[...] = jnp.dot(ub, w_ref[:, 0:o0], preferred_element_type=_F32).astype(_BF)
    k_ref[...] = jnp.dot(ub, w_ref[:, o0:o1], preferred_element_type=_F32).astype(_BF)
    v_ref[...] = jnp.dot(ub, w_ref[:, o1:o2], preferred_element_type=_F32).astype(_BF)
    c = jnp.dot(ub, w_ref[:, o2:o3], preferred_element_type=_F32) + bci_ref[...]
    c_ref[...] = (c[:, :cc] * _sigmoid(c[:, cc:])).astype(_BF)
    g = jnp.dot(ub, w_ref[:, o3:], preferred_element_type=_F32) + bg_ref[...]
    g_ref[...] = _sigmoid(g).astype(_BF)


def _inproj(x2, norm_mix, w_in_bf, b_conv_in, b_gate, qk, aw, cc):
    n, d = x2.shape
    tm = TM_INPROJ
    ng = b_gate.shape[-1]
    return pl.pallas_call(
        _inproj_kernel,
        grid=(n // tm,),
        in_specs=[
            pl.BlockSpec((tm, d), lambda i: (i, 0)),
            _const_spec((1, d)),
            _const_spec(w_in_bf.shape),
            _const_spec((1, 2 * cc)),
            _const_spec((1, ng)),
        ],
        out_specs=[
            pl.BlockSpec((tm, qk), lambda i: (i, 0)),
            pl.BlockSpec((tm, qk), lambda i: (i, 0)),
            pl.BlockSpec((tm, aw), lambda i: (i, 0)),
            pl.BlockSpec((tm, cc), lambda i: (i, 0)),
            pl.BlockSpec((tm, ng), lambda i: (i, 0)),
        ],
        out_shape=[
            jax.ShapeDtypeStruct((n, qk), _BF),
            jax.ShapeDtypeStruct((n, qk), _BF),
            jax.ShapeDtypeStruct((n, aw), _BF),
            jax.ShapeDtypeStruct((n, cc), _BF),
            jax.ShapeDtypeStruct((n, ng), _BF),
        ],
        compiler_params=pltpu.CompilerParams(
            dimension_semantics=("arbitrary",), vmem_limit_bytes=VMEM_LIMIT),
        name="inproj",
    )(x2, norm_mix.reshape(1, d), w_in_bf, b_conv_in.reshape(1, -1), b_gate.reshape(1, -1))


def _half_rmsnorm(x, gain_row, lo_mask):
    x2 = x * x
    s_lo = jnp.sum(jnp.where(lo_mask, x2, 0.0), axis=-1, keepdims=True)
    s_hi = jnp.sum(jnp.where(lo_mask, 0.0, x2), axis=-1, keepdims=True)
    r = jnp.where(lo_mask, lax.rsqrt(s_lo * (1.0 / HEAD_DIM) + EPS), lax.rsqrt(s_hi * (1.0 / HEAD_DIM) + EPS))
    return x * r * gain_row


def _attn_kernel(lam_ref, q_ref, k_ref, v_ref, qn_ref, kn_ref, sub_ref, o_ref, kn_s, va_s, qs_s, m_s, acc_s):
    s_len = q_ref.shape[1]
    tq = ATTN_TQ
    tk = ATTN_TK
    nq = s_len // tq
    sub = tq // tk
    lane = lax.broadcasted_iota(jnp.int32, (1, V_DIM), 1)
    lo_mask = lane < HEAD_DIM

    lv = lam_ref[...]
    d1 = jnp.sum(lv[0:1, :] * lv[1:2, :], axis=-1, keepdims=True)
    d2 = jnp.sum(lv[2:3, :] * lv[3:4, :], axis=-1, keepdims=True)
    lam = jnp.exp(d1) - jnp.exp(d2) + LAMBDA_INIT

    kgain = kn_ref[...]
    qgain = qn_ref[...] * (math.log2(math.e) / math.sqrt(HEAD_DIM))

    def prep_kv(i, carry):
        r0 = pl.multiple_of(i * tq, tq)
        kk = k_ref[0, pl.ds(r0, tq), :].astype(_F32)
        kn_s[pl.ds(r0, tq), :] = _half_rmsnorm(kk, kgain, lo_mask).astype(_BF)
        va_s[pl.ds(r0, tq), 0:V_DIM] = v_ref[0, pl.ds(r0, tq), :]
        va_s[pl.ds(r0, tq), V_DIM:2 * V_DIM] = jnp.ones((tq, V_DIM), _BF)
        return carry

    lax.fori_loop(0, nq, prep_kv, 0)

    def kv_step(c0, diag_off):
        kc = kn_s[pl.ds(c0, tk), :]
        vc = va_s[pl.ds(c0, tk), :]
        s = lax.dot_general(qs_s[...], kc, (((1,), (1,)), ((), ())), preferred_element_type=_F32)
        if diag_off is not None:
            row_in_blk = lax.broadcasted_iota(jnp.int32, (2 * tq, tk), 0) & (tq - 1)
            col_in_blk = lax.broadcasted_iota(jnp.int32, (2 * tq, tk), 1) + diag_off
            s = jnp.where(col_in_blk <= row_in_blk, s, NEG_BIG)
        m_old = m_s[...]
        m_new = jnp.maximum(m_old, jnp.max(s, axis=-1, keepdims=True))
        alpha = jnp.exp2(m_old - m_new)
        p = jnp.concatenate(
            [jnp.exp2(s[:, c * LANES:(c + 1) * LANES] - m_new) for c in range(tk // LANES)], axis=1)
        pv = jnp.dot(p.astype(_BF), vc, preferred_element_type=_F32)
        acc_s[...] = jnp.concatenate([alpha, alpha], axis=1) * acc_s[...] + pv
        m_s[...] = m_new

    def q_block(qi, carry):
        r0 = pl.multiple_of(qi * tq, tq)
        qq = _half_rmsnorm(q_ref[0, pl.ds(r0, tq), :].astype(_F32), qgain, lo_mask)
        qs_s[0:tq, :] = jnp.where(lo_mask, qq, 0.0).astype(_BF)
        qs_s[tq:2 * tq, :] = jnp.where(lo_mask, 0.0, qq).astype(_BF)
        m_s[...] = jnp.full(m_s.shape, NEG_BIG, _F32)
        acc_s[...] = jnp.zeros(acc_s.shape, _F32)

        def full_step(j, c):
            kv_step(pl.multiple_of(j * tk, tk), None)
            return c

        lax.fori_loop(0, qi * sub, full_step, 0)
        for u in range(sub):
            kv_step(pl.multiple_of(r0 + u * tk, tk), u * tk)

        acc = acc_s[...]
        o = acc[:, 0:V_DIM] / acc[:, V_DIM:2 * V_DIM]
        a = o[0:tq, :] - lam * o[tq:2 * tq, :]
        a = a * lax.rsqrt(jnp.mean(a * a, axis=-1, keepdims=True) + EPS) * sub_ref[...] * (1.0 - LAMBDA_INIT)
        o_ref[0, pl.ds(r0, tq), :] = a.astype(_BF)
        return carry

    lax.fori_loop(0, nq, q_block, 0)


def _attention(lam_vecs, q3, k3, v3, q_norm, k_norm, subln):
    b, s, _ = q3.shape
    tq = ATTN_TQ
    slab = lambda bi, hi: (bi, 0, hi)
    return pl.pallas_call(
        _attn_kernel,
        grid=(b, ATTN_HEADS),
        in_specs=[
            _const_spec((4, HEAD_DIM)),
            pl.BlockSpec((1, s, V_DIM), slab),
            pl.BlockSpec((1, s, V_DIM), slab),
            pl.BlockSpec((1, s, V_DIM), slab),
            _const_spec((1, V_DIM)),
            _const_spec((1, V_DIM)),
            _const_spec((1, V_DIM)),
        ],
        out_specs=pl.BlockSpec((1, s, V_DIM), slab),
        out_shape=jax.ShapeDtypeStruct((b, s, ATTN_HEADS * V_DIM), _BF),
        scratch_shapes=[
            pltpu.VMEM((s, V_DIM), _BF),
            pltpu.VMEM((s, 2 * V_DIM), _BF),
            pltpu.VMEM((2 * tq, V_DIM), _BF),
            pltpu.VMEM((2 * tq, V_DIM), _F32),
            pltpu.VMEM((2 * tq, 2 * V_DIM), _F32),
        ],
        compiler_params=pltpu.CompilerParams(
            dimension_semantics=("arbitrary", "arbitrary"), vmem_limit_bytes=VMEM_LIMIT),
        name="diffattn",
    )(lam_vecs, q3, k3, v3, q_norm.reshape(1, V_DIM), k_norm.reshape(1, V_DIM), subln.reshape(1, V_DIM))


def _mix_kernel(x_ref, a_ref, c_ref, ch_ref, g_ref, wa_ref, wc_ref, wo_ref, cw_ref, cb_ref, lng_ref, lnb_ref,
                bco_ref, nf_ref, wr_ref, wrh_ref, br_ref,
                h_ref, t_ref, key_ref, cnt_ref, ext_s, xsh_s, cv_s, run_s, *, tiles_per_seq):
    i = pl.program_id(0)
    tm, d = x_ref.shape
    cc = c_ref.shape[1]
    ext_rows = CONV_HALO + tm

    @pl.when(i == 0)
    def _():
        run_s[...] = jnp.zeros(run_s.shape, _F32)
        ext_s[ext_rows:ext_rows + SUBLANES, :] = jnp.zeros((SUBLANES, cc), _F32)

    first = (i % tiles_per_seq) == 0
    halo = ch_ref[...].astype(_F32)
    ext_s[0:CONV_HALO, :] = jnp.where(first, 0.0, halo)
    ext_s[CONV_HALO:ext_rows, :] = c_ref[...].astype(_F32)
    off = CONV_HALO - (CONV_WIDTH - 1)

    for rho in range(1, SUBLANES):
        xsh_s[rho - 1] = ext_s[rho:rho + ext_rows, :]

    for r0 in range(0, tm, CONV_ROWS):
        acc = jnp.zeros((CONV_ROWS, cc), _F32)
        for j in range(CONV_WIDTH):
            a, rho = divmod(off + j, SUBLANES)
            lo_row = r0 + a * SUBLANES
            if rho == 0:
                xv = ext_s[lo_row:lo_row + CONV_ROWS, :]
            else:
                xv = xsh_s[rho - 1, lo_row:lo_row + CONV_ROWS, :]
            acc = acc + cw_ref[j:j + 1, :] * xv
        cv_s[r0:r0 + CONV_ROWS, :] = acc

    cv = cv_s[...] + cb_ref[...]
    mu = jnp.mean(cv, axis=-1, keepdims=True)
    xc = cv - mu
    var = jnp.mean(xc * xc, axis=-1, keepdims=True)
    ln = xc * lax.rsqrt(var + EPS) * lng_ref[...] + lnb_ref[...]
    act = ln * _sigmoid(ln)
    y_b = jnp.dot(act.astype(_BF), wc_ref[...], preferred_element_type=_F32) + bco_ref[...]
    y_a = jnp.dot(a_ref[...], wa_ref[...], preferred_element_type=_F32)

    merged = g_ref[:, 0:d].astype(_F32) * y_a + g_ref[:, d:2 * d].astype(_F32) * y_b
    h = x_ref[...] + jnp.dot(merged.astype(_BF), wo_ref[...], preferred_element_type=_F32)
    h_ref[...] = h

    t = h * lax.rsqrt(jnp.mean(h * h, axis=-1, keepdims=True) + EPS) * nf_ref[...]
    t_ref[:, 0:d] = t

    t_hi = t.astype(_BF)
    t_lo = (t - t_hi.astype(_F32)).astype(_BF)
    l_hi = jnp.dot(t_hi, wr_ref[...], preferred_element_type=_F32)
    l_lo = jnp.dot(t_lo, wrh_ref[...], preferred_element_type=_F32)
    lg = l_hi[:, 0:LANES] + l_hi[:, LANES:2 * LANES] + l_lo + br_ref[...]
    lane = lax.broadcasted_iota(jnp.int32, (tm, LANES), 1)
    gl = jnp.where(lane < N_GROUPS, lg, NEG_BIG)
    gmax = jnp.max(gl, axis=-1, keepdims=True)
    gidx = jnp.min(jnp.where(gl == gmax, lane, LANES), axis=-1, keepdims=True)
    gsum = jnp.sum(jnp.where(lane < N_GROUPS, jnp.exp(gl - gmax), 0.0), axis=-1, keepdims=True)
    g_gate = 1.0 / gsum
    e_lo = N_GROUPS + EXPERTS_PER_GROUP * gidx
    el = jnp.where((lane >= e_lo) & (lane < e_lo + EXPERTS_PER_GROUP), lg, NEG_BIG)
    m1 = jnp.max(el, axis=-1, keepdims=True)
    i1 = jnp.min(jnp.where(el == m1, lane, LANES), axis=-1, keepdims=True)
    el2 = jnp.where(lane == i1, NEG_BIG, el)
    m2 = jnp.max(el2, axis=-1, keepdims=True)
    i2 = jnp.min(jnp.where(el2 == m2, lane, LANES), axis=-1, keepdims=True)
    e21 = jnp.exp(m2 - m1)
    w_top1 = g_gate * (1.0 / (1.0 + e21))
    w_top2 = g_gate * (e21 / (1.0 + e21))
    a1 = i1 - e_lo
    a2 = i2 - e_lo
    lo = jnp.minimum(a1, a2)
    hi = jnp.maximum(a1, a2)
    w_lo = jnp.where(a1 < a2, w_top1, w_top2)
    w_hi = jnp.where(a1 < a2, w_top2, w_top1)
    pidx = ((lo * (2 * EXPERTS_PER_GROUP - 1 - lo)) >> 1) + (hi - lo - 1)
    cls = gidx * PAIRS_PER_GROUP + pidx
    t_ref[:, d:d + LANES] = jnp.where(lane == 0, w_lo, jnp.where(lane == 1, w_hi, 0.0))

    onehot = lane == cls
    oh_bf = jnp.where(onehot, 1.0, 0.0).astype(_BF)
    rr = lax.broadcasted_iota(jnp.int32, (tm, tm), 0)
    ccol = lax.broadcasted_iota(jnp.int32, (tm, tm), 1)
    tri = jnp.where(ccol < rr, 1.0, 0.0).astype(_BF)
    before = jnp.dot(tri, oh_bf, preferred_element_type=_F32) + run_s[...]
    rank = jnp.sum(jnp.where(onehot, before, 0.0), axis=-1, keepdims=True)
    run_new = run_s[...] + jnp.sum(jnp.where(onehot, 1.0, 0.0), axis=0, keepdims=True)
    run_s[...] = run_new
    cnt_ref[...] = run_new
    keyf = jnp.broadcast_to(cls.astype(_F32) * float(1 << RANK_BITS) + rank, (tm, LANES))
    for g in range(tm // LANES):
        kt = keyf[g * LANES:(g + 1) * LANES, :].T
        key_ref[0, g:g + 1, :] = kt[0:1, :].astype(jnp.int32)


def _mix(x2, attn2, cglu2, gates2, wa_bf, wc_bf, wo_bf, conv_w, conv_b, ln_g, ln_b, b_conv_out, norm_ffn,
         w_router_cat, w_router_hi, b_router, seq_len):
    n, d = x2.shape
    tm = TM_MIX
    aw = attn2.shape[1]
    cc = cglu2.shape[1]
    tiles_per_seq = seq_len // tm
    halo_blocks = tm // CONV_HALO
    row = lambda i: (i, 0)
    kern = functools.partial(_mix_kernel, tiles_per_seq=tiles_per_seq)
    return pl.pallas_call(
        kern,
        grid=(n // tm,),
        in_specs=[
            pl.BlockSpec((tm, d), row),
            pl.BlockSpec((tm, aw), row),
            pl.BlockSpec((tm, cc), row),
            pl.BlockSpec((CONV_HALO, cc), lambda i: (jnp.maximum(i * halo_blocks - 1, 0), 0)),
            pl.BlockSpec((tm, 2 * d), row),
            _const_spec(wa_bf.shape),
            _const_spec(wc_bf.shape),
            _const_spec(wo_bf.shape),
            _const_spec((CONV_HALO, cc)),
            _const_spec((1, cc)),
            _const_spec((1, cc)),
            _const_spec((1, cc)),
            _const_spec((1, d)),
            _const_spec((1, d)),
            _const_spec((d, 2 * LANES)),
            _const_spec((d, LANES)),
            _const_spec((1, LANES)),
        ],
        out_specs=[
            pl.BlockSpec((tm, d), row),
            pl.BlockSpec((tm, d + LANES), row),
            pl.BlockSpec((1, tm // LANES, LANES), lambda i: (i, 0, 0)),
            _const_spec((1, LANES)),
        ],
        out_shape=[
            jax.ShapeDtypeStruct((n, d), _F32),
            jax.ShapeDtypeStruct((n, d + LANES), _F32),
            jax.ShapeDtypeStruct((n // tm, tm // LANES, LANES), jnp.int32),
            jax.ShapeDtypeStruct((1, LANES), _F32),
        ],
        scratch_shapes=[
            pltpu.VMEM((CONV_HALO + tm + SUBLANES, cc), _F32),
            pltpu.VMEM((SUBLANES - 1, CONV_HALO + tm, cc), _F32),
            pltpu.VMEM((tm, cc), _F32),
            pltpu.VMEM((1, LANES), _F32),
        ],
        compiler_params=pltpu.CompilerParams(
            dimension_semantics=("arbitrary",), vmem_limit_bytes=VMEM_LIMIT),
        name="mix_router",
    )(x2, attn2, cglu2, cglu2, gates2, wa_bf, wc_bf, wo_bf, conv_w, conv_b.reshape(1, cc), ln_g.reshape(1, cc),
      ln_b.reshape(1, cc), b_conv_out.reshape(1, d), norm_ffn.reshape(1, d), w_router_cat, w_router_hi, b_router)


def _sorted_row(pstart_ref, key):
    return pstart_ref[key >> RANK_BITS] + (key & ((1 << RANK_BITS) - 1))


def _scatter_kernel(pstart_ref, padstart_ref, padlen_ref, nvalid_ref, key_ref, t_ref, xs_ref, zero_s, sem, zsem):
    i = pl.program_id(0)
    ts = t_ref.shape[0]
    bm = zero_s.shape[0]
    nblk = xs_ref.shape[0] // bm

    def start_rows(g, c):
        for u in range(DMA_UNROLL):
            r = g * DMA_UNROLL + u
            dst = _sorted_row(pstart_ref, key_ref[0, 0, r])
            pltpu.make_async_copy(t_ref.at[pl.ds(r, 1)], xs_ref.at[pl.ds(dst, 1)], sem).start()
        return c

    def wait_rows(g, c):
        for _ in range(DMA_UNROLL):
            pltpu.make_async_copy(t_ref.at[pl.ds(0, 1)], xs_ref.at[pl.ds(0, 1)], sem).wait()
        return c

    lax.fori_loop(0, ts // DMA_UNROLL, start_rows, 0)

    @pl.when(i == 0)
    def _():
        zero_s[...] = jnp.zeros(zero_s.shape, zero_s.dtype)

        def pad_copies(c, act):
            ln = padlen_ref[c]
            st = padstart_ref[c]
            head = ln & (SUBLANES - 1)
            for u in range(SUBLANES - 1):
                @pl.when(u < head)
                def _():
                    act(pltpu.make_async_copy(zero_s.at[pl.ds(0, 1)], xs_ref.at[pl.ds(st + u, 1)], zsem))

            body = st + head
            rest = ln - head
            for k in range(SUBLANES.bit_length() - 1, bm.bit_length() - 1):
                sz = 1 << k

                @pl.when(((rest >> k) & 1) == 1)
                def _():
                    dst = pl.multiple_of(body + (rest & (sz - 1)), SUBLANES)
                    act(pltpu.make_async_copy(zero_s.at[pl.ds(0, sz)], xs_ref.at[pl.ds(dst, sz)], zsem))

        def tail_copy(j, act):
            @pl.when(j >= nvalid_ref[0])
            def _():
                act(pltpu.make_async_copy(zero_s, xs_ref.at[pl.ds(pl.multiple_of(j * bm, bm), bm)], zsem))

        for act in (lambda cp: cp.start(), lambda cp: cp.wait()):
            lax.fori_loop(0, N_CLASSES, lambda c, carry, act=act: (pad_copies(c, act), carry)[1], 0)
            lax.fori_loop(0, nblk, lambda j, carry, act=act: (tail_copy(j, act), carry)[1], 0)

    lax.fori_loop(0, ts // DMA_UNROLL, wait_rows, 0)


def _scatter_rows(pstart, padstart, padlen, nvalid, keys, t_aug, n_rows):
    n, w = t_aug.shape
    ts = TS_SCATTER
    grid_spec = pltpu.PrefetchScalarGridSpec(
        num_scalar_prefetch=4,
        grid=(n // ts,),
        in_specs=[
            pl.BlockSpec((1, 1, ts), lambda i, *_: (i, 0, 0), memory_space=pltpu.SMEM),
            pl.BlockSpec((ts, w), lambda i, *_: (i, 0)),
        ],
        out_specs=pl.BlockSpec(memory_space=pl.ANY),
        scratch_shapes=[pltpu.VMEM((BM_MOE, w), _F32), pltpu.SemaphoreType.DMA, pltpu.SemaphoreType.DMA],
    )
    return pl.pallas_call(
        _scatter_kernel,
        grid_spec=grid_spec,
        out_shape=jax.ShapeDtypeStruct((n_rows, w), _F32),
        compiler_params=pltpu.CompilerParams(
            dimension_semantics=("arbitrary",), vmem_limit_bytes=VMEM_LIMIT),
        name="scatter_rows",
    )(pstart, padstart, padlen, nvalid, keys.reshape(n // ts, 1, ts), t_aug)


def _moe_kernel(ea_ref, eb_ref, valid_ref, xs_ref, w1a, w3a, w2a, w1b, w3b, w2b, y_ref):
    j = pl.program_id(0)
    d = y_ref.shape[1]

    @pl.when(valid_ref[j] == 0)
    def _():
        y_ref[...] = jnp.zeros(y_ref.shape, y_ref.dtype)

    @pl.when(valid_ref[j] != 0)
    def _():
        xb = xs_ref[:, 0:d].astype(_BF)
        wts = xs_ref[:, d:d + LANES]

        def expert(w1, w3, w2):
            h1 = jnp.dot(xb, w1[0], preferred_element_type=_F32)
            h3 = jnp.dot(xb, w3[0], preferred_element_type=_F32)
            hdn = h1 * _sigmoid(h1) * h3
            return jnp.dot(hdn.astype(_BF), w2[0], preferred_element_type=_F32)

        y_ref[...] = wts[:, 0:1] * expert(w1a, w3a, w2a) + wts[:, 1:2] * expert(w1b, w3b, w2b)


def _moe(ea, eb, valid, xs, w1_bf, w3_bf, w2_bf):
    r, w = xs.shape
    d = w - LANES
    ff = w1_bf.shape[2]
    bm = BM_MOE
    nb = r // bm
    grid_spec = pltpu.PrefetchScalarGridSpec(
        num_scalar_prefetch=3,
        grid=(nb,),
        in_specs=[
            pl.BlockSpec((bm, w), lambda j, ea, eb, va: (jnp.where(va[j] != 0, j, 0), 0)),
            pl.BlockSpec((1, d, ff), lambda j, ea, eb, va: (ea[j], 0, 0)),
            pl.BlockSpec((1, d, ff), lambda j, ea, eb, va: (ea[j], 0, 0)),
            pl.BlockSpec((1, ff, d), lambda j, ea, eb, va: (ea[j], 0, 0)),
            pl.BlockSpec((1, d, ff), lambda j, ea, eb, va: (eb[j], 0, 0)),
            pl.BlockSpec((1, d, ff), lambda j, ea, eb, va: (eb[j], 0, 0)),
            pl.BlockSpec((1, ff, d), lambda j, ea, eb, va: (eb[j], 0, 0)),
        ],
        out_specs=pl.BlockSpec((bm, d), lambda j, ea, eb, va: (j, 0)),
    )
    return pl.pallas_call(
        _moe_kernel,
        grid_spec=grid_spec,
        out_shape=jax.ShapeDtypeStruct((r, d), _F32),
        compiler_params=pltpu.CompilerParams(
            dimension_semantics=("arbitrary",), vmem_limit_bytes=VMEM_LIMIT),
        name="moe_experts",
    )(ea, eb, valid, xs, w1_bf, w3_bf, w2_bf, w1_bf, w3_bf, w2_bf)


def _final_kernel(pstart_ref, key_ref, keyn_ref, h_ref, p_ref, ys_ref, np_ref, wg_ref, bg_ref, wp_ref, o_ref,
                  y_s, sem):
    i = pl.program_id(0)
    nsteps = pl.num_programs(0)
    tm = h_ref.shape[0]
    slot = i % 2

    def issue(kref, sl):
        def body(g, c):
            for u in range(DMA_UNROLL):
                r = g * DMA_UNROLL + u
                src = _sorted_row(pstart_ref, kref[0, 0, r])
                pltpu.make_async_copy(ys_ref.at[pl.ds(src, 1)], y_s.at[sl, pl.ds(r, 1)], sem.at[sl]).start()
            return c

        lax.fori_loop(0, tm // DMA_UNROLL, body, 0)

    @pl.when(i == 0)
    def _():
        issue(key_ref, 0)

    @pl.when(i + 1 < nsteps)
    def _():
        issue(keyn_ref, 1 - slot)

    def wait_rows(g, c):
        for _ in range(DMA_UNROLL):
            pltpu.make_async_copy(ys_ref.at[pl.ds(0, 1)], y_s.at[slot, pl.ds(0, 1)], sem.at[slot]).wait()
        return c

    lax.fori_loop(0, tm // DMA_UNROLL, wait_rows, 0)

    proj = jnp.dot(p_ref[...].astype(_BF), wp_ref[...], preferred_element_type=_F32)
    h = h_ref[...] + y_s[slot]
    u = h * lax.rsqrt(jnp.mean(h * h, axis=-1, keepdims=True) + EPS) * np_ref[...]
    gate = _sigmoid(jnp.dot(u.astype(_BF), wg_ref[...], preferred_element_type=_F32) + bg_ref[...])
    o_ref[...] = h + gate * proj


def _final(pstart, keys, h1, p2, ys, norm_ple, wg_bf, b_ple_gate, wp_bf):
    n, d = h1.shape
    tm = TM_FINAL
    nsteps = n // tm
    pd = p2.shape[1]
    row = lambda i, *_: (i, 0)
    const = lambda shape: pl.BlockSpec(shape, lambda i, *_: (0,) * len(shape))
    keys3 = keys.reshape(nsteps, 1, tm)
    grid_spec = pltpu.PrefetchScalarGridSpec(
        num_scalar_prefetch=1,
        grid=(nsteps,),
        in_specs=[
            pl.BlockSpec((1, 1, tm), lambda i, *_: (i, 0, 0), memory_space=pltpu.SMEM),
            pl.BlockSpec((1, 1, tm), lambda i, *_: (jnp.minimum(i + 1, nsteps - 1), 0, 0), memory_space=pltpu.SMEM),
            pl.BlockSpec((tm, d), row),
            pl.BlockSpec((tm, pd), row),
            pl.BlockSpec(memory_space=pl.ANY),
            const((1, d)),
            const(wg_bf.shape),
            const((1, d)),
            const(wp_bf.shape),
        ],
        out_specs=pl.BlockSpec((tm, d), row),
        scratch_shapes=[pltpu.VMEM((2, tm, d), _F32), pltpu.SemaphoreType.DMA((2,))],
    )
    return pl.pallas_call(
        _final_kernel,
        grid_spec=grid_spec,
        out_shape=jax.ShapeDtypeStruct((n, d), _F32),
        compiler_params=pltpu.CompilerParams(
            dimension_semantics=("arbitrary",), vmem_limit_bytes=VMEM_LIMIT),
        name="final_ple",
    )(pstart, keys3, keys3, h1, p2, ys, norm_ple.reshape(1, d), wg_bf, b_ple_gate.reshape(1, d), wp_bf)


def _class_tables():
    ea, eb = [], []
    for g in range(N_GROUPS):
        for lo in range(EXPERTS_PER_GROUP):
            for hi in range(lo + 1, EXPERTS_PER_GROUP):
                ea.append(g * EXPERTS_PER_GROUP + lo)
                eb.append(g * EXPERTS_PER_GROUP + hi)
    return np.asarray(ea, np.int32), np.asarray(eb, np.int32)


_CLASS_EA, _CLASS_EB = _class_tables()


def kernel(x, p, norm_mix, w_in, b_conv_in, b_gate, q_norm, k_norm, lambda_q1, lambda_k1, lambda_q2, lambda_k2, subln, w_attn_out, conv_w, conv_b, conv_ln_g, conv_ln_b, w_conv_out, b_conv_out, w_o, norm_ffn, w_router_group, b_router_group, w_router_expert, b_router_expert, w1, w3, w2, norm_ple, w_ple_gate, b_ple_gate, w_ple_proj):
    b, s, d = x.shape
    n = b * s
    depth = w_in.shape[0]
    qk = ATTN_HEADS * 2 * HEAD_DIM
    aw = ATTN_HEADS * V_DIM
    cc = conv_w.shape[-1]
    assert depth == 1 and s % TM_MIX == 0 and s % ATTN_TQ == 0 and n % TM_INPROJ == 0
    assert n % TS_SCATTER == 0 and n % TM_FINAL == 0 and n % BM_MOE == 0
    i = 0
    h = x.reshape(n, d)

    q2, k2, v2, cglu2, gates2 = _inproj(h, norm_mix[i], w_in[i].astype(_BF), b_conv_in[i], b_gate[i], qk, aw, cc)

    lam_vecs = jnp.stack([lambda_q1[i], lambda_k1[i], lambda_q2[i], lambda_k2[i]]).astype(_F32)
    attn = _attention(lam_vecs, q2.reshape(b, s, qk), k2.reshape(b, s, qk), v2.reshape(b, s, aw),
                      q_norm[i], k_norm[i], subln[i])

    w_router = jnp.concatenate([w_router_group[i], w_router_expert[i]], axis=1)
    w_router = jnp.pad(w_router, ((0, 0), (0, LANES - w_router.shape[1])))
    b_router = jnp.concatenate([b_router_group[i], b_router_expert[i]])
    b_router = jnp.pad(b_router, (0, LANES - b_router.shape[0])).reshape(1, LANES)
    w_router_hi = w_router.astype(_BF)
    w_router_lo = (w_router - w_router_hi.astype(_F32)).astype(_BF)
    w_router_cat = jnp.concatenate([w_router_hi, w_router_lo], axis=1)
    conv_w_pad = jnp.pad(conv_w[i], ((0, CONV_HALO - CONV_WIDTH), (0, 0)))
    h1, t_aug, keys, cnt = _mix(h, attn.reshape(n, aw), cglu2, gates2, w_attn_out[i].astype(_BF),
                                w_conv_out[i].astype(_BF), w_o[i].astype(_BF), conv_w_pad, conv_b[i],
                                conv_ln_g[i], conv_ln_b[i], b_conv_out[i], norm_ffn[i], w_router_cat, w_router_hi,
                                b_router, s)

    bm = BM_MOE
    nb = n // bm + N_CLASSES
    counts = cnt[0].astype(jnp.int32)
    padded = ((counts + bm - 1) // bm) * bm
    pend = jnp.cumsum(padded)
    pstart = (pend - padded).astype(jnp.int32)
    total = pend[-1]
    blk = jnp.arange(nb, dtype=jnp.int32) * bm
    valid = blk < total
    last_cls = jnp.searchsorted(pend, total - 1, side='right')
    bcls = jnp.where(valid, jnp.searchsorted(pend, blk, side='right'), last_cls)
    bcls = jnp.minimum(bcls, N_CLASSES - 1).astype(jnp.int32)
    ea = jnp.asarray(_CLASS_EA)[bcls]
    eb = jnp.asarray(_CLASS_EB)[bcls]
    nvalid = (total // bm).astype(jnp.int32).reshape(1)

    xs = _scatter_rows(pstart, (pstart + counts).astype(jnp.int32), (padded - counts).astype(jnp.int32), nvalid,
                       keys, t_aug, nb * bm)
    ys = _moe(ea, eb, valid.astype(jnp.int32), xs, w1[i].astype(_BF), w3[i].astype(_BF), w2[i].astype(_BF))

    out = _final(pstart, keys, h1, p[i].reshape(n, -1), ys, norm_ple[i], w_ple_gate[i].astype(_BF), b_ple_gate[i],
                 w_ple_proj[i].astype(_BF))
    return out.reshape(b, s, d)
```

```python
import functools
import math

import numpy as np
import jax
import jax.numpy as jnp
from jax import lax
from jax.experimental import pallas as pl
from jax.experimental.pallas import tpu as pltpu

ATTN_HEADS = 4
HEAD_DIM = 64
V_DIM = 2 * HEAD_DIM
CONV_WIDTH = 31
N_GROUPS = 4
EXPERTS_PER_GROUP = 8
N_EXPERTS = N_GROUPS * EXPERTS_PER_GROUP
PAIRS_PER_GROUP = EXPERTS_PER_GROUP * (EXPERTS_PER_GROUP - 1) // 2
N_CLASSES = N_GROUPS * PAIRS_PER_GROUP
EPS = 1e-6
LAMBDA_INIT = 0.8 - 0.6 * math.exp(-0.3 * 0)

LANES = 128
SUBLANES = 8
CONV_HALO = 32
NEG_BIG = -1e30
RANK_BITS = 17
DMA_UNROLL = 8

TM_INPROJ = 512
ATTN_TK = 512
TM_MIX = 512
CONV_ROWS = 64
TS_SCATTER = 512
BM_MOE = 256
TM_FINAL = 512
VMEM_LIMIT = 56 * 1024 * 1024

_BF = jnp.bfloat16
_F32 = jnp.float32


def _const_spec(shape):
    nd = len(shape)
    return pl.BlockSpec(shape, lambda *_: (0,) * nd)


def _sigmoid(x):
    return 1.0 / (1.0 + jnp.exp(-x))


def _inproj_kernel(x_ref, nm_ref, w_ref, bci_ref, bg_ref, q_ref, k_ref, v_ref, c_ref, g_ref):
    x = x_ref[...]
    u = x * lax.rsqrt(jnp.mean(x * x, axis=-1, keepdims=True) + EPS) * nm_ref[...]
    ub = u.astype(_BF)
    qk = q_ref.shape[1]
    aw = v_ref.shape[1]
    cc = c_ref.shape[1]
    o0, o1, o2, o3 = qk, 2 * qk, 2 * qk + aw, 2 * qk + aw + 2 * cc
    q_ref[...] = jnp.dot(ub, w_ref[:, 0:o0], preferred_element_type=_F32).astype(_BF)
    k_ref[...] = jnp.dot(ub, w_ref[:, o0:o1], preferred_element_type=_F32).astype(_BF)
    v_ref[...] = jnp.dot(ub, w_ref[:, o1:o2], preferred_element_type=_F32).astype(_BF)
    c = jnp.dot(ub, w_ref[:, o2:o3], preferred_element_type=_F32) + bci_ref[...]
    c_ref[...] = (c[:, :cc] * _sigmoid(c[:, cc:])).astype(_BF)
    g = jnp.dot(ub, w_ref[:, o3:], preferred_element_type=_F32) + bg_ref[...]
    g_ref[...] = _sigmoid(g).astype(_BF)


def _inproj(x2, norm_mix, w_in_bf, b_conv_in, b_gate, qk, aw, cc):
    n, d = x2.shape
    tm = TM_INPROJ
    ng = b_gate.shape[-1]
    return pl.pallas_call(
        _inproj_kernel,
        grid=(n // tm,),
        in_specs=[
            pl.BlockSpec((tm, d), lambda i: (i, 0)),
            _const_spec((1, d)),
            _const_spec(w_in_bf.shape),
            _const_spec((1, 2 * cc)),
            _const_spec((1, ng)),
        ],
        out_specs=[
            pl.BlockSpec((tm, qk), lambda i: (i, 0)),
            pl.BlockSpec((tm, qk), lambda i: (i, 0)),
            pl.BlockSpec((tm, aw), lambda i: (i, 0)),
            pl.BlockSpec((tm, cc), lambda i: (i, 0)),
            pl.BlockSpec((tm, ng), lambda i: (i, 0)),
        ],
        out_shape=[
            jax.ShapeDtypeStruct((n, qk), _BF),
            jax.ShapeDtypeStruct((n, qk), _BF),
            jax.ShapeDtypeStruct((n, aw), _BF),
            jax.ShapeDtypeStruct((n, cc), _BF),
            jax.ShapeDtypeStruct((n, ng), _BF),
        ],
        compiler_params=pltpu.CompilerParams(
            dimension_semantics=("arbitrary",), vmem_limit_bytes=VMEM_LIMIT),
        name="inproj",
    )(x2, norm_mix.reshape(1, d), w_in_bf, b_conv_in.reshape(1, -1), b_gate.reshape(1, -1))


def _half_rmsnorm(x, gain_row, lo_mask):
    x2 = x * x
    s_lo = jnp.sum(jnp.where(lo_mask, x2, 0.0), axis=-1, keepdims=True)
    s_hi = jnp.sum(jnp.where(lo_mask, 0.0, x2), axis=-1, keepdims=True)
    r = jnp.where(lo_mask, lax.rsqrt(s_lo * (1.0 / HEAD_DIM) + EPS), lax.rsqrt(s_hi * (1.0 / HEAD_DIM) + EPS))
    return x * r * gain_row


def _attn_kernel(lam_ref, q_ref, k_ref, v_ref, qn_ref, kn_ref, sub_ref, o_ref, kn_s, va_s, qs_s, m_s, acc_s,
                 sa_s, sb_s):
    s_len = q_ref.shape[1]
    tk = ATTN_TK
    tq = 2 * tk
    nq = s_len // tq
    lane = lax.broadcasted_iota(jnp.int32, (1, V_DIM), 1)
    lo_mask = lane < HEAD_DIM

    lv = lam_ref[...]
    d1 = jnp.sum(lv[0:1, :] * lv[1:2, :], axis=-1, keepdims=True)
    d2 = jnp.sum(lv[2:3, :] * lv[3:4, :], axis=-1, keepdims=True)
    lam = jnp.exp(d1) - jnp.exp(d2) + LAMBDA_INIT

    kgain = kn_ref[...]
    qgain = qn_ref[...] * (math.log2(math.e) / math.sqrt(HEAD_DIM))

    def prep_kv(i, carry):
        r0 = pl.multiple_of(i * tq, tq)
        kk = k_ref[0, pl.ds(r0, tq), :].astype(_F32)
        kn_s[pl.ds(r0, tq), :] = _half_rmsnorm(kk, kgain, lo_mask).astype(_BF)
        va_s[pl.ds(r0, tq), 0:V_DIM] = v_ref[0, pl.ds(r0, tq), :]
        va_s[pl.ds(r0, tq), V_DIM:2 * V_DIM] = jnp.ones((tq, V_DIM), _BF)
        return carry

    lax.fori_loop(0, nq, prep_kv, 0)

    def scores(c0, ra, nr):
        kc = kn_s[pl.ds(c0, tk), :]
        return lax.dot_general(qs_s[ra:ra + nr, :], kc, (((1,), (1,)), ((), ())), preferred_element_type=_F32)

    def consume(s, c0, ra, nr, n_masked):
        vc = va_s[pl.ds(c0, tk), :]
        if n_masked:
            keep = (lax.broadcasted_iota(jnp.int32, (tk, tk), 1) <= lax.broadcasted_iota(jnp.int32, (tk, tk), 0))
            groups = [s[g * tk:(g + 1) * tk, :] for g in range(nr // tk)]
            groups = [jnp.where(keep, sg, NEG_BIG) if g < n_masked else sg for g, sg in enumerate(groups)]
            s = jnp.concatenate(groups, axis=0)
        m_old = m_s[ra:ra + nr, :]
        m_new = jnp.maximum(m_old, jnp.max(s, axis=-1, keepdims=True))
        alpha = jnp.exp2(m_old - m_new)
        p = jnp.concatenate(
            [jnp.exp2(s[:, c * LANES:(c + 1) * LANES] - m_new) for c in range(tk // LANES)], axis=1)
        pv = jnp.dot(p.astype(_BF), vc, preferred_element_type=_F32)
        acc_s[ra:ra + nr, :] = jnp.concatenate([alpha, alpha], axis=1) * acc_s[ra:ra + nr, :] + pv
        m_s[ra:ra + nr, :] = m_new

    def q_block(qi, carry):
        r0 = pl.multiple_of(qi * tq, tq)
        for hf in range(2):
            qq = _half_rmsnorm(q_ref[0, pl.ds(r0 + hf * tk, tk), :].astype(_F32), qgain, lo_mask)
            qs_s[(2 * hf) * tk:(2 * hf + 1) * tk, :] = jnp.where(lo_mask, qq, 0.0).astype(_BF)
            qs_s[(2 * hf + 1) * tk:(2 * hf + 2) * tk, :] = jnp.where(lo_mask, 0.0, qq).astype(_BF)
        m_s[...] = jnp.full(m_s.shape, NEG_BIG, _F32)
        acc_s[...] = jnp.zeros(acc_s.shape, _F32)

        sa_s[...] = scores(0, 0, 4 * tk)

        def two_chunks(t, c):
            c0 = pl.multiple_of(2 * t * tk, tk)
            c1 = pl.multiple_of(c0 + tk, tk)
            c2 = pl.multiple_of(c0 + 2 * tk, tk)
            sb_s[...] = scores(c1, 0, 4 * tk)
            consume(sa_s[...], c0, 0, 4 * tk, 0)
            sa_s[...] = scores(c2, 0, 4 * tk)
            consume(sb_s[...], c1, 0, 4 * tk, 0)
            return c

        lax.fori_loop(0, qi, two_chunks, 0)
        c1 = pl.multiple_of(r0 + tk, tk)
        sb_s[0:2 * tk, :] = scores(c1, 2 * tk, 2 * tk)
        consume(sa_s[...], r0, 0, 4 * tk, 2)
        consume(sb_s[0:2 * tk, :], c1, 2 * tk, 2 * tk, 2)

        acc = acc_s[...]
        o = acc[:, 0:V_DIM] / acc[:, V_DIM:2 * V_DIM]
        for hf in range(2):
            a = o[(2 * hf) * tk:(2 * hf + 1) * tk, :] - lam * o[(2 * hf + 1) * tk:(2 * hf + 2) * tk, :]
            a = a * lax.rsqrt(jnp.mean(a * a, axis=-1, keepdims=True) + EPS) * sub_ref[...] * (1.0 - LAMBDA_INIT)
            o_ref[0, pl.ds(r0 + hf * tk, tk), :] = a.astype(_BF)
        return carry

    lax.fori_loop(0, nq, q_block, 0)


def _attention(lam_vecs, q3, k3, v3, q_norm, k_norm, subln):
    b, s, _ = q3.shape
    tq = 2 * ATTN_TK
    slab = lambda bi, hi: (bi, 0, hi)
    return pl.pallas_call(
        _attn_kernel,
        grid=(b, ATTN_HEADS),
        in_specs=[
            _const_spec((4, HEAD_DIM)),
            pl.BlockSpec((1, s, V_DIM), slab),
            pl.BlockSpec((1, s, V_DIM), slab),
            pl.BlockSpec((1, s, V_DIM), slab),
            _const_spec((1, V_DIM)),
            _const_spec((1, V_DIM)),
            _const_spec((1, V_DIM)),
        ],
        out_specs=pl.BlockSpec((1, s, V_DIM), slab),
        out_shape=jax.ShapeDtypeStruct((b, s, ATTN_HEADS * V_DIM), _BF),
        scratch_shapes=[
            pltpu.VMEM((s, V_DIM), _BF),
            pltpu.VMEM((s, 2 * V_DIM), _BF),
            pltpu.VMEM((2 * tq, V_DIM), _BF),
            pltpu.VMEM((2 * tq, V_DIM), _F32),
            pltpu.VMEM((2 * tq, 2 * V_DIM), _F32),
            pltpu.VMEM((2 * tq, ATTN_TK), _F32),
            pltpu.VMEM((2 * tq, ATTN_TK), _F32),
        ],
        compiler_params=pltpu.CompilerParams(
            dimension_semantics=("arbitrary", "arbitrary"), vmem_limit_bytes=VMEM_LIMIT),
        name="diffattn",
    )(lam_vecs, q3, k3, v3, q_norm.reshape(1, V_DIM), k_norm.reshape(1, V_DIM), subln.reshape(1, V_DIM))


def _router_logits(t, wr_ref, wrh_ref, br_ref):
    t_hi = t.astype(_BF)
    t_lo = (t - t_hi.astype(_F32)).astype(_BF)
    l_hi = jnp.dot(t_hi, wr_ref[...], preferred_element_type=_F32)
    l_lo = jnp.dot(t_lo, wrh_ref[...], preferred_element_type=_F32)
    return l_hi[:, 0:LANES] + l_hi[:, LANES:2 * LANES] + l_lo + br_ref[...]


def _mix_kernel(x_ref, a_ref, c_ref, ch_ref, g_ref, wa_ref, wc_ref, wo_ref, cw_ref, cb_ref, lng_ref, lnb_ref,
                bco_ref, nf_ref, wr_ref, wrh_ref, br_ref,
                h_ref, t_ref, key_ref, cnt_ref, ext_s, xsh_s, cv_s, run_s, *, tiles_per_seq):
    i = pl.program_id(0)
    tm, d = x_ref.shape
    cc = c_ref.shape[1]
    ext_rows = CONV_HALO + tm

    @pl.when(i == 0)
    def _():
        run_s[...] = jnp.zeros(run_s.shape, _F32)
        ext_s[ext_rows:ext_rows + SUBLANES, :] = jnp.zeros((SUBLANES, cc), _F32)

    first = (i % tiles_per_seq) == 0
    halo = ch_ref[...].astype(_F32)
    ext_s[0:CONV_HALO, :] = jnp.where(first, 0.0, halo)
    ext_s[CONV_HALO:ext_rows, :] = c_ref[...].astype(_F32)
    off = CONV_HALO - (CONV_WIDTH - 1)

    for rho in range(1, SUBLANES):
        xsh_s[rho - 1] = ext_s[rho:rho + ext_rows, :]

    for r0 in range(0, tm, CONV_ROWS):
        acc = jnp.zeros((CONV_ROWS, cc), _F32)
        for j in range(CONV_WIDTH):
            a, rho = divmod(off + j, SUBLANES)
            lo_row = r0 + a * SUBLANES
            if rho == 0:
                xv = ext_s[lo_row:lo_row + CONV_ROWS, :]
            else:
                xv = xsh_s[rho - 1, lo_row:lo_row + CONV_ROWS, :]
            acc = acc + cw_ref[j:j + 1, :] * xv
        cv_s[r0:r0 + CONV_ROWS, :] = acc

    cv = cv_s[...] + cb_ref[...]
    mu = jnp.mean(cv, axis=-1, keepdims=True)
    xc = cv - mu
    var = jnp.mean(xc * xc, axis=-1, keepdims=True)
    ln = xc * lax.rsqrt(var + EPS) * lng_ref[...] + lnb_ref[...]
    act = ln * _sigmoid(ln)
    y_b = jnp.dot(act.astype(_BF), wc_ref[...], preferred_element_type=_F32) + bco_ref[...]
    y_a = jnp.dot(a_ref[...], wa_ref[...], preferred_element_type=_F32)

    merged = g_ref[:, 0:d].astype(_F32) * y_a + g_ref[:, d:2 * d].astype(_F32) * y_b
    h = x_ref[...] + jnp.dot(merged.astype(_BF), wo_ref[...], preferred_element_type=_F32)
    h_ref[...] = h

    t = h * lax.rsqrt(jnp.mean(h * h, axis=-1, keepdims=True) + EPS) * nf_ref[...]
    for sg in range(d // LANES):
        t_ref[pl.ds(sg, tm, stride=SUBLANES), :] = t[:, sg * LANES:(sg + 1) * LANES]

    lg = _router_logits(t, wr_ref, wrh_ref, br_ref)
    lane = lax.broadcasted_iota(jnp.int32, (tm, LANES), 1)
    gl = jnp.where(lane < N_GROUPS, lg, NEG_BIG)
    gmax = jnp.max(gl, axis=-1, keepdims=True)
    gidx = jnp.min(jnp.where(gl == gmax, lane, LANES), axis=-1, keepdims=True)
    e_lo = N_GROUPS + EXPERTS_PER_GROUP * gidx
    el = jnp.where((lane >= e_lo) & (lane < e_lo + EXPERTS_PER_GROUP), lg, NEG_BIG)
    m1 = jnp.max(el, axis=-1, keepdims=True)
    i1 = jnp.min(jnp.where(el == m1, lane, LANES), axis=-1, keepdims=True)
    el2 = jnp.where(lane == i1, NEG_BIG, el)
    m2 = jnp.max(el2, axis=-1, keepdims=True)
    i2 = jnp.min(jnp.where(el2 == m2, lane, LANES), axis=-1, keepdims=True)
    a1 = i1 - e_lo
    a2 = i2 - e_lo
    lo = jnp.minimum(a1, a2)
    hi = jnp.maximum(a1, a2)
    pidx = ((lo * (2 * EXPERTS_PER_GROUP - 1 - lo)) >> 1) + (hi - lo - 1)
    cls = gidx * PAIRS_PER_GROUP + pidx

    onehot = lane == cls
    oh_bf = jnp.where(onehot, 1.0, 0.0).astype(_BF)
    rr = lax.broadcasted_iota(jnp.int32, (tm, tm), 0)
    ccol = lax.broadcasted_iota(jnp.int32, (tm, tm), 1)
    tri = jnp.where(ccol < rr, 1.0, 0.0).astype(_BF)
    before = jnp.dot(tri, oh_bf, preferred_element_type=_F32) + run_s[...]
    rank = jnp.sum(jnp.where(onehot, before, 0.0), axis=-1, keepdims=True)
    run_new = run_s[...] + jnp.sum(jnp.where(onehot, 1.0, 0.0), axis=0, keepdims=True)
    run_s[...] = run_new
    cnt_ref[...] = run_new
    keyf = jnp.broadcast_to(cls.astype(_F32) * float(1 << RANK_BITS) + rank, (tm, LANES))
    for g in range(tm // LANES):
        kt = keyf[g * LANES:(g + 1) * LANES, :].T
        key_ref[0, g:g + 1, :] = kt[0:1, :].astype(jnp.int32)


def _mix(x2, attn2, cglu2, gates2, wa_bf, wc_bf, wo_bf, conv_w, conv_b, ln_g, ln_b, b_conv_out, norm_ffn,
         w_router_cat, w_router_hi, b_router, seq_len):
    n, d = x2.shape
    tm = TM_MIX
    aw = attn2.shape[1]
    cc = cglu2.shape[1]
    tiles_per_seq = seq_len // tm
    halo_blocks = tm // CONV_HALO
    row = lambda i: (i, 0)
    kern = functools.partial(_mix_kernel, tiles_per_seq=tiles_per_seq)
    return pl.pallas_call(
        kern,
        grid=(n // tm,),
        in_specs=[
            pl.BlockSpec((tm, d), row),
            pl.BlockSpec((tm, aw), row),
            pl.BlockSpec((tm, cc), row),
            pl.BlockSpec((CONV_HALO, cc), lambda i: (jnp.maximum(i * halo_blocks - 1, 0), 0)),
            pl.BlockSpec((tm, 2 * d), row),
            _const_spec(wa_bf.shape),
            _const_spec(wc_bf.shape),
            _const_spec(wo_bf.shape),
            _const_spec((CONV_HALO, cc)),
            _const_spec((1, cc)),
            _const_spec((1, cc)),
            _const_spec((1, cc)),
            _const_spec((1, d)),
            _const_spec((1, d)),
            _const_spec((d, 2 * LANES)),
            _const_spec((d, LANES)),
            _const_spec((1, LANES)),
        ],
        out_specs=[
            pl.BlockSpec((tm, d), row),
            pl.BlockSpec((tm * SUBLANES, LANES), row),
            pl.BlockSpec((1, tm // LANES, LANES), lambda i: (i, 0, 0)),
            _const_spec((1, LANES)),
        ],
        out_shape=[
            jax.ShapeDtypeStruct((n, d), _F32),
            jax.ShapeDtypeStruct((n * SUBLANES, LANES), _F32),
            jax.ShapeDtypeStruct((n // tm, tm // LANES, LANES), jnp.int32),
            jax.ShapeDtypeStruct((1, LANES), _F32),
        ],
        scratch_shapes=[
            pltpu.VMEM((CONV_HALO + tm + SUBLANES, cc), _F32),
            pltpu.VMEM((SUBLANES - 1, CONV_HALO + tm, cc), _F32),
            pltpu.VMEM((tm, cc), _F32),
            pltpu.VMEM((1, LANES), _F32),
        ],
        compiler_params=pltpu.CompilerParams(
            dimension_semantics=("arbitrary",), vmem_limit_bytes=VMEM_LIMIT),
        name="mix_router",
    )(x2, attn2, cglu2, cglu2, gates2, wa_bf, wc_bf, wo_bf, conv_w, conv_b.reshape(1, cc), ln_g.reshape(1, cc),
      ln_b.reshape(1, cc), b_conv_out.reshape(1, d), norm_ffn.reshape(1, d), w_router_cat, w_router_hi, b_router)


def _sorted_row(pstart_ref, key):
    return pstart_ref[key >> RANK_BITS] + (key & ((1 << RANK_BITS) - 1))


def _token_rows(idx, count=1):
    return pl.ds(pl.multiple_of(idx * SUBLANES, SUBLANES), count * SUBLANES)


def _scatter_kernel(pstart_ref, padstart_ref, padlen_ref, nvalid_ref, key_ref, t_ref, xs_ref, zero_s, sem, zsem):
    i = pl.program_id(0)
    ts = t_ref.shape[0] // SUBLANES
    bm = zero_s.shape[0] // SUBLANES
    nblk = xs_ref.shape[0] // (bm * SUBLANES)

    def start_rows(g, c):
        for u in range(DMA_UNROLL):
            r = g * DMA_UNROLL + u
            dst = _sorted_row(pstart_ref, key_ref[0, 0, r])
            pltpu.make_async_copy(t_ref.at[_token_rows(r)], xs_ref.at[_token_rows(dst)], sem).start()
        return c

    def wait_rows(g, c):
        for _ in range(DMA_UNROLL):
            pltpu.make_async_copy(t_ref.at[_token_rows(0)], xs_ref.at[_token_rows(0)], sem).wait()
        return c

    lax.fori_loop(0, ts // DMA_UNROLL, start_rows, 0)

    @pl.when(i == 0)
    def _():
        zero_s[...] = jnp.zeros(zero_s.shape, zero_s.dtype)

        def pad_copies(c, act):
            ln = padlen_ref[c]
            st = padstart_ref[c]
            for k in range(bm.bit_length() - 1):
                sz = 1 << k

                @pl.when(((ln >> k) & 1) == 1)
                def _():
                    dst = st + (ln & (sz - 1))
                    act(pltpu.make_async_copy(zero_s.at[_token_rows(0, sz)], xs_ref.at[_token_rows(dst, sz)], zsem))

        def tail_copy(j, act):
            @pl.when(j >= nvalid_ref[0])
            def _():
                act(pltpu.make_async_copy(zero_s, xs_ref.at[_token_rows(j * bm, bm)], zsem))

        for act in (lambda cp: cp.start(), lambda cp: cp.wait()):
            lax.fori_loop(0, N_CLASSES, lambda c, carry, act=act: (pad_copies(c, act), carry)[1], 0)
            lax.fori_loop(0, nblk, lambda j, carry, act=act: (tail_copy(j, act), carry)[1], 0)

    lax.fori_loop(0, ts // DMA_UNROLL, wait_rows, 0)


def _scatter_rows(pstart, padstart, padlen, nvalid, keys, t_rows, n_rows):
    n = t_rows.shape[0] // SUBLANES
    ts = TS_SCATTER
    grid_spec = pltpu.PrefetchScalarGridSpec(
        num_scalar_prefetch=4,
        grid=(n // ts,),
        in_specs=[
            pl.BlockSpec((1, 1, ts), lambda i, *_: (i, 0, 0), memory_space=pltpu.SMEM),
            pl.BlockSpec((ts * SUBLANES, LANES), lambda i, *_: (i, 0)),
        ],
        out_specs=pl.BlockSpec(memory_space=pl.ANY),
        scratch_shapes=[pltpu.VMEM((BM_MOE * SUBLANES, LANES), _F32), pltpu.SemaphoreType.DMA,
                        pltpu.SemaphoreType.DMA],
    )
    return pl.pallas_call(
        _scatter_kernel,
        grid_spec=grid_spec,
        out_shape=jax.ShapeDtypeStruct((n_rows * SUBLANES, LANES), _F32),
        compiler_params=pltpu.CompilerParams(
            dimension_semantics=("arbitrary",), vmem_limit_bytes=VMEM_LIMIT),
        name="scatter_rows",
    )(pstart, padstart, padlen, nvalid, keys.reshape(n // ts, 1, ts), t_rows)


def _moe_kernel(ea_ref, eb_ref, valid_ref, xs_ref, wr_ref, wrh_ref, br_ref, w1a, w3a, w2a, w1b, w3b, w2b, y_ref):
    j = pl.program_id(0)
    bm = xs_ref.shape[0] // SUBLANES

    @pl.when(valid_ref[j] == 0)
    def _():
        y_ref[...] = jnp.zeros(y_ref.shape, y_ref.dtype)

    @pl.when(valid_ref[j] != 0)
    def _():
        x = jnp.concatenate([xs_ref[pl.ds(sg, bm, stride=SUBLANES), :] for sg in range(SUBLANES)], axis=1)
        xb = x.astype(_BF)

        lg = _router_logits(x, wr_ref, wrh_ref, br_ref)
        lane = lax.broadcasted_iota(jnp.int32, (bm, LANES), 1)
        gl = jnp.where(lane < N_GROUPS, lg, NEG_BIG)
        gmax = jnp.max(gl, axis=-1, keepdims=True)
        gsum = jnp.sum(jnp.where(lane < N_GROUPS, jnp.exp(gl - gmax), 0.0), axis=-1, keepdims=True)
        g_gate = 1.0 / gsum
        la = jnp.sum(jnp.where(lane == N_GROUPS + ea_ref[j], lg, 0.0), axis=-1, keepdims=True)
        lb = jnp.sum(jnp.where(lane == N_GROUPS + eb_ref[j], lg, 0.0), axis=-1, keepdims=True)
        w_a = g_gate * (1.0 / (1.0 + jnp.exp(lb - la)))
        w_b = g_gate * (1.0 / (1.0 + jnp.exp(la - lb)))

        def expert(w1, w3, w2):
            h1 = jnp.dot(xb, w1[0], preferred_element_type=_F32)
            h3 = jnp.dot(xb, w3[0], preferred_element_type=_F32)
            hdn = h1 * _sigmoid(h1) * h3
            return jnp.dot(hdn.astype(_BF), w2[0], preferred_element_type=_F32)

        y = w_a * expert(w1a, w3a, w2a) + w_b * expert(w1b, w3b, w2b)
        for sg in range(SUBLANES):
            y_ref[pl.ds(sg, bm, stride=SUBLANES), :] = y[:, sg * LANES:(sg + 1) * LANES]


def _moe(ea, eb, valid, xs, w_router_cat, w_router_hi, b_router, w1_bf, w3_bf, w2_bf):
    d, ff = w1_bf.shape[1], w1_bf.shape[2]
    assert d == SUBLANES * LANES
    bm = BM_MOE
    nb = xs.shape[0] // (bm * SUBLANES)
    const = lambda shape: pl.BlockSpec(shape, lambda j, *_: (0,) * len(shape))
    grid_spec = pltpu.PrefetchScalarGridSpec(
        num_scalar_prefetch=3,
        grid=(nb,),
        in_specs=[
            pl.BlockSpec((bm * SUBLANES, LANES), lambda j, ea, eb, va: (jnp.where(va[j] != 0, j, 0), 0)),
            const(w_router_cat.shape),
            const(w_router_hi.shape),
            const(b_router.shape),
            pl.BlockSpec((1, d, ff), lambda j, ea, eb, va: (ea[j], 0, 0)),
            pl.BlockSpec((1, d, ff), lambda j, ea, eb, va: (ea[j], 0, 0)),
            pl.BlockSpec((1, ff, d), lambda j, ea, eb, va: (ea[j], 0, 0)),
            pl.BlockSpec((1, d, ff), lambda j, ea, eb, va: (eb[j], 0, 0)),
            pl.BlockSpec((1, d, ff), lambda j, ea, eb, va: (eb[j], 0, 0)),
            pl.BlockSpec((1, ff, d), lambda j, ea, eb, va: (eb[j], 0, 0)),
        ],
        out_specs=pl.BlockSpec((bm * SUBLANES, LANES), lambda j, ea, eb, va: (j, 0)),
    )
    return pl.pallas_call(
        _moe_kernel,
        grid_spec=grid_spec,
        out_shape=jax.ShapeDtypeStruct(xs.shape, _F32),
        compiler_params=pltpu.CompilerParams(
            dimension_semantics=("arbitrary",), vmem_limit_bytes=VMEM_LIMIT),
        name="moe_experts",
    )(ea, eb, valid, xs, w_router_cat, w_router_hi, b_router, w1_bf, w3_bf, w2_bf, w1_bf, w3_bf, w2_bf)


def _final_kernel(pstart_ref, key_ref, keyn_ref, h_ref, p_ref, ys_ref, np_ref, wg_ref, bg_ref, wp_ref, o_ref,
                  y_s, sem):
    i = pl.program_id(0)
    nsteps = pl.num_programs(0)
    tm = h_ref.shape[0]
    slot = i % 2

    def issue(kref, sl):
        def body(g, c):
            for u in range(DMA_UNROLL):
                r = g * DMA_UNROLL + u
                src = _sorted_row(pstart_ref, kref[0, 0, r])
                pltpu.make_async_copy(ys_ref.at[_token_rows(src)], y_s.at[sl, _token_rows(r)], sem.at[sl]).start()
            return c

        lax.fori_loop(0, tm // DMA_UNROLL, body, 0)

    @pl.when(i == 0)
    def _():
        issue(key_ref, 0)

    @pl.when(i + 1 < nsteps)
    def _():
        issue(keyn_ref, 1 - slot)

    def wait_rows(g, c):
        for _ in range(DMA_UNROLL):
            pltpu.make_async_copy(ys_ref.at[_token_rows(0)], y_s.at[slot, _token_rows(0)], sem.at[slot]).wait()
        return c

    lax.fori_loop(0, tm // DMA_UNROLL, wait_rows, 0)

    proj = jnp.dot(p_ref[...].astype(_BF), wp_ref[...], preferred_element_type=_F32)
    y = jnp.concatenate([y_s[slot, pl.ds(sg, tm, stride=SUBLANES), :] for sg in range(SUBLANES)], axis=1)
    h = h_ref[...] + y
    u = h * lax.rsqrt(jnp.mean(h * h, axis=-1, keepdims=True) + EPS) * np_ref[...]
    gate = _sigmoid(jnp.dot(u.astype(_BF), wg_ref[...], preferred_element_type=_F32) + bg_ref[...])
    o_ref[...] = h + gate * proj


def _final(pstart, keys, h1, p2, ys, norm_ple, wg_bf, b_ple_gate, wp_bf):
    n, d = h1.shape
    assert d == SUBLANES * LANES
    tm = TM_FINAL
    nsteps = n // tm
    pd = p2.shape[1]
    row = lambda i, *_: (i, 0)
    const = lambda shape: pl.BlockSpec(shape, lambda i, *_: (0,) * len(shape))
    keys3 = keys.reshape(nsteps, 1, tm)
    grid_spec = pltpu.PrefetchScalarGridSpec(
        num_scalar_prefetch=1,
        grid=(nsteps,),
        in_specs=[
            pl.BlockSpec((1, 1, tm), lambda i, *_: (i, 0, 0), memory_space=pltpu.SMEM),
            pl.BlockSpec((1, 1, tm), lambda i, *_: (jnp.minimum(i + 1, nsteps - 1), 0, 0), memory_space=pltpu.SMEM),
            pl.BlockSpec((tm, d), row),
            pl.BlockSpec((tm, pd), row),
            pl.BlockSpec(memory_space=pl.ANY),
            const((1, d)),
            const(wg_bf.shape),
            const((1, d)),
            const(wp_bf.shape),
        ],
        out_specs=pl.BlockSpec((tm, d), row),
        scratch_shapes=[pltpu.VMEM((2, tm * SUBLANES, LANES), _F32), pltpu.SemaphoreType.DMA((2,))],
    )
    return pl.pallas_call(
        _final_kernel,
        grid_spec=grid_spec,
        out_shape=jax.ShapeDtypeStruct((n, d), _F32),
        compiler_params=pltpu.CompilerParams(
            dimension_semantics=("arbitrary",), vmem_limit_bytes=VMEM_LIMIT),
        name="final_ple",
    )(pstart, keys3, keys3, h1, p2, ys, norm_ple.reshape(1, d), wg_bf, b_ple_gate.reshape(1, d), wp_bf)


def _class_tables():
    ea, eb = [], []
    for g in range(N_GROUPS):
        for lo in range(EXPERTS_PER_GROUP):
            for hi in range(lo + 1, EXPERTS_PER_GROUP):
                ea.append(g * EXPERTS_PER_GROUP + lo)
                eb.append(g * EXPERTS_PER_GROUP + hi)
    return np.asarray(ea, np.int32), np.asarray(eb, np.int32)


_CLASS_EA, _CLASS_EB = _class_tables()


def kernel(x, p, norm_mix, w_in, b_conv_in, b_gate, q_norm, k_norm, lambda_q1, lambda_k1, lambda_q2, lambda_k2, subln, w_attn_out, conv_w, conv_b, conv_ln_g, conv_ln_b, w_conv_out, b_conv_out, w_o, norm_ffn, w_router_group, b_router_group, w_router_expert, b_router_expert, w1, w3, w2, norm_ple, w_ple_gate, b_ple_gate, w_ple_proj):
    b, s, d = x.shape
    n = b * s
    depth = w_in.shape[0]
    qk = ATTN_HEADS * 2 * HEAD_DIM
    aw = ATTN_HEADS * V_DIM
    cc = conv_w.shape[-1]
    assert depth == 1 and s % TM_MIX == 0 and s % (2 * ATTN_TK) == 0 and n % TM_INPROJ == 0
    assert n % TS_SCATTER == 0 and n % TM_FINAL == 0 and n % BM_MOE == 0
    i = 0
    h = x.reshape(n, d)

    q2, k2, v2, cglu2, gates2 = _inproj(h, norm_mix[i], w_in[i].astype(_BF), b_conv_in[i], b_gate[i], qk, aw, cc)

    lam_vecs = jnp.stack([lambda_q1[i], lambda_k1[i], lambda_q2[i], lambda_k2[i]]).astype(_F32)
    attn = _attention(lam_vecs, q2.reshape(b, s, qk), k2.reshape(b, s, qk), v2.reshape(b, s, aw),
                      q_norm[i], k_norm[i], subln[i])

    w_router = jnp.concatenate([w_router_group[i], w_router_expert[i]], axis=1)
    w_router = jnp.pad(w_router, ((0, 0), (0, LANES - w_router.shape[1])))
    b_router = jnp.concatenate([b_router_group[i], b_router_expert[i]])
    b_router = jnp.pad(b_router, (0, LANES - b_router.shape[0])).reshape(1, LANES)
    w_router_hi = w_router.astype(_BF)
    w_router_lo = (w_router - w_router_hi.astype(_F32)).astype(_BF)
    w_router_cat = jnp.concatenate([w_router_hi, w_router_lo], axis=1)
    conv_w_pad = jnp.pad(conv_w[i], ((0, CONV_HALO - CONV_WIDTH), (0, 0)))
    h1, t_rows, keys, cnt = _mix(h, attn.reshape(n, aw), cglu2, gates2, w_attn_out[i].astype(_BF),
                                w_conv_out[i].astype(_BF), w_o[i].astype(_BF), conv_w_pad, conv_b[i],
                                conv_ln_g[i], conv_ln_b[i], b_conv_out[i], norm_ffn[i], w_router_cat, w_router_hi,
                                b_router, s)

    bm = BM_MOE
    nb = n // bm + N_CLASSES
    counts = cnt[0].astype(jnp.int32)
    padded = ((counts + bm - 1) // bm) * bm
    pend = jnp.cumsum(padded)
    pstart = (pend - padded).astype(jnp.int32)
    total = pend[-1]
    blk = jnp.arange(nb, dtype=jnp.int32) * bm
    valid = blk < total
    last_cls = jnp.searchsorted(pend, total - 1, side='right')
    bcls = jnp.where(valid, jnp.searchsorted(pend, blk, side='right'), last_cls)
    bcls = jnp.minimum(bcls, N_CLASSES - 1).astype(jnp.int32)
    ea = jnp.asarray(_CLASS_EA)[bcls]
    eb = jnp.asarray(_CLASS_EB)[bcls]
    nvalid = (total // bm).astype(jnp.int32).reshape(1)

    xs = _scatter_rows(pstart, (pstart + counts).astype(jnp.int32), (padded - counts).astype(jnp.int32), nvalid,
                       keys, t_rows, nb * bm)
    ys = _moe(ea, eb, valid.astype(jnp.int32), xs, w_router_cat, w_router_hi, b_router,
              w1[i].astype(_BF), w3[i].astype(_BF), w2[i].astype(_BF))

    out = _final(pstart, keys, h1, p[i].reshape(n, -1), ys, norm_ple[i], w_ple_gate[i].astype(_BF), b_ple_gate[i],
                 w_ple_proj[i].astype(_BF))
    return out.reshape(b, s, d)
```

```python
import functools
import math

import numpy as np
import jax
import jax.numpy as jnp
from jax import lax
from jax.experimental import pallas as pl
from jax.experimental.pallas import tpu as pltpu

ATTN_HEADS = 4
HEAD_DIM = 64
V_DIM = 2 * HEAD_DIM
CONV_WIDTH = 31
N_GROUPS = 4
EXPERTS_PER_GROUP = 8
N_EXPERTS = N_GROUPS * EXPERTS_PER_GROUP
PAIRS_PER_GROUP = EXPERTS_PER_GROUP * (EXPERTS_PER_GROUP - 1) // 2
N_CLASSES = N_GROUPS * PAIRS_PER_GROUP
EPS = 1e-6
LAMBDA_INIT = 0.8 - 0.6 * math.exp(-0.3 * 0)

LANES = 128
SUBLANES = 8
CONV_HALO = 32
NEG_BIG = -1e30
RANK_BITS = 17
DMA_UNROLL = 8

TM_INPROJ = 512
ATTN_TK = 512
TM_MIX = 512
CONV_ROWS = 64
TS_SCATTER = 512
BM_MOE = 256
TM_FINAL = 512
VMEM_LIMIT = 56 * 1024 * 1024

_BF = jnp.bfloat16
_F32 = jnp.float32


def _const_spec(shape):
    nd = len(shape)
    return pl.BlockSpec(shape, lambda *_: (0,) * nd)


def _sigmoid(x):
    return 1.0 / (1.0 + jnp.exp(-x))


def _inproj_kernel(x_ref, nm_ref, w_ref, bci_ref, bg_ref, cw_ref, cb_ref, lng_ref, lnb_ref,
                   q_ref, k_ref, v_ref, c_ref, g_ref, ext_s, xsh_s, cv_s, *, tiles_per_seq):
    i = pl.program_id(0)
    tm = x_ref.shape[0]
    qk = q_ref.shape[1]
    aw = v_ref.shape[1]
    cc = c_ref.shape[1]
    ext_rows = CONV_HALO + tm
    o0, o1, o2, o3 = qk, 2 * qk, 2 * qk + aw, 2 * qk + aw + 2 * cc

    @pl.when(i == 0)
    def _():
        ext_s[ext_rows:ext_rows + SUBLANES, :] = jnp.zeros((SUBLANES, cc), _F32)

    @pl.when((i % tiles_per_seq) == 0)
    def _():
        ext_s[0:CONV_HALO, :] = jnp.zeros((CONV_HALO, cc), _F32)

    x = x_ref[...]
    u = x * lax.rsqrt(jnp.mean(x * x, axis=-1, keepdims=True) + EPS) * nm_ref[...]
    ub = u.astype(_BF)
    c = jnp.dot(ub, w_ref[:, o2:o3], preferred_element_type=_F32) + bci_ref[...]
    ext_s[CONV_HALO:ext_rows, :] = c[:, :cc] * _sigmoid(c[:, cc:])
    q_ref[...] = jnp.dot(ub, w_ref[:, 0:o0], preferred_element_type=_F32).astype(_BF)
    k_ref[...] = jnp.dot(ub, w_ref[:, o0:o1], preferred_element_type=_F32).astype(_BF)
    v_ref[...] = jnp.dot(ub, w_ref[:, o1:o2], preferred_element_type=_F32).astype(_BF)
    g = jnp.dot(ub, w_ref[:, o3:], preferred_element_type=_F32) + bg_ref[...]
    g_ref[...] = _sigmoid(g).astype(_BF)
    off = CONV_HALO - (CONV_WIDTH - 1)

    for rho in range(1, SUBLANES):
        xsh_s[rho - 1] = ext_s[rho:rho + ext_rows, :]

    for r0 in range(0, tm, CONV_ROWS):
        acc = jnp.zeros((CONV_ROWS, cc), _F32)
        for j in range(CONV_WIDTH):
            a, rho = divmod(off + j, SUBLANES)
            lo_row = r0 + a * SUBLANES
            if rho == 0:
                xv = ext_s[lo_row:lo_row + CONV_ROWS, :]
            else:
                xv = xsh_s[rho - 1, lo_row:lo_row + CONV_ROWS, :]
            acc = acc + cw_ref[j:j + 1, :] * xv
        cv_s[r0:r0 + CONV_ROWS, :] = acc

    ext_s[0:CONV_HALO, :] = ext_s[tm:ext_rows, :]

    cv = cv_s[...] + cb_ref[...]
    mu = jnp.mean(cv, axis=-1, keepdims=True)
    xc = cv - mu
    var = jnp.mean(xc * xc, axis=-1, keepdims=True)
    ln = xc * lax.rsqrt(var + EPS) * lng_ref[...] + lnb_ref[...]
    c_ref[...] = (ln * _sigmoid(ln)).astype(_BF)


def _inproj(x2, norm_mix, w_in_bf, b_conv_in, b_gate, conv_w, conv_b, ln_g, ln_b, qk, aw, cc, seq_len):
    n, d = x2.shape
    tm = TM_INPROJ
    ng = b_gate.shape[-1]
    kern = functools.partial(_inproj_kernel, tiles_per_seq=seq_len // tm)
    return pl.pallas_call(
        kern,
        grid=(n // tm,),
        in_specs=[
            pl.BlockSpec((tm, d), lambda i: (i, 0)),
            _const_spec((1, d)),
            pl.BlockSpec(w_in_bf.shape, lambda i: (0, 0), pipeline_mode=pl.Buffered(1)),
            _const_spec((1, 2 * cc)),
            _const_spec((1, ng)),
            _const_spec((CONV_HALO, cc)),
            _const_spec((1, cc)),
            _const_spec((1, cc)),
            _const_spec((1, cc)),
        ],
        out_specs=[
            pl.BlockSpec((tm, qk), lambda i: (i, 0)),
            pl.BlockSpec((tm, qk), lambda i: (i, 0)),
            pl.BlockSpec((tm, aw), lambda i: (i, 0)),
            pl.BlockSpec((tm, cc), lambda i: (i, 0)),
            pl.BlockSpec((tm, ng), lambda i: (i, 0)),
        ],
        out_shape=[
            jax.ShapeDtypeStruct((n, qk), _BF),
            jax.ShapeDtypeStruct((n, qk), _BF),
            jax.ShapeDtypeStruct((n, aw), _BF),
            jax.ShapeDtypeStruct((n, cc), _BF),
            jax.ShapeDtypeStruct((n, ng), _BF),
        ],
        scratch_shapes=[
            pltpu.VMEM((CONV_HALO + tm + SUBLANES, cc), _F32),
            pltpu.VMEM((SUBLANES - 1, CONV_HALO + tm, cc), _F32),
            pltpu.VMEM((tm, cc), _F32),
        ],
        compiler_params=pltpu.CompilerParams(
            dimension_semantics=("arbitrary",), vmem_limit_bytes=VMEM_LIMIT),
        name="inproj",
    )(x2, norm_mix.reshape(1, d), w_in_bf, b_conv_in.reshape(1, -1), b_gate.reshape(1, -1), conv_w,
      conv_b.reshape(1, cc), ln_g.reshape(1, cc), ln_b.reshape(1, cc))


def _half_rmsnorm(x, gain_row, lo_mask):
    x2 = x * x
    s_lo = jnp.sum(jnp.where(lo_mask, x2, 0.0), axis=-1, keepdims=True)
    s_hi = jnp.sum(jnp.where(lo_mask, 0.0, x2), axis=-1, keepdims=True)
    r = jnp.where(lo_mask, lax.rsqrt(s_lo * (1.0 / HEAD_DIM) + EPS), lax.rsqrt(s_hi * (1.0 / HEAD_DIM) + EPS))
    return x * r * gain_row


def _attn_kernel(lam_ref, q_ref, k_ref, v_ref, qn_ref, kn_ref, sub_ref, o_ref, kn_s, va_s, qs_s, m_s, acc_s,
                 sa_s, sb_s):
    s_len = q_ref.shape[1]
    tk = ATTN_TK
    tq = 2 * tk
    nq = s_len // tq
    lane = lax.broadcasted_iota(jnp.int32, (1, V_DIM), 1)
    lo_mask = lane < HEAD_DIM

    lv = lam_ref[...]
    d1 = jnp.sum(lv[0:1, :] * lv[1:2, :], axis=-1, keepdims=True)
    d2 = jnp.sum(lv[2:3, :] * lv[3:4, :], axis=-1, keepdims=True)
    lam = jnp.exp(d1) - jnp.exp(d2) + LAMBDA_INIT

    kgain = kn_ref[...]
    qgain = qn_ref[...] * (math.log2(math.e) / math.sqrt(HEAD_DIM))

    def prep_kv(i, carry):
        r0 = pl.multiple_of(i * tq, tq)
        kk = k_ref[0, pl.ds(r0, tq), :].astype(_F32)
        kn_s[pl.ds(r0, tq), :] = _half_rmsnorm(kk, kgain, lo_mask).astype(_BF)
        va_s[pl.ds(r0, tq), 0:V_DIM] = v_ref[0, pl.ds(r0, tq), :]
        va_s[pl.ds(r0, tq), V_DIM:2 * V_DIM] = jnp.ones((tq, V_DIM), _BF)
        return carry

    lax.fori_loop(0, nq, prep_kv, 0)

    def scores(c0, ra, nr):
        kc = kn_s[pl.ds(c0, tk), :]
        return lax.dot_general(qs_s[ra:ra + nr, :], kc, (((1,), (1,)), ((), ())), preferred_element_type=_F32)

    def consume(s, c0, ra, nr, n_masked):
        vc = va_s[pl.ds(c0, tk), :]
        if n_masked:
            keep = (lax.broadcasted_iota(jnp.int32, (tk, tk), 1) <= lax.broadcasted_iota(jnp.int32, (tk, tk), 0))
            groups = [s[g * tk:(g + 1) * tk, :] for g in range(nr // tk)]
            groups = [jnp.where(keep, sg, NEG_BIG) if g < n_masked else sg for g, sg in enumerate(groups)]
            s = jnp.concatenate(groups, axis=0)
        m_old = m_s[ra:ra + nr, :]
        m_new = jnp.maximum(m_old, jnp.max(s, axis=-1, keepdims=True))
        alpha = jnp.exp2(m_old - m_new)
        p = jnp.concatenate(
            [jnp.exp2(s[:, c * LANES:(c + 1) * LANES] - m_new) for c in range(tk // LANES)], axis=1)
        pv = jnp.dot(p.astype(_BF), vc, preferred_element_type=_F32)
        acc_s[ra:ra + nr, :] = jnp.concatenate([alpha, alpha], axis=1) * acc_s[ra:ra + nr, :] + pv
        m_s[ra:ra + nr, :] = m_new

    def q_block(qi, carry):
        r0 = pl.multiple_of(qi * tq, tq)
        for hf in range(2):
            qq = _half_rmsnorm(q_ref[0, pl.ds(r0 + hf * tk, tk), :].astype(_F32), qgain, lo_mask)
            qs_s[(2 * hf) * tk:(2 * hf + 1) * tk, :] = jnp.where(lo_mask, qq, 0.0).astype(_BF)
            qs_s[(2 * hf + 1) * tk:(2 * hf + 2) * tk, :] = jnp.where(lo_mask, 0.0, qq).astype(_BF)
        m_s[...] = jnp.full(m_s.shape, NEG_BIG, _F32)
        acc_s[...] = jnp.zeros(acc_s.shape, _F32)

        sa_s[...] = scores(0, 0, 4 * tk)

        def two_chunks(t, c):
            c0 = pl.multiple_of(2 * t * tk, tk)
            c1 = pl.multiple_of(c0 + tk, tk)
            c2 = pl.multiple_of(c0 + 2 * tk, tk)
            sb_s[...] = scores(c1, 0, 4 * tk)
            consume(sa_s[...], c0, 0, 4 * tk, 0)
            sa_s[...] = scores(c2, 0, 4 * tk)
            consume(sb_s[...], c1, 0, 4 * tk, 0)
            return c

        lax.fori_loop(0, qi, two_chunks, 0)
        c1 = pl.multiple_of(r0 + tk, tk)
        sb_s[0:2 * tk, :] = scores(c1, 2 * tk, 2 * tk)
        consume(sa_s[...], r0, 0, 4 * tk, 2)
        consume(sb_s[0:2 * tk, :], c1, 2 * tk, 2 * tk, 2)

        acc = acc_s[...]
        o = acc[:, 0:V_DIM] / acc[:, V_DIM:2 * V_DIM]
        for hf in range(2):
            a = o[(2 * hf) * tk:(2 * hf + 1) * tk, :] - lam * o[(2 * hf + 1) * tk:(2 * hf + 2) * tk, :]
            a = a * lax.rsqrt(jnp.mean(a * a, axis=-1, keepdims=True) + EPS) * sub_ref[...] * (1.0 - LAMBDA_INIT)
            o_ref[0, pl.ds(r0 + hf * tk, tk), :] = a.astype(_BF)
        return carry

    lax.fori_loop(0, nq, q_block, 0)


def _attention(lam_vecs, q3, k3, v3, q_norm, k_norm, subln):
    b, s, _ = q3.shape
    tq = 2 * ATTN_TK
    slab = lambda bi, hi: (bi, 0, hi)
    return pl.pallas_call(
        _attn_kernel,
        grid=(b, ATTN_HEADS),
        in_specs=[
            _const_spec((4, HEAD_DIM)),
            pl.BlockSpec((1, s, V_DIM), slab),
            pl.BlockSpec((1, s, V_DIM), slab),
            pl.BlockSpec((1, s, V_DIM), slab),
            _const_spec((1, V_DIM)),
            _const_spec((1, V_DIM)),
            _const_spec((1, V_DIM)),
        ],
        out_specs=pl.BlockSpec((1, s, V_DIM), slab),
        out_shape=jax.ShapeDtypeStruct((b, s, ATTN_HEADS * V_DIM), _BF),
        scratch_shapes=[
            pltpu.VMEM((s, V_DIM), _BF),
            pltpu.VMEM((s, 2 * V_DIM), _BF),
            pltpu.VMEM((2 * tq, V_DIM), _BF),
            pltpu.VMEM((2 * tq, V_DIM), _F32),
            pltpu.VMEM((2 * tq, 2 * V_DIM), _F32),
            pltpu.VMEM((2 * tq, ATTN_TK), _F32),
            pltpu.VMEM((2 * tq, ATTN_TK), _F32),
        ],
        compiler_params=pltpu.CompilerParams(
            dimension_semantics=("arbitrary", "arbitrary"), vmem_limit_bytes=VMEM_LIMIT),
        name="diffattn",
    )(lam_vecs, q3, k3, v3, q_norm.reshape(1, V_DIM), k_norm.reshape(1, V_DIM), subln.reshape(1, V_DIM))


def _router_logits(t, wr_ref, wrh_ref, br_ref):
    t_hi = t.astype(_BF)
    t_lo = (t - t_hi.astype(_F32)).astype(_BF)
    l_hi = jnp.dot(t_hi, wr_ref[...], preferred_element_type=_F32)
    l_lo = jnp.dot(t_lo, wrh_ref[...], preferred_element_type=_F32)
    return l_hi[:, 0:LANES] + l_hi[:, LANES:2 * LANES] + l_lo + br_ref[...]


def _mix_kernel(x_ref, a_ref, c_ref, g_ref, wa_ref, wc_ref, wo_ref, bco_ref, nf_ref, wr_ref, wrh_ref, br_ref,
                h_ref, t_ref, key_ref, cnt_ref, run_s):
    i = pl.program_id(0)
    tm, d = x_ref.shape

    @pl.when(i == 0)
    def _():
        run_s[...] = jnp.zeros(run_s.shape, _F32)

    y_b = jnp.dot(c_ref[...], wc_ref[...], preferred_element_type=_F32) + bco_ref[...]
    y_a = jnp.dot(a_ref[...], wa_ref[...], preferred_element_type=_F32)

    merged = g_ref[:, 0:d].astype(_F32) * y_a + g_ref[:, d:2 * d].astype(_F32) * y_b
    h = x_ref[...] + jnp.dot(merged.astype(_BF), wo_ref[...], preferred_element_type=_F32)
    h_ref[...] = h

    t = h * lax.rsqrt(jnp.mean(h * h, axis=-1, keepdims=True) + EPS) * nf_ref[...]
    for sg in range(d // LANES):
        t_ref[pl.ds(sg, tm, stride=SUBLANES), :] = t[:, sg * LANES:(sg + 1) * LANES]

    lg = _router_logits(t, wr_ref, wrh_ref, br_ref)
    lane = lax.broadcasted_iota(jnp.int32, (tm, LANES), 1)
    gl = jnp.where(lane < N_GROUPS, lg, NEG_BIG)
    gmax = jnp.max(gl, axis=-1, keepdims=True)
    gidx = jnp.min(jnp.where(gl == gmax, lane, LANES), axis=-1, keepdims=True)
    e_lo = N_GROUPS + EXPERTS_PER_GROUP * gidx
    el = jnp.where((lane >= e_lo) & (lane < e_lo + EXPERTS_PER_GROUP), lg, NEG_BIG)
    m1 = jnp.max(el, axis=-1, keepdims=True)
    i1 = jnp.min(jnp.where(el == m1, lane, LANES), axis=-1, keepdims=True)
    el2 = jnp.where(lane == i1, NEG_BIG, el)
    m2 = jnp.max(el2, axis=-1, keepdims=True)
    i2 = jnp.min(jnp.where(el2 == m2, lane, LANES), axis=-1, keepdims=True)
    a1 = i1 - e_lo
    a2 = i2 - e_lo
    lo = jnp.minimum(a1, a2)
    hi = jnp.maximum(a1, a2)
    pidx = ((lo * (2 * EXPERTS_PER_GROUP - 1 - lo)) >> 1) + (hi - lo - 1)
    cls = gidx * PAIRS_PER_GROUP + pidx

    onehot = lane == cls
    oh_bf = jnp.where(onehot, 1.0, 0.0).astype(_BF)
    rr = lax.broadcasted_iota(jnp.int32, (tm, tm), 0)
    ccol = lax.broadcasted_iota(jnp.int32, (tm, tm), 1)
    tri = jnp.where(ccol < rr, 1.0, 0.0).astype(_BF)
    before = jnp.dot(tri, oh_bf, preferred_element_type=_F32) + run_s[...]
    rank = jnp.sum(jnp.where(onehot, before, 0.0), axis=-1, keepdims=True)
    run_new = run_s[...] + jnp.sum(jnp.where(onehot, 1.0, 0.0), axis=0, keepdims=True)
    run_s[...] = run_new
    cnt_ref[...] = run_new
    keyf = jnp.broadcast_to(cls.astype(_F32) * float(1 << RANK_BITS) + rank, (tm, LANES))
    for g in range(tm // LANES):
        kt = keyf[g * LANES:(g + 1) * LANES, :].T
        key_ref[0, g:g + 1, :] = kt[0:1, :].astype(jnp.int32)


def _mix(x2, attn2, cact2, gates2, wa_bf, wc_bf, wo_bf, b_conv_out, norm_ffn, w_router_cat, w_router_hi, b_router):
    n, d = x2.shape
    tm = TM_MIX
    aw = attn2.shape[1]
    cc = cact2.shape[1]
    row = lambda i: (i, 0)
    return pl.pallas_call(
        _mix_kernel,
        grid=(n // tm,),
        in_specs=[
            pl.BlockSpec((tm, d), row),
            pl.BlockSpec((tm, aw), row),
            pl.BlockSpec((tm, cc), row),
            pl.BlockSpec((tm, 2 * d), row),
            _const_spec(wa_bf.shape),
            _const_spec(wc_bf.shape),
            _const_spec(wo_bf.shape),
            _const_spec((1, d)),
            _const_spec((1, d)),
            _const_spec((d, 2 * LANES)),
            _const_spec((d, LANES)),
            _const_spec((1, LANES)),
        ],
        out_specs=[
            pl.BlockSpec((tm, d), row),
            pl.BlockSpec((tm * SUBLANES, LANES), row),
            pl.BlockSpec((1, tm // LANES, LANES), lambda i: (i, 0, 0)),
            _const_spec((1, LANES)),
        ],
        out_shape=[
            jax.ShapeDtypeStruct((n, d), _F32),
            jax.ShapeDtypeStruct((n * SUBLANES, LANES), _F32),
            jax.ShapeDtypeStruct((n // tm, tm // LANES, LANES), jnp.int32),
            jax.ShapeDtypeStruct((1, LANES), _F32),
        ],
        scratch_shapes=[pltpu.VMEM((1, LANES), _F32)],
        compiler_params=pltpu.CompilerParams(
            dimension_semantics=("arbitrary",), vmem_limit_bytes=VMEM_LIMIT),
        name="mix_router",
    )(x2, attn2, cact2, gates2, wa_bf, wc_bf, wo_bf, b_conv_out.reshape(1, d), norm_ffn.reshape(1, d),
      w_router_cat, w_router_hi, b_router)


def _sorted_rows_kernel(pstart_ref, key_ref, o_ref):
    key = key_ref[...]
    cls = key >> RANK_BITS
    row = key & ((1 << RANK_BITS) - 1)
    for c in range(N_CLASSES):
        row = row + jnp.where(cls == c, pstart_ref[c], 0)
    o_ref[...] = row


def _sorted_rows(pstart, keys2):
    grid_spec = pltpu.PrefetchScalarGridSpec(
        num_scalar_prefetch=1,
        grid=(1,),
        in_specs=[pl.BlockSpec(keys2.shape, lambda i, *_: (0, 0))],
        out_specs=pl.BlockSpec(keys2.shape, lambda i, *_: (0, 0)),
    )
    return pl.pallas_call(
        _sorted_rows_kernel,
        grid_spec=grid_spec,
        out_shape=jax.ShapeDtypeStruct(keys2.shape, jnp.int32),
        name="sorted_rows",
    )(pstart, keys2)


def _token_rows(idx, count=1):
    return pl.ds(pl.multiple_of(idx * SUBLANES, SUBLANES), count * SUBLANES)


def _scatter_kernel(padstart_ref, padlen_ref, nvalid_ref, dest_ref, t_ref, xs_ref, zero_s, sem, zsem):
    i = pl.program_id(0)
    ts = t_ref.shape[0] // SUBLANES
    bm = zero_s.shape[0] // SUBLANES
    nblk = xs_ref.shape[0] // (bm * SUBLANES)

    def start_rows(g, c):
        for u in range(DMA_UNROLL):
            r = g * DMA_UNROLL + u
            pltpu.make_async_copy(t_ref.at[_token_rows(r)], xs_ref.at[_token_rows(dest_ref[0, 0, r])], sem).start()
        return c

    def wait_rows(g, c):
        for _ in range(DMA_UNROLL):
            pltpu.make_async_copy(t_ref.at[_token_rows(0)], xs_ref.at[_token_rows(0)], sem).wait()
        return c

    lax.fori_loop(0, ts // DMA_UNROLL, start_rows, 0)

    @pl.when(i == 0)
    def _():
        zero_s[...] = jnp.zeros(zero_s.shape, zero_s.dtype)

        def pad_copies(c, act):
            ln = padlen_ref[c]
            st = padstart_ref[c]
            for k in range(bm.bit_length() - 1):
                sz = 1 << k

                @pl.when(((ln >> k) & 1) == 1)
                def _():
                    dst = st + (ln & (sz - 1))
                    act(pltpu.make_async_copy(zero_s.at[_token_rows(0, sz)], xs_ref.at[_token_rows(dst, sz)], zsem))

        def tail_copy(j, act):
            @pl.when(j >= nvalid_ref[0])
            def _():
                act(pltpu.make_async_copy(zero_s, xs_ref.at[_token_rows(j * bm, bm)], zsem))

        for act in (lambda cp: cp.start(), lambda cp: cp.wait()):
            lax.fori_loop(0, N_CLASSES, lambda c, carry, act=act: (pad_copies(c, act), carry)[1], 0)
            lax.fori_loop(0, nblk, lambda j, carry, act=act: (tail_copy(j, act), carry)[1], 0)

    lax.fori_loop(0, ts // DMA_UNROLL, wait_rows, 0)


def _scatter_rows(padstart, padlen, nvalid, dest, t_rows, n_rows):
    n = t_rows.shape[0] // SUBLANES
    ts = TS_SCATTER
    grid_spec = pltpu.PrefetchScalarGridSpec(
        num_scalar_prefetch=3,
        grid=(n // ts,),
        in_specs=[
            pl.BlockSpec((1, 1, ts), lambda i, *_: (i, 0, 0), memory_space=pltpu.SMEM),
            pl.BlockSpec((ts * SUBLANES, LANES), lambda i, *_: (i, 0)),
        ],
        out_specs=pl.BlockSpec(memory_space=pl.ANY),
        scratch_shapes=[pltpu.VMEM((BM_MOE * SUBLANES, LANES), _F32), pltpu.SemaphoreType.DMA,
                        pltpu.SemaphoreType.DMA],
    )
    return pl.pallas_call(
        _scatter_kernel,
        grid_spec=grid_spec,
        out_shape=jax.ShapeDtypeStruct((n_rows * SUBLANES, LANES), _F32),
        compiler_params=pltpu.CompilerParams(
            dimension_semantics=("arbitrary",), vmem_limit_bytes=VMEM_LIMIT),
        name="scatter_rows",
    )(padstart, padlen, nvalid, dest.reshape(n // ts, 1, ts), t_rows)


def _moe_kernel(ea_ref, eb_ref, valid_ref, xs_ref, wr_ref, wrh_ref, br_ref, w1a, w3a, w2a, w1b, w3b, w2b, y_ref):
    j = pl.program_id(0)
    bm = xs_ref.shape[0] // SUBLANES

    @pl.when(valid_ref[j] == 0)
    def _():
        y_ref[...] = jnp.zeros(y_ref.shape, y_ref.dtype)

    @pl.when(valid_ref[j] != 0)
    def _():
        x = jnp.concatenate([xs_ref[pl.ds(sg, bm, stride=SUBLANES), :] for sg in range(SUBLANES)], axis=1)
        xb = x.astype(_BF)

        lg = _router_logits(x, wr_ref, wrh_ref, br_ref)
        lane = lax.broadcasted_iota(jnp.int32, (bm, LANES), 1)
        gl = jnp.where(lane < N_GROUPS, lg, NEG_BIG)
        gmax = jnp.max(gl, axis=-1, keepdims=True)
        gsum = jnp.sum(jnp.where(lane < N_GROUPS, jnp.exp(gl - gmax), 0.0), axis=-1, keepdims=True)
        g_gate = 1.0 / gsum
        la = jnp.sum(jnp.where(lane == N_GROUPS + ea_ref[j], lg, 0.0), axis=-1, keepdims=True)
        lb = jnp.sum(jnp.where(lane == N_GROUPS + eb_ref[j], lg, 0.0), axis=-1, keepdims=True)
        w_a = g_gate * (1.0 / (1.0 + jnp.exp(lb - la)))
        w_b = g_gate * (1.0 / (1.0 + jnp.exp(la - lb)))

        def expert(w1, w3, w2):
            h1 = jnp.dot(xb, w1[0], preferred_element_type=_F32)
            h3 = jnp.dot(xb, w3[0], preferred_element_type=_F32)
            hdn = h1 * _sigmoid(h1) * h3
            return jnp.dot(hdn.astype(_BF), w2[0], preferred_element_type=_F32)

        y = w_a * expert(w1a, w3a, w2a) + w_b * expert(w1b, w3b, w2b)
        for sg in range(SUBLANES):
            y_ref[pl.ds(sg, bm, stride=SUBLANES), :] = y[:, sg * LANES:(sg + 1) * LANES]


def _moe(ea, eb, valid, xs, w_router_cat, w_router_hi, b_router, w1_bf, w3_bf, w2_bf):
    d, ff = w1_bf.shape[1], w1_bf.shape[2]
    assert d == SUBLANES * LANES
    bm = BM_MOE
    nb = xs.shape[0] // (bm * SUBLANES)
    const = lambda shape: pl.BlockSpec(shape, lambda j, *_: (0,) * len(shape))
    grid_spec = pltpu.PrefetchScalarGridSpec(
        num_scalar_prefetch=3,
        grid=(nb,),
        in_specs=[
            pl.BlockSpec((bm * SUBLANES, LANES), lambda j, ea, eb, va: (jnp.where(va[j] != 0, j, 0), 0)),
            const(w_router_cat.shape),
            const(w_router_hi.shape),
            const(b_router.shape),
            pl.BlockSpec((1, d, ff), lambda j, ea, eb, va: (ea[j], 0, 0)),
            pl.BlockSpec((1, d, ff), lambda j, ea, eb, va: (ea[j], 0, 0)),
            pl.BlockSpec((1, ff, d), lambda j, ea, eb, va: (ea[j], 0, 0)),
            pl.BlockSpec((1, d, ff), lambda j, ea, eb, va: (eb[j], 0, 0)),
            pl.BlockSpec((1, d, ff), lambda j, ea, eb, va: (eb[j], 0, 0)),
            pl.BlockSpec((1, ff, d), lambda j, ea, eb, va: (eb[j], 0, 0)),
        ],
        out_specs=pl.BlockSpec((bm * SUBLANES, LANES), lambda j, ea, eb, va: (j, 0)),
    )
    return pl.pallas_call(
        _moe_kernel,
        grid_spec=grid_spec,
        out_shape=jax.ShapeDtypeStruct(xs.shape, _F32),
        compiler_params=pltpu.CompilerParams(
            dimension_semantics=("arbitrary",), vmem_limit_bytes=VMEM_LIMIT),
        name="moe_experts",
    )(ea, eb, valid, xs, w_router_cat, w_router_hi, b_router, w1_bf, w3_bf, w2_bf, w1_bf, w3_bf, w2_bf)


def _final_kernel(key_ref, keyn_ref, h_ref, p_ref, ys_ref, np_ref, wg_ref, bg_ref, wp_ref, o_ref, y_s, sem):
    i = pl.program_id(0)
    nsteps = pl.num_programs(0)
    tm = h_ref.shape[0]

    def start_row(kref, sl, r):
        pltpu.make_async_copy(ys_ref.at[_token_rows(kref[0, 0, r])], y_s.at[sl, _token_rows(r)], sem.at[sl]).start()

    def wait_slot(sl):
        def wait_rows(g, c):
            for _ in range(DMA_UNROLL):
                pltpu.make_async_copy(ys_ref.at[_token_rows(0)], y_s.at[sl, _token_rows(0)], sem.at[sl]).wait()
            return c

        lax.fori_loop(0, tm // DMA_UNROLL, wait_rows, 0)

    @pl.when(i == 0)
    def _():
        def body(g, c):
            for u in range(DMA_UNROLL):
                start_row(key_ref, 0, g * DMA_UNROLL + u)
            return c

        lax.fori_loop(0, tm // DMA_UNROLL, body, 0)

    def step(sl):
        wait_slot(sl)
        for r in range(tm):
            start_row(keyn_ref, 1 - sl, r)
        proj = jnp.dot(p_ref[...].astype(_BF), wp_ref[...], preferred_element_type=_F32)
        y = jnp.concatenate([y_s[sl, pl.ds(sg, tm, stride=SUBLANES), :] for sg in range(SUBLANES)], axis=1)
        h = h_ref[...] + y
        u = h * lax.rsqrt(jnp.mean(h * h, axis=-1, keepdims=True) + EPS) * np_ref[...]
        gate = _sigmoid(jnp.dot(u.astype(_BF), wg_ref[...], preferred_element_type=_F32) + bg_ref[...])
        o_ref[...] = h + gate * proj

        @pl.when(i == nsteps - 1)
        def _():
            wait_slot(1 - sl)

    for sl in range(2):
        @pl.when(i % 2 == sl)
        def _(sl=sl):
            step(sl)


def _final(dest, h1, p2, ys, norm_ple, wg_bf, b_ple_gate, wp_bf):
    n, d = h1.shape
    assert d == SUBLANES * LANES
    tm = TM_FINAL
    nsteps = n // tm
    pd = p2.shape[1]
    row = lambda i: (i, 0)
    dest3 = dest.reshape(nsteps, 1, tm)
    return pl.pallas_call(
        _final_kernel,
        grid=(nsteps,),
        in_specs=[
            pl.BlockSpec((1, 1, tm), lambda i: (i, 0, 0), memory_space=pltpu.SMEM),
            pl.BlockSpec((1, 1, tm), lambda i: (jnp.minimum(i + 1, nsteps - 1), 0, 0), memory_space=pltpu.SMEM),
            pl.BlockSpec((tm, d), row),
            pl.BlockSpec((tm, pd), row),
            pl.BlockSpec(memory_space=pl.ANY),
            _const_spec((1, d)),
            _const_spec(wg_bf.shape),
            _const_spec((1, d)),
            _const_spec(wp_bf.shape),
        ],
        out_specs=pl.BlockSpec((tm, d), row),
        out_shape=jax.ShapeDtypeStruct((n, d), _F32),
        scratch_shapes=[pltpu.VMEM((2, tm * SUBLANES, LANES), _F32), pltpu.SemaphoreType.DMA((2,))],
        compiler_params=pltpu.CompilerParams(
            dimension_semantics=("arbitrary",), vmem_limit_bytes=VMEM_LIMIT),
        name="final_ple",
    )(dest3, dest3, h1, p2, ys, norm_ple.reshape(1, d), wg_bf, b_ple_gate.reshape(1, d), wp_bf)


def _class_tables():
    ea, eb = [], []
    for g in range(N_GROUPS):
        for lo in range(EXPERTS_PER_GROUP):
            for hi in range(lo + 1, EXPERTS_PER_GROUP):
                ea.append(g * EXPERTS_PER_GROUP + lo)
                eb.append(g * EXPERTS_PER_GROUP + hi)
    return np.asarray(ea, np.int32), np.asarray(eb, np.int32)


_CLASS_EA, _CLASS_EB = _class_tables()


def kernel(x, p, norm_mix, w_in, b_conv_in, b_gate, q_norm, k_norm, lambda_q1, lambda_k1, lambda_q2, lambda_k2, subln, w_attn_out, conv_w, conv_b, conv_ln_g, conv_ln_b, w_conv_out, b_conv_out, w_o, norm_ffn, w_router_group, b_router_group, w_router_expert, b_router_expert, w1, w3, w2, norm_ple, w_ple_gate, b_ple_gate, w_ple_proj):
    b, s, d = x.shape
    n = b * s
    depth = w_in.shape[0]
    qk = ATTN_HEADS * 2 * HEAD_DIM
    aw = ATTN_HEADS * V_DIM
    cc = conv_w.shape[-1]
    assert depth == 1 and n % TM_MIX == 0 and s % (2 * ATTN_TK) == 0 and s % TM_INPROJ == 0
    assert n % TS_SCATTER == 0 and n % TM_FINAL == 0 and n % BM_MOE == 0
    i = 0
    h = x.reshape(n, d)

    conv_w_pad = jnp.pad(conv_w[i], ((0, CONV_HALO - CONV_WIDTH), (0, 0)))
    q2, k2, v2, cact2, gates2 = _inproj(h, norm_mix[i], w_in[i].astype(_BF), b_conv_in[i], b_gate[i], conv_w_pad,
                                        conv_b[i], conv_ln_g[i], conv_ln_b[i], qk, aw, cc, s)

    lam_vecs = jnp.stack([lambda_q1[i], lambda_k1[i], lambda_q2[i], lambda_k2[i]]).astype(_F32)
    attn = _attention(lam_vecs, q2.reshape(b, s, qk), k2.reshape(b, s, qk), v2.reshape(b, s, aw),
                      q_norm[i], k_norm[i], subln[i])

    w_router = jnp.concatenate([w_router_group[i], w_router_expert[i]], axis=1)
    w_router = jnp.pad(w_router, ((0, 0), (0, LANES - w_router.shape[1])))
    b_router = jnp.concatenate([b_router_group[i], b_router_expert[i]])
    b_router = jnp.pad(b_router, (0, LANES - b_router.shape[0])).reshape(1, LANES)
    w_router_hi = w_router.astype(_BF)
    w_router_lo = (w_router - w_router_hi.astype(_F32)).astype(_BF)
    w_router_cat = jnp.concatenate([w_router_hi, w_router_lo], axis=1)
    h1, t_rows, keys, cnt = _mix(h, attn.reshape(n, aw), cact2, gates2, w_attn_out[i].astype(_BF),
                                 w_conv_out[i].astype(_BF), w_o[i].astype(_BF), b_conv_out[i], norm_ffn[i],
                                 w_router_cat, w_router_hi, b_router)

    bm = BM_MOE
    nb = n // bm + N_CLASSES
    counts = cnt[0].astype(jnp.int32)
    padded = ((counts + bm - 1) // bm) * bm
    pend = jnp.cumsum(padded)
    pstart = (pend - padded).astype(jnp.int32)
    total = pend[-1]
    blk = jnp.arange(nb, dtype=jnp.int32) * bm
    valid = blk < total
    last_cls = jnp.searchsorted(pend, total - 1, side='right')
    bcls = jnp.where(valid, jnp.searchsorted(pend, blk, side='right'), last_cls)
    bcls = jnp.minimum(bcls, N_CLASSES - 1).astype(jnp.int32)
    ea = jnp.asarray(_CLASS_EA)[bcls]
    eb = jnp.asarray(_CLASS_EB)[bcls]
    nvalid = (total // bm).astype(jnp.int32).reshape(1)

    dest = _sorted_rows(pstart, keys.reshape(n // LANES, LANES))
    xs = _scatter_rows((pstart + counts).astype(jnp.int32), (padded - counts).astype(jnp.int32), nvalid,
                       dest, t_rows, nb * bm)
    ys = _moe(ea, eb, valid.astype(jnp.int32), xs, w_router_cat, w_router_hi, b_router,
              w1[i].astype(_BF), w3[i].astype(_BF), w2[i].astype(_BF))

    out = _final(dest, h1, p[i].reshape(n, -1), ys, norm_ple[i], w_ple_gate[i].astype(_BF), b_ple_gate[i],
                 w_ple_proj[i].astype(_BF))
    return out.reshape(b, s, d)
```

```python
import functools
import math

import numpy as np
import jax
import jax.numpy as jnp
from jax import lax
from jax.experimental import pallas as pl
from jax.experimental.pallas import tpu as pltpu

ATTN_HEADS = 4
HEAD_DIM = 64
V_DIM = 2 * HEAD_DIM
CONV_WIDTH = 31
N_GROUPS = 4
EXPERTS_PER_GROUP = 8
N_EXPERTS = N_GROUPS * EXPERTS_PER_GROUP
PAIRS_PER_GROUP = EXPERTS_PER_GROUP * (EXPERTS_PER_GROUP - 1) // 2
N_CLASSES = N_GROUPS * PAIRS_PER_GROUP
EPS = 1e-6
LAMBDA_INIT = 0.8 - 0.6 * math.exp(-0.3 * 0)

LANES = 128
SUBLANES = 8
CONV_HALO = 32
NEG_BIG = -1e30
RANK_BITS = 17
DMA_UNROLL = 8

TM_INPROJ = 512
ATTN_TK = 512
TM_MIX = 512
CONV_ROWS = 32
CONV_LANES = 256
TS_SCATTER = 512
BM_MOE = 256
TM_FINAL = 512
VMEM_LIMIT = 56 * 1024 * 1024

_BF = jnp.bfloat16
_F32 = jnp.float32


def _const_spec(shape):
    nd = len(shape)
    return pl.BlockSpec(shape, lambda *_: (0,) * nd)


def _sigmoid(x):
    return 1.0 / (1.0 + jnp.exp(-x))


def _inproj_kernel(x_ref, nm_ref, w_ref, bci_ref, bg_ref, cw_ref, cb_ref, lng_ref, lnb_ref,
                   q_ref, k_ref, v_ref, c_ref, g_ref, ext_s, xsh_s, cv_s, ub_s, *, tiles_per_seq):
    i = pl.program_id(0)
    tm = x_ref.shape[0]
    qk = q_ref.shape[1]
    aw = v_ref.shape[1]
    cc = c_ref.shape[1]
    ext_rows = CONV_HALO + tm
    o0, o1, o2, o3 = qk, 2 * qk, 2 * qk + aw, 2 * qk + aw + 2 * cc

    @pl.when(i == 0)
    def _():
        ext_s[ext_rows:ext_rows + SUBLANES, :] = jnp.zeros((SUBLANES, cc), _F32)

    @pl.when((i % tiles_per_seq) == 0)
    def _():
        ext_s[0:CONV_HALO, :] = jnp.zeros((CONV_HALO, cc), _F32)

    stage_no = [0]

    def stage(fn):
        stage_no[0] += 1
        pl.when(i > -stage_no[0])(fn)

    off = CONV_HALO - (CONV_WIDTH - 1)

    def conv_unit(r0, l0):
        acc = jnp.zeros((CONV_ROWS, CONV_LANES), _F32)
        for j in range(CONV_WIDTH):
            a, rho = divmod(off + j, SUBLANES)
            lo_row = r0 + a * SUBLANES
            if rho == 0:
                xv = ext_s[lo_row:lo_row + CONV_ROWS, l0:l0 + CONV_LANES]
            else:
                xv = xsh_s[rho - 1, lo_row:lo_row + CONV_ROWS, l0:l0 + CONV_LANES]
            acc = acc + cw_ref[j:j + 1, l0:l0 + CONV_LANES] * xv
        cv_s[r0:r0 + CONV_ROWS, l0:l0 + CONV_LANES] = acc

    def project(lo, hi):
        return jnp.dot(ub_s[...], w_ref[:, lo:hi], preferred_element_type=_F32)

    x = x_ref[...]
    u = x * lax.rsqrt(jnp.mean(x * x, axis=-1, keepdims=True) + EPS) * nm_ref[...]
    ub_s[...] = u.astype(_BF)
    c = project(o2, o3) + bci_ref[...]
    ext_s[CONV_HALO:ext_rows, :] = c[:, :cc] * _sigmoid(c[:, cc:])

    @stage
    def _():
        for rho in range(1, SUBLANES):
            xsh_s[rho - 1] = ext_s[rho:rho + ext_rows, :]
        q_ref[...] = project(0, o0).astype(_BF)

    units = [(r0, l0) for r0 in range(0, tm, CONV_ROWS) for l0 in range(0, cc, CONV_LANES)]
    gw = g_ref.shape[1] // 4
    col_slices = [(k_ref, o0, o1, None), (v_ref, o1, o2, None)] + [
        (g_ref, o3 + t * gw, o3 + (t + 1) * gw, t * gw) for t in range(4)]
    per_stage = -(-len(units) // len(col_slices))
    for si, (out_ref, lo, hi, gcol) in enumerate(col_slices):
        mine = units[si * per_stage:(si + 1) * per_stage]
        last = si == len(col_slices) - 1

        @stage
        def _(out_ref=out_ref, lo=lo, hi=hi, gcol=gcol, mine=mine, last=last):
            for r0, l0 in mine:
                conv_unit(r0, l0)
            if gcol is None:
                out_ref[...] = project(lo, hi).astype(_BF)
            else:
                gg = project(lo, hi) + bg_ref[:, gcol:gcol + gw]
                out_ref[:, gcol:gcol + gw] = _sigmoid(gg).astype(_BF)
            if last:
                cv = cv_s[...] + cb_ref[...]
                mu = jnp.mean(cv, axis=-1, keepdims=True)
                xc = cv - mu
                var = jnp.mean(xc * xc, axis=-1, keepdims=True)
                ln = xc * lax.rsqrt(var + EPS) * lng_ref[...] + lnb_ref[...]
                c_ref[...] = (ln * _sigmoid(ln)).astype(_BF)

    ext_s[0:CONV_HALO, :] = ext_s[tm:ext_rows, :]


def _inproj(x2, norm_mix, w_in_bf, b_conv_in, b_gate, conv_w, conv_b, ln_g, ln_b, qk, aw, cc, seq_len):
    n, d = x2.shape
    tm = TM_INPROJ
    ng = b_gate.shape[-1]
    kern = functools.partial(_inproj_kernel, tiles_per_seq=seq_len // tm)
    return pl.pallas_call(
        kern,
        grid=(n // tm,),
        in_specs=[
            pl.BlockSpec((tm, d), lambda i: (i, 0)),
            _const_spec((1, d)),
            pl.BlockSpec(w_in_bf.shape, lambda i: (0, 0), pipeline_mode=pl.Buffered(1)),
            _const_spec((1, 2 * cc)),
            _const_spec((1, ng)),
            _const_spec((CONV_HALO, cc)),
            _const_spec((1, cc)),
            _const_spec((1, cc)),
            _const_spec((1, cc)),
        ],
        out_specs=[
            pl.BlockSpec((tm, qk), lambda i: (i, 0)),
            pl.BlockSpec((tm, qk), lambda i: (i, 0)),
            pl.BlockSpec((tm, aw), lambda i: (i, 0)),
            pl.BlockSpec((tm, cc), lambda i: (i, 0)),
            pl.BlockSpec((tm, ng), lambda i: (i, 0)),
        ],
        out_shape=[
            jax.ShapeDtypeStruct((n, qk), _BF),
            jax.ShapeDtypeStruct((n, qk), _BF),
            jax.ShapeDtypeStruct((n, aw), _BF),
            jax.ShapeDtypeStruct((n, cc), _BF),
            jax.ShapeDtypeStruct((n, ng), _BF),
        ],
        scratch_shapes=[
            pltpu.VMEM((CONV_HALO + tm + SUBLANES, cc), _F32),
            pltpu.VMEM((SUBLANES - 1, CONV_HALO + tm, cc), _F32),
            pltpu.VMEM((tm, cc), _F32),
            pltpu.VMEM((tm, d), _BF),
        ],
        compiler_params=pltpu.CompilerParams(
            dimension_semantics=("arbitrary",), vmem_limit_bytes=VMEM_LIMIT),
        name="inproj",
    )(x2, norm_mix.reshape(1, d), w_in_bf, b_conv_in.reshape(1, -1), b_gate.reshape(1, -1), conv_w,
      conv_b.reshape(1, cc), ln_g.reshape(1, cc), ln_b.reshape(1, cc))


def _half_rmsnorm(x, gain_row, lo_mask):
    x2 = x * x
    s_lo = jnp.sum(jnp.where(lo_mask, x2, 0.0), axis=-1, keepdims=True)
    s_hi = jnp.sum(jnp.where(lo_mask, 0.0, x2), axis=-1, keepdims=True)
    r = jnp.where(lo_mask, lax.rsqrt(s_lo * (1.0 / HEAD_DIM) + EPS), lax.rsqrt(s_hi * (1.0 / HEAD_DIM) + EPS))
    return x * r * gain_row


def _attn_kernel(lam_ref, q_ref, k_ref, v_ref, qn_ref, kn_ref, sub_ref, o_ref, kn_s, va_s, qs_s, m_s, acc_s,
                 sa_s, sb_s, sc_s):
    s_len = q_ref.shape[1]
    tk = ATTN_TK
    tq = 2 * tk
    nq = s_len // tq
    lane = lax.broadcasted_iota(jnp.int32, (1, V_DIM), 1)
    lo_mask = lane < HEAD_DIM

    lv = lam_ref[...]
    d1 = jnp.sum(lv[0:1, :] * lv[1:2, :], axis=-1, keepdims=True)
    d2 = jnp.sum(lv[2:3, :] * lv[3:4, :], axis=-1, keepdims=True)
    lam = jnp.exp(d1) - jnp.exp(d2) + LAMBDA_INIT

    kgain = kn_ref[...]
    qgain = qn_ref[...] * (math.log2(math.e) / math.sqrt(HEAD_DIM))

    def prep_kv(i, carry):
        r0 = pl.multiple_of(i * tq, tq)
        kk = k_ref[0, pl.ds(r0, tq), :].astype(_F32)
        kn_s[pl.ds(r0, tq), :] = _half_rmsnorm(kk, kgain, lo_mask).astype(_BF)
        va_s[pl.ds(r0, tq), 0:V_DIM] = v_ref[0, pl.ds(r0, tq), :]
        va_s[pl.ds(r0, tq), V_DIM:2 * V_DIM] = jnp.ones((tq, V_DIM), _BF)
        return carry

    lax.fori_loop(0, nq, prep_kv, 0)

    def scores(c0, ra, nr):
        kc = kn_s[pl.ds(c0, tk), :]
        return lax.dot_general(qs_s[ra:ra + nr, :], kc, (((1,), (1,)), ((), ())), preferred_element_type=_F32)

    def consume(s, c0, ra, nr, n_masked):
        vc = va_s[pl.ds(c0, tk), :]
        if n_masked:
            keep = (lax.broadcasted_iota(jnp.int32, (tk, tk), 1) <= lax.broadcasted_iota(jnp.int32, (tk, tk), 0))
            groups = [s[g * tk:(g + 1) * tk, :] for g in range(nr // tk)]
            groups = [jnp.where(keep, sg, NEG_BIG) if g < n_masked else sg for g, sg in enumerate(groups)]
            s = jnp.concatenate(groups, axis=0)
        m_old = m_s[ra:ra + nr, :]
        m_new = jnp.maximum(m_old, jnp.max(s, axis=-1, keepdims=True))
        alpha = jnp.exp2(m_old - m_new)
        p = jnp.concatenate(
            [jnp.exp2(s[:, c * LANES:(c + 1) * LANES] - m_new) for c in range(tk // LANES)], axis=1)
        pv = jnp.dot(p.astype(_BF), vc, preferred_element_type=_F32)
        acc_s[ra:ra + nr, :] = jnp.concatenate([alpha, alpha], axis=1) * acc_s[ra:ra + nr, :] + pv
        m_s[ra:ra + nr, :] = m_new

    def prep_q(r0):
        for hf in range(2):
            qq = _half_rmsnorm(q_ref[0, pl.ds(r0 + hf * tk, tk), :].astype(_F32), qgain, lo_mask)
            qs_s[(2 * hf) * tk:(2 * hf + 1) * tk, :] = jnp.where(lo_mask, qq, 0.0).astype(_BF)
            qs_s[(2 * hf + 1) * tk:(2 * hf + 2) * tk, :] = jnp.where(lo_mask, 0.0, qq).astype(_BF)

    bufs = (sa_s, sb_s, sc_s)
    prep_q(0)
    sa_s[...] = scores(0, 0, 4 * tk)

    for qi in range(nq):
        r0 = qi * tq
        cur = bufs[(2 * qi) % 3]
        nxt = bufs[(2 * qi + 1) % 3]
        ahead = bufs[(2 * qi + 2) % 3]
        m_s[...] = jnp.full(m_s.shape, NEG_BIG, _F32)
        acc_s[...] = jnp.zeros(acc_s.shape, _F32)

        def two_chunks(t, c, cur=cur, nxt=nxt):
            c0 = pl.multiple_of(2 * t * tk, tk)
            c1 = pl.multiple_of(c0 + tk, tk)
            c2 = pl.multiple_of(c0 + 2 * tk, tk)
            nxt[...] = scores(c1, 0, 4 * tk)
            consume(cur[...], c0, 0, 4 * tk, 0)
            cur[...] = scores(c2, 0, 4 * tk)
            consume(nxt[...], c1, 0, 4 * tk, 0)
            return c

        lax.fori_loop(0, qi, two_chunks, 0)
        nxt[0:2 * tk, :] = scores(r0 + tk, 2 * tk, 2 * tk)
        if qi + 1 < nq:
            prep_q(r0 + tq)
            ahead[...] = scores(0, 0, 4 * tk)
        consume(cur[...], r0, 0, 4 * tk, 2)
        consume(nxt[0:2 * tk, :], r0 + tk, 2 * tk, 2 * tk, 2)

        acc = acc_s[...]
        o = acc[:, 0:V_DIM] / acc[:, V_DIM:2 * V_DIM]
        for hf in range(2):
            a = o[(2 * hf) * tk:(2 * hf + 1) * tk, :] - lam * o[(2 * hf + 1) * tk:(2 * hf + 2) * tk, :]
            a = a * lax.rsqrt(jnp.mean(a * a, axis=-1, keepdims=True) + EPS) * sub_ref[...] * (1.0 - LAMBDA_INIT)
            o_ref[0, r0 + hf * tk:r0 + (hf + 1) * tk, :] = a.astype(_BF)


def _attention(lam_vecs, q3, k3, v3, q_norm, k_norm, subln):
    b, s, _ = q3.shape
    tq = 2 * ATTN_TK
    slab = lambda bi, hi: (bi, 0, hi)
    return pl.pallas_call(
        _attn_kernel,
        grid=(b, ATTN_HEADS),
        in_specs=[
            _const_spec((4, HEAD_DIM)),
            pl.BlockSpec((1, s, V_DIM), slab),
            pl.BlockSpec((1, s, V_DIM), slab),
            pl.BlockSpec((1, s, V_DIM), slab),
            _const_spec((1, V_DIM)),
            _const_spec((1, V_DIM)),
            _const_spec((1, V_DIM)),
        ],
        out_specs=pl.BlockSpec((1, s, V_DIM), slab),
        out_shape=jax.ShapeDtypeStruct((b, s, ATTN_HEADS * V_DIM), _BF),
        scratch_shapes=[
            pltpu.VMEM((s, V_DIM), _BF),
            pltpu.VMEM((s, 2 * V_DIM), _BF),
            pltpu.VMEM((2 * tq, V_DIM), _BF),
            pltpu.VMEM((2 * tq, V_DIM), _F32),
            pltpu.VMEM((2 * tq, 2 * V_DIM), _F32),
            pltpu.VMEM((2 * tq, ATTN_TK), _F32),
            pltpu.VMEM((2 * tq, ATTN_TK), _F32),
            pltpu.VMEM((2 * tq, ATTN_TK), _F32),
        ],
        compiler_params=pltpu.CompilerParams(
            dimension_semantics=("arbitrary", "arbitrary"), vmem_limit_bytes=VMEM_LIMIT),
        name="diffattn",
    )(lam_vecs, q3, k3, v3, q_norm.reshape(1, V_DIM), k_norm.reshape(1, V_DIM), subln.reshape(1, V_DIM))


def _router_logits(t, wr_ref, wrh_ref, br_ref):
    t_hi = t.astype(_BF)
    t_lo = (t - t_hi.astype(_F32)).astype(_BF)
    l_hi = jnp.dot(t_hi, wr_ref[...], preferred_element_type=_F32)
    l_lo = jnp.dot(t_lo, wrh_ref[...], preferred_element_type=_F32)
    return l_hi[:, 0:LANES] + l_hi[:, LANES:2 * LANES] + l_lo + br_ref[...]


def _mix_kernel(x_ref, a_ref, c_ref, g_ref, wa_ref, wc_ref, wo_ref, bco_ref, nf_ref, wr_ref, wrh_ref, br_ref,
                h_ref, t_ref, key_ref, cnt_ref, run_s):
    i = pl.program_id(0)
    tm, d = x_ref.shape

    @pl.when(i == 0)
    def _():
        run_s[...] = jnp.zeros(run_s.shape, _F32)

    y_b = jnp.dot(c_ref[...], wc_ref[...], preferred_element_type=_F32) + bco_ref[...]
    y_a = jnp.dot(a_ref[...], wa_ref[...], preferred_element_type=_F32)

    merged = g_ref[:, 0:d].astype(_F32) * y_a + g_ref[:, d:2 * d].astype(_F32) * y_b
    h = x_ref[...] + jnp.dot(merged.astype(_BF), wo_ref[...], preferred_element_type=_F32)
    h_ref[...] = h

    t = h * lax.rsqrt(jnp.mean(h * h, axis=-1, keepdims=True) + EPS) * nf_ref[...]
    for sg in range(d // LANES):
        t_ref[pl.ds(sg, tm, stride=SUBLANES), :] = t[:, sg * LANES:(sg + 1) * LANES]

    lg = _router_logits(t, wr_ref, wrh_ref, br_ref)
    lane = lax.broadcasted_iota(jnp.int32, (tm, LANES), 1)
    gl = jnp.where(lane < N_GROUPS, lg, NEG_BIG)
    gmax = jnp.max(gl, axis=-1, keepdims=True)
    gidx = jnp.min(jnp.where(gl == gmax, lane, LANES), axis=-1, keepdims=True)
    e_lo = N_GROUPS + EXPERTS_PER_GROUP * gidx
    el = jnp.where((lane >= e_lo) & (lane < e_lo + EXPERTS_PER_GROUP), lg, NEG_BIG)
    m1 = jnp.max(el, axis=-1, keepdims=True)
    i1 = jnp.min(jnp.where(el == m1, lane, LANES), axis=-1, keepdims=True)
    el2 = jnp.where(lane == i1, NEG_BIG, el)
    m2 = jnp.max(el2, axis=-1, keepdims=True)
    i2 = jnp.min(jnp.where(el2 == m2, lane, LANES), axis=-1, keepdims=True)
    a1 = i1 - e_lo
    a2 = i2 - e_lo
    lo = jnp.minimum(a1, a2)
    hi = jnp.maximum(a1, a2)
    pidx = ((lo * (2 * EXPERTS_PER_GROUP - 1 - lo)) >> 1) + (hi - lo - 1)
    cls = gidx * PAIRS_PER_GROUP + pidx

    onehot = lane == cls
    oh_bf = jnp.where(onehot, 1.0, 0.0).astype(_BF)
    rr = lax.broadcasted_iota(jnp.int32, (tm, tm), 0)
    ccol = lax.broadcasted_iota(jnp.int32, (tm, tm), 1)
    tri = jnp.where(ccol < rr, 1.0, 0.0).astype(_BF)
    before = jnp.dot(tri, oh_bf, preferred_element_type=_F32) + run_s[...]
    rank = jnp.sum(jnp.where(onehot, before, 0.0), axis=-1, keepdims=True)
    run_new = run_s[...] + jnp.sum(jnp.where(onehot, 1.0, 0.0), axis=0, keepdims=True)
    run_s[...] = run_new
    cnt_ref[...] = run_new
    keyf = jnp.broadcast_to(cls.astype(_F32) * float(1 << RANK_BITS) + rank, (tm, LANES))
    for g in range(tm // LANES):
        kt = keyf[g * LANES:(g + 1) * LANES, :].T
        key_ref[0, g:g + 1, :] = kt[0:1, :].astype(jnp.int32)


def _mix(x2, attn2, cact2, gates2, wa_bf, wc_bf, wo_bf, b_conv_out, norm_ffn, w_router_cat, w_router_hi, b_router):
    n, d = x2.shape
    tm = TM_MIX
    aw = attn2.shape[1]
    cc = cact2.shape[1]
    row = lambda i: (i, 0)
    return pl.pallas_call(
        _mix_kernel,
        grid=(n // tm,),
        in_specs=[
            pl.BlockSpec((tm, d), row),
            pl.BlockSpec((tm, aw), row),
            pl.BlockSpec((tm, cc), row),
            pl.BlockSpec((tm, 2 * d), row),
            _const_spec(wa_bf.shape),
            _const_spec(wc_bf.shape),
            _const_spec(wo_bf.shape),
            _const_spec((1, d)),
            _const_spec((1, d)),
            _const_spec((d, 2 * LANES)),
            _const_spec((d, LANES)),
            _const_spec((1, LANES)),
        ],
        out_specs=[
            pl.BlockSpec((tm, d), row),
            pl.BlockSpec((tm * SUBLANES, LANES), row),
            pl.BlockSpec((1, tm // LANES, LANES), lambda i: (i, 0, 0)),
            _const_spec((1, LANES)),
        ],
        out_shape=[
            jax.ShapeDtypeStruct((n, d), _F32),
            jax.ShapeDtypeStruct((n * SUBLANES, LANES), _F32),
            jax.ShapeDtypeStruct((n // tm, tm // LANES, LANES), jnp.int32),
            jax.ShapeDtypeStruct((1, LANES), _F32),
        ],
        scratch_shapes=[pltpu.VMEM((1, LANES), _F32)],
        compiler_params=pltpu.CompilerParams(
            dimension_semantics=("arbitrary",), vmem_limit_bytes=VMEM_LIMIT),
        name="mix_router",
    )(x2, attn2, cact2, gates2, wa_bf, wc_bf, wo_bf, b_conv_out.reshape(1, d), norm_ffn.reshape(1, d),
      w_router_cat, w_router_hi, b_router)


def _sorted_rows_kernel(pstart_ref, key_ref, o_ref):
    key = key_ref[...]
    cls = key >> RANK_BITS
    row = key & ((1 << RANK_BITS) - 1)
    for c in range(N_CLASSES):
        row = row + jnp.where(cls == c, pstart_ref[c], 0)
    o_ref[...] = row


def _sorted_rows(pstart, keys2):
    grid_spec = pltpu.PrefetchScalarGridSpec(
        num_scalar_prefetch=1,
        grid=(1,),
        in_specs=[pl.BlockSpec(keys2.shape, lambda i, *_: (0, 0))],
        out_specs=pl.BlockSpec(keys2.shape, lambda i, *_: (0, 0)),
    )
    return pl.pallas_call(
        _sorted_rows_kernel,
        grid_spec=grid_spec,
        out_shape=jax.ShapeDtypeStruct(keys2.shape, jnp.int32),
        name="sorted_rows",
    )(pstart, keys2)


def _token_rows(idx, count=1):
    return pl.ds(pl.multiple_of(idx * SUBLANES, SUBLANES), count * SUBLANES)


def _scatter_kernel(padstart_ref, padlen_ref, nvalid_ref, dest_ref, t_ref, xs_ref, zero_s, sem, zsem):
    i = pl.program_id(0)
    ts = t_ref.shape[0] // SUBLANES
    bm = zero_s.shape[0] // SUBLANES
    nblk = xs_ref.shape[0] // (bm * SUBLANES)

    def start_rows(g, c):
        for u in range(DMA_UNROLL):
            r = g * DMA_UNROLL + u
            pltpu.make_async_copy(t_ref.at[_token_rows(r)], xs_ref.at[_token_rows(dest_ref[0, 0, r])], sem).start(
                priority=u % 2)
        return c

    def wait_rows(g, c):
        for _ in range(DMA_UNROLL):
            pltpu.make_async_copy(t_ref.at[_token_rows(0)], xs_ref.at[_token_rows(0)], sem).wait()
        return c

    lax.fori_loop(0, ts // DMA_UNROLL, start_rows, 0)

    @pl.when(i == 0)
    def _():
        zero_s[...] = jnp.zeros(zero_s.shape, zero_s.dtype)

        def pad_copies(c, act):
            ln = padlen_ref[c]
            st = padstart_ref[c]
            for k in range(bm.bit_length() - 1):
                sz = 1 << k

                @pl.when(((ln >> k) & 1) == 1)
                def _():
                    dst = st + (ln & (sz - 1))
                    act(pltpu.make_async_copy(zero_s.at[_token_rows(0, sz)], xs_ref.at[_token_rows(dst, sz)], zsem))

        def tail_copy(j, act):
            @pl.when(j >= nvalid_ref[0])
            def _():
                act(pltpu.make_async_copy(zero_s, xs_ref.at[_token_rows(j * bm, bm)], zsem))

        for act in (lambda cp: cp.start(), lambda cp: cp.wait()):
            lax.fori_loop(0, N_CLASSES, lambda c, carry, act=act: (pad_copies(c, act), carry)[1], 0)
            lax.fori_loop(0, nblk, lambda j, carry, act=act: (tail_copy(j, act), carry)[1], 0)

    lax.fori_loop(0, ts // DMA_UNROLL, wait_rows, 0)


def _scatter_rows(padstart, padlen, nvalid, dest, t_rows, n_rows):
    n = t_rows.shape[0] // SUBLANES
    ts = TS_SCATTER
    grid_spec = pltpu.PrefetchScalarGridSpec(
        num_scalar_prefetch=3,
        grid=(n // ts,),
        in_specs=[
            pl.BlockSpec((1, 1, ts), lambda i, *_: (i, 0, 0), memory_space=pltpu.SMEM),
            pl.BlockSpec((ts * SUBLANES, LANES), lambda i, *_: (i, 0)),
        ],
        out_specs=pl.BlockSpec(memory_space=pl.ANY),
        scratch_shapes=[pltpu.VMEM((BM_MOE * SUBLANES, LANES), _F32), pltpu.SemaphoreType.DMA,
                        pltpu.SemaphoreType.DMA],
    )
    return pl.pallas_call(
        _scatter_kernel,
        grid_spec=grid_spec,
        out_shape=jax.ShapeDtypeStruct((n_rows * SUBLANES, LANES), _F32),
        compiler_params=pltpu.CompilerParams(
            dimension_semantics=("arbitrary",), vmem_limit_bytes=VMEM_LIMIT),
        name="scatter_rows",
    )(padstart, padlen, nvalid, dest.reshape(n // ts, 1, ts), t_rows)


def _moe_kernel(ea_ref, eb_ref, valid_ref, xs_ref, wr_ref, wrh_ref, br_ref, w1a, w3a, w2a, w1b, w3b, w2b, y_ref):
    j = pl.program_id(0)
    bm = xs_ref.shape[0] // SUBLANES

    @pl.when(valid_ref[j] == 0)
    def _():
        y_ref[...] = jnp.zeros(y_ref.shape, y_ref.dtype)

    @pl.when(valid_ref[j] != 0)
    def _():
        x = jnp.concatenate([xs_ref[pl.ds(sg, bm, stride=SUBLANES), :] for sg in range(SUBLANES)], axis=1)
        xb = x.astype(_BF)

        lg = _router_logits(x, wr_ref, wrh_ref, br_ref)
        lane = lax.broadcasted_iota(jnp.int32, (bm, LANES), 1)
        gl = jnp.where(lane < N_GROUPS, lg, NEG_BIG)
        gmax = jnp.max(gl, axis=-1, keepdims=True)
        gsum = jnp.sum(jnp.where(lane < N_GROUPS, jnp.exp(gl - gmax), 0.0), axis=-1, keepdims=True)
        g_gate = 1.0 / gsum
        la = jnp.sum(jnp.where(lane == N_GROUPS + ea_ref[j], lg, 0.0), axis=-1, keepdims=True)
        lb = jnp.sum(jnp.where(lane == N_GROUPS + eb_ref[j], lg, 0.0), axis=-1, keepdims=True)
        w_a = g_gate * (1.0 / (1.0 + jnp.exp(lb - la)))
        w_b = g_gate * (1.0 / (1.0 + jnp.exp(la - lb)))

        def expert(w1, w3, w2):
            h1 = jnp.dot(xb, w1[0], preferred_element_type=_F32)
            h3 = jnp.dot(xb, w3[0], preferred_element_type=_F32)
            hdn = h1 * _sigmoid(h1) * h3
            return jnp.dot(hdn.astype(_BF), w2[0], preferred_element_type=_F32)

        y = w_a * expert(w1a, w3a, w2a) + w_b * expert(w1b, w3b, w2b)
        for sg in range(SUBLANES):
            y_ref[pl.ds(sg, bm, stride=SUBLANES), :] = y[:, sg * LANES:(sg + 1) * LANES]


def _moe(ea, eb, valid, xs, w_router_cat, w_router_hi, b_router, w1_bf, w3_bf, w2_bf):
    d, ff = w1_bf.shape[1], w1_bf.shape[2]
    assert d == SUBLANES * LANES
    bm = BM_MOE
    nb = xs.shape[0] // (bm * SUBLANES)
    const = lambda shape: pl.BlockSpec(shape, lambda j, *_: (0,) * len(shape))
    grid_spec = pltpu.PrefetchScalarGridSpec(
        num_scalar_prefetch=3,
        grid=(nb,),
        in_specs=[
            pl.BlockSpec((bm * SUBLANES, LANES), lambda j, ea, eb, va: (jnp.where(va[j] != 0, j, 0), 0)),
            const(w_router_cat.shape),
            const(w_router_hi.shape),
            const(b_router.shape),
            pl.BlockSpec((1, d, ff), lambda j, ea, eb, va: (ea[j], 0, 0)),
            pl.BlockSpec((1, d, ff), lambda j, ea, eb, va: (ea[j], 0, 0)),
            pl.BlockSpec((1, ff, d), lambda j, ea, eb, va: (ea[j], 0, 0)),
            pl.BlockSpec((1, d, ff), lambda j, ea, eb, va: (eb[j], 0, 0)),
            pl.BlockSpec((1, d, ff), lambda j, ea, eb, va: (eb[j], 0, 0)),
            pl.BlockSpec((1, ff, d), lambda j, ea, eb, va: (eb[j], 0, 0)),
        ],
        out_specs=pl.BlockSpec((bm * SUBLANES, LANES), lambda j, ea, eb, va: (j, 0)),
    )
    return pl.pallas_call(
        _moe_kernel,
        grid_spec=grid_spec,
        out_shape=jax.ShapeDtypeStruct(xs.shape, _F32),
        compiler_params=pltpu.CompilerParams(
            dimension_semantics=("arbitrary",), vmem_limit_bytes=VMEM_LIMIT),
        name="moe_experts",
    )(ea, eb, valid, xs, w_router_cat, w_router_hi, b_router, w1_bf, w3_bf, w2_bf, w1_bf, w3_bf, w2_bf)


def _final_kernel(key_ref, keyn_ref, h_ref, p_ref, ys_ref, np_ref, wg_ref, bg_ref, wp_ref, o_ref, y_s, sem):
    i = pl.program_id(0)
    nsteps = pl.num_programs(0)
    tm = h_ref.shape[0]

    def start_row(kref, sl, r, queue):
        pltpu.make_async_copy(ys_ref.at[_token_rows(kref[0, 0, r])], y_s.at[sl, _token_rows(r)], sem.at[sl]).start(
            priority=queue)

    def wait_slot(sl):
        def wait_rows(g, c):
            for _ in range(DMA_UNROLL):
                pltpu.make_async_copy(ys_ref.at[_token_rows(0)], y_s.at[sl, _token_rows(0)], sem.at[sl]).wait()
            return c

        lax.fori_loop(0, tm // DMA_UNROLL, wait_rows, 0)

    @pl.when(i == 0)
    def _():
        def body(g, c):
            for u in range(DMA_UNROLL):
                start_row(key_ref, 0, g * DMA_UNROLL + u, u % 2)
            return c

        lax.fori_loop(0, tm // DMA_UNROLL, body, 0)

    def step(sl):
        wait_slot(sl)
        for r in range(tm):
            start_row(keyn_ref, 1 - sl, r, r % 2)
        proj = jnp.dot(p_ref[...].astype(_BF), wp_ref[...], preferred_element_type=_F32)
        y = jnp.concatenate([y_s[sl, pl.ds(sg, tm, stride=SUBLANES), :] for sg in range(SUBLANES)], axis=1)
        h = h_ref[...] + y
        u = h * lax.rsqrt(jnp.mean(h * h, axis=-1, keepdims=True) + EPS) * np_ref[...]
        gate = _sigmoid(jnp.dot(u.astype(_BF), wg_ref[...], preferred_element_type=_F32) + bg_ref[...])
        o_ref[...] = h + gate * proj

        @pl.when(i == nsteps - 1)
        def _():
            wait_slot(1 - sl)

    for sl in range(2):
        @pl.when(i % 2 == sl)
        def _(sl=sl):
            step(sl)


def _final(dest, h1, p2, ys, norm_ple, wg_bf, b_ple_gate, wp_bf):
    n, d = h1.shape
    assert d == SUBLANES * LANES
    tm = TM_FINAL
    nsteps = n // tm
    pd = p2.shape[1]
    row = lambda i: (i, 0)
    dest3 = dest.reshape(nsteps, 1, tm)
    return pl.pallas_call(
        _final_kernel,
        grid=(nsteps,),
        in_specs=[
            pl.BlockSpec((1, 1, tm), lambda i: (i, 0, 0), memory_space=pltpu.SMEM),
            pl.BlockSpec((1, 1, tm), lambda i: (jnp.minimum(i + 1, nsteps - 1), 0, 0), memory_space=pltpu.SMEM),
            pl.BlockSpec((tm, d), row),
            pl.BlockSpec((tm, pd), row),
            pl.BlockSpec(memory_space=pl.ANY),
            _const_spec((1, d)),
            _const_spec(wg_bf.shape),
            _const_spec((1, d)),
            _const_spec(wp_bf.shape),
        ],
        out_specs=pl.BlockSpec((tm, d), row),
        out_shape=jax.ShapeDtypeStruct((n, d), _F32),
        scratch_shapes=[pltpu.VMEM((2, tm * SUBLANES, LANES), _F32), pltpu.SemaphoreType.DMA((2,))],
        compiler_params=pltpu.CompilerParams(
            dimension_semantics=("arbitrary",), vmem_limit_bytes=VMEM_LIMIT),
        name="final_ple",
    )(dest3, dest3, h1, p2, ys, norm_ple.reshape(1, d), wg_bf, b_ple_gate.reshape(1, d), wp_bf)


def _class_tables():
    ea, eb = [], []
    for g in range(N_GROUPS):
        for lo in range(EXPERTS_PER_GROUP):
            for hi in range(lo + 1, EXPERTS_PER_GROUP):
                ea.append(g * EXPERTS_PER_GROUP + lo)
                eb.append(g * EXPERTS_PER_GROUP + hi)
    return np.asarray(ea, np.int32), np.asarray(eb, np.int32)


_CLASS_EA, _CLASS_EB = _class_tables()


def kernel(x, p, norm_mix, w_in, b_conv_in, b_gate, q_norm, k_norm, lambda_q1, lambda_k1, lambda_q2, lambda_k2, subln, w_attn_out, conv_w, conv_b, conv_ln_g, conv_ln_b, w_conv_out, b_conv_out, w_o, norm_ffn, w_router_group, b_router_group, w_router_expert, b_router_expert, w1, w3, w2, norm_ple, w_ple_gate, b_ple_gate, w_ple_proj):
    b, s, d = x.shape
    n = b * s
    depth = w_in.shape[0]
    qk = ATTN_HEADS * 2 * HEAD_DIM
    aw = ATTN_HEADS * V_DIM
    cc = conv_w.shape[-1]
    assert depth == 1 and n % TM_MIX == 0 and s % (2 * ATTN_TK) == 0 and s % TM_INPROJ == 0
    assert n % TS_SCATTER == 0 and n % TM_FINAL == 0 and n % BM_MOE == 0
    i = 0
    h = x.reshape(n, d)

    conv_w_pad = jnp.pad(conv_w[i], ((0, CONV_HALO - CONV_WIDTH), (0, 0)))
    q2, k2, v2, cact2, gates2 = _inproj(h, norm_mix[i], w_in[i].astype(_BF), b_conv_in[i], b_gate[i], conv_w_pad,
                                        conv_b[i], conv_ln_g[i], conv_ln_b[i], qk, aw, cc, s)

    lam_vecs = jnp.stack([lambda_q1[i], lambda_k1[i], lambda_q2[i], lambda_k2[i]]).astype(_F32)
    attn = _attention(lam_vecs, q2.reshape(b, s, qk), k2.reshape(b, s, qk), v2.reshape(b, s, aw),
                      q_norm[i], k_norm[i], subln[i])

    w_router = jnp.concatenate([w_router_group[i], w_router_expert[i]], axis=1)
    w_router = jnp.pad(w_router, ((0, 0), (0, LANES - w_router.shape[1])))
    b_router = jnp.concatenate([b_router_group[i], b_router_expert[i]])
    b_router = jnp.pad(b_router, (0, LANES - b_router.shape[0])).reshape(1, LANES)
    w_router_hi = w_router.astype(_BF)
    w_router_lo = (w_router - w_router_hi.astype(_F32)).astype(_BF)
    w_router_cat = jnp.concatenate([w_router_hi, w_router_lo], axis=1)
    h1, t_rows, keys, cnt = _mix(h, attn.reshape(n, aw), cact2, gates2, w_attn_out[i].astype(_BF),
                                 w_conv_out[i].astype(_BF), w_o[i].astype(_BF), b_conv_out[i], norm_ffn[i],
                                 w_router_cat, w_router_hi, b_router)

    bm = BM_MOE
    nb = n // bm + N_CLASSES
    counts = cnt[0].astype(jnp.int32)
    padded = ((counts + bm - 1) // bm) * bm
    pend = jnp.cumsum(padded)
    pstart = (pend - padded).astype(jnp.int32)
    total = pend[-1]
    blk = jnp.arange(nb, dtype=jnp.int32) * bm
    valid = blk < total
    last_cls = jnp.searchsorted(pend, total - 1, side='right')
    bcls = jnp.where(valid, jnp.searchsorted(pend, blk, side='right'), last_cls)
    bcls = jnp.minimum(bcls, N_CLASSES - 1).astype(jnp.int32)
    ea = jnp.asarray(_CLASS_EA)[bcls]
    eb = jnp.asarray(_CLASS_EB)[bcls]
    nvalid = (total // bm).astype(jnp.int32).reshape(1)

    dest = _sorted_rows(pstart, keys.reshape(n // LANES, LANES))
    xs = _scatter_rows((pstart + counts).astype(jnp.int32), (padded - counts).astype(jnp.int32), nvalid,
                       dest, t_rows, nb * bm)
    ys = _moe(ea, eb, valid.astype(jnp.int32), xs, w_router_cat, w_router_hi, b_router,
              w1[i].astype(_BF), w3[i].astype(_BF), w2[i].astype(_BF))

    out = _final(dest, h1, p[i].reshape(n, -1), ys, norm_ple[i], w_ple_gate[i].astype(_BF), b_ple_gate[i],
                 w_ple_proj[i].astype(_BF))
    return out.reshape(b, s, d)
```

```python
import functools
import math

import numpy as np
import jax
import jax.numpy as jnp
from jax import lax
from jax.experimental import pallas as pl
from jax.experimental.pallas import tpu as pltpu

ATTN_HEADS = 4
HEAD_DIM = 64
V_DIM = 2 * HEAD_DIM
CONV_WIDTH = 31
N_GROUPS = 4
EXPERTS_PER_GROUP = 8
N_EXPERTS = N_GROUPS * EXPERTS_PER_GROUP
PAIRS_PER_GROUP = EXPERTS_PER_GROUP * (EXPERTS_PER_GROUP - 1) // 2
N_CLASSES = N_GROUPS * PAIRS_PER_GROUP
EPS = 1e-6
LAMBDA_INIT = 0.8 - 0.6 * math.exp(-0.3 * 0)

LANES = 128
SUBLANES = 8
CONV_HALO = 32
NEG_BIG = -1e30
RANK_BITS = 17
DMA_UNROLL = 8

TM_INPROJ = 512
ATTN_TK = 512
TM_MIX = 512
CONV_ROWS = 64
TS_SCATTER = 512
BM_MOE = 256
TM_FINAL = 512
VMEM_LIMIT = 56 * 1024 * 1024

_BF = jnp.bfloat16
_F32 = jnp.float32


def _const_spec(shape):
    nd = len(shape)
    return pl.BlockSpec(shape, lambda *_: (0,) * nd)


def _sigmoid(x):
    return 1.0 / (1.0 + jnp.exp(-x))


def _inproj_kernel(x_ref, nm_ref, w_ref, bci_ref, bg_ref, cw_ref, cb_ref, lng_ref, lnb_ref,
                   q_ref, k_ref, v_ref, c_ref, g_ref, ext_s, xsh_s, cv_s, *, tiles_per_seq):
    i = pl.program_id(0)
    tm = x_ref.shape[0]
    qk = q_ref.shape[1]
    aw = v_ref.shape[1]
    cc = c_ref.shape[1]
    ext_rows = CONV_HALO + tm
    o0, o1, o2, o3 = qk, 2 * qk, 2 * qk + aw, 2 * qk + aw + 2 * cc

    @pl.when(i == 0)
    def _():
        ext_s[ext_rows:ext_rows + SUBLANES, :] = jnp.zeros((SUBLANES, cc), _F32)

    @pl.when((i % tiles_per_seq) == 0)
    def _():
        ext_s[0:CONV_HALO, :] = jnp.zeros((CONV_HALO, cc), _F32)

    x = x_ref[...]
    u = x * lax.rsqrt(jnp.mean(x * x, axis=-1, keepdims=True) + EPS) * nm_ref[...]
    ub = u.astype(_BF)
    c = jnp.dot(ub, w_ref[:, o2:o3], preferred_element_type=_F32) + bci_ref[...]
    ext_s[CONV_HALO:ext_rows, :] = c[:, :cc] * _sigmoid(c[:, cc:])
    q_ref[...] = jnp.dot(ub, w_ref[:, 0:o0], preferred_element_type=_F32).astype(_BF)
    k_ref[...] = jnp.dot(ub, w_ref[:, o0:o1], preferred_element_type=_F32).astype(_BF)
    v_ref[...] = jnp.dot(ub, w_ref[:, o1:o2], preferred_element_type=_F32).astype(_BF)
    g = jnp.dot(ub, w_ref[:, o3:], preferred_element_type=_F32) + bg_ref[...]
    g_ref[...] = _sigmoid(g).astype(_BF)
    off = CONV_HALO - (CONV_WIDTH - 1)

    for rho in range(1, SUBLANES):
        xsh_s[rho - 1] = ext_s[rho:rho + ext_rows, :]

    for r0 in range(0, tm, CONV_ROWS):
        acc = jnp.zeros((CONV_ROWS, cc), _F32)
        for j in range(CONV_WIDTH):
            a, rho = divmod(off + j, SUBLANES)
            lo_row = r0 + a * SUBLANES
            if rho == 0:
                xv = ext_s[lo_row:lo_row + CONV_ROWS, :]
            else:
                xv = xsh_s[rho - 1, lo_row:lo_row + CONV_ROWS, :]
            acc = acc + cw_ref[j:j + 1, :] * xv
        cv_s[r0:r0 + CONV_ROWS, :] = acc

    ext_s[0:CONV_HALO, :] = ext_s[tm:ext_rows, :]

    cv = cv_s[...] + cb_ref[...]
    mu = jnp.mean(cv, axis=-1, keepdims=True)
    xc = cv - mu
    var = jnp.mean(xc * xc, axis=-1, keepdims=True)
    ln = xc * lax.rsqrt(var + EPS) * lng_ref[...] + lnb_ref[...]
    c_ref[...] = (ln * _sigmoid(ln)).astype(_BF)


def _inproj(x2, norm_mix, w_in_bf, b_conv_in, b_gate, conv_w, conv_b, ln_g, ln_b, qk, aw, cc, seq_len):
    n, d = x2.shape
    tm = TM_INPROJ
    ng = b_gate.shape[-1]
    kern = functools.partial(_inproj_kernel, tiles_per_seq=seq_len // tm)
    return pl.pallas_call(
        kern,
        grid=(n // tm,),
        in_specs=[
            pl.BlockSpec((tm, d), lambda i: (i, 0)),
            _const_spec((1, d)),
            pl.BlockSpec(w_in_bf.shape, lambda i: (0, 0), pipeline_mode=pl.Buffered(1)),
            _const_spec((1, 2 * cc)),
            _const_spec((1, ng)),
            _const_spec((CONV_HALO, cc)),
            _const_spec((1, cc)),
            _const_spec((1, cc)),
            _const_spec((1, cc)),
        ],
        out_specs=[
            pl.BlockSpec((tm, qk), lambda i: (i, 0)),
            pl.BlockSpec((tm, qk), lambda i: (i, 0)),
            pl.BlockSpec((tm, aw), lambda i: (i, 0)),
            pl.BlockSpec((tm, cc), lambda i: (i, 0)),
            pl.BlockSpec((tm, ng), lambda i: (i, 0)),
        ],
        out_shape=[
            jax.ShapeDtypeStruct((n, qk), _BF),
            jax.ShapeDtypeStruct((n, qk), _BF),
            jax.ShapeDtypeStruct((n, aw), _BF),
            jax.ShapeDtypeStruct((n, cc), _BF),
            jax.ShapeDtypeStruct((n, ng), _BF),
        ],
        scratch_shapes=[
            pltpu.VMEM((CONV_HALO + tm + SUBLANES, cc), _F32),
            pltpu.VMEM((SUBLANES - 1, CONV_HALO + tm, cc), _F32),
            pltpu.VMEM((tm, cc), _F32),
        ],
        compiler_params=pltpu.CompilerParams(
            dimension_semantics=("arbitrary",), vmem_limit_bytes=VMEM_LIMIT),
        name="inproj",
    )(x2, norm_mix.reshape(1, d), w_in_bf, b_conv_in.reshape(1, -1), b_gate.reshape(1, -1), conv_w,
      conv_b.reshape(1, cc), ln_g.reshape(1, cc), ln_b.reshape(1, cc))


def _half_rmsnorm(x, gain_row, lo_mask):
    x2 = x * x
    s_lo = jnp.sum(jnp.where(lo_mask, x2, 0.0), axis=-1, keepdims=True)
    s_hi = jnp.sum(jnp.where(lo_mask, 0.0, x2), axis=-1, keepdims=True)
    r = jnp.where(lo_mask, lax.rsqrt(s_lo * (1.0 / HEAD_DIM) + EPS), lax.rsqrt(s_hi * (1.0 / HEAD_DIM) + EPS))
    return x * r * gain_row


def _attn_kernel(lam_ref, q_ref, k_ref, v_ref, qn_ref, kn_ref, sub_ref, o_ref, kn_s, va_s, qs_s, m_s, acc_s,
                 sa_s, sb_s, sc_s):
    s_len = q_ref.shape[1]
    tk = ATTN_TK
    tq = 2 * tk
    nq = s_len // tq
    lane = lax.broadcasted_iota(jnp.int32, (1, V_DIM), 1)
    lo_mask = lane < HEAD_DIM

    lv = lam_ref[...]
    d1 = jnp.sum(lv[0:1, :] * lv[1:2, :], axis=-1, keepdims=True)
    d2 = jnp.sum(lv[2:3, :] * lv[3:4, :], axis=-1, keepdims=True)
    lam = jnp.exp(d1) - jnp.exp(d2) + LAMBDA_INIT

    kgain = kn_ref[...]
    qgain = qn_ref[...] * (math.log2(math.e) / math.sqrt(HEAD_DIM))

    def prep_kv(i, carry):
        r0 = pl.multiple_of(i * tq, tq)
        kk = k_ref[0, pl.ds(r0, tq), :].astype(_F32)
        kn_s[pl.ds(r0, tq), :] = _half_rmsnorm(kk, kgain, lo_mask).astype(_BF)
        va_s[pl.ds(r0, tq), 0:V_DIM] = v_ref[0, pl.ds(r0, tq), :]
        va_s[pl.ds(r0, tq), V_DIM:2 * V_DIM] = jnp.ones((tq, V_DIM), _BF)
        return carry

    lax.fori_loop(0, nq, prep_kv, 0)

    def scores(c0, ra, nr):
        kc = kn_s[pl.ds(c0, tk), :]
        return lax.dot_general(qs_s[ra:ra + nr, :], kc, (((1,), (1,)), ((), ())), preferred_element_type=_F32)

    def consume(s, c0, ra, nr, n_masked):
        vc = va_s[pl.ds(c0, tk), :]
        if n_masked:
            keep = (lax.broadcasted_iota(jnp.int32, (tk, tk), 1) <= lax.broadcasted_iota(jnp.int32, (tk, tk), 0))
            groups = [s[g * tk:(g + 1) * tk, :] for g in range(nr // tk)]
            groups = [jnp.where(keep, sg, NEG_BIG) if g < n_masked else sg for g, sg in enumerate(groups)]
            s = jnp.concatenate(groups, axis=0)
        m_old = m_s[ra:ra + nr, :]
        m_new = jnp.maximum(m_old, jnp.max(s, axis=-1, keepdims=True))
        alpha = jnp.exp2(m_old - m_new)
        p = jnp.concatenate(
            [jnp.exp2(s[:, c * LANES:(c + 1) * LANES] - m_new) for c in range(tk // LANES)], axis=1)
        pv = jnp.dot(p.astype(_BF), vc, preferred_element_type=_F32)
        acc_s[ra:ra + nr, :] = jnp.concatenate([alpha, alpha], axis=1) * acc_s[ra:ra + nr, :] + pv
        m_s[ra:ra + nr, :] = m_new

    def prep_q(r0):
        for hf in range(2):
            qq = _half_rmsnorm(q_ref[0, pl.ds(r0 + hf * tk, tk), :].astype(_F32), qgain, lo_mask)
            qs_s[(2 * hf) * tk:(2 * hf + 1) * tk, :] = jnp.where(lo_mask, qq, 0.0).astype(_BF)
            qs_s[(2 * hf + 1) * tk:(2 * hf + 2) * tk, :] = jnp.where(lo_mask, 0.0, qq).astype(_BF)

    bufs = (sa_s, sb_s, sc_s)
    prep_q(0)
    sa_s[...] = scores(0, 0, 4 * tk)

    for qi in range(nq):
        r0 = qi * tq
        cur = bufs[(2 * qi) % 3]
        nxt = bufs[(2 * qi + 1) % 3]
        ahead = bufs[(2 * qi + 2) % 3]
        m_s[...] = jnp.full(m_s.shape, NEG_BIG, _F32)
        acc_s[...] = jnp.zeros(acc_s.shape, _F32)

        def two_chunks(t, c, cur=cur, nxt=nxt):
            c0 = pl.multiple_of(2 * t * tk, tk)
            c1 = pl.multiple_of(c0 + tk, tk)
            c2 = pl.multiple_of(c0 + 2 * tk, tk)
            nxt[...] = scores(c1, 0, 4 * tk)
            consume(cur[...], c0, 0, 4 * tk, 0)
            cur[...] = scores(c2, 0, 4 * tk)
            consume(nxt[...], c1, 0, 4 * tk, 0)
            return c

        lax.fori_loop(0, qi, two_chunks, 0)
        nxt[0:2 * tk, :] = scores(r0 + tk, 2 * tk, 2 * tk)
        if qi + 1 < nq:
            prep_q(r0 + tq)
            ahead[...] = scores(0, 0, 4 * tk)
        consume(cur[...], r0, 0, 4 * tk, 2)
        consume(nxt[0:2 * tk, :], r0 + tk, 2 * tk, 2 * tk, 2)

        acc = acc_s[...]
        o = acc[:, 0:V_DIM] / acc[:, V_DIM:2 * V_DIM]
        for hf in range(2):
            a = o[(2 * hf) * tk:(2 * hf + 1) * tk, :] - lam * o[(2 * hf + 1) * tk:(2 * hf + 2) * tk, :]
            a = a * lax.rsqrt(jnp.mean(a * a, axis=-1, keepdims=True) + EPS) * sub_ref[...] * (1.0 - LAMBDA_INIT)
            o_ref[0, r0 + hf * tk:r0 + (hf + 1) * tk, :] = a.astype(_BF)


def _attention(lam_vecs, q3, k3, v3, q_norm, k_norm, subln):
    b, s, _ = q3.shape
    tq = 2 * ATTN_TK
    slab = lambda bi, hi: (bi, 0, hi)
    return pl.pallas_call(
        _attn_kernel,
        grid=(b, ATTN_HEADS),
        in_specs=[
            _const_spec((4, HEAD_DIM)),
            pl.BlockSpec((1, s, V_DIM), slab),
            pl.BlockSpec((1, s, V_DIM), slab),
            pl.BlockSpec((1, s, V_DIM), slab),
            _const_spec((1, V_DIM)),
            _const_spec((1, V_DIM)),
            _const_spec((1, V_DIM)),
        ],
        out_specs=pl.BlockSpec((1, s, V_DIM), slab),
        out_shape=jax.ShapeDtypeStruct((b, s, ATTN_HEADS * V_DIM), _BF),
        scratch_shapes=[
            pltpu.VMEM((s, V_DIM), _BF),
            pltpu.VMEM((s, 2 * V_DIM), _BF),
            pltpu.VMEM((2 * tq, V_DIM), _BF),
            pltpu.VMEM((2 * tq, V_DIM), _F32),
            pltpu.VMEM((2 * tq, 2 * V_DIM), _F32),
            pltpu.VMEM((2 * tq, ATTN_TK), _F32),
            pltpu.VMEM((2 * tq, ATTN_TK), _F32),
            pltpu.VMEM((2 * tq, ATTN_TK), _F32),
        ],
        compiler_params=pltpu.CompilerParams(
            dimension_semantics=("arbitrary", "arbitrary"), vmem_limit_bytes=VMEM_LIMIT),
        name="diffattn",
    )(lam_vecs, q3, k3, v3, q_norm.reshape(1, V_DIM), k_norm.reshape(1, V_DIM), subln.reshape(1, V_DIM))


def _router_logits(t, wr_ref, wrh_ref, br_ref):
    t_hi = t.astype(_BF)
    t_lo = (t - t_hi.astype(_F32)).astype(_BF)
    l_hi = jnp.dot(t_hi, wr_ref[...], preferred_element_type=_F32)
    l_lo = jnp.dot(t_lo, wrh_ref[...], preferred_element_type=_F32)
    return l_hi[:, 0:LANES] + l_hi[:, LANES:2 * LANES] + l_lo + br_ref[...]


def _mix_kernel(x_ref, a_ref, c_ref, g_ref, wa_ref, wc_ref, wo_ref, bco_ref, nf_ref, wr_ref, wrh_ref, br_ref,
                h_ref, t_ref, key_ref, cnt_ref, run_s):
    i = pl.program_id(0)
    tm, d = x_ref.shape

    @pl.when(i == 0)
    def _():
        run_s[...] = jnp.zeros(run_s.shape, _F32)

    y_b = jnp.dot(c_ref[...], wc_ref[...], preferred_element_type=_F32) + bco_ref[...]
    y_a = jnp.dot(a_ref[...], wa_ref[...], preferred_element_type=_F32)

    merged = g_ref[:, 0:d].astype(_F32) * y_a + g_ref[:, d:2 * d].astype(_F32) * y_b
    h = x_ref[...] + jnp.dot(merged.astype(_BF), wo_ref[...], preferred_element_type=_F32)
    h_ref[...] = h

    t = h * lax.rsqrt(jnp.mean(h * h, axis=-1, keepdims=True) + EPS) * nf_ref[...]
    for sg in range(d // LANES):
        t_ref[pl.ds(sg, tm, stride=SUBLANES), :] = t[:, sg * LANES:(sg + 1) * LANES]

    lg = _router_logits(t, wr_ref, wrh_ref, br_ref)
    lane = lax.broadcasted_iota(jnp.int32, (tm, LANES), 1)
    gl = jnp.where(lane < N_GROUPS, lg, NEG_BIG)
    gmax = jnp.max(gl, axis=-1, keepdims=True)
    gidx = jnp.min(jnp.where(gl == gmax, lane, LANES), axis=-1, keepdims=True)
    e_lo = N_GROUPS + EXPERTS_PER_GROUP * gidx
    el = jnp.where((lane >= e_lo) & (lane < e_lo + EXPERTS_PER_GROUP), lg, NEG_BIG)
    m1 = jnp.max(el, axis=-1, keepdims=True)
    i1 = jnp.min(jnp.where(el == m1, lane, LANES), axis=-1, keepdims=True)
    el2 = jnp.where(lane == i1, NEG_BIG, el)
    m2 = jnp.max(el2, axis=-1, keepdims=True)
    i2 = jnp.min(jnp.where(el2 == m2, lane, LANES), axis=-1, keepdims=True)
    a1 = i1 - e_lo
    a2 = i2 - e_lo
    lo = jnp.minimum(a1, a2)
    hi = jnp.maximum(a1, a2)
    pidx = ((lo * (2 * EXPERTS_PER_GROUP - 1 - lo)) >> 1) + (hi - lo - 1)
    cls = gidx * PAIRS_PER_GROUP + pidx

    onehot = lane == cls
    oh_bf = jnp.where(onehot, 1.0, 0.0).astype(_BF)
    rr = lax.broadcasted_iota(jnp.int32, (tm, tm), 0)
    ccol = lax.broadcasted_iota(jnp.int32, (tm, tm), 1)
    tri = jnp.where(ccol < rr, 1.0, 0.0).astype(_BF)
    before = jnp.dot(tri, oh_bf, preferred_element_type=_F32) + run_s[...]
    rank = jnp.sum(jnp.where(onehot, before, 0.0), axis=-1, keepdims=True)
    run_new = run_s[...] + jnp.sum(jnp.where(onehot, 1.0, 0.0), axis=0, keepdims=True)
    run_s[...] = run_new
    cnt_ref[...] = run_new
    keyf = jnp.broadcast_to(cls.astype(_F32) * float(1 << RANK_BITS) + rank, (tm, LANES))
    for g in range(tm // LANES):
        kt = keyf[g * LANES:(g + 1) * LANES, :].T
        key_ref[0, g:g + 1, :] = kt[0:1, :].astype(jnp.int32)


def _mix(x2, attn2, cact2, gates2, wa_bf, wc_bf, wo_bf, b_conv_out, norm_ffn, w_router_cat, w_router_hi, b_router):
    n, d = x2.shape
    tm = TM_MIX
    aw = attn2.shape[1]
    cc = cact2.shape[1]
    row = lambda i: (i, 0)
    return pl.pallas_call(
        _mix_kernel,
        grid=(n // tm,),
        in_specs=[
            pl.BlockSpec((tm, d), row),
            pl.BlockSpec((tm, aw), row),
            pl.BlockSpec((tm, cc), row),
            pl.BlockSpec((tm, 2 * d), row),
            _const_spec(wa_bf.shape),
            _const_spec(wc_bf.shape),
            _const_spec(wo_bf.shape),
            _const_spec((1, d)),
            _const_spec((1, d)),
            _const_spec((d, 2 * LANES)),
            _const_spec((d, LANES)),
            _const_spec((1, LANES)),
        ],
        out_specs=[
            pl.BlockSpec((tm, d), row),
            pl.BlockSpec((tm * SUBLANES, LANES), row),
            pl.BlockSpec((1, tm // LANES, LANES), lambda i: (i, 0, 0)),
            _const_spec((1, LANES)),
        ],
        out_shape=[
            jax.ShapeDtypeStruct((n, d), _F32),
            jax.ShapeDtypeStruct((n * SUBLANES, LANES), _F32),
            jax.ShapeDtypeStruct((n // tm, tm // LANES, LANES), jnp.int32),
            jax.ShapeDtypeStruct((1, LANES), _F32),
        ],
        scratch_shapes=[pltpu.VMEM((1, LANES), _F32)],
        compiler_params=pltpu.CompilerParams(
            dimension_semantics=("arbitrary",), vmem_limit_bytes=VMEM_LIMIT),
        name="mix_router",
    )(x2, attn2, cact2, gates2, wa_bf, wc_bf, wo_bf, b_conv_out.reshape(1, d), norm_ffn.reshape(1, d),
      w_router_cat, w_router_hi, b_router)


def _sorted_rows_kernel(pstart_ref, key_ref, o_ref):
    key = key_ref[...]
    cls = key >> RANK_BITS
    row = key & ((1 << RANK_BITS) - 1)
    for c in range(N_CLASSES):
        row = row + jnp.where(cls == c, pstart_ref[c], 0)
    o_ref[...] = row


def _sorted_rows(pstart, keys2):
    grid_spec = pltpu.PrefetchScalarGridSpec(
        num_scalar_prefetch=1,
        grid=(1,),
        in_specs=[pl.BlockSpec(keys2.shape, lambda i, *_: (0, 0))],
        out_specs=pl.BlockSpec(keys2.shape, lambda i, *_: (0, 0)),
    )
    return pl.pallas_call(
        _sorted_rows_kernel,
        grid_spec=grid_spec,
        out_shape=jax.ShapeDtypeStruct(keys2.shape, jnp.int32),
        name="sorted_rows",
    )(pstart, keys2)


def _token_rows(idx, count=1):
    return pl.ds(pl.multiple_of(idx * SUBLANES, SUBLANES), count * SUBLANES)


def _scatter_kernel(padstart_ref, padlen_ref, nvalid_ref, dest_ref, t_ref, xs_ref, zero_s, sem, zsem):
    i = pl.program_id(0)
    ts = t_ref.shape[0] // SUBLANES
    bm = zero_s.shape[0] // SUBLANES
    nblk = xs_ref.shape[0] // (bm * SUBLANES)

    def start_rows(g, c):
        for u in range(DMA_UNROLL):
            r = g * DMA_UNROLL + u
            pltpu.make_async_copy(t_ref.at[_token_rows(r)], xs_ref.at[_token_rows(dest_ref[0, 0, r])], sem).start(
                priority=u % 2)
        return c

    def wait_rows(g, c):
        for _ in range(DMA_UNROLL):
            pltpu.make_async_copy(t_ref.at[_token_rows(0)], xs_ref.at[_token_rows(0)], sem).wait()
        return c

    lax.fori_loop(0, ts // DMA_UNROLL, start_rows, 0)

    @pl.when(i == 0)
    def _():
        zero_s[...] = jnp.zeros(zero_s.shape, zero_s.dtype)

        def pad_copies(c, act):
            ln = padlen_ref[c]
            st = padstart_ref[c]
            for k in range(bm.bit_length() - 1):
                sz = 1 << k

                @pl.when(((ln >> k) & 1) == 1)
                def _():
                    dst = st + (ln & (sz - 1))
                    act(pltpu.make_async_copy(zero_s.at[_token_rows(0, sz)], xs_ref.at[_token_rows(dst, sz)], zsem))

        def tail_copy(j, act):
            @pl.when(j >= nvalid_ref[0])
            def _():
                act(pltpu.make_async_copy(zero_s, xs_ref.at[_token_rows(j * bm, bm)], zsem))

        for act in (lambda cp: cp.start(), lambda cp: cp.wait()):
            lax.fori_loop(0, N_CLASSES, lambda c, carry, act=act: (pad_copies(c, act), carry)[1], 0)
            lax.fori_loop(0, nblk, lambda j, carry, act=act: (tail_copy(j, act), carry)[1], 0)

    lax.fori_loop(0, ts // DMA_UNROLL, wait_rows, 0)


def _scatter_rows(padstart, padlen, nvalid, dest, t_rows, n_rows):
    n = t_rows.shape[0] // SUBLANES
    ts = TS_SCATTER
    grid_spec = pltpu.PrefetchScalarGridSpec(
        num_scalar_prefetch=3,
        grid=(n // ts,),
        in_specs=[
            pl.BlockSpec((1, 1, ts), lambda i, *_: (i, 0, 0), memory_space=pltpu.SMEM),
            pl.BlockSpec((ts * SUBLANES, LANES), lambda i, *_: (i, 0)),
        ],
        out_specs=pl.BlockSpec(memory_space=pl.ANY),
        scratch_shapes=[pltpu.VMEM((BM_MOE * SUBLANES, LANES), _F32), pltpu.SemaphoreType.DMA,
                        pltpu.SemaphoreType.DMA],
    )
    return pl.pallas_call(
        _scatter_kernel,
        grid_spec=grid_spec,
        out_shape=jax.ShapeDtypeStruct((n_rows * SUBLANES, LANES), _F32),
        compiler_params=pltpu.CompilerParams(
            dimension_semantics=("arbitrary",), vmem_limit_bytes=VMEM_LIMIT),
        name="scatter_rows",
    )(padstart, padlen, nvalid, dest.reshape(n // ts, 1, ts), t_rows)


def _moe_kernel(ea_ref, eb_ref, valid_ref, xs_ref, wr_ref, wrh_ref, br_ref, w1a, w3a, w2a, w1b, w3b, w2b, y_ref):
    j = pl.program_id(0)
    bm = xs_ref.shape[0] // SUBLANES

    @pl.when(valid_ref[j] == 0)
    def _():
        y_ref[...] = jnp.zeros(y_ref.shape, y_ref.dtype)

    @pl.when(valid_ref[j] != 0)
    def _():
        x = jnp.concatenate([xs_ref[pl.ds(sg, bm, stride=SUBLANES), :] for sg in range(SUBLANES)], axis=1)
        xb = x.astype(_BF)

        lg = _router_logits(x, wr_ref, wrh_ref, br_ref)
        lane = lax.broadcasted_iota(jnp.int32, (bm, LANES), 1)
        gl = jnp.where(lane < N_GROUPS, lg, NEG_BIG)
        gmax = jnp.max(gl, axis=-1, keepdims=True)
        gsum = jnp.sum(jnp.where(lane < N_GROUPS, jnp.exp(gl - gmax), 0.0), axis=-1, keepdims=True)
        g_gate = 1.0 / gsum
        la = jnp.sum(jnp.where(lane == N_GROUPS + ea_ref[j], lg, 0.0), axis=-1, keepdims=True)
        lb = jnp.sum(jnp.where(lane == N_GROUPS + eb_ref[j], lg, 0.0), axis=-1, keepdims=True)
        w_a = g_gate * (1.0 / (1.0 + jnp.exp(lb - la)))
        w_b = g_gate * (1.0 / (1.0 + jnp.exp(la - lb)))

        def expert(w1, w3, w2):
            h1 = jnp.dot(xb, w1[0], preferred_element_type=_F32)
            h3 = jnp.dot(xb, w3[0], preferred_element_type=_F32)
            hdn = h1 * _sigmoid(h1) * h3
            return jnp.dot(hdn.astype(_BF), w2[0], preferred_element_type=_F32)

        y = w_a * expert(w1a, w3a, w2a) + w_b * expert(w1b, w3b, w2b)
        for sg in range(SUBLANES):
            y_ref[pl.ds(sg, bm, stride=SUBLANES), :] = y[:, sg * LANES:(sg + 1) * LANES]


def _moe(ea, eb, valid, xs, w_router_cat, w_router_hi, b_router, w1_bf, w3_bf, w2_bf):
    d, ff = w1_bf.shape[1], w1_bf.shape[2]
    assert d == SUBLANES * LANES
    bm = BM_MOE
    nb = xs.shape[0] // (bm * SUBLANES)
    const = lambda shape: pl.BlockSpec(shape, lambda j, *_: (0,) * len(shape))
    grid_spec = pltpu.PrefetchScalarGridSpec(
        num_scalar_prefetch=3,
        grid=(nb,),
        in_specs=[
            pl.BlockSpec((bm * SUBLANES, LANES), lambda j, ea, eb, va: (jnp.where(va[j] != 0, j, 0), 0)),
            const(w_router_cat.shape),
            const(w_router_hi.shape),
            const(b_router.shape),
            pl.BlockSpec((1, d, ff), lambda j, ea, eb, va: (ea[j], 0, 0)),
            pl.BlockSpec((1, d, ff), lambda j, ea, eb, va: (ea[j], 0, 0)),
            pl.BlockSpec((1, ff, d), lambda j, ea, eb, va: (ea[j], 0, 0)),
            pl.BlockSpec((1, d, ff), lambda j, ea, eb, va: (eb[j], 0, 0)),
            pl.BlockSpec((1, d, ff), lambda j, ea, eb, va: (eb[j], 0, 0)),
            pl.BlockSpec((1, ff, d), lambda j, ea, eb, va: (eb[j], 0, 0)),
        ],
        out_specs=pl.BlockSpec((bm * SUBLANES, LANES), lambda j, ea, eb, va: (j, 0)),
    )
    return pl.pallas_call(
        _moe_kernel,
        grid_spec=grid_spec,
        out_shape=jax.ShapeDtypeStruct(xs.shape, _F32),
        compiler_params=pltpu.CompilerParams(
            dimension_semantics=("arbitrary",), vmem_limit_bytes=VMEM_LIMIT),
        name="moe_experts",
    )(ea, eb, valid, xs, w_router_cat, w_router_hi, b_router, w1_bf, w3_bf, w2_bf, w1_bf, w3_bf, w2_bf)


def _final_kernel(key_ref, keyn_ref, h_ref, p_ref, ys_ref, np_ref, wg_ref, bg_ref, wp_ref, o_ref, y_s, sem):
    i = pl.program_id(0)
    nsteps = pl.num_programs(0)
    tm = h_ref.shape[0]

    def start_row(kref, sl, r, queue):
        pltpu.make_async_copy(ys_ref.at[_token_rows(kref[0, 0, r])], y_s.at[sl, _token_rows(r)], sem.at[sl]).start(
            priority=queue)

    def wait_slot(sl):
        def wait_rows(g, c):
            for _ in range(DMA_UNROLL):
                pltpu.make_async_copy(ys_ref.at[_token_rows(0)], y_s.at[sl, _token_rows(0)], sem.at[sl]).wait()
            return c

        lax.fori_loop(0, tm // DMA_UNROLL, wait_rows, 0)

    @pl.when(i == 0)
    def _():
        def body(g, c):
            for u in range(DMA_UNROLL):
                start_row(key_ref, 0, g * DMA_UNROLL + u, u % 2)
            return c

        lax.fori_loop(0, tm // DMA_UNROLL, body, 0)

    def step(sl):
        wait_slot(sl)
        for r in range(tm):
            start_row(keyn_ref, 1 - sl, r, r % 2)
        proj = jnp.dot(p_ref[...].astype(_BF), wp_ref[...], preferred_element_type=_F32)
        y = jnp.concatenate([y_s[sl, pl.ds(sg, tm, stride=SUBLANES), :] for sg in range(SUBLANES)], axis=1)
        h = h_ref[...] + y
        u = h * lax.rsqrt(jnp.mean(h * h, axis=-1, keepdims=True) + EPS) * np_ref[...]
        gate = _sigmoid(jnp.dot(u.astype(_BF), wg_ref[...], preferred_element_type=_F32) + bg_ref[...])
        o_ref[...] = h + gate * proj

        @pl.when(i == nsteps - 1)
        def _():
            wait_slot(1 - sl)

    for sl in range(2):
        @pl.when(i % 2 == sl)
        def _(sl=sl):
            step(sl)


def _final(dest, h1, p2, ys, norm_ple, wg_bf, b_ple_gate, wp_bf):
    n, d = h1.shape
    assert d == SUBLANES * LANES
    tm = TM_FINAL
    nsteps = n // tm
    pd = p2.shape[1]
    row = lambda i: (i, 0)
    dest3 = dest.reshape(nsteps, 1, tm)
    return pl.pallas_call(
        _final_kernel,
        grid=(nsteps,),
        in_specs=[
            pl.BlockSpec((1, 1, tm), lambda i: (i, 0, 0), memory_space=pltpu.SMEM),
            pl.BlockSpec((1, 1, tm), lambda i: (jnp.minimum(i + 1, nsteps - 1), 0, 0), memory_space=pltpu.SMEM),
            pl.BlockSpec((tm, d), row),
            pl.BlockSpec((tm, pd), row),
            pl.BlockSpec(memory_space=pl.ANY),
            _const_spec((1, d)),
            _const_spec(wg_bf.shape),
            _const_spec((1, d)),
            _const_spec(wp_bf.shape),
        ],
        out_specs=pl.BlockSpec((tm, d), row),
        out_shape=jax.ShapeDtypeStruct((n, d), _F32),
        scratch_shapes=[pltpu.VMEM((2, tm * SUBLANES, LANES), _F32), pltpu.SemaphoreType.DMA((2,))],
        compiler_params=pltpu.CompilerParams(
            dimension_semantics=("arbitrary",), vmem_limit_bytes=VMEM_LIMIT),
        name="final_ple",
    )(dest3, dest3, h1, p2, ys, norm_ple.reshape(1, d), wg_bf, b_ple_gate.reshape(1, d), wp_bf)


def _class_tables():
    ea, eb = [], []
    for g in range(N_GROUPS):
        for lo in range(EXPERTS_PER_GROUP):
            for hi in range(lo + 1, EXPERTS_PER_GROUP):
                ea.append(g * EXPERTS_PER_GROUP + lo)
                eb.append(g * EXPERTS_PER_GROUP + hi)
    return np.asarray(ea, np.int32), np.asarray(eb, np.int32)


_CLASS_EA, _CLASS_EB = _class_tables()


def kernel(x, p, norm_mix, w_in, b_conv_in, b_gate, q_norm, k_norm, lambda_q1, lambda_k1, lambda_q2, lambda_k2, subln, w_attn_out, conv_w, conv_b, conv_ln_g, conv_ln_b, w_conv_out, b_conv_out, w_o, norm_ffn, w_router_group, b_router_group, w_router_expert, b_router_expert, w1, w3, w2, norm_ple, w_ple_gate, b_ple_gate, w_ple_proj):
    b, s, d = x.shape
    n = b * s
    depth = w_in.shape[0]
    qk = ATTN_HEADS * 2 * HEAD_DIM
    aw = ATTN_HEADS * V_DIM
    cc = conv_w.shape[-1]
    assert depth == 1 and n % TM_MIX == 0 and s % (2 * ATTN_TK) == 0 and s % TM_INPROJ == 0
    assert n % TS_SCATTER == 0 and n % TM_FINAL == 0 and n % BM_MOE == 0
    i = 0
    h = x.reshape(n, d)

    conv_w_pad = jnp.pad(conv_w[i], ((0, CONV_HALO - CONV_WIDTH), (0, 0)))
    q2, k2, v2, cact2, gates2 = _inproj(h, norm_mix[i], w_in[i].astype(_BF), b_conv_in[i], b_gate[i], conv_w_pad,
                                        conv_b[i], conv_ln_g[i], conv_ln_b[i], qk, aw, cc, s)

    lam_vecs = jnp.stack([lambda_q1[i], lambda_k1[i], lambda_q2[i], lambda_k2[i]]).astype(_F32)
    attn = _attention(lam_vecs, q2.reshape(b, s, qk), k2.reshape(b, s, qk), v2.reshape(b, s, aw),
                      q_norm[i], k_norm[i], subln[i])

    w_router = jnp.concatenate([w_router_group[i], w_router_expert[i]], axis=1)
    w_router = jnp.pad(w_router, ((0, 0), (0, LANES - w_router.shape[1])))
    b_router = jnp.concatenate([b_router_group[i], b_router_expert[i]])
    b_router = jnp.pad(b_router, (0, LANES - b_router.shape[0])).reshape(1, LANES)
    w_router_hi = w_router.astype(_BF)
    w_router_lo = (w_router - w_router_hi.astype(_F32)).astype(_BF)
    w_router_cat = jnp.concatenate([w_router_hi, w_router_lo], axis=1)
    h1, t_rows, keys, cnt = _mix(h, attn.reshape(n, aw), cact2, gates2, w_attn_out[i].astype(_BF),
                                 w_conv_out[i].astype(_BF), w_o[i].astype(_BF), b_conv_out[i], norm_ffn[i],
                                 w_router_cat, w_router_hi, b_router)

    bm = BM_MOE
    nb = n // bm + N_CLASSES
    counts = cnt[0].astype(jnp.int32)
    padded = ((counts + bm - 1) // bm) * bm
    pend = jnp.cumsum(padded)
    pstart = (pend - padded).astype(jnp.int32)
    total = pend[-1]
    blk = jnp.arange(nb, dtype=jnp.int32) * bm
    valid = blk < total
    last_cls = jnp.searchsorted(pend, total - 1, side='right')
    bcls = jnp.where(valid, jnp.searchsorted(pend, blk, side='right'), last_cls)
    bcls = jnp.minimum(bcls, N_CLASSES - 1).astype(jnp.int32)
    ea = jnp.asarray(_CLASS_EA)[bcls]
    eb = jnp.asarray(_CLASS_EB)[bcls]
    nvalid = (total // bm).astype(jnp.int32).reshape(1)

    dest = _sorted_rows(pstart, keys.reshape(n // LANES, LANES))
    xs = _scatter_rows((pstart + counts).astype(jnp.int32), (padded - counts).astype(jnp.int32), nvalid,
                       dest, t_rows, nb * bm)
    ys = _moe(ea, eb, valid.astype(jnp.int32), xs, w_router_cat, w_router_hi, b_router,
              w1[i].astype(_BF), w3[i].astype(_BF), w2[i].astype(_BF))

    out = _final(dest, h1, p[i].reshape(n, -1), ys, norm_ple[i], w_ple_gate[i].astype(_BF), b_ple_gate[i],
                 w_ple_proj[i].astype(_BF))
    return out.reshape(b, s, d)
```

```python
import functools
import math

import numpy as np
import jax
import jax.numpy as jnp
from jax import lax
from jax.experimental import pallas as pl
from jax.experimental.pallas import tpu as pltpu

ATTN_HEADS = 4
HEAD_DIM = 64
V_DIM = 2 * HEAD_DIM
CONV_WIDTH = 31
N_GROUPS = 4
EXPERTS_PER_GROUP = 8
N_EXPERTS = N_GROUPS * EXPERTS_PER_GROUP
PAIRS_PER_GROUP = EXPERTS_PER_GROUP * (EXPERTS_PER_GROUP - 1) // 2
N_CLASSES = N_GROUPS * PAIRS_PER_GROUP
EPS = 1e-6
LAMBDA_INIT = 0.8 - 0.6 * math.exp(-0.3 * 0)

LANES = 128
SUBLANES = 8
CONV_HALO = 32
NEG_BIG = -1e30
RANK_BITS = 17
META_SUBLANE = 4
META_W_LO, META_W_HI, META_TOKEN = 0, 1, 2
DMA_UNROLL = 8

TM_INPROJ = 512
ATTN_TK = 512
TM_MIX = 512
CONV_ROWS = 64
TS_SCATTER = 512
BM_MOE = 256
TM_FINAL = 512
VMEM_LIMIT = 56 * 1024 * 1024

_BF = jnp.bfloat16
_F32 = jnp.float32


def _const_spec(shape):
    nd = len(shape)
    return pl.BlockSpec(shape, lambda *_: (0,) * nd)


def _sigmoid(x):
    return 1.0 / (1.0 + jnp.exp(-x))


def _inproj_kernel(x_ref, nm_ref, w_ref, bci_ref, bg_ref, cw_ref, cb_ref, lng_ref, lnb_ref,
                   q_ref, k_ref, v_ref, c_ref, g_ref, ext_s, xsh_s, cv_s, *, tiles_per_seq):
    i = pl.program_id(0)
    tm = x_ref.shape[0]
    qk = q_ref.shape[1]
    aw = v_ref.shape[1]
    cc = c_ref.shape[1]
    ext_rows = CONV_HALO + tm
    o0, o1, o2, o3 = qk, 2 * qk, 2 * qk + aw, 2 * qk + aw + 2 * cc

    @pl.when(i == 0)
    def _():
        ext_s[ext_rows:ext_rows + SUBLANES, :] = jnp.zeros((SUBLANES, cc), _F32)

    @pl.when((i % tiles_per_seq) == 0)
    def _():
        ext_s[0:CONV_HALO, :] = jnp.zeros((CONV_HALO, cc), _F32)

    x = x_ref[...]
    u = x * lax.rsqrt(jnp.mean(x * x, axis=-1, keepdims=True) + EPS) * nm_ref[...]
    ub = u.astype(_BF)
    c = jnp.dot(ub, w_ref[:, o2:o3], preferred_element_type=_F32) + bci_ref[...]
    ext_s[CONV_HALO:ext_rows, :] = c[:, :cc] * _sigmoid(c[:, cc:])
    q_ref[...] = jnp.dot(ub, w_ref[:, 0:o0], preferred_element_type=_F32).astype(_BF)
    k_ref[...] = jnp.dot(ub, w_ref[:, o0:o1], preferred_element_type=_F32).astype(_BF)
    v_ref[...] = jnp.dot(ub, w_ref[:, o1:o2], preferred_element_type=_F32).astype(_BF)
    g = jnp.dot(ub, w_ref[:, o3:], preferred_element_type=_F32) + bg_ref[...]
    g_ref[...] = _sigmoid(g).astype(_BF)
    off = CONV_HALO - (CONV_WIDTH - 1)

    for rho in range(1, SUBLANES):
        xsh_s[rho - 1] = ext_s[rho:rho + ext_rows, :]

    for r0 in range(0, tm, CONV_ROWS):
        acc = jnp.zeros((CONV_ROWS, cc), _F32)
        for j in range(CONV_WIDTH):
            a, rho = divmod(off + j, SUBLANES)
            lo_row = r0 + a * SUBLANES
            if rho == 0:
                xv = ext_s[lo_row:lo_row + CONV_ROWS, :]
            else:
                xv = xsh_s[rho - 1, lo_row:lo_row + CONV_ROWS, :]
            acc = acc + cw_ref[j:j + 1, :] * xv
        cv_s[r0:r0 + CONV_ROWS, :] = acc

    ext_s[0:CONV_HALO, :] = ext_s[tm:ext_rows, :]

    cv = cv_s[...] + cb_ref[...]
    mu = jnp.mean(cv, axis=-1, keepdims=True)
    xc = cv - mu
    var = jnp.mean(xc * xc, axis=-1, keepdims=True)
    ln = xc * lax.rsqrt(var + EPS) * lng_ref[...] + lnb_ref[...]
    c_ref[...] = (ln * _sigmoid(ln)).astype(_BF)


def _inproj(x2, norm_mix, w_in_bf, b_conv_in, b_gate, conv_w, conv_b, ln_g, ln_b, qk, aw, cc, seq_len):
    n, d = x2.shape
    tm = TM_INPROJ
    ng = b_gate.shape[-1]
    kern = functools.partial(_inproj_kernel, tiles_per_seq=seq_len // tm)
    return pl.pallas_call(
        kern,
        grid=(n // tm,),
        in_specs=[
            pl.BlockSpec((tm, d), lambda i: (i, 0)),
            _const_spec((1, d)),
            pl.BlockSpec(w_in_bf.shape, lambda i: (0, 0), pipeline_mode=pl.Buffered(1)),
            _const_spec((1, 2 * cc)),
            _const_spec((1, ng)),
            _const_spec((CONV_HALO, cc)),
            _const_spec((1, cc)),
            _const_spec((1, cc)),
            _const_spec((1, cc)),
        ],
        out_specs=[
            pl.BlockSpec((tm, qk), lambda i: (i, 0)),
            pl.BlockSpec((tm, qk), lambda i: (i, 0)),
            pl.BlockSpec((tm, aw), lambda i: (i, 0)),
            pl.BlockSpec((tm, cc), lambda i: (i, 0)),
            pl.BlockSpec((tm, ng), lambda i: (i, 0)),
        ],
        out_shape=[
            jax.ShapeDtypeStruct((n, qk), _BF),
            jax.ShapeDtypeStruct((n, qk), _BF),
            jax.ShapeDtypeStruct((n, aw), _BF),
            jax.ShapeDtypeStruct((n, cc), _BF),
            jax.ShapeDtypeStruct((n, ng), _BF),
        ],
        scratch_shapes=[
            pltpu.VMEM((CONV_HALO + tm + SUBLANES, cc), _F32),
            pltpu.VMEM((SUBLANES - 1, CONV_HALO + tm, cc), _F32),
            pltpu.VMEM((tm, cc), _F32),
        ],
        compiler_params=pltpu.CompilerParams(
            dimension_semantics=("arbitrary",), vmem_limit_bytes=VMEM_LIMIT),
        name="inproj",
    )(x2, norm_mix.reshape(1, d), w_in_bf, b_conv_in.reshape(1, -1), b_gate.reshape(1, -1), conv_w,
      conv_b.reshape(1, cc), ln_g.reshape(1, cc), ln_b.reshape(1, cc))


def _half_rmsnorm(x, gain_row, lo_mask):
    x2 = x * x
    s_lo = jnp.sum(jnp.where(lo_mask, x2, 0.0), axis=-1, keepdims=True)
    s_hi = jnp.sum(jnp.where(lo_mask, 0.0, x2), axis=-1, keepdims=True)
    r = jnp.where(lo_mask, lax.rsqrt(s_lo * (1.0 / HEAD_DIM) + EPS), lax.rsqrt(s_hi * (1.0 / HEAD_DIM) + EPS))
    return x * r * gain_row


def _attn_kernel(lam_ref, q_ref, k_ref, v_ref, qn_ref, kn_ref, sub_ref, o_ref, kn_s, va_s, qs_s, m_s, acc_s,
                 sa_s, sb_s, sc_s):
    s_len = q_ref.shape[1]
    tk = ATTN_TK
    tq = 2 * tk
    nq = s_len // tq
    lane = lax.broadcasted_iota(jnp.int32, (1, V_DIM), 1)
    lo_mask = lane < HEAD_DIM

    lv = lam_ref[...]
    d1 = jnp.sum(lv[0:1, :] * lv[1:2, :], axis=-1, keepdims=True)
    d2 = jnp.sum(lv[2:3, :] * lv[3:4, :], axis=-1, keepdims=True)
    lam = jnp.exp(d1) - jnp.exp(d2) + LAMBDA_INIT

    kgain = kn_ref[...]
    qgain = qn_ref[...] * (math.log2(math.e) / math.sqrt(HEAD_DIM))

    def prep_kv(i, carry):
        r0 = pl.multiple_of(i * tq, tq)
        kk = k_ref[0, pl.ds(r0, tq), :].astype(_F32)
        kn_s[pl.ds(r0, tq), :] = _half_rmsnorm(kk, kgain, lo_mask).astype(_BF)
        va_s[pl.ds(r0, tq), 0:V_DIM] = v_ref[0, pl.ds(r0, tq), :]
        va_s[pl.ds(r0, tq), V_DIM:2 * V_DIM] = jnp.ones((tq, V_DIM), _BF)
        return carry

    lax.fori_loop(0, nq, prep_kv, 0)

    def scores(c0, ra, nr):
        kc = kn_s[pl.ds(c0, tk), :]
        return lax.dot_general(qs_s[ra:ra + nr, :], kc, (((1,), (1,)), ((), ())), preferred_element_type=_F32)

    def consume(s, c0, ra, nr, n_masked):
        vc = va_s[pl.ds(c0, tk), :]
        if n_masked:
            keep = (lax.broadcasted_iota(jnp.int32, (tk, tk), 1) <= lax.broadcasted_iota(jnp.int32, (tk, tk), 0))
            groups = [s[g * tk:(g + 1) * tk, :] for g in range(nr // tk)]
            groups = [jnp.where(keep, sg, NEG_BIG) if g < n_masked else sg for g, sg in enumerate(groups)]
            s = jnp.concatenate(groups, axis=0)
        m_old = m_s[ra:ra + nr, :]
        m_new = jnp.maximum(m_old, jnp.max(s, axis=-1, keepdims=True))
        alpha = jnp.exp2(m_old - m_new)
        p = jnp.concatenate(
            [jnp.exp2(s[:, c * LANES:(c + 1) * LANES] - m_new) for c in range(tk // LANES)], axis=1)
        pv = jnp.dot(p.astype(_BF), vc, preferred_element_type=_F32)
        acc_s[ra:ra + nr, :] = jnp.concatenate([alpha, alpha], axis=1) * acc_s[ra:ra + nr, :] + pv
        m_s[ra:ra + nr, :] = m_new

    def prep_q(r0):
        for hf in range(2):
            qq = _half_rmsnorm(q_ref[0, pl.ds(r0 + hf * tk, tk), :].astype(_F32), qgain, lo_mask)
            qs_s[(2 * hf) * tk:(2 * hf + 1) * tk, :] = jnp.where(lo_mask, qq, 0.0).astype(_BF)
            qs_s[(2 * hf + 1) * tk:(2 * hf + 2) * tk, :] = jnp.where(lo_mask, 0.0, qq).astype(_BF)

    bufs = (sa_s, sb_s, sc_s)
    prep_q(0)
    sa_s[...] = scores(0, 0, 4 * tk)

    for qi in range(nq):
        r0 = qi * tq
        cur = bufs[(2 * qi) % 3]
        nxt = bufs[(2 * qi + 1) % 3]
        ahead = bufs[(2 * qi + 2) % 3]
        m_s[...] = jnp.full(m_s.shape, NEG_BIG, _F32)
        acc_s[...] = jnp.zeros(acc_s.shape, _F32)

        def two_chunks(t, c, cur=cur, nxt=nxt):
            c0 = pl.multiple_of(2 * t * tk, tk)
            c1 = pl.multiple_of(c0 + tk, tk)
            c2 = pl.multiple_of(c0 + 2 * tk, tk)
            nxt[...] = scores(c1, 0, 4 * tk)
            consume(cur[...], c0, 0, 4 * tk, 0)
            cur[...] = scores(c2, 0, 4 * tk)
            consume(nxt[...], c1, 0, 4 * tk, 0)
            return c

        lax.fori_loop(0, qi, two_chunks, 0)
        nxt[0:2 * tk, :] = scores(r0 + tk, 2 * tk, 2 * tk)
        if qi + 1 < nq:
            prep_q(r0 + tq)
            ahead[...] = scores(0, 0, 4 * tk)
        consume(cur[...], r0, 0, 4 * tk, 2)
        consume(nxt[0:2 * tk, :], r0 + tk, 2 * tk, 2 * tk, 2)

        acc = acc_s[...]
        o = acc[:, 0:V_DIM] / acc[:, V_DIM:2 * V_DIM]
        for hf in range(2):
            a = o[(2 * hf) * tk:(2 * hf + 1) * tk, :] - lam * o[(2 * hf + 1) * tk:(2 * hf + 2) * tk, :]
            a = a * lax.rsqrt(jnp.mean(a * a, axis=-1, keepdims=True) + EPS) * sub_ref[...] * (1.0 - LAMBDA_INIT)
            o_ref[0, r0 + hf * tk:r0 + (hf + 1) * tk, :] = a.astype(_BF)


def _attention(lam_vecs, q3, k3, v3, q_norm, k_norm, subln):
    b, s, _ = q3.shape
    tq = 2 * ATTN_TK
    slab = lambda bi, hi: (bi, 0, hi)
    return pl.pallas_call(
        _attn_kernel,
        grid=(b, ATTN_HEADS),
        in_specs=[
            _const_spec((4, HEAD_DIM)),
            pl.BlockSpec((1, s, V_DIM), slab),
            pl.BlockSpec((1, s, V_DIM), slab),
            pl.BlockSpec((1, s, V_DIM), slab),
            _const_spec((1, V_DIM)),
            _const_spec((1, V_DIM)),
            _const_spec((1, V_DIM)),
        ],
        out_specs=pl.BlockSpec((1, s, V_DIM), slab),
        out_shape=jax.ShapeDtypeStruct((b, s, ATTN_HEADS * V_DIM), _BF),
        scratch_shapes=[
            pltpu.VMEM((s, V_DIM), _BF),
            pltpu.VMEM((s, 2 * V_DIM), _BF),
            pltpu.VMEM((2 * tq, V_DIM), _BF),
            pltpu.VMEM((2 * tq, V_DIM), _F32),
            pltpu.VMEM((2 * tq, 2 * V_DIM), _F32),
            pltpu.VMEM((2 * tq, ATTN_TK), _F32),
            pltpu.VMEM((2 * tq, ATTN_TK), _F32),
            pltpu.VMEM((2 * tq, ATTN_TK), _F32),
        ],
        compiler_params=pltpu.CompilerParams(
            dimension_semantics=("arbitrary", "arbitrary"), vmem_limit_bytes=VMEM_LIMIT),
        name="diffattn",
    )(lam_vecs, q3, k3, v3, q_norm.reshape(1, V_DIM), k_norm.reshape(1, V_DIM), subln.reshape(1, V_DIM))


def _router_logits(t, wr_ref, wrh_ref, br_ref):
    t_hi = t.astype(_BF)
    t_lo = (t - t_hi.astype(_F32)).astype(_BF)
    l_hi = jnp.dot(t_hi, wr_ref[...], preferred_element_type=_F32)
    l_lo = jnp.dot(t_lo, wrh_ref[...], preferred_element_type=_F32)
    return l_hi[:, 0:LANES] + l_hi[:, LANES:2 * LANES] + l_lo + br_ref[...]


def _mix_kernel(x_ref, a_ref, c_ref, g_ref, wa_ref, wc_ref, wo_ref, bco_ref, nf_ref, wr_ref, wrh_ref, br_ref,
                h_ref, t_ref, key_ref, cnt_ref, run_s):
    i = pl.program_id(0)
    tm, d = x_ref.shape

    @pl.when(i == 0)
    def _():
        run_s[...] = jnp.zeros(run_s.shape, _F32)

    y_b = jnp.dot(c_ref[...], wc_ref[...], preferred_element_type=_F32) + bco_ref[...]
    y_a = jnp.dot(a_ref[...], wa_ref[...], preferred_element_type=_F32)

    merged = g_ref[:, 0:d].astype(_F32) * y_a + g_ref[:, d:2 * d].astype(_F32) * y_b
    h = x_ref[...] + jnp.dot(merged.astype(_BF), wo_ref[...], preferred_element_type=_F32)
    h_ref[...] = h

    t = h * lax.rsqrt(jnp.mean(h * h, axis=-1, keepdims=True) + EPS) * nf_ref[...]
    half_groups = d // LANES // 2
    for sg in range(half_groups):
        hi_bits = lax.bitcast_convert_type(
            t[:, sg * LANES:(sg + 1) * LANES].astype(_BF).astype(_F32), jnp.uint32)
        lo_bits = lax.bitcast_convert_type(
            t[:, (sg + half_groups) * LANES:(sg + half_groups + 1) * LANES].astype(_BF).astype(_F32), jnp.uint32)
        word = hi_bits | (lo_bits >> 16)
        t_ref[pl.ds(sg, tm, stride=SUBLANES), :] = lax.bitcast_convert_type(word, jnp.int32)

    lg = _router_logits(t, wr_ref, wrh_ref, br_ref)
    lane = lax.broadcasted_iota(jnp.int32, (tm, LANES), 1)
    gl = jnp.where(lane < N_GROUPS, lg, NEG_BIG)
    gmax = jnp.max(gl, axis=-1, keepdims=True)
    gidx = jnp.min(jnp.where(gl == gmax, lane, LANES), axis=-1, keepdims=True)
    gsum = jnp.sum(jnp.where(lane < N_GROUPS, jnp.exp(gl - gmax), 0.0), axis=-1, keepdims=True)
    g_gate = 1.0 / gsum
    e_lo = N_GROUPS + EXPERTS_PER_GROUP * gidx
    el = jnp.where((lane >= e_lo) & (lane < e_lo + EXPERTS_PER_GROUP), lg, NEG_BIG)
    m1 = jnp.max(el, axis=-1, keepdims=True)
    i1 = jnp.min(jnp.where(el == m1, lane, LANES), axis=-1, keepdims=True)
    el2 = jnp.where(lane == i1, NEG_BIG, el)
    m2 = jnp.max(el2, axis=-1, keepdims=True)
    i2 = jnp.min(jnp.where(el2 == m2, lane, LANES), axis=-1, keepdims=True)
    e21 = jnp.exp(m2 - m1)
    w_top1 = g_gate * (1.0 / (1.0 + e21))
    w_top2 = g_gate * (e21 / (1.0 + e21))
    a1 = i1 - e_lo
    a2 = i2 - e_lo
    lo = jnp.minimum(a1, a2)
    hi = jnp.maximum(a1, a2)
    w_lo = jnp.where(a1 < a2, w_top1, w_top2)
    w_hi = jnp.where(a1 < a2, w_top2, w_top1)
    pidx = ((lo * (2 * EXPERTS_PER_GROUP - 1 - lo)) >> 1) + (hi - lo - 1)
    cls = gidx * PAIRS_PER_GROUP + pidx
    token = i * tm + lax.broadcasted_iota(jnp.int32, (tm, LANES), 0)
    meta = jnp.where(lane == META_W_LO, lax.bitcast_convert_type(w_lo, jnp.int32),
                     jnp.where(lane == META_W_HI, lax.bitcast_convert_type(w_hi, jnp.int32),
                               jnp.where(lane == META_TOKEN, token, 0)))
    t_ref[pl.ds(META_SUBLANE, tm, stride=SUBLANES), :] = meta
    for sg in range(META_SUBLANE + 1, SUBLANES):
        t_ref[pl.ds(sg, tm, stride=SUBLANES), :] = jnp.zeros((tm, LANES), jnp.int32)

    onehot = lane == cls
    oh_bf = jnp.where(onehot, 1.0, 0.0).astype(_BF)
    rr = lax.broadcasted_iota(jnp.int32, (tm, tm), 0)
    ccol = lax.broadcasted_iota(jnp.int32, (tm, tm), 1)
    tri = jnp.where(ccol < rr, 1.0, 0.0).astype(_BF)
    before = jnp.dot(tri, oh_bf, preferred_element_type=_F32) + run_s[...]
    rank = jnp.sum(jnp.where(onehot, before, 0.0), axis=-1, keepdims=True)
    run_new = run_s[...] + jnp.sum(jnp.where(onehot, 1.0, 0.0), axis=0, keepdims=True)
    run_s[...] = run_new
    cnt_ref[...] = run_new
    keyf = jnp.broadcast_to(cls.astype(_F32) * float(1 << RANK_BITS) + rank, (tm, LANES))
    for g in range(tm // LANES):
        kt = keyf[g * LANES:(g + 1) * LANES, :].T
        key_ref[0, g:g + 1, :] = kt[0:1, :].astype(jnp.int32)


def _mix(x2, attn2, cact2, gates2, wa_bf, wc_bf, wo_bf, b_conv_out, norm_ffn, w_router_cat, w_router_hi, b_router):
    n, d = x2.shape
    tm = TM_MIX
    aw = attn2.shape[1]
    cc = cact2.shape[1]
    row = lambda i: (i, 0)
    return pl.pallas_call(
        _mix_kernel,
        grid=(n // tm,),
        in_specs=[
            pl.BlockSpec((tm, d), row),
            pl.BlockSpec((tm, aw), row),
            pl.BlockSpec((tm, cc), row),
            pl.BlockSpec((tm, 2 * d), row),
            _const_spec(wa_bf.shape),
            _const_spec(wc_bf.shape),
            _const_spec(wo_bf.shape),
            _const_spec((1, d)),
            _const_spec((1, d)),
            _const_spec((d, 2 * LANES)),
            _const_spec((d, LANES)),
            _const_spec((1, LANES)),
        ],
        out_specs=[
            pl.BlockSpec((tm, d), row),
            pl.BlockSpec((tm * SUBLANES, LANES), row),
            pl.BlockSpec((1, tm // LANES, LANES), lambda i: (i, 0, 0)),
            _const_spec((1, LANES)),
        ],
        out_shape=[
            jax.ShapeDtypeStruct((n, d), _F32),
            jax.ShapeDtypeStruct((n * SUBLANES, LANES), jnp.int32),
            jax.ShapeDtypeStruct((n // tm, tm // LANES, LANES), jnp.int32),
            jax.ShapeDtypeStruct((1, LANES), _F32),
        ],
        scratch_shapes=[pltpu.VMEM((1, LANES), _F32)],
        compiler_params=pltpu.CompilerParams(
            dimension_semantics=("arbitrary",), vmem_limit_bytes=VMEM_LIMIT),
        name="mix_router",
    )(x2, attn2, cact2, gates2, wa_bf, wc_bf, wo_bf, b_conv_out.reshape(1, d), norm_ffn.reshape(1, d),
      w_router_cat, w_router_hi, b_router)


def _sorted_rows_kernel(pstart_ref, key_ref, o_ref):
    key = key_ref[...]
    cls = key >> RANK_BITS
    row = key & ((1 << RANK_BITS) - 1)
    for c in range(N_CLASSES):
        row = row + jnp.where(cls == c, pstart_ref[c], 0)
    o_ref[...] = row


def _sorted_rows(pstart, keys2):
    grid_spec = pltpu.PrefetchScalarGridSpec(
        num_scalar_prefetch=1,
        grid=(1,),
        in_specs=[pl.BlockSpec(keys2.shape, lambda i, *_: (0, 0))],
        out_specs=pl.BlockSpec(keys2.shape, lambda i, *_: (0, 0)),
    )
    return pl.pallas_call(
        _sorted_rows_kernel,
        grid_spec=grid_spec,
        out_shape=jax.ShapeDtypeStruct(keys2.shape, jnp.int32),
        name="sorted_rows",
    )(pstart, keys2)


def _token_rows(idx, count=1):
    return pl.ds(pl.multiple_of(idx * SUBLANES, SUBLANES), count * SUBLANES)


def _scatter_kernel(padstart_ref, padlen_ref, nvalid_ref, dest_ref, t_ref, xs_ref, zero_s, sem, zsem):
    i = pl.program_id(0)
    ts = t_ref.shape[0] // SUBLANES
    n_tokens = ts * pl.num_programs(0)
    bm = zero_s.shape[0] // SUBLANES
    nblk = xs_ref.shape[0] // (bm * SUBLANES)

    def start_rows(g, c):
        for u in range(DMA_UNROLL):
            r = g * DMA_UNROLL + u
            pltpu.make_async_copy(t_ref.at[_token_rows(r)], xs_ref.at[_token_rows(dest_ref[0, 0, r])], sem).start(
                priority=u % 2)
        return c

    def wait_rows(g, c):
        for _ in range(DMA_UNROLL):
            pltpu.make_async_copy(t_ref.at[_token_rows(0)], xs_ref.at[_token_rows(0)], sem).wait()
        return c

    lax.fori_loop(0, ts // DMA_UNROLL, start_rows, 0)

    @pl.when(i == 0)
    def _():
        rows = lax.broadcasted_iota(jnp.int32, zero_s.shape, 0)
        lanes = lax.broadcasted_iota(jnp.int32, zero_s.shape, 1)
        is_token = ((rows & (SUBLANES - 1)) == META_SUBLANE) & (lanes == META_TOKEN)
        zero_s[...] = jnp.where(is_token, n_tokens + (rows >> (SUBLANES.bit_length() - 1)), 0)

        def pad_copies(c, act):
            ln = padlen_ref[c]
            st = padstart_ref[c]
            for k in range(bm.bit_length() - 1):
                sz = 1 << k

                @pl.when(((ln >> k) & 1) == 1)
                def _():
                    dst = st + (ln & (sz - 1))
                    src = dst & (bm - 1)
                    act(pltpu.make_async_copy(zero_s.at[_token_rows(src, sz)], xs_ref.at[_token_rows(dst, sz)], zsem))

        def tail_copy(j, act):
            @pl.when(j >= nvalid_ref[0])
            def _():
                act(pltpu.make_async_copy(zero_s, xs_ref.at[_token_rows(j * bm, bm)], zsem))

        for act in (lambda cp: cp.start(), lambda cp: cp.wait()):
            lax.fori_loop(0, N_CLASSES, lambda c, carry, act=act: (pad_copies(c, act), carry)[1], 0)
            lax.fori_loop(0, nblk, lambda j, carry, act=act: (tail_copy(j, act), carry)[1], 0)

    lax.fori_loop(0, ts // DMA_UNROLL, wait_rows, 0)


def _scatter_rows(padstart, padlen, nvalid, dest, t_rows, n_rows):
    n = t_rows.shape[0] // SUBLANES
    ts = TS_SCATTER
    grid_spec = pltpu.PrefetchScalarGridSpec(
        num_scalar_prefetch=3,
        grid=(n // ts,),
        in_specs=[
            pl.BlockSpec((1, 1, ts), lambda i, *_: (i, 0, 0), memory_space=pltpu.SMEM),
            pl.BlockSpec((ts * SUBLANES, LANES), lambda i, *_: (i, 0)),
        ],
        out_specs=pl.BlockSpec(memory_space=pl.ANY),
        scratch_shapes=[pltpu.VMEM((BM_MOE * SUBLANES, LANES), jnp.int32), pltpu.SemaphoreType.DMA,
                        pltpu.SemaphoreType.DMA],
    )
    return pl.pallas_call(
        _scatter_kernel,
        grid_spec=grid_spec,
        out_shape=jax.ShapeDtypeStruct((n_rows * SUBLANES, LANES), jnp.int32),
        compiler_params=pltpu.CompilerParams(
            dimension_semantics=("arbitrary",), vmem_limit_bytes=VMEM_LIMIT),
        name="scatter_rows",
    )(padstart, padlen, nvalid, dest.reshape(n // ts, 1, ts), t_rows)


def _moe_kernel(ea_ref, eb_ref, valid_ref, xs_ref, w1a, w3a, w2a, w1b, w3b, w2b, out_ref,
                y_s, ids_v, ids_sm, rsem, isem):
    j = pl.program_id(0)
    nb = pl.num_programs(0)
    bm = xs_ref.shape[0] // SUBLANES
    ng = bm // LANES
    n_tokens = out_ref.shape[0] // SUBLANES - bm

    def ids_copy(sl):
        return pltpu.make_async_copy(ids_v.at[sl], ids_sm.at[sl], isem.at[sl])

    def row_copy(sl, r, tok):
        return pltpu.make_async_copy(y_s.at[sl, _token_rows(r)], out_ref.at[_token_rows(tok)], rsem.at[sl])

    def issue_rows(sl):
        for r in range(bm):
            row_copy(sl, r, ids_sm[sl, r // LANES, r % LANES]).start(priority=r % 2)

    def wait_rows(sl):
        def body(g, c):
            for _ in range(DMA_UNROLL):
                row_copy(sl, 0, 0).wait()
            return c

        lax.fori_loop(0, bm // DMA_UNROLL, body, 0)

    @pl.when(j == 0)
    def _():
        y_s[1] = jnp.zeros(y_s.shape[1:], _F32)
        pos = (lax.broadcasted_iota(jnp.int32, (ng, LANES), 0) * LANES
               + lax.broadcasted_iota(jnp.int32, (ng, LANES), 1))
        ids_v[1] = n_tokens + pos
        ids_copy(1).start()

    def step(sl, is_valid):
        other = 1 - sl
        ids_copy(other).wait()

        @pl.when(j >= 1)
        def _():
            wait_rows(sl)

        issue_rows(other)

        meta = xs_ref[pl.ds(META_SUBLANE, bm, stride=SUBLANES), :]
        lane = lax.broadcasted_iota(jnp.int32, (bm, LANES), 1)
        tok = jnp.sum(jnp.where(lane == META_TOKEN, meta.astype(_F32), 0.0), axis=-1, keepdims=True)
        tokb = jnp.broadcast_to(tok, (bm, LANES))
        for g in range(ng):
            ids_v[sl, g:g + 1, :] = tokb[g * LANES:(g + 1) * LANES, :].T[0:1, :].astype(jnp.int32)
        ids_copy(sl).start()

        if is_valid:
            words = [xs_ref[pl.ds(sg, bm, stride=SUBLANES), :] for sg in range(META_SUBLANE)]
            hi = [lax.bitcast_convert_type(w & jnp.int32(-65536), _F32).astype(_BF) for w in words]
            lo = [lax.bitcast_convert_type(w << 16, _F32).astype(_BF) for w in words]
            xb = jnp.concatenate(hi + lo, axis=1)
            metaf = lax.bitcast_convert_type(meta, _F32)
            w_a = jnp.sum(jnp.where(lane == META_W_LO, metaf, 0.0), axis=-1, keepdims=True)
            w_b = jnp.sum(jnp.where(lane == META_W_HI, metaf, 0.0), axis=-1, keepdims=True)

            def expert(w1, w3, w2):
                h1 = jnp.dot(xb, w1[0], preferred_element_type=_F32)
                h3 = jnp.dot(xb, w3[0], preferred_element_type=_F32)
                hdn = h1 * _sigmoid(h1) * h3
                return jnp.dot(hdn.astype(_BF), w2[0], preferred_element_type=_F32)

            y = w_a * expert(w1a, w3a, w2a) + w_b * expert(w1b, w3b, w2b)
            for sg in range(SUBLANES):
                y_s[sl, pl.ds(sg, bm, stride=SUBLANES), :] = y[:, sg * LANES:(sg + 1) * LANES]
        else:
            y_s[sl] = jnp.zeros(y_s.shape[1:], _F32)

        @pl.when(j == nb - 1)
        def _():
            wait_rows(other)
            ids_copy(sl).wait()
            issue_rows(sl)
            wait_rows(sl)

    for sl in range(2):
        for is_valid in (True, False):
            @pl.when((j % 2 == sl) & ((valid_ref[j] != 0) == is_valid))
            def _(sl=sl, is_valid=is_valid):
                step(sl, is_valid)


def _moe(ea, eb, valid, xs, w1_bf, w3_bf, w2_bf, n_tokens):
    d, ff = w1_bf.shape[1], w1_bf.shape[2]
    assert d == SUBLANES * LANES
    bm = BM_MOE
    nb = xs.shape[0] // (bm * SUBLANES)
    grid_spec = pltpu.PrefetchScalarGridSpec(
        num_scalar_prefetch=3,
        grid=(nb,),
        in_specs=[
            pl.BlockSpec((bm * SUBLANES, LANES), lambda j, ea, eb, va: (j, 0)),
            pl.BlockSpec((1, d, ff), lambda j, ea, eb, va: (ea[j], 0, 0)),
            pl.BlockSpec((1, d, ff), lambda j, ea, eb, va: (ea[j], 0, 0)),
            pl.BlockSpec((1, ff, d), lambda j, ea, eb, va: (ea[j], 0, 0)),
            pl.BlockSpec((1, d, ff), lambda j, ea, eb, va: (eb[j], 0, 0)),
            pl.BlockSpec((1, d, ff), lambda j, ea, eb, va: (eb[j], 0, 0)),
            pl.BlockSpec((1, ff, d), lambda j, ea, eb, va: (eb[j], 0, 0)),
        ],
        out_specs=pl.BlockSpec(memory_space=pl.ANY),
        scratch_shapes=[
            pltpu.VMEM((2, bm * SUBLANES, LANES), _F32),
            pltpu.VMEM((2, bm // LANES, LANES), jnp.int32),
            pltpu.SMEM((2, bm // LANES, LANES), jnp.int32),
            pltpu.SemaphoreType.DMA((2,)),
            pltpu.SemaphoreType.DMA((2,)),
        ],
    )
    return pl.pallas_call(
        _moe_kernel,
        grid_spec=grid_spec,
        out_shape=jax.ShapeDtypeStruct(((n_tokens + bm) * SUBLANES, LANES), _F32),
        compiler_params=pltpu.CompilerParams(
            dimension_semantics=("arbitrary",), vmem_limit_bytes=VMEM_LIMIT),
        name="moe_experts",
    )(ea, eb, valid, xs, w1_bf, w3_bf, w2_bf, w1_bf, w3_bf, w2_bf)


def _final_kernel(h_ref, p_ref, y_ref, np_ref, wg_ref, bg_ref, wp_ref, o_ref):
    tm = h_ref.shape[0]
    proj = jnp.dot(p_ref[...].astype(_BF), wp_ref[...], preferred_element_type=_F32)
    y = jnp.concatenate([y_ref[pl.ds(sg, tm, stride=SUBLANES), :] for sg in range(SUBLANES)], axis=1)
    h = h_ref[...] + y
    u = h * lax.rsqrt(jnp.mean(h * h, axis=-1, keepdims=True) + EPS) * np_ref[...]
    gate = _sigmoid(jnp.dot(u.astype(_BF), wg_ref[...], preferred_element_type=_F32) + bg_ref[...])
    o_ref[...] = h + gate * proj


def _final(h1, p2, y_rows, norm_ple, wg_bf, b_ple_gate, wp_bf):
    n, d = h1.shape
    assert d == SUBLANES * LANES
    tm = TM_FINAL
    pd = p2.shape[1]
    row = lambda i: (i, 0)
    return pl.pallas_call(
        _final_kernel,
        grid=(n // tm,),
        in_specs=[
            pl.BlockSpec((tm, d), row),
            pl.BlockSpec((tm, pd), row),
            pl.BlockSpec((tm * SUBLANES, LANES), row),
            _const_spec((1, d)),
            _const_spec(wg_bf.shape),
            _const_spec((1, d)),
            _const_spec(wp_bf.shape),
        ],
        out_specs=pl.BlockSpec((tm, d), row),
        out_shape=jax.ShapeDtypeStruct((n, d), _F32),
        compiler_params=pltpu.CompilerParams(
            dimension_semantics=("arbitrary",), vmem_limit_bytes=VMEM_LIMIT),
        name="final_ple",
    )(h1, p2, y_rows, norm_ple.reshape(1, d), wg_bf, b_ple_gate.reshape(1, d), wp_bf)


def _class_tables():
    ea, eb = [], []
    for g in range(N_GROUPS):
        for lo in range(EXPERTS_PER_GROUP):
            for hi in range(lo + 1, EXPERTS_PER_GROUP):
                ea.append(g * EXPERTS_PER_GROUP + lo)
                eb.append(g * EXPERTS_PER_GROUP + hi)
    return np.asarray(ea, np.int32), np.asarray(eb, np.int32)


_CLASS_EA, _CLASS_EB = _class_tables()


def kernel(x, p, norm_mix, w_in, b_conv_in, b_gate, q_norm, k_norm, lambda_q1, lambda_k1, lambda_q2, lambda_k2, subln, w_attn_out, conv_w, conv_b, conv_ln_g, conv_ln_b, w_conv_out, b_conv_out, w_o, norm_ffn, w_router_group, b_router_group, w_router_expert, b_router_expert, w1, w3, w2, norm_ple, w_ple_gate, b_ple_gate, w_ple_proj):
    b, s, d = x.shape
    n = b * s
    depth = w_in.shape[0]
    qk = ATTN_HEADS * 2 * HEAD_DIM
    aw = ATTN_HEADS * V_DIM
    cc = conv_w.shape[-1]
    assert depth == 1 and n % TM_MIX == 0 and s % (2 * ATTN_TK) == 0 and s % TM_INPROJ == 0
    assert n % TS_SCATTER == 0 and n % TM_FINAL == 0 and n % BM_MOE == 0
    i = 0
    h = x.reshape(n, d)

    conv_w_pad = jnp.pad(conv_w[i], ((0, CONV_HALO - CONV_WIDTH), (0, 0)))
    q2, k2, v2, cact2, gates2 = _inproj(h, norm_mix[i], w_in[i].astype(_BF), b_conv_in[i], b_gate[i], conv_w_pad,
                                        conv_b[i], conv_ln_g[i], conv_ln_b[i], qk, aw, cc, s)

    lam_vecs = jnp.stack([lambda_q1[i], lambda_k1[i], lambda_q2[i], lambda_k2[i]]).astype(_F32)
    attn = _attention(lam_vecs, q2.reshape(b, s, qk), k2.reshape(b, s, qk), v2.reshape(b, s, aw),
                      q_norm[i], k_norm[i], subln[i])

    w_router = jnp.concatenate([w_router_group[i], w_router_expert[i]], axis=1)
    w_router = jnp.pad(w_router, ((0, 0), (0, LANES - w_router.shape[1])))
    b_router = jnp.concatenate([b_router_group[i], b_router_expert[i]])
    b_router = jnp.pad(b_router, (0, LANES - b_router.shape[0])).reshape(1, LANES)
    w_router_hi = w_router.astype(_BF)
    w_router_lo = (w_router - w_router_hi.astype(_F32)).astype(_BF)
    w_router_cat = jnp.concatenate([w_router_hi, w_router_lo], axis=1)
    h1, t_rows, keys, cnt = _mix(h, attn.reshape(n, aw), cact2, gates2, w_attn_out[i].astype(_BF),
                                 w_conv_out[i].astype(_BF), w_o[i].astype(_BF), b_conv_out[i], norm_ffn[i],
                                 w_router_cat, w_router_hi, b_router)

    bm = BM_MOE
    nb = n // bm + N_CLASSES
    counts = cnt[0].astype(jnp.int32)
    padded = ((counts + bm - 1) // bm) * bm
    pend = jnp.cumsum(padded)
    pstart = (pend - padded).astype(jnp.int32)
    total = pend[-1]
    blk = jnp.arange(nb, dtype=jnp.int32) * bm
    valid = blk < total
    last_cls = jnp.searchsorted(pend, total - 1, side='right')
    bcls = jnp.where(valid, jnp.searchsorted(pend, blk, side='right'), last_cls)
    bcls = jnp.minimum(bcls, N_CLASSES - 1).astype(jnp.int32)
    ea = jnp.asarray(_CLASS_EA)[bcls]
    eb = jnp.asarray(_CLASS_EB)[bcls]
    nvalid = (total // bm).astype(jnp.int32).reshape(1)

    dest = _sorted_rows(pstart, keys.reshape(n // LANES, LANES))
    xs = _scatter_rows((pstart + counts).astype(jnp.int32), (padded - counts).astype(jnp.int32), nvalid,
                       dest, t_rows, nb * bm)
    y_rows = _moe(ea, eb, valid.astype(jnp.int32), xs, w1[i].astype(_BF), w3[i].astype(_BF), w2[i].astype(_BF), n)

    out = _final(h1, p[i].reshape(n, -1), y_rows, norm_ple[i], w_ple_gate[i].astype(_BF), b_ple_gate[i],
                 w_ple_proj[i].astype(_BF))
    return out.reshape(b, s, d)
```

```python
import functools
import math

import numpy as np
import jax
import jax.numpy as jnp
from jax import lax
from jax.experimental import pallas as pl
from jax.experimental.pallas import tpu as pltpu

ATTN_HEADS = 4
HEAD_DIM = 64
V_DIM = 2 * HEAD_DIM
CONV_WIDTH = 31
N_GROUPS = 4
EXPERTS_PER_GROUP = 8
N_EXPERTS = N_GROUPS * EXPERTS_PER_GROUP
PAIRS_PER_GROUP = EXPERTS_PER_GROUP * (EXPERTS_PER_GROUP - 1) // 2
N_CLASSES = N_GROUPS * PAIRS_PER_GROUP
EPS = 1e-6
LAMBDA_INIT = 0.8 - 0.6 * math.exp(-0.3 * 0)

LANES = 128
SUBLANES = 8
CONV_HALO = 32
NEG_BIG = -1e30
RANK_BITS = 17
META_SUBLANE = 4
META_W_LO, META_W_HI, META_TOKEN = 0, 1, 2
DMA_UNROLL = 8

TM_INPROJ = 512
ATTN_TK = 512
TM_MIX = 512
CONV_ROWS = 64
TS_SCATTER = 512
BM_MOE = 256
TM_FINAL = 512
VMEM_LIMIT = 56 * 1024 * 1024

_BF = jnp.bfloat16
_F32 = jnp.float32


def _const_spec(shape):
    nd = len(shape)
    return pl.BlockSpec(shape, lambda *_: (0,) * nd)


def _sigmoid(x):
    return 1.0 / (1.0 + jnp.exp(-x))


def _inproj_kernel(x_ref, nm_ref, w_ref, bci_ref, bg_ref, cw_ref, cb_ref, lng_ref, lnb_ref,
                   q_ref, k_ref, v_ref, c_ref, g_ref, ext_s, xsh_s, cv_s, *, tiles_per_seq):
    i = pl.program_id(0)
    tm = x_ref.shape[0]
    qk = q_ref.shape[1]
    aw = v_ref.shape[1]
    cc = c_ref.shape[1]
    ext_rows = CONV_HALO + tm
    o0, o1, o2, o3 = qk, 2 * qk, 2 * qk + aw, 2 * qk + aw + 2 * cc

    @pl.when(i == 0)
    def _():
        ext_s[ext_rows:ext_rows + SUBLANES, :] = jnp.zeros((SUBLANES, cc), _F32)

    @pl.when((i % tiles_per_seq) == 0)
    def _():
        ext_s[0:CONV_HALO, :] = jnp.zeros((CONV_HALO, cc), _F32)

    x = x_ref[...]
    u = x * lax.rsqrt(jnp.mean(x * x, axis=-1, keepdims=True) + EPS) * nm_ref[...]
    ub = u.astype(_BF)
    c = jnp.dot(ub, w_ref[:, o2:o3], preferred_element_type=_F32) + bci_ref[...]
    ext_s[CONV_HALO:ext_rows, :] = c[:, :cc] * _sigmoid(c[:, cc:])
    q_ref[...] = jnp.dot(ub, w_ref[:, 0:o0], preferred_element_type=_F32).astype(_BF)
    k_ref[...] = jnp.dot(ub, w_ref[:, o0:o1], preferred_element_type=_F32).astype(_BF)
    v_ref[...] = jnp.dot(ub, w_ref[:, o1:o2], preferred_element_type=_F32).astype(_BF)
    g = jnp.dot(ub, w_ref[:, o3:], preferred_element_type=_F32) + bg_ref[...]
    g_ref[...] = _sigmoid(g).astype(_BF)
    off = CONV_HALO - (CONV_WIDTH - 1)

    for rho in range(1, SUBLANES):
        xsh_s[rho - 1] = ext_s[rho:rho + ext_rows, :]

    for r0 in range(0, tm, CONV_ROWS):
        acc = jnp.zeros((CONV_ROWS, cc), _F32)
        for j in range(CONV_WIDTH):
            a, rho = divmod(off + j, SUBLANES)
            lo_row = r0 + a * SUBLANES
            if rho == 0:
                xv = ext_s[lo_row:lo_row + CONV_ROWS, :]
            else:
                xv = xsh_s[rho - 1, lo_row:lo_row + CONV_ROWS, :]
            acc = acc + cw_ref[j:j + 1, :] * xv
        cv_s[r0:r0 + CONV_ROWS, :] = acc

    ext_s[0:CONV_HALO, :] = ext_s[tm:ext_rows, :]

    cv = cv_s[...] + cb_ref[...]
    mu = jnp.mean(cv, axis=-1, keepdims=True)
    xc = cv - mu
    var = jnp.mean(xc * xc, axis=-1, keepdims=True)
    ln = xc * lax.rsqrt(var + EPS) * lng_ref[...] + lnb_ref[...]
    c_ref[...] = (ln * _sigmoid(ln)).astype(_BF)


def _inproj(x2, norm_mix, w_in_bf, b_conv_in, b_gate, conv_w, conv_b, ln_g, ln_b, qk, aw, cc, seq_len):
    n, d = x2.shape
    tm = TM_INPROJ
    ng = b_gate.shape[-1]
    kern = functools.partial(_inproj_kernel, tiles_per_seq=seq_len // tm)
    return pl.pallas_call(
        kern,
        grid=(n // tm,),
        in_specs=[
            pl.BlockSpec((tm, d), lambda i: (i, 0)),
            _const_spec((1, d)),
            pl.BlockSpec(w_in_bf.shape, lambda i: (0, 0), pipeline_mode=pl.Buffered(1)),
            _const_spec((1, 2 * cc)),
            _const_spec((1, ng)),
            _const_spec((CONV_HALO, cc)),
            _const_spec((1, cc)),
            _const_spec((1, cc)),
            _const_spec((1, cc)),
        ],
        out_specs=[
            pl.BlockSpec((tm, qk), lambda i: (i, 0)),
            pl.BlockSpec((tm, qk), lambda i: (i, 0)),
            pl.BlockSpec((tm, aw), lambda i: (i, 0)),
            pl.BlockSpec((tm, cc), lambda i: (i, 0)),
            pl.BlockSpec((tm, ng), lambda i: (i, 0)),
        ],
        out_shape=[
            jax.ShapeDtypeStruct((n, qk), _BF),
            jax.ShapeDtypeStruct((n, qk), _BF),
            jax.ShapeDtypeStruct((n, aw), _BF),
            jax.ShapeDtypeStruct((n, cc), _BF),
            jax.ShapeDtypeStruct((n, ng), _BF),
        ],
        scratch_shapes=[
            pltpu.VMEM((CONV_HALO + tm + SUBLANES, cc), _F32),
            pltpu.VMEM((SUBLANES - 1, CONV_HALO + tm, cc), _F32),
            pltpu.VMEM((tm, cc), _F32),
        ],
        compiler_params=pltpu.CompilerParams(
            dimension_semantics=("arbitrary",), vmem_limit_bytes=VMEM_LIMIT),
        name="inproj",
    )(x2, norm_mix.reshape(1, d), w_in_bf, b_conv_in.reshape(1, -1), b_gate.reshape(1, -1), conv_w,
      conv_b.reshape(1, cc), ln_g.reshape(1, cc), ln_b.reshape(1, cc))


def _half_rmsnorm(x, gain_row, lo_mask):
    x2 = x * x
    s_lo = jnp.sum(jnp.where(lo_mask, x2, 0.0), axis=-1, keepdims=True)
    s_hi = jnp.sum(jnp.where(lo_mask, 0.0, x2), axis=-1, keepdims=True)
    r = jnp.where(lo_mask, lax.rsqrt(s_lo * (1.0 / HEAD_DIM) + EPS), lax.rsqrt(s_hi * (1.0 / HEAD_DIM) + EPS))
    return x * r * gain_row


def _attn_kernel(lam_ref, q_ref, k_ref, v_ref, qn_ref, kn_ref, sub_ref, o_ref, kn_s, va_s, qs_s, m_s, acc_s,
                 sa_s, sb_s, sc_s):
    s_len = q_ref.shape[1]
    tk = ATTN_TK
    tq = 2 * tk
    nq = s_len // tq
    lane = lax.broadcasted_iota(jnp.int32, (1, V_DIM), 1)
    lo_mask = lane < HEAD_DIM

    lv = lam_ref[...]
    d1 = jnp.sum(lv[0:1, :] * lv[1:2, :], axis=-1, keepdims=True)
    d2 = jnp.sum(lv[2:3, :] * lv[3:4, :], axis=-1, keepdims=True)
    lam = jnp.exp(d1) - jnp.exp(d2) + LAMBDA_INIT

    kgain = kn_ref[...]
    qgain = qn_ref[...] * (math.log2(math.e) / math.sqrt(HEAD_DIM))

    def prep_kv(i, carry):
        r0 = pl.multiple_of(i * tq, tq)
        kk = k_ref[0, pl.ds(r0, tq), :].astype(_F32)
        kn_s[pl.ds(r0, tq), :] = _half_rmsnorm(kk, kgain, lo_mask).astype(_BF)
        va_s[pl.ds(r0, tq), 0:V_DIM] = v_ref[0, pl.ds(r0, tq), :]
        va_s[pl.ds(r0, tq), V_DIM:2 * V_DIM] = jnp.ones((tq, V_DIM), _BF)
        return carry

    lax.fori_loop(0, nq, prep_kv, 0)

    def scores(c0, ra, nr):
        kc = kn_s[pl.ds(c0, tk), :]
        return lax.dot_general(qs_s[ra:ra + nr, :], kc, (((1,), (1,)), ((), ())), preferred_element_type=_F32)

    def consume(s, c0, ra, nr, n_masked):
        vc = va_s[pl.ds(c0, tk), :]
        if n_masked:
            keep = (lax.broadcasted_iota(jnp.int32, (tk, tk), 1) <= lax.broadcasted_iota(jnp.int32, (tk, tk), 0))
            groups = [s[g * tk:(g + 1) * tk, :] for g in range(nr // tk)]
            groups = [jnp.where(keep, sg, NEG_BIG) if g < n_masked else sg for g, sg in enumerate(groups)]
            s = jnp.concatenate(groups, axis=0)
        m_old = m_s[ra:ra + nr, :]
        m_new = jnp.maximum(m_old, jnp.max(s, axis=-1, keepdims=True))
        alpha = jnp.exp2(m_old - m_new)
        p = jnp.concatenate(
            [jnp.exp2(s[:, c * LANES:(c + 1) * LANES] - m_new) for c in range(tk // LANES)], axis=1)
        pv = jnp.dot(p.astype(_BF), vc, preferred_element_type=_F32)
        acc_s[ra:ra + nr, :] = jnp.concatenate([alpha, alpha], axis=1) * acc_s[ra:ra + nr, :] + pv
        m_s[ra:ra + nr, :] = m_new

    def prep_q(r0):
        for hf in range(2):
            qq = _half_rmsnorm(q_ref[0, pl.ds(r0 + hf * tk, tk), :].astype(_F32), qgain, lo_mask)
            qs_s[(2 * hf) * tk:(2 * hf + 1) * tk, :] = jnp.where(lo_mask, qq, 0.0).astype(_BF)
            qs_s[(2 * hf + 1) * tk:(2 * hf + 2) * tk, :] = jnp.where(lo_mask, 0.0, qq).astype(_BF)

    bufs = (sa_s, sb_s, sc_s)
    prep_q(0)
    sa_s[...] = scores(0, 0, 4 * tk)

    for qi in range(nq):
        r0 = qi * tq
        cur = bufs[(2 * qi) % 3]
        nxt = bufs[(2 * qi + 1) % 3]
        ahead = bufs[(2 * qi + 2) % 3]
        m_s[...] = jnp.full(m_s.shape, NEG_BIG, _F32)
        acc_s[...] = jnp.zeros(acc_s.shape, _F32)

        def two_chunks(t, c, cur=cur, nxt=nxt):
            c0 = pl.multiple_of(2 * t * tk, tk)
            c1 = pl.multiple_of(c0 + tk, tk)
            c2 = pl.multiple_of(c0 + 2 * tk, tk)
            nxt[...] = scores(c1, 0, 4 * tk)
            consume(cur[...], c0, 0, 4 * tk, 0)
            cur[...] = scores(c2, 0, 4 * tk)
            consume(nxt[...], c1, 0, 4 * tk, 0)
            return c

        lax.fori_loop(0, qi, two_chunks, 0)
        nxt[0:2 * tk, :] = scores(r0 + tk, 2 * tk, 2 * tk)
        if qi + 1 < nq:
            prep_q(r0 + tq)
            ahead[...] = scores(0, 0, 4 * tk)
        consume(cur[...], r0, 0, 4 * tk, 2)
        consume(nxt[0:2 * tk, :], r0 + tk, 2 * tk, 2 * tk, 2)

        acc = acc_s[...]
        o = acc[:, 0:V_DIM] / acc[:, V_DIM:2 * V_DIM]
        for hf in range(2):
            a = o[(2 * hf) * tk:(2 * hf + 1) * tk, :] - lam * o[(2 * hf + 1) * tk:(2 * hf + 2) * tk, :]
            a = a * lax.rsqrt(jnp.mean(a * a, axis=-1, keepdims=True) + EPS) * sub_ref[...] * (1.0 - LAMBDA_INIT)
            o_ref[0, r0 + hf * tk:r0 + (hf + 1) * tk, :] = a.astype(_BF)


def _attention(lam_vecs, q3, k3, v3, q_norm, k_norm, subln):
    b, s, _ = q3.shape
    tq = 2 * ATTN_TK
    slab = lambda bi, hi: (bi, 0, hi)
    return pl.pallas_call(
        _attn_kernel,
        grid=(b, ATTN_HEADS),
        in_specs=[
            _const_spec((4, HEAD_DIM)),
            pl.BlockSpec((1, s, V_DIM), slab),
            pl.BlockSpec((1, s, V_DIM), slab),
            pl.BlockSpec((1, s, V_DIM), slab),
            _const_spec((1, V_DIM)),
            _const_spec((1, V_DIM)),
            _const_spec((1, V_DIM)),
        ],
        out_specs=pl.BlockSpec((1, s, V_DIM), slab),
        out_shape=jax.ShapeDtypeStruct((b, s, ATTN_HEADS * V_DIM), _BF),
        scratch_shapes=[
            pltpu.VMEM((s, V_DIM), _BF),
            pltpu.VMEM((s, 2 * V_DIM), _BF),
            pltpu.VMEM((2 * tq, V_DIM), _BF),
            pltpu.VMEM((2 * tq, V_DIM), _F32),
            pltpu.VMEM((2 * tq, 2 * V_DIM), _F32),
            pltpu.VMEM((2 * tq, ATTN_TK), _F32),
            pltpu.VMEM((2 * tq, ATTN_TK), _F32),
            pltpu.VMEM((2 * tq, ATTN_TK), _F32),
        ],
        compiler_params=pltpu.CompilerParams(
            dimension_semantics=("arbitrary", "arbitrary"), vmem_limit_bytes=VMEM_LIMIT),
        name="diffattn",
    )(lam_vecs, q3, k3, v3, q_norm.reshape(1, V_DIM), k_norm.reshape(1, V_DIM), subln.reshape(1, V_DIM))


def _router_logits(t, wr_ref, wrh_ref, br_ref):
    t_hi = t.astype(_BF)
    t_lo = (t - t_hi.astype(_F32)).astype(_BF)
    l_hi = jnp.dot(t_hi, wr_ref[...], preferred_element_type=_F32)
    l_lo = jnp.dot(t_lo, wrh_ref[...], preferred_element_type=_F32)
    return l_hi[:, 0:LANES] + l_hi[:, LANES:2 * LANES] + l_lo + br_ref[...]


def _mix_kernel(x_ref, a_ref, c_ref, g_ref, wa_ref, wc_ref, wo_ref, bco_ref, nf_ref, wr_ref, wrh_ref, br_ref,
                h_ref, t_ref, key_ref, cnt_ref, run_s, lg_s, tw_s):
    i = pl.program_id(0)
    tm, d = x_ref.shape
    half_groups = d // LANES // 2

    @pl.when(i == 0)
    def _():
        run_s[...] = jnp.zeros(run_s.shape, _F32)
        lg_s[1] = jnp.zeros(lg_s.shape[1:], _F32)
        tw_s[1] = jnp.zeros(tw_s.shape[1:], jnp.int32)

    def tile_matmuls(sl):
        y_b = jnp.dot(c_ref[...], wc_ref[...], preferred_element_type=_F32) + bco_ref[...]
        y_a = jnp.dot(a_ref[...], wa_ref[...], preferred_element_type=_F32)
        merged = g_ref[:, 0:d].astype(_F32) * y_a + g_ref[:, d:2 * d].astype(_F32) * y_b
        h = x_ref[...] + jnp.dot(merged.astype(_BF), wo_ref[...], preferred_element_type=_F32)
        h_ref[...] = h
        t = h * lax.rsqrt(jnp.mean(h * h, axis=-1, keepdims=True) + EPS) * nf_ref[...]
        for sg in range(half_groups):
            hi_bits = lax.bitcast_convert_type(
                t[:, sg * LANES:(sg + 1) * LANES].astype(_BF).astype(_F32), jnp.uint32)
            lo_bits = lax.bitcast_convert_type(
                t[:, (sg + half_groups) * LANES:(sg + half_groups + 1) * LANES].astype(_BF).astype(_F32),
                jnp.uint32)
            tw_s[sl, sg] = lax.bitcast_convert_type(hi_bits | (lo_bits >> 16), jnp.int32)
        lg_s[sl] = _router_logits(t, wr_ref, wrh_ref, br_ref)

    for sl in range(2):
        @pl.when(i % 2 == sl)
        def _(sl=sl):
            tile_matmuls(sl)
            _route_tile(i - 1, lg_s[1 - sl], tw_s.at[1 - sl], t_ref, key_ref, cnt_ref, run_s)


def _route_tile(tile, lg, words_ref, t_ref, key_ref, cnt_ref, run_s):
    tm = lg.shape[0]
    for sg in range(META_SUBLANE):
        t_ref[pl.ds(sg, tm, stride=SUBLANES), :] = words_ref[sg]
    lane = lax.broadcasted_iota(jnp.int32, (tm, LANES), 1)
    gl = jnp.where(lane < N_GROUPS, lg, NEG_BIG)
    gmax = jnp.max(gl, axis=-1, keepdims=True)
    gidx = jnp.min(jnp.where(gl == gmax, lane, LANES), axis=-1, keepdims=True)
    gsum = jnp.sum(jnp.where(lane < N_GROUPS, jnp.exp(gl - gmax), 0.0), axis=-1, keepdims=True)
    g_gate = 1.0 / gsum
    e_lo = N_GROUPS + EXPERTS_PER_GROUP * gidx
    el = jnp.where((lane >= e_lo) & (lane < e_lo + EXPERTS_PER_GROUP), lg, NEG_BIG)
    m1 = jnp.max(el, axis=-1, keepdims=True)
    i1 = jnp.min(jnp.where(el == m1, lane, LANES), axis=-1, keepdims=True)
    el2 = jnp.where(lane == i1, NEG_BIG, el)
    m2 = jnp.max(el2, axis=-1, keepdims=True)
    i2 = jnp.min(jnp.where(el2 == m2, lane, LANES), axis=-1, keepdims=True)
    e21 = jnp.exp(m2 - m1)
    w_top1 = g_gate * (1.0 / (1.0 + e21))
    w_top2 = g_gate * (e21 / (1.0 + e21))
    a1 = i1 - e_lo
    a2 = i2 - e_lo
    lo = jnp.minimum(a1, a2)
    hi = jnp.maximum(a1, a2)
    w_lo = jnp.where(a1 < a2, w_top1, w_top2)
    w_hi = jnp.where(a1 < a2, w_top2, w_top1)
    pidx = ((lo * (2 * EXPERTS_PER_GROUP - 1 - lo)) >> 1) + (hi - lo - 1)
    cls = gidx * PAIRS_PER_GROUP + pidx
    token = tile * tm + lax.broadcasted_iota(jnp.int32, (tm, LANES), 0)
    meta = jnp.where(lane == META_W_LO, lax.bitcast_convert_type(w_lo, jnp.int32),
                     jnp.where(lane == META_W_HI, lax.bitcast_convert_type(w_hi, jnp.int32),
                               jnp.where(lane == META_TOKEN, token, 0)))
    t_ref[pl.ds(META_SUBLANE, tm, stride=SUBLANES), :] = meta
    for sg in range(META_SUBLANE + 1, SUBLANES):
        t_ref[pl.ds(sg, tm, stride=SUBLANES), :] = jnp.zeros((tm, LANES), jnp.int32)

    onehot = lane == cls
    oh_bf = jnp.where(onehot, 1.0, 0.0).astype(_BF)
    rr = lax.broadcasted_iota(jnp.int32, (tm, tm), 0)
    ccol = lax.broadcasted_iota(jnp.int32, (tm, tm), 1)
    tri = jnp.where(ccol < rr, 1.0, 0.0).astype(_BF)
    before = jnp.dot(tri, oh_bf, preferred_element_type=_F32) + run_s[...]
    rank = jnp.sum(jnp.where(onehot, before, 0.0), axis=-1, keepdims=True)
    counted = jnp.where(tile >= 0, 1.0, 0.0)
    run_new = run_s[...] + counted * jnp.sum(jnp.where(onehot, 1.0, 0.0), axis=0, keepdims=True)
    run_s[...] = run_new
    cnt_ref[...] = run_new
    keyf = jnp.broadcast_to(cls.astype(_F32) * float(1 << RANK_BITS) + rank, (tm, LANES))
    for g in range(tm // LANES):
        kt = keyf[g * LANES:(g + 1) * LANES, :].T
        key_ref[0, g:g + 1, :] = kt[0:1, :].astype(jnp.int32)


def _mix(x2, attn2, cact2, gates2, wa_bf, wc_bf, wo_bf, b_conv_out, norm_ffn, w_router_cat, w_router_hi, b_router):
    n, d = x2.shape
    tm = TM_MIX
    aw = attn2.shape[1]
    cc = cact2.shape[1]
    nt = n // tm
    row = lambda i: (jnp.minimum(i, nt - 1), 0)
    prev = lambda i: (jnp.maximum(i - 1, 0), 0)
    return pl.pallas_call(
        _mix_kernel,
        grid=(nt + 1,),
        in_specs=[
            pl.BlockSpec((tm, d), row),
            pl.BlockSpec((tm, aw), row),
            pl.BlockSpec((tm, cc), row),
            pl.BlockSpec((tm, 2 * d), row),
            _const_spec(wa_bf.shape),
            _const_spec(wc_bf.shape),
            _const_spec(wo_bf.shape),
            _const_spec((1, d)),
            _const_spec((1, d)),
            _const_spec((d, 2 * LANES)),
            _const_spec((d, LANES)),
            _const_spec((1, LANES)),
        ],
        out_specs=[
            pl.BlockSpec((tm, d), row),
            pl.BlockSpec((tm * SUBLANES, LANES), prev),
            pl.BlockSpec((1, tm // LANES, LANES), lambda i: (jnp.maximum(i - 1, 0), 0, 0)),
            _const_spec((1, LANES)),
        ],
        out_shape=[
            jax.ShapeDtypeStruct((n, d), _F32),
            jax.ShapeDtypeStruct((n * SUBLANES, LANES), jnp.int32),
            jax.ShapeDtypeStruct((n // tm, tm // LANES, LANES), jnp.int32),
            jax.ShapeDtypeStruct((1, LANES), _F32),
        ],
        scratch_shapes=[
            pltpu.VMEM((1, LANES), _F32),
            pltpu.VMEM((2, tm, LANES), _F32),
            pltpu.VMEM((2, META_SUBLANE, tm, LANES), jnp.int32),
        ],
        compiler_params=pltpu.CompilerParams(
            dimension_semantics=("arbitrary",), vmem_limit_bytes=VMEM_LIMIT),
        name="mix_router",
    )(x2, attn2, cact2, gates2, wa_bf, wc_bf, wo_bf, b_conv_out.reshape(1, d), norm_ffn.reshape(1, d),
      w_router_cat, w_router_hi, b_router)


def _sorted_rows_kernel(pstart_ref, key_ref, o_ref):
    key = key_ref[...]
    cls = key >> RANK_BITS
    row = key & ((1 << RANK_BITS) - 1)
    for c in range(N_CLASSES):
        row = row + jnp.where(cls == c, pstart_ref[c], 0)
    o_ref[...] = row


def _sorted_rows(pstart, keys2):
    grid_spec = pltpu.PrefetchScalarGridSpec(
        num_scalar_prefetch=1,
        grid=(1,),
        in_specs=[pl.BlockSpec(keys2.shape, lambda i, *_: (0, 0))],
        out_specs=pl.BlockSpec(keys2.shape, lambda i, *_: (0, 0)),
    )
    return pl.pallas_call(
        _sorted_rows_kernel,
        grid_spec=grid_spec,
        out_shape=jax.ShapeDtypeStruct(keys2.shape, jnp.int32),
        name="sorted_rows",
    )(pstart, keys2)


def _token_rows(idx, count=1):
    return pl.ds(pl.multiple_of(idx * SUBLANES, SUBLANES), count * SUBLANES)


def _scatter_kernel(padstart_ref, padlen_ref, nvalid_ref, dest_ref, t_ref, xs_ref, zero_s, sem, zsem):
    i = pl.program_id(0)
    ts = t_ref.shape[0] // SUBLANES
    n_tokens = ts * pl.num_programs(0)
    bm = zero_s.shape[0] // SUBLANES
    nblk = xs_ref.shape[0] // (bm * SUBLANES)

    def start_rows(g, c):
        for u in range(DMA_UNROLL):
            r = g * DMA_UNROLL + u
            pltpu.make_async_copy(t_ref.at[_token_rows(r)], xs_ref.at[_token_rows(dest_ref[0, 0, r])], sem).start(
                priority=u % 2)
        return c

    def wait_rows(g, c):
        for _ in range(DMA_UNROLL):
            pltpu.make_async_copy(t_ref.at[_token_rows(0)], xs_ref.at[_token_rows(0)], sem).wait()
        return c

    lax.fori_loop(0, ts // DMA_UNROLL, start_rows, 0)

    @pl.when(i == 0)
    def _():
        rows = lax.broadcasted_iota(jnp.int32, zero_s.shape, 0)
        lanes = lax.broadcasted_iota(jnp.int32, zero_s.shape, 1)
        is_token = ((rows & (SUBLANES - 1)) == META_SUBLANE) & (lanes == META_TOKEN)
        zero_s[...] = jnp.where(is_token, n_tokens + (rows >> (SUBLANES.bit_length() - 1)), 0)

        def pad_copies(c, act):
            ln = padlen_ref[c]
            st = padstart_ref[c]
            for k in range(bm.bit_length() - 1):
                sz = 1 << k

                @pl.when(((ln >> k) & 1) == 1)
                def _():
                    dst = st + (ln & (sz - 1))
                    src = dst & (bm - 1)
                    act(pltpu.make_async_copy(zero_s.at[_token_rows(src, sz)], xs_ref.at[_token_rows(dst, sz)], zsem))

        def tail_copy(j, act):
            @pl.when(j >= nvalid_ref[0])
            def _():
                act(pltpu.make_async_copy(zero_s, xs_ref.at[_token_rows(j * bm, bm)], zsem))

        for act in (lambda cp: cp.start(), lambda cp: cp.wait()):
            lax.fori_loop(0, N_CLASSES, lambda c, carry, act=act: (pad_copies(c, act), carry)[1], 0)
            lax.fori_loop(0, nblk, lambda j, carry, act=act: (tail_copy(j, act), carry)[1], 0)

    lax.fori_loop(0, ts // DMA_UNROLL, wait_rows, 0)


def _scatter_rows(padstart, padlen, nvalid, dest, t_rows, n_rows):
    n = t_rows.shape[0] // SUBLANES
    ts = TS_SCATTER
    grid_spec = pltpu.PrefetchScalarGridSpec(
        num_scalar_prefetch=3,
        grid=(n // ts,),
        in_specs=[
            pl.BlockSpec((1, 1, ts), lambda i, *_: (i, 0, 0), memory_space=pltpu.SMEM),
            pl.BlockSpec((ts * SUBLANES, LANES), lambda i, *_: (i, 0)),
        ],
        out_specs=pl.BlockSpec(memory_space=pl.ANY),
        scratch_shapes=[pltpu.VMEM((BM_MOE * SUBLANES, LANES), jnp.int32), pltpu.SemaphoreType.DMA,
                        pltpu.SemaphoreType.DMA],
    )
    return pl.pallas_call(
        _scatter_kernel,
        grid_spec=grid_spec,
        out_shape=jax.ShapeDtypeStruct((n_rows * SUBLANES, LANES), jnp.int32),
        compiler_params=pltpu.CompilerParams(
            dimension_semantics=("arbitrary",), vmem_limit_bytes=VMEM_LIMIT),
        name="scatter_rows",
    )(padstart, padlen, nvalid, dest.reshape(n // ts, 1, ts), t_rows)


def _moe_kernel(ea_ref, eb_ref, valid_ref, xs_ref, w1a, w3a, w2a, w1b, w3b, w2b, out_ref,
                y_s, ids_v, ids_sm, rsem, isem):
    j = pl.program_id(0)
    nb = pl.num_programs(0)
    bm = xs_ref.shape[0] // SUBLANES
    ng = bm // LANES
    n_tokens = out_ref.shape[0] // SUBLANES - bm

    def ids_copy(sl):
        return pltpu.make_async_copy(ids_v.at[sl], ids_sm.at[sl], isem.at[sl])

    def row_copy(sl, r, tok):
        return pltpu.make_async_copy(y_s.at[sl, _token_rows(r)], out_ref.at[_token_rows(tok)], rsem.at[sl])

    def issue_rows(sl):
        for r in range(bm):
            row_copy(sl, r, ids_sm[sl, r // LANES, r % LANES]).start(priority=1)

    def wait_rows(sl):
        def body(g, c):
            for _ in range(DMA_UNROLL):
                row_copy(sl, 0, 0).wait()
            return c

        lax.fori_loop(0, bm // DMA_UNROLL, body, 0)

    @pl.when(j == 0)
    def _():
        y_s[1] = jnp.zeros(y_s.shape[1:], _F32)
        pos = (lax.broadcasted_iota(jnp.int32, (ng, LANES), 0) * LANES
               + lax.broadcasted_iota(jnp.int32, (ng, LANES), 1))
        ids_v[1] = n_tokens + pos
        ids_copy(1).start()

    def step(sl, is_valid):
        other = 1 - sl
        ids_copy(other).wait()

        @pl.when(j >= 1)
        def _():
            wait_rows(sl)

        issue_rows(other)

        meta = xs_ref[pl.ds(META_SUBLANE, bm, stride=SUBLANES), :]
        lane = lax.broadcasted_iota(jnp.int32, (bm, LANES), 1)
        tok = jnp.sum(jnp.where(lane == META_TOKEN, meta.astype(_F32), 0.0), axis=-1, keepdims=True)
        tokb = jnp.broadcast_to(tok, (bm, LANES))
        for g in range(ng):
            ids_v[sl, g:g + 1, :] = tokb[g * LANES:(g + 1) * LANES, :].T[0:1, :].astype(jnp.int32)
        ids_copy(sl).start()

        if is_valid:
            words = [xs_ref[pl.ds(sg, bm, stride=SUBLANES), :] for sg in range(META_SUBLANE)]
            hi = [lax.bitcast_convert_type(w & jnp.int32(-65536), _F32).astype(_BF) for w in words]
            lo = [lax.bitcast_convert_type(w << 16, _F32).astype(_BF) for w in words]
            xb = jnp.concatenate(hi + lo, axis=1)
            metaf = lax.bitcast_convert_type(meta, _F32)
            w_a = jnp.sum(jnp.where(lane == META_W_LO, metaf, 0.0), axis=-1, keepdims=True)
            w_b = jnp.sum(jnp.where(lane == META_W_HI, metaf, 0.0), axis=-1, keepdims=True)

            def expert(w1, w3, w2):
                h1 = jnp.dot(xb, w1[0], preferred_element_type=_F32)
                h3 = jnp.dot(xb, w3[0], preferred_element_type=_F32)
                hdn = h1 * _sigmoid(h1) * h3
                return jnp.dot(hdn.astype(_BF), w2[0], preferred_element_type=_F32)

            y = w_a * expert(w1a, w3a, w2a) + w_b * expert(w1b, w3b, w2b)
            for sg in range(SUBLANES):
                y_s[sl, pl.ds(sg, bm, stride=SUBLANES), :] = y[:, sg * LANES:(sg + 1) * LANES]
        else:
            y_s[sl] = jnp.zeros(y_s.shape[1:], _F32)

        @pl.when(j == nb - 1)
        def _():
            wait_rows(other)
            ids_copy(sl).wait()
            issue_rows(sl)
            wait_rows(sl)

    for sl in range(2):
        for is_valid in (True, False):
            @pl.when((j % 2 == sl) & ((valid_ref[j] != 0) == is_valid))
            def _(sl=sl, is_valid=is_valid):
                step(sl, is_valid)


def _moe(ea, eb, valid, xs, w1_bf, w3_bf, w2_bf, n_tokens):
    d, ff = w1_bf.shape[1], w1_bf.shape[2]
    assert d == SUBLANES * LANES
    bm = BM_MOE
    nb = xs.shape[0] // (bm * SUBLANES)
    grid_spec = pltpu.PrefetchScalarGridSpec(
        num_scalar_prefetch=3,
        grid=(nb,),
        in_specs=[
            pl.BlockSpec((bm * SUBLANES, LANES), lambda j, ea, eb, va: (j, 0)),
            pl.BlockSpec((1, d, ff), lambda j, ea, eb, va: (ea[j], 0, 0)),
            pl.BlockSpec((1, d, ff), lambda j, ea, eb, va: (ea[j], 0, 0)),
            pl.BlockSpec((1, ff, d), lambda j, ea, eb, va: (ea[j], 0, 0)),
            pl.BlockSpec((1, d, ff), lambda j, ea, eb, va: (eb[j], 0, 0)),
            pl.BlockSpec((1, d, ff), lambda j, ea, eb, va: (eb[j], 0, 0)),
            pl.BlockSpec((1, ff, d), lambda j, ea, eb, va: (eb[j], 0, 0)),
        ],
        out_specs=pl.BlockSpec(memory_space=pl.ANY),
        scratch_shapes=[
            pltpu.VMEM((2, bm * SUBLANES, LANES), _F32),
            pltpu.VMEM((2, bm // LANES, LANES), jnp.int32),
            pltpu.SMEM((2, bm // LANES, LANES), jnp.int32),
            pltpu.SemaphoreType.DMA((2,)),
            pltpu.SemaphoreType.DMA((2,)),
        ],
    )
    return pl.pallas_call(
        _moe_kernel,
        grid_spec=grid_spec,
        out_shape=jax.ShapeDtypeStruct(((n_tokens + bm) * SUBLANES, LANES), _F32),
        compiler_params=pltpu.CompilerParams(
            dimension_semantics=("arbitrary",), vmem_limit_bytes=VMEM_LIMIT),
        name="moe_experts",
    )(ea, eb, valid, xs, w1_bf, w3_bf, w2_bf, w1_bf, w3_bf, w2_bf)


def _final_kernel(h_ref, p_ref, y_ref, np_ref, wg_ref, bg_ref, wp_ref, o_ref):
    tm = h_ref.shape[0]
    proj = jnp.dot(p_ref[...].astype(_BF), wp_ref[...], preferred_element_type=_F32)
    y = jnp.concatenate([y_ref[pl.ds(sg, tm, stride=SUBLANES), :] for sg in range(SUBLANES)], axis=1)
    h = h_ref[...] + y
    u = h * lax.rsqrt(jnp.mean(h * h, axis=-1, keepdims=True) + EPS) * np_ref[...]
    gate = _sigmoid(jnp.dot(u.astype(_BF), wg_ref[...], preferred_element_type=_F32) + bg_ref[...])
    o_ref[...] = h + gate * proj


def _final(h1, p2, y_rows, norm_ple, wg_bf, b_ple_gate, wp_bf):
    n, d = h1.shape
    assert d == SUBLANES * LANES
    tm = TM_FINAL
    pd = p2.shape[1]
    row = lambda i: (i, 0)
    return pl.pallas_call(
        _final_kernel,
        grid=(n // tm,),
        in_specs=[
            pl.BlockSpec((tm, d), row),
            pl.BlockSpec((tm, pd), row),
            pl.BlockSpec((tm * SUBLANES, LANES), row),
            _const_spec((1, d)),
            _const_spec(wg_bf.shape),
            _const_spec((1, d)),
            _const_spec(wp_bf.shape),
        ],
        out_specs=pl.BlockSpec((tm, d), row),
        out_shape=jax.ShapeDtypeStruct((n, d), _F32),
        compiler_params=pltpu.CompilerParams(
            dimension_semantics=("arbitrary",), vmem_limit_bytes=VMEM_LIMIT),
        name="final_ple",
    )(h1, p2, y_rows, norm_ple.reshape(1, d), wg_bf, b_ple_gate.reshape(1, d), wp_bf)


def _class_tables():
    ea, eb = [], []
    for g in range(N_GROUPS):
        for lo in range(EXPERTS_PER_GROUP):
            for hi in range(lo + 1, EXPERTS_PER_GROUP):
                ea.append(g * EXPERTS_PER_GROUP + lo)
                eb.append(g * EXPERTS_PER_GROUP + hi)
    return np.asarray(ea, np.int32), np.asarray(eb, np.int32)


_CLASS_EA, _CLASS_EB = _class_tables()


def kernel(x, p, norm_mix, w_in, b_conv_in, b_gate, q_norm, k_norm, lambda_q1, lambda_k1, lambda_q2, lambda_k2, subln, w_attn_out, conv_w, conv_b, conv_ln_g, conv_ln_b, w_conv_out, b_conv_out, w_o, norm_ffn, w_router_group, b_router_group, w_router_expert, b_router_expert, w1, w3, w2, norm_ple, w_ple_gate, b_ple_gate, w_ple_proj):
    b, s, d = x.shape
    n = b * s
    depth = w_in.shape[0]
    qk = ATTN_HEADS * 2 * HEAD_DIM
    aw = ATTN_HEADS * V_DIM
    cc = conv_w.shape[-1]
    assert depth == 1 and n % TM_MIX == 0 and s % (2 * ATTN_TK) == 0 and s % TM_INPROJ == 0
    assert n % TS_SCATTER == 0 and n % TM_FINAL == 0 and n % BM_MOE == 0
    i = 0
    h = x.reshape(n, d)

    conv_w_pad = jnp.pad(conv_w[i], ((0, CONV_HALO - CONV_WIDTH), (0, 0)))
    q2, k2, v2, cact2, gates2 = _inproj(h, norm_mix[i], w_in[i].astype(_BF), b_conv_in[i], b_gate[i], conv_w_pad,
                                        conv_b[i], conv_ln_g[i], conv_ln_b[i], qk, aw, cc, s)

    lam_vecs = jnp.stack([lambda_q1[i], lambda_k1[i], lambda_q2[i], lambda_k2[i]]).astype(_F32)
    attn = _attention(lam_vecs, q2.reshape(b, s, qk), k2.reshape(b, s, qk), v2.reshape(b, s, aw),
                      q_norm[i], k_norm[i], subln[i])

    w_router = jnp.concatenate([w_router_group[i], w_router_expert[i]], axis=1)
    w_router = jnp.pad(w_router, ((0, 0), (0, LANES - w_router.shape[1])))
    b_router = jnp.concatenate([b_router_group[i], b_router_expert[i]])
    b_router = jnp.pad(b_router, (0, LANES - b_router.shape[0])).reshape(1, LANES)
    w_router_hi = w_router.astype(_BF)
    w_router_lo = (w_router - w_router_hi.astype(_F32)).astype(_BF)
    w_router_cat = jnp.concatenate([w_router_hi, w_router_lo], axis=1)
    h1, t_rows, keys, cnt = _mix(h, attn.reshape(n, aw), cact2, gates2, w_attn_out[i].astype(_BF),
                                 w_conv_out[i].astype(_BF), w_o[i].astype(_BF), b_conv_out[i], norm_ffn[i],
                                 w_router_cat, w_router_hi, b_router)

    bm = BM_MOE
    nb = n // bm + N_CLASSES
    counts = cnt[0].astype(jnp.int32)
    padded = ((counts + bm - 1) // bm) * bm
    pend = jnp.cumsum(padded)
    pstart = (pend - padded).astype(jnp.int32)
    total = pend[-1]
    blk = jnp.arange(nb, dtype=jnp.int32) * bm
    valid = blk < total
    first_row = jnp.where(valid, blk, total - 1)
    bcls = jnp.sum((pend[None, :] <= first_row[:, None]).astype(jnp.int32), axis=1)
    bcls = jnp.minimum(bcls, N_CLASSES - 1)
    cls_onehot = (bcls[:, None] == jnp.arange(N_CLASSES, dtype=jnp.int32)[None, :]).astype(jnp.int32)
    ea = jnp.sum(cls_onehot * jnp.asarray(_CLASS_EA)[None, :], axis=1)
    eb = jnp.sum(cls_onehot * jnp.asarray(_CLASS_EB)[None, :], axis=1)
    nvalid = (total // bm).astype(jnp.int32).reshape(1)

    dest = _sorted_rows(pstart, keys.reshape(n // LANES, LANES))
    xs = _scatter_rows((pstart + counts).astype(jnp.int32), (padded - counts).astype(jnp.int32), nvalid,
                       dest, t_rows, nb * bm)
    y_rows = _moe(ea, eb, valid.astype(jnp.int32), xs, w1[i].astype(_BF), w3[i].astype(_BF), w2[i].astype(_BF), n)

    out = _final(h1, p[i].reshape(n, -1), y_rows, norm_ple[i], w_ple_gate[i].astype(_BF), b_ple_gate[i],
                 w_ple_proj[i].astype(_BF))
    return out.reshape(b, s, d)
```

```python
import functools
import math

import numpy as np
import jax
import jax.numpy as jnp
from jax import lax
from jax.experimental import pallas as pl
from jax.experimental.pallas import tpu as pltpu

ATTN_HEADS = 4
HEAD_DIM = 64
V_DIM = 2 * HEAD_DIM
CONV_WIDTH = 31
N_GROUPS = 4
EXPERTS_PER_GROUP = 8
N_EXPERTS = N_GROUPS * EXPERTS_PER_GROUP
PAIRS_PER_GROUP = EXPERTS_PER_GROUP * (EXPERTS_PER_GROUP - 1) // 2
N_CLASSES = N_GROUPS * PAIRS_PER_GROUP
EPS = 1e-6
LAMBDA_INIT = 0.8 - 0.6 * math.exp(-0.3 * 0)

LANES = 128
SUBLANES = 8
CONV_HALO = 32
NEG_BIG = -1e30
RANK_BITS = 17
META_SUBLANE = 4
META_W_LO, META_W_HI, META_TOKEN = 0, 1, 2
DMA_UNROLL = 8

TM_INPROJ = 512
ATTN_TK = 512
TM_MIX = 512
CONV_ROWS = 64
TS_SCATTER = 512
BM_MOE = 256
TM_FINAL = 512
VMEM_LIMIT = 56 * 1024 * 1024

_BF = jnp.bfloat16
_F32 = jnp.float32


def _const_spec(shape):
    nd = len(shape)
    return pl.BlockSpec(shape, lambda *_: (0,) * nd)


def _sigmoid(x):
    return 1.0 / (1.0 + jnp.exp(-x))


def _inproj_kernel(x_ref, nm_ref, w_ref, bci_ref, bg_ref, cw_ref, cb_ref, lng_ref, lnb_ref,
                   q_ref, k_ref, v_ref, c_ref, g_ref, ext_s, xsh_s, cv_s, *, tiles_per_seq):
    i = pl.program_id(0)
    tm = x_ref.shape[0]
    qk = q_ref.shape[1]
    aw = v_ref.shape[1]
    cc = c_ref.shape[1]
    ext_rows = CONV_HALO + tm
    o0, o1, o2, o3 = qk, 2 * qk, 2 * qk + aw, 2 * qk + aw + 2 * cc

    @pl.when(i == 0)
    def _():
        ext_s[ext_rows:ext_rows + SUBLANES, :] = jnp.zeros((SUBLANES, cc), _F32)

    @pl.when((i % tiles_per_seq) == 0)
    def _():
        ext_s[0:CONV_HALO, :] = jnp.zeros((CONV_HALO, cc), _F32)

    x = x_ref[...]
    u = x * lax.rsqrt(jnp.mean(x * x, axis=-1, keepdims=True) + EPS) * nm_ref[...]
    ub = u.astype(_BF)
    c = jnp.dot(ub, w_ref[:, o2:o3], preferred_element_type=_F32) + bci_ref[...]
    ext_s[CONV_HALO:ext_rows, :] = c[:, :cc] * _sigmoid(c[:, cc:])
    q_ref[...] = jnp.dot(ub, w_ref[:, 0:o0], preferred_element_type=_F32).astype(_BF)
    k_ref[...] = jnp.dot(ub, w_ref[:, o0:o1], preferred_element_type=_F32).astype(_BF)
    v_ref[...] = jnp.dot(ub, w_ref[:, o1:o2], preferred_element_type=_F32).astype(_BF)
    g = jnp.dot(ub, w_ref[:, o3:], preferred_element_type=_F32) + bg_ref[...]
    g_ref[...] = _sigmoid(g).astype(_BF)
    off = CONV_HALO - (CONV_WIDTH - 1)

    for rho in range(1, SUBLANES):
        xsh_s[rho - 1] = ext_s[rho:rho + ext_rows, :]

    for r0 in range(0, tm, CONV_ROWS):
        acc = jnp.zeros((CONV_ROWS, cc), _F32)
        for j in range(CONV_WIDTH):
            a, rho = divmod(off + j, SUBLANES)
            lo_row = r0 + a * SUBLANES
            if rho == 0:
                xv = ext_s[lo_row:lo_row + CONV_ROWS, :]
            else:
                xv = xsh_s[rho - 1, lo_row:lo_row + CONV_ROWS, :]
            acc = acc + cw_ref[j:j + 1, :] * xv
        cv_s[r0:r0 + CONV_ROWS, :] = acc

    ext_s[0:CONV_HALO, :] = ext_s[tm:ext_rows, :]

    cv = cv_s[...] + cb_ref[...]
    mu = jnp.mean(cv, axis=-1, keepdims=True)
    xc = cv - mu
    var = jnp.mean(xc * xc, axis=-1, keepdims=True)
    ln = xc * lax.rsqrt(var + EPS) * lng_ref[...] + lnb_ref[...]
    c_ref[...] = (ln * _sigmoid(ln)).astype(_BF)


def _inproj(x2, norm_mix, w_in_bf, b_conv_in, b_gate, conv_w, conv_b, ln_g, ln_b, qk, aw, cc, seq_len):
    n, d = x2.shape
    tm = TM_INPROJ
    ng = b_gate.shape[-1]
    kern = functools.partial(_inproj_kernel, tiles_per_seq=seq_len // tm)
    return pl.pallas_call(
        kern,
        grid=(n // tm,),
        in_specs=[
            pl.BlockSpec((tm, d), lambda i: (i, 0)),
            _const_spec((1, d)),
            pl.BlockSpec(w_in_bf.shape, lambda i: (0, 0), pipeline_mode=pl.Buffered(1)),
            _const_spec((1, 2 * cc)),
            _const_spec((1, ng)),
            _const_spec((CONV_HALO, cc)),
            _const_spec((1, cc)),
            _const_spec((1, cc)),
            _const_spec((1, cc)),
        ],
        out_specs=[
            pl.BlockSpec((tm, qk), lambda i: (i, 0)),
            pl.BlockSpec((tm, qk), lambda i: (i, 0)),
            pl.BlockSpec((tm, aw), lambda i: (i, 0)),
            pl.BlockSpec((tm, cc), lambda i: (i, 0)),
            pl.BlockSpec((tm, ng), lambda i: (i, 0)),
        ],
        out_shape=[
            jax.ShapeDtypeStruct((n, qk), _BF),
            jax.ShapeDtypeStruct((n, qk), _BF),
            jax.ShapeDtypeStruct((n, aw), _BF),
            jax.ShapeDtypeStruct((n, cc), _BF),
            jax.ShapeDtypeStruct((n, ng), _BF),
        ],
        scratch_shapes=[
            pltpu.VMEM((CONV_HALO + tm + SUBLANES, cc), _F32),
            pltpu.VMEM((SUBLANES - 1, CONV_HALO + tm, cc), _F32),
            pltpu.VMEM((tm, cc), _F32),
        ],
        compiler_params=pltpu.CompilerParams(
            dimension_semantics=("arbitrary",), vmem_limit_bytes=VMEM_LIMIT),
        name="inproj",
    )(x2, norm_mix.reshape(1, d), w_in_bf, b_conv_in.reshape(1, -1), b_gate.reshape(1, -1), conv_w,
      conv_b.reshape(1, cc), ln_g.reshape(1, cc), ln_b.reshape(1, cc))


def _half_rmsnorm(x, gain_row, lo_mask):
    x2 = x * x
    s_lo = jnp.sum(jnp.where(lo_mask, x2, 0.0), axis=-1, keepdims=True)
    s_hi = jnp.sum(jnp.where(lo_mask, 0.0, x2), axis=-1, keepdims=True)
    r = jnp.where(lo_mask, lax.rsqrt(s_lo * (1.0 / HEAD_DIM) + EPS), lax.rsqrt(s_hi * (1.0 / HEAD_DIM) + EPS))
    return x * r * gain_row


def _attn_kernel(lam_ref, q_ref, k_ref, v_ref, qn_ref, kn_ref, sub_ref, o_ref, kn_s, va_s, qs_s, m_s, acc_s,
                 sa_s, sb_s, sc_s):
    s_len = q_ref.shape[1]
    tk = ATTN_TK
    tq = 2 * tk
    nq = s_len // tq
    lane = lax.broadcasted_iota(jnp.int32, (1, V_DIM), 1)
    lo_mask = lane < HEAD_DIM

    lv = lam_ref[...]
    d1 = jnp.sum(lv[0:1, :] * lv[1:2, :], axis=-1, keepdims=True)
    d2 = jnp.sum(lv[2:3, :] * lv[3:4, :], axis=-1, keepdims=True)
    lam = jnp.exp(d1) - jnp.exp(d2) + LAMBDA_INIT

    kgain = kn_ref[...]
    qgain = qn_ref[...] * (math.log2(math.e) / math.sqrt(HEAD_DIM))

    def prep_kv(i, carry):
        r0 = pl.multiple_of(i * tq, tq)
        kk = k_ref[0, pl.ds(r0, tq), :].astype(_F32)
        kn_s[pl.ds(r0, tq), :] = _half_rmsnorm(kk, kgain, lo_mask).astype(_BF)
        va_s[pl.ds(r0, tq), 0:V_DIM] = v_ref[0, pl.ds(r0, tq), :]
        va_s[pl.ds(r0, tq), V_DIM:2 * V_DIM] = jnp.ones((tq, V_DIM), _BF)
        return carry

    lax.fori_loop(0, nq, prep_kv, 0)

    def scores(c0, ra, nr):
        kc = kn_s[pl.ds(c0, tk), :]
        return lax.dot_general(qs_s[ra:ra + nr, :], kc, (((1,), (1,)), ((), ())), preferred_element_type=_F32)

    def consume(s, c0, ra, nr, n_masked):
        vc = va_s[pl.ds(c0, tk), :]
        if n_masked:
            keep = (lax.broadcasted_iota(jnp.int32, (tk, tk), 1) <= lax.broadcasted_iota(jnp.int32, (tk, tk), 0))
            groups = [s[g * tk:(g + 1) * tk, :] for g in range(nr // tk)]
            groups = [jnp.where(keep, sg, NEG_BIG) if g < n_masked else sg for g, sg in enumerate(groups)]
            s = jnp.concatenate(groups, axis=0)
        m_old = m_s[ra:ra + nr, :]
        m_new = jnp.maximum(m_old, jnp.max(s, axis=-1, keepdims=True))
        alpha = jnp.exp2(m_old - m_new)
        p = jnp.concatenate(
            [jnp.exp2(s[:, c * LANES:(c + 1) * LANES] - m_new) for c in range(tk // LANES)], axis=1)
        pv = jnp.dot(p.astype(_BF), vc, preferred_element_type=_F32)
        acc_s[ra:ra + nr, :] = jnp.concatenate([alpha, alpha], axis=1) * acc_s[ra:ra + nr, :] + pv
        m_s[ra:ra + nr, :] = m_new

    def prep_q(r0):
        for hf in range(2):
            qq = _half_rmsnorm(q_ref[0, pl.ds(r0 + hf * tk, tk), :].astype(_F32), qgain, lo_mask)
            qs_s[(2 * hf) * tk:(2 * hf + 1) * tk, :] = jnp.where(lo_mask, qq, 0.0).astype(_BF)
            qs_s[(2 * hf + 1) * tk:(2 * hf + 2) * tk, :] = jnp.where(lo_mask, 0.0, qq).astype(_BF)

    bufs = (sa_s, sb_s, sc_s)
    prep_q(0)
    sa_s[...] = scores(0, 0, 4 * tk)

    for qi in range(nq):
        r0 = qi * tq
        cur = bufs[(2 * qi) % 3]
        nxt = bufs[(2 * qi + 1) % 3]
        ahead = bufs[(2 * qi + 2) % 3]
        m_s[...] = jnp.full(m_s.shape, NEG_BIG, _F32)
        acc_s[...] = jnp.zeros(acc_s.shape, _F32)

        def two_chunks(t, c, cur=cur, nxt=nxt):
            c0 = pl.multiple_of(2 * t * tk, tk)
            c1 = pl.multiple_of(c0 + tk, tk)
            c2 = pl.multiple_of(c0 + 2 * tk, tk)
            nxt[...] = scores(c1, 0, 4 * tk)
            consume(cur[...], c0, 0, 4 * tk, 0)
            cur[...] = scores(c2, 0, 4 * tk)
            consume(nxt[...], c1, 0, 4 * tk, 0)
            return c

        lax.fori_loop(0, qi, two_chunks, 0)
        nxt[0:2 * tk, :] = scores(r0 + tk, 2 * tk, 2 * tk)
        if qi + 1 < nq:
            prep_q(r0 + tq)
            ahead[...] = scores(0, 0, 4 * tk)
        consume(cur[...], r0, 0, 4 * tk, 2)
        consume(nxt[0:2 * tk, :], r0 + tk, 2 * tk, 2 * tk, 2)

        acc = acc_s[...]
        o = acc[:, 0:V_DIM] / acc[:, V_DIM:2 * V_DIM]
        for hf in range(2):
            a = o[(2 * hf) * tk:(2 * hf + 1) * tk, :] - lam * o[(2 * hf + 1) * tk:(2 * hf + 2) * tk, :]
            a = a * lax.rsqrt(jnp.mean(a * a, axis=-1, keepdims=True) + EPS) * sub_ref[...] * (1.0 - LAMBDA_INIT)
            o_ref[0, r0 + hf * tk:r0 + (hf + 1) * tk, :] = a.astype(_BF)


def _attention(lam_vecs, q3, k3, v3, q_norm, k_norm, subln):
    b, s, _ = q3.shape
    tq = 2 * ATTN_TK
    slab = lambda bi, hi: (bi, 0, hi)
    return pl.pallas_call(
        _attn_kernel,
        grid=(b, ATTN_HEADS),
        in_specs=[
            _const_spec((4, HEAD_DIM)),
            pl.BlockSpec((1, s, V_DIM), slab),
            pl.BlockSpec((1, s, V_DIM), slab),
            pl.BlockSpec((1, s, V_DIM), slab),
            _const_spec((1, V_DIM)),
            _const_spec((1, V_DIM)),
            _const_spec((1, V_DIM)),
        ],
        out_specs=pl.BlockSpec((1, s, V_DIM), slab),
        out_shape=jax.ShapeDtypeStruct((b, s, ATTN_HEADS * V_DIM), _BF),
        scratch_shapes=[
            pltpu.VMEM((s, V_DIM), _BF),
            pltpu.VMEM((s, 2 * V_DIM), _BF),
            pltpu.VMEM((2 * tq, V_DIM), _BF),
            pltpu.VMEM((2 * tq, V_DIM), _F32),
            pltpu.VMEM((2 * tq, 2 * V_DIM), _F32),
            pltpu.VMEM((2 * tq, ATTN_TK), _F32),
            pltpu.VMEM((2 * tq, ATTN_TK), _F32),
            pltpu.VMEM((2 * tq, ATTN_TK), _F32),
        ],
        compiler_params=pltpu.CompilerParams(
            dimension_semantics=("arbitrary", "arbitrary"), vmem_limit_bytes=VMEM_LIMIT),
        name="diffattn",
    )(lam_vecs, q3, k3, v3, q_norm.reshape(1, V_DIM), k_norm.reshape(1, V_DIM), subln.reshape(1, V_DIM))


def _router_logits(t, wr_ref, wrh_ref, br_ref):
    t_hi = t.astype(_BF)
    t_lo = (t - t_hi.astype(_F32)).astype(_BF)
    l_hi = jnp.dot(t_hi, wr_ref[...], preferred_element_type=_F32)
    l_lo = jnp.dot(t_lo, wrh_ref[...], preferred_element_type=_F32)
    return l_hi[:, 0:LANES] + l_hi[:, LANES:2 * LANES] + l_lo + br_ref[...]


def _mix_kernel(x_ref, a_ref, c_ref, g_ref, wa_ref, wc_ref, wo_ref, bco_ref, nf_ref, wr_ref, wrh_ref, br_ref,
                h_ref, t_ref, key_ref, cnt_ref, run_s, lg_s, tw_s):
    i = pl.program_id(0)
    tm, d = x_ref.shape
    half_groups = d // LANES // 2

    @pl.when(i == 0)
    def _():
        run_s[...] = jnp.zeros(run_s.shape, _F32)
        lg_s[1] = jnp.zeros(lg_s.shape[1:], _F32)
        tw_s[1] = jnp.zeros(tw_s.shape[1:], jnp.int32)

    def tile_matmuls(sl):
        y_b = jnp.dot(c_ref[...], wc_ref[...], preferred_element_type=_F32) + bco_ref[...]
        y_a = jnp.dot(a_ref[...], wa_ref[...], preferred_element_type=_F32)
        merged = g_ref[:, 0:d].astype(_F32) * y_a + g_ref[:, d:2 * d].astype(_F32) * y_b
        h = x_ref[...] + jnp.dot(merged.astype(_BF), wo_ref[...], preferred_element_type=_F32)
        h_ref[...] = h
        t = h * lax.rsqrt(jnp.mean(h * h, axis=-1, keepdims=True) + EPS) * nf_ref[...]
        for sg in range(half_groups):
            hi_bits = lax.bitcast_convert_type(
                t[:, sg * LANES:(sg + 1) * LANES].astype(_BF).astype(_F32), jnp.uint32)
            lo_bits = lax.bitcast_convert_type(
                t[:, (sg + half_groups) * LANES:(sg + half_groups + 1) * LANES].astype(_BF).astype(_F32),
                jnp.uint32)
            tw_s[sl, sg] = lax.bitcast_convert_type(hi_bits | (lo_bits >> 16), jnp.int32)
        lg_s[sl] = _router_logits(t, wr_ref, wrh_ref, br_ref)

    for sl in range(2):
        @pl.when(i % 2 == sl)
        def _(sl=sl):
            tile_matmuls(sl)
            _route_tile(i - 1, lg_s[1 - sl], tw_s.at[1 - sl], t_ref, key_ref, cnt_ref, run_s)


def _route_tile(tile, lg, words_ref, t_ref, key_ref, cnt_ref, run_s):
    tm = lg.shape[0]
    for sg in range(META_SUBLANE):
        t_ref[pl.ds(sg, tm, stride=SUBLANES), :] = words_ref[sg]
    lane = lax.broadcasted_iota(jnp.int32, (tm, LANES), 1)
    gl = jnp.where(lane < N_GROUPS, lg, NEG_BIG)
    gmax = jnp.max(gl, axis=-1, keepdims=True)
    gidx = jnp.min(jnp.where(gl == gmax, lane, LANES), axis=-1, keepdims=True)
    gsum = jnp.sum(jnp.where(lane < N_GROUPS, jnp.exp(gl - gmax), 0.0), axis=-1, keepdims=True)
    g_gate = 1.0 / gsum
    e_lo = N_GROUPS + EXPERTS_PER_GROUP * gidx
    el = jnp.where((lane >= e_lo) & (lane < e_lo + EXPERTS_PER_GROUP), lg, NEG_BIG)
    m1 = jnp.max(el, axis=-1, keepdims=True)
    i1 = jnp.min(jnp.where(el == m1, lane, LANES), axis=-1, keepdims=True)
    el2 = jnp.where(lane == i1, NEG_BIG, el)
    m2 = jnp.max(el2, axis=-1, keepdims=True)
    i2 = jnp.min(jnp.where(el2 == m2, lane, LANES), axis=-1, keepdims=True)
    e21 = jnp.exp(m2 - m1)
    w_top1 = g_gate * (1.0 / (1.0 + e21))
    w_top2 = g_gate * (e21 / (1.0 + e21))
    a1 = i1 - e_lo
    a2 = i2 - e_lo
    lo = jnp.minimum(a1, a2)
    hi = jnp.maximum(a1, a2)
    w_lo = jnp.where(a1 < a2, w_top1, w_top2)
    w_hi = jnp.where(a1 < a2, w_top2, w_top1)
    pidx = ((lo * (2 * EXPERTS_PER_GROUP - 1 - lo)) >> 1) + (hi - lo - 1)
    cls = gidx * PAIRS_PER_GROUP + pidx
    token = tile * tm + lax.broadcasted_iota(jnp.int32, (tm, LANES), 0)
    meta = jnp.where(lane == META_W_LO, lax.bitcast_convert_type(w_lo, jnp.int32),
                     jnp.where(lane == META_W_HI, lax.bitcast_convert_type(w_hi, jnp.int32),
                               jnp.where(lane == META_TOKEN, token, 0)))
    t_ref[pl.ds(META_SUBLANE, tm, stride=SUBLANES), :] = meta
    for sg in range(META_SUBLANE + 1, SUBLANES):
        t_ref[pl.ds(sg, tm, stride=SUBLANES), :] = jnp.zeros((tm, LANES), jnp.int32)

    onehot = lane == cls
    oh_bf = jnp.where(onehot, 1.0, 0.0).astype(_BF)
    rr = lax.broadcasted_iota(jnp.int32, (tm, tm), 0)
    ccol = lax.broadcasted_iota(jnp.int32, (tm, tm), 1)
    tri = jnp.where(ccol < rr, 1.0, 0.0).astype(_BF)
    before = jnp.dot(tri, oh_bf, preferred_element_type=_F32) + run_s[...]
    rank = jnp.sum(jnp.where(onehot, before, 0.0), axis=-1, keepdims=True)
    counted = jnp.where(tile >= 0, 1.0, 0.0)
    run_new = run_s[...] + counted * jnp.sum(jnp.where(onehot, 1.0, 0.0), axis=0, keepdims=True)
    run_s[...] = run_new
    cnt_ref[...] = run_new
    keyf = jnp.broadcast_to(cls.astype(_F32) * float(1 << RANK_BITS) + rank, (tm, LANES))
    for g in range(tm // LANES):
        kt = keyf[g * LANES:(g + 1) * LANES, :].T
        key_ref[0, g:g + 1, :] = kt[0:1, :].astype(jnp.int32)


def _mix(x2, attn2, cact2, gates2, wa_bf, wc_bf, wo_bf, b_conv_out, norm_ffn, w_router_cat, w_router_hi, b_router):
    n, d = x2.shape
    tm = TM_MIX
    aw = attn2.shape[1]
    cc = cact2.shape[1]
    nt = n // tm
    row = lambda i: (jnp.minimum(i, nt - 1), 0)
    prev = lambda i: (jnp.maximum(i - 1, 0), 0)
    return pl.pallas_call(
        _mix_kernel,
        grid=(nt + 1,),
        in_specs=[
            pl.BlockSpec((tm, d), row),
            pl.BlockSpec((tm, aw), row),
            pl.BlockSpec((tm, cc), row),
            pl.BlockSpec((tm, 2 * d), row),
            _const_spec(wa_bf.shape),
            _const_spec(wc_bf.shape),
            _const_spec(wo_bf.shape),
            _const_spec((1, d)),
            _const_spec((1, d)),
            _const_spec((d, 2 * LANES)),
            _const_spec((d, LANES)),
            _const_spec((1, LANES)),
        ],
        out_specs=[
            pl.BlockSpec((tm, d), row),
            pl.BlockSpec((tm * SUBLANES, LANES), prev),
            pl.BlockSpec((1, tm // LANES, LANES), lambda i: (jnp.maximum(i - 1, 0), 0, 0)),
            _const_spec((1, LANES)),
        ],
        out_shape=[
            jax.ShapeDtypeStruct((n, d), _F32),
            jax.ShapeDtypeStruct((n * SUBLANES, LANES), jnp.int32),
            jax.ShapeDtypeStruct((n // tm, tm // LANES, LANES), jnp.int32),
            jax.ShapeDtypeStruct((1, LANES), _F32),
        ],
        scratch_shapes=[
            pltpu.VMEM((1, LANES), _F32),
            pltpu.VMEM((2, tm, LANES), _F32),
            pltpu.VMEM((2, META_SUBLANE, tm, LANES), jnp.int32),
        ],
        compiler_params=pltpu.CompilerParams(
            dimension_semantics=("arbitrary",), vmem_limit_bytes=VMEM_LIMIT),
        name="mix_router",
    )(x2, attn2, cact2, gates2, wa_bf, wc_bf, wo_bf, b_conv_out.reshape(1, d), norm_ffn.reshape(1, d),
      w_router_cat, w_router_hi, b_router)


def _sorted_rows_kernel(pstart_ref, key_ref, o_ref):
    key = key_ref[...]
    cls = key >> RANK_BITS
    row = key & ((1 << RANK_BITS) - 1)
    for c in range(N_CLASSES):
        row = row + jnp.where(cls == c, pstart_ref[c], 0)
    o_ref[...] = row


def _sorted_rows(pstart, keys2):
    grid_spec = pltpu.PrefetchScalarGridSpec(
        num_scalar_prefetch=1,
        grid=(1,),
        in_specs=[pl.BlockSpec(keys2.shape, lambda i, *_: (0, 0))],
        out_specs=pl.BlockSpec(keys2.shape, lambda i, *_: (0, 0)),
    )
    return pl.pallas_call(
        _sorted_rows_kernel,
        grid_spec=grid_spec,
        out_shape=jax.ShapeDtypeStruct(keys2.shape, jnp.int32),
        name="sorted_rows",
    )(pstart, keys2)


def _token_rows(idx, count=1):
    return pl.ds(pl.multiple_of(idx * SUBLANES, SUBLANES), count * SUBLANES)


def _scatter_kernel(padstart_ref, padlen_ref, nvalid_ref, dest_ref, t_ref, xs_ref, zero_s, sem, zsem):
    i = pl.program_id(0)
    ts = t_ref.shape[0] // SUBLANES
    n_tokens = ts * pl.num_programs(0)
    bm = zero_s.shape[0] // SUBLANES
    nblk = xs_ref.shape[0] // (bm * SUBLANES)

    def start_rows(g, c):
        for u in range(DMA_UNROLL):
            r = g * DMA_UNROLL + u
            pltpu.make_async_copy(t_ref.at[_token_rows(r)], xs_ref.at[_token_rows(dest_ref[0, 0, r])], sem).start(
                priority=u % 2)
        return c

    def wait_rows(g, c):
        for _ in range(DMA_UNROLL):
            pltpu.make_async_copy(t_ref.at[_token_rows(0)], xs_ref.at[_token_rows(0)], sem).wait()
        return c

    lax.fori_loop(0, ts // DMA_UNROLL, start_rows, 0)

    @pl.when(i == 0)
    def _():
        rows = lax.broadcasted_iota(jnp.int32, zero_s.shape, 0)
        lanes = lax.broadcasted_iota(jnp.int32, zero_s.shape, 1)
        is_token = ((rows & (SUBLANES - 1)) == META_SUBLANE) & (lanes == META_TOKEN)
        zero_s[...] = jnp.where(is_token, n_tokens + (rows >> (SUBLANES.bit_length() - 1)), 0)

        def pad_copies(c, act):
            ln = padlen_ref[c]
            st = padstart_ref[c]
            for k in range(bm.bit_length() - 1):
                sz = 1 << k

                @pl.when(((ln >> k) & 1) == 1)
                def _():
                    dst = st + (ln & (sz - 1))
                    src = dst & (bm - 1)
                    act(pltpu.make_async_copy(zero_s.at[_token_rows(src, sz)], xs_ref.at[_token_rows(dst, sz)], zsem))

        def tail_copy(j, act):
            @pl.when(j >= nvalid_ref[0])
            def _():
                act(pltpu.make_async_copy(zero_s, xs_ref.at[_token_rows(j * bm, bm)], zsem))

        for act in (lambda cp: cp.start(), lambda cp: cp.wait()):
            lax.fori_loop(0, N_CLASSES, lambda c, carry, act=act: (pad_copies(c, act), carry)[1], 0)
            lax.fori_loop(0, nblk, lambda j, carry, act=act: (tail_copy(j, act), carry)[1], 0)

    lax.fori_loop(0, ts // DMA_UNROLL, wait_rows, 0)


def _scatter_rows(padstart, padlen, nvalid, dest, t_rows, n_rows):
    n = t_rows.shape[0] // SUBLANES
    ts = TS_SCATTER
    grid_spec = pltpu.PrefetchScalarGridSpec(
        num_scalar_prefetch=3,
        grid=(n // ts,),
        in_specs=[
            pl.BlockSpec((1, 1, ts), lambda i, *_: (i, 0, 0), memory_space=pltpu.SMEM),
            pl.BlockSpec((ts * SUBLANES, LANES), lambda i, *_: (i, 0)),
        ],
        out_specs=pl.BlockSpec(memory_space=pl.ANY),
        scratch_shapes=[pltpu.VMEM((BM_MOE * SUBLANES, LANES), jnp.int32), pltpu.SemaphoreType.DMA,
                        pltpu.SemaphoreType.DMA],
    )
    return pl.pallas_call(
        _scatter_kernel,
        grid_spec=grid_spec,
        out_shape=jax.ShapeDtypeStruct((n_rows * SUBLANES, LANES), jnp.int32),
        compiler_params=pltpu.CompilerParams(
            dimension_semantics=("arbitrary",), vmem_limit_bytes=VMEM_LIMIT),
        name="scatter_rows",
    )(padstart, padlen, nvalid, dest.reshape(n // ts, 1, ts), t_rows)


def _moe_kernel(ea_ref, eb_ref, valid_ref, xs_ref, w1a, w3a, w2a, w1b, w3b, w2b, out_ref,
                y_s, ids_v, ids_sm, rsem, isem):
    j = pl.program_id(0)
    nb = pl.num_programs(0)
    bm = xs_ref.shape[0] // SUBLANES
    ng = bm // LANES
    n_tokens = out_ref.shape[0] // SUBLANES - bm

    def ids_copy(sl):
        return pltpu.make_async_copy(ids_v.at[sl], ids_sm.at[sl], isem.at[sl])

    def row_copy(sl, r, tok):
        return pltpu.make_async_copy(y_s.at[sl, _token_rows(r)], out_ref.at[_token_rows(tok)], rsem.at[sl])

    def issue_rows(sl):
        for r in range(bm):
            row_copy(sl, r, ids_sm[sl, r // LANES, r % LANES]).start(priority=r % 2)

    def wait_rows(sl):
        def body(g, c):
            for _ in range(DMA_UNROLL):
                row_copy(sl, 0, 0).wait()
            return c

        lax.fori_loop(0, bm // DMA_UNROLL, body, 0)

    @pl.when(j == 0)
    def _():
        y_s[1] = jnp.zeros(y_s.shape[1:], _F32)
        pos = (lax.broadcasted_iota(jnp.int32, (ng, LANES), 0) * LANES
               + lax.broadcasted_iota(jnp.int32, (ng, LANES), 1))
        ids_v[1] = n_tokens + pos
        ids_copy(1).start()

    def send_previous(sl):
        ids_copy(1 - sl).wait()

        @pl.when(j >= 1)
        def _():
            wait_rows(sl)

        issue_rows(1 - sl)

    def step(sl):
        other = 1 - sl
        send_previous(sl)

        meta = xs_ref[pl.ds(META_SUBLANE, bm, stride=SUBLANES), :]
        lane = lax.broadcasted_iota(jnp.int32, (bm, LANES), 1)
        tok = jnp.sum(jnp.where(lane == META_TOKEN, meta.astype(_F32), 0.0), axis=-1, keepdims=True)
        tokb = jnp.broadcast_to(tok, (bm, LANES))
        for g in range(ng):
            ids_v[sl, g:g + 1, :] = tokb[g * LANES:(g + 1) * LANES, :].T[0:1, :].astype(jnp.int32)
        ids_copy(sl).start()

        words = [xs_ref[pl.ds(sg, bm, stride=SUBLANES), :] for sg in range(META_SUBLANE)]
        hi = [lax.bitcast_convert_type(w & jnp.int32(-65536), _F32).astype(_BF) for w in words]
        lo = [lax.bitcast_convert_type(w << 16, _F32).astype(_BF) for w in words]
        xb = jnp.concatenate(hi + lo, axis=1)
        metaf = lax.bitcast_convert_type(meta, _F32)
        w_a = jnp.sum(jnp.where(lane == META_W_LO, metaf, 0.0), axis=-1, keepdims=True)
        w_b = jnp.sum(jnp.where(lane == META_W_HI, metaf, 0.0), axis=-1, keepdims=True)

        def expert(w1, w3, w2):
            h1 = jnp.dot(xb, w1[0], preferred_element_type=_F32)
            h3 = jnp.dot(xb, w3[0], preferred_element_type=_F32)
            hdn = h1 * _sigmoid(h1) * h3
            return jnp.dot(hdn.astype(_BF), w2[0], preferred_element_type=_F32)

        y = w_a * expert(w1a, w3a, w2a) + w_b * expert(w1b, w3b, w2b)
        for sg in range(SUBLANES):
            y_s[sl, pl.ds(sg, bm, stride=SUBLANES), :] = y[:, sg * LANES:(sg + 1) * LANES]

        @pl.when(j == nb - 1)
        def _():
            wait_rows(other)
            ids_copy(sl).wait()
            issue_rows(sl)
            wait_rows(sl)

    is_valid = valid_ref[j] != 0
    prev_valid = valid_ref[jnp.maximum(j - 1, 0)] != 0
    for sl in range(2):
        @pl.when((j % 2 == sl) & is_valid)
        def _(sl=sl):
            step(sl)

        @pl.when((j % 2 == sl) & jnp.logical_not(is_valid) & prev_valid)
        def _(sl=sl):
            send_previous(sl)
            wait_rows(1 - sl)


def _moe(ea, eb, valid, xs, w1_bf, w3_bf, w2_bf, n_tokens):
    d, ff = w1_bf.shape[1], w1_bf.shape[2]
    assert d == SUBLANES * LANES
    bm = BM_MOE
    nb = xs.shape[0] // (bm * SUBLANES)
    grid_spec = pltpu.PrefetchScalarGridSpec(
        num_scalar_prefetch=3,
        grid=(nb,),
        in_specs=[
            pl.BlockSpec((bm * SUBLANES, LANES), lambda j, ea, eb, va: (j, 0)),
            pl.BlockSpec((1, d, ff), lambda j, ea, eb, va: (ea[j], 0, 0)),
            pl.BlockSpec((1, d, ff), lambda j, ea, eb, va: (ea[j], 0, 0)),
            pl.BlockSpec((1, ff, d), lambda j, ea, eb, va: (ea[j], 0, 0)),
            pl.BlockSpec((1, d, ff), lambda j, ea, eb, va: (eb[j], 0, 0)),
            pl.BlockSpec((1, d, ff), lambda j, ea, eb, va: (eb[j], 0, 0)),
            pl.BlockSpec((1, ff, d), lambda j, ea, eb, va: (eb[j], 0, 0)),
        ],
        out_specs=pl.BlockSpec(memory_space=pl.ANY),
        scratch_shapes=[
            pltpu.VMEM((2, bm * SUBLANES, LANES), _F32),
            pltpu.VMEM((2, bm // LANES, LANES), jnp.int32),
            pltpu.SMEM((2, bm // LANES, LANES), jnp.int32),
            pltpu.SemaphoreType.DMA((2,)),
            pltpu.SemaphoreType.DMA((2,)),
        ],
    )
    return pl.pallas_call(
        _moe_kernel,
        grid_spec=grid_spec,
        out_shape=jax.ShapeDtypeStruct(((n_tokens + bm) * SUBLANES, LANES), _F32),
        compiler_params=pltpu.CompilerParams(
            dimension_semantics=("arbitrary",), vmem_limit_bytes=VMEM_LIMIT),
        name="moe_experts",
    )(ea, eb, valid, xs, w1_bf, w3_bf, w2_bf, w1_bf, w3_bf, w2_bf)


def _final_kernel(h_ref, p_ref, y_ref, np_ref, wg_ref, bg_ref, wp_ref, o_ref):
    tm = h_ref.shape[0]
    proj = jnp.dot(p_ref[...].astype(_BF), wp_ref[...], preferred_element_type=_F32)
    y = jnp.concatenate([y_ref[pl.ds(sg, tm, stride=SUBLANES), :] for sg in range(SUBLANES)], axis=1)
    h = h_ref[...] + y
    u = h * lax.rsqrt(jnp.mean(h * h, axis=-1, keepdims=True) + EPS) * np_ref[...]
    gate = _sigmoid(jnp.dot(u.astype(_BF), wg_ref[...], preferred_element_type=_F32) + bg_ref[...])
    o_ref[...] = h + gate * proj


def _final(h1, p2, y_rows, norm_ple, wg_bf, b_ple_gate, wp_bf):
    n, d = h1.shape
    assert d == SUBLANES * LANES
    tm = TM_FINAL
    pd = p2.shape[1]
    row = lambda i: (i, 0)
    return pl.pallas_call(
        _final_kernel,
        grid=(n // tm,),
        in_specs=[
            pl.BlockSpec((tm, d), row),
            pl.BlockSpec((tm, pd), row),
            pl.BlockSpec((tm * SUBLANES, LANES), row),
            _const_spec((1, d)),
            _const_spec(wg_bf.shape),
            _const_spec((1, d)),
            _const_spec(wp_bf.shape),
        ],
        out_specs=pl.BlockSpec((tm, d), row),
        out_shape=jax.ShapeDtypeStruct((n, d), _F32),
        compiler_params=pltpu.CompilerParams(
            dimension_semantics=("arbitrary",), vmem_limit_bytes=VMEM_LIMIT),
        name="final_ple",
    )(h1, p2, y_rows, norm_ple.reshape(1, d), wg_bf, b_ple_gate.reshape(1, d), wp_bf)


def _class_tables():
    ea, eb = [], []
    for g in range(N_GROUPS):
        for lo in range(EXPERTS_PER_GROUP):
            for hi in range(lo + 1, EXPERTS_PER_GROUP):
                ea.append(g * EXPERTS_PER_GROUP + lo)
                eb.append(g * EXPERTS_PER_GROUP + hi)
    return np.asarray(ea, np.int32), np.asarray(eb, np.int32)


_CLASS_EA, _CLASS_EB = _class_tables()


def kernel(x, p, norm_mix, w_in, b_conv_in, b_gate, q_norm, k_norm, lambda_q1, lambda_k1, lambda_q2, lambda_k2, subln, w_attn_out, conv_w, conv_b, conv_ln_g, conv_ln_b, w_conv_out, b_conv_out, w_o, norm_ffn, w_router_group, b_router_group, w_router_expert, b_router_expert, w1, w3, w2, norm_ple, w_ple_gate, b_ple_gate, w_ple_proj):
    b, s, d = x.shape
    n = b * s
    depth = w_in.shape[0]
    qk = ATTN_HEADS * 2 * HEAD_DIM
    aw = ATTN_HEADS * V_DIM
    cc = conv_w.shape[-1]
    assert depth == 1 and n % TM_MIX == 0 and s % (2 * ATTN_TK) == 0 and s % TM_INPROJ == 0
    assert n % TS_SCATTER == 0 and n % TM_FINAL == 0 and n % BM_MOE == 0
    i = 0
    h = x.reshape(n, d)

    conv_w_pad = jnp.pad(conv_w[i], ((0, CONV_HALO - CONV_WIDTH), (0, 0)))
    q2, k2, v2, cact2, gates2 = _inproj(h, norm_mix[i], w_in[i].astype(_BF), b_conv_in[i], b_gate[i], conv_w_pad,
                                        conv_b[i], conv_ln_g[i], conv_ln_b[i], qk, aw, cc, s)

    lam_vecs = jnp.stack([lambda_q1[i], lambda_k1[i], lambda_q2[i], lambda_k2[i]]).astype(_F32)
    attn = _attention(lam_vecs, q2.reshape(b, s, qk), k2.reshape(b, s, qk), v2.reshape(b, s, aw),
                      q_norm[i], k_norm[i], subln[i])

    w_router = jnp.concatenate([w_router_group[i], w_router_expert[i]], axis=1)
    w_router = jnp.pad(w_router, ((0, 0), (0, LANES - w_router.shape[1])))
    b_router = jnp.concatenate([b_router_group[i], b_router_expert[i]])
    b_router = jnp.pad(b_router, (0, LANES - b_router.shape[0])).reshape(1, LANES)
    w_router_hi = w_router.astype(_BF)
    w_router_lo = (w_router - w_router_hi.astype(_F32)).astype(_BF)
    w_router_cat = jnp.concatenate([w_router_hi, w_router_lo], axis=1)
    h1, t_rows, keys, cnt = _mix(h, attn.reshape(n, aw), cact2, gates2, w_attn_out[i].astype(_BF),
                                 w_conv_out[i].astype(_BF), w_o[i].astype(_BF), b_conv_out[i], norm_ffn[i],
                                 w_router_cat, w_router_hi, b_router)

    bm = BM_MOE
    nb = n // bm + N_CLASSES
    counts = cnt[0].astype(jnp.int32)
    padded = ((counts + bm - 1) // bm) * bm
    pend = jnp.cumsum(padded)
    pstart = (pend - padded).astype(jnp.int32)
    total = pend[-1]
    blk = jnp.arange(nb, dtype=jnp.int32) * bm
    valid = blk < total
    first_row = jnp.where(valid, blk, total - 1)
    bcls = jnp.sum((pend[None, :] <= first_row[:, None]).astype(jnp.int32), axis=1)
    bcls = jnp.minimum(bcls, N_CLASSES - 1)
    cls_onehot = (bcls[:, None] == jnp.arange(N_CLASSES, dtype=jnp.int32)[None, :]).astype(jnp.int32)
    ea = jnp.sum(cls_onehot * jnp.asarray(_CLASS_EA)[None, :], axis=1)
    eb = jnp.sum(cls_onehot * jnp.asarray(_CLASS_EB)[None, :], axis=1)
    nvalid = (total // bm).astype(jnp.int32).reshape(1)

    dest = _sorted_rows(pstart, keys.reshape(n // LANES, LANES))
    xs = _scatter_rows((pstart + counts).astype(jnp.int32), (padded - counts).astype(jnp.int32), nvalid,
                       dest, t_rows, nb * bm)
    y_rows = _moe(ea, eb, valid.astype(jnp.int32), xs, w1[i].astype(_BF), w3[i].astype(_BF), w2[i].astype(_BF), n)

    out = _final(h1, p[i].reshape(n, -1), y_rows, norm_ple[i], w_ple_gate[i].astype(_BF), b_ple_gate[i],
                 w_ple_proj[i].astype(_BF))
    return out.reshape(b, s, d)
```

```python
import functools
import math

import numpy as np
import jax
import jax.numpy as jnp
from jax import lax
from jax.experimental import pallas as pl
from jax.experimental.pallas import tpu as pltpu

ATTN_HEADS = 4
HEAD_DIM = 64
V_DIM = 2 * HEAD_DIM
CONV_WIDTH = 31
N_GROUPS = 4
EXPERTS_PER_GROUP = 8
N_EXPERTS = N_GROUPS * EXPERTS_PER_GROUP
PAIRS_PER_GROUP = EXPERTS_PER_GROUP * (EXPERTS_PER_GROUP - 1) // 2
N_CLASSES = N_GROUPS * PAIRS_PER_GROUP
EPS = 1e-6
LAMBDA_INIT = 0.8 - 0.6 * math.exp(-0.3 * 0)

LANES = 128
SUBLANES = 8
CONV_HALO = 32
NEG_BIG = -1e30
RANK_BITS = 17
META_SUBLANE = 4
META_W_LO, META_W_HI, META_TOKEN = 0, 1, 2
DMA_UNROLL = 8

TM_INPROJ = 512
ATTN_TK = 512
TM_MIX = 512
CONV_ROWS = 64
TS_SCATTER = 512
BM_MOE = 512
TM_FINAL = 512
VMEM_LIMIT = 56 * 1024 * 1024

_BF = jnp.bfloat16
_F32 = jnp.float32


def _const_spec(shape):
    nd = len(shape)
    return pl.BlockSpec(shape, lambda *_: (0,) * nd)


def _sigmoid(x):
    return 1.0 / (1.0 + jnp.exp(-x))


def _inproj_kernel(x_ref, nm_ref, w_ref, bci_ref, bg_ref, cw_ref, cb_ref, lng_ref, lnb_ref,
                   q_ref, k_ref, v_ref, c_ref, g_ref, ext_s, xsh_s, cv_s, *, tiles_per_seq):
    i = pl.program_id(0)
    tm = x_ref.shape[0]
    qk = q_ref.shape[1]
    aw = v_ref.shape[1]
    cc = c_ref.shape[1]
    ext_rows = CONV_HALO + tm
    o0, o1, o2, o3 = qk, 2 * qk, 2 * qk + aw, 2 * qk + aw + 2 * cc

    @pl.when(i == 0)
    def _():
        ext_s[ext_rows:ext_rows + SUBLANES, :] = jnp.zeros((SUBLANES, cc), _F32)

    @pl.when((i % tiles_per_seq) == 0)
    def _():
        ext_s[0:CONV_HALO, :] = jnp.zeros((CONV_HALO, cc), _F32)

    x = x_ref[...]
    u = x * lax.rsqrt(jnp.mean(x * x, axis=-1, keepdims=True) + EPS) * nm_ref[...]
    ub = u.astype(_BF)
    c = jnp.dot(ub, w_ref[:, o2:o3], preferred_element_type=_F32) + bci_ref[...]
    ext_s[CONV_HALO:ext_rows, :] = c[:, :cc] * _sigmoid(c[:, cc:])
    q_ref[...] = jnp.dot(ub, w_ref[:, 0:o0], preferred_element_type=_F32).astype(_BF)
    k_ref[...] = jnp.dot(ub, w_ref[:, o0:o1], preferred_element_type=_F32).astype(_BF)
    v_ref[...] = jnp.dot(ub, w_ref[:, o1:o2], preferred_element_type=_F32).astype(_BF)
    g = jnp.dot(ub, w_ref[:, o3:], preferred_element_type=_F32) + bg_ref[...]
    g_ref[...] = _sigmoid(g).astype(_BF)
    off = CONV_HALO - (CONV_WIDTH - 1)

    for rho in range(1, SUBLANES):
        xsh_s[rho - 1] = ext_s[rho:rho + ext_rows, :]

    for r0 in range(0, tm, CONV_ROWS):
        acc = jnp.zeros((CONV_ROWS, cc), _F32)
        for j in range(CONV_WIDTH):
            a, rho = divmod(off + j, SUBLANES)
            lo_row = r0 + a * SUBLANES
            if rho == 0:
                xv = ext_s[lo_row:lo_row + CONV_ROWS, :]
            else:
                xv = xsh_s[rho - 1, lo_row:lo_row + CONV_ROWS, :]
            acc = acc + cw_ref[j:j + 1, :] * xv
        cv_s[r0:r0 + CONV_ROWS, :] = acc

    ext_s[0:CONV_HALO, :] = ext_s[tm:ext_rows, :]

    cv = cv_s[...] + cb_ref[...]
    mu = jnp.mean(cv, axis=-1, keepdims=True)
    xc = cv - mu
    var = jnp.mean(xc * xc, axis=-1, keepdims=True)
    ln = xc * lax.rsqrt(var + EPS) * lng_ref[...] + lnb_ref[...]
    c_ref[...] = (ln * _sigmoid(ln)).astype(_BF)


def _inproj(x2, norm_mix, w_in_bf, b_conv_in, b_gate, conv_w, conv_b, ln_g, ln_b, qk, aw, cc, seq_len):
    n, d = x2.shape
    tm = TM_INPROJ
    ng = b_gate.shape[-1]
    kern = functools.partial(_inproj_kernel, tiles_per_seq=seq_len // tm)
    return pl.pallas_call(
        kern,
        grid=(n // tm,),
        in_specs=[
            pl.BlockSpec((tm, d), lambda i: (i, 0)),
            _const_spec((1, d)),
            pl.BlockSpec(w_in_bf.shape, lambda i: (0, 0), pipeline_mode=pl.Buffered(1)),
            _const_spec((1, 2 * cc)),
            _const_spec((1, ng)),
            _const_spec((CONV_HALO, cc)),
            _const_spec((1, cc)),
            _const_spec((1, cc)),
            _const_spec((1, cc)),
        ],
        out_specs=[
            pl.BlockSpec((tm, qk), lambda i: (i, 0)),
            pl.BlockSpec((tm, qk), lambda i: (i, 0)),
            pl.BlockSpec((tm, aw), lambda i: (i, 0)),
            pl.BlockSpec((tm, cc), lambda i: (i, 0)),
            pl.BlockSpec((tm, ng), lambda i: (i, 0)),
        ],
        out_shape=[
            jax.ShapeDtypeStruct((n, qk), _BF),
            jax.ShapeDtypeStruct((n, qk), _BF),
            jax.ShapeDtypeStruct((n, aw), _BF),
            jax.ShapeDtypeStruct((n, cc), _BF),
            jax.ShapeDtypeStruct((n, ng), _BF),
        ],
        scratch_shapes=[
            pltpu.VMEM((CONV_HALO + tm + SUBLANES, cc), _F32),
            pltpu.VMEM((SUBLANES - 1, CONV_HALO + tm, cc), _F32),
            pltpu.VMEM((tm, cc), _F32),
        ],
        compiler_params=pltpu.CompilerParams(
            dimension_semantics=("arbitrary",), vmem_limit_bytes=VMEM_LIMIT),
        name="inproj",
    )(x2, norm_mix.reshape(1, d), w_in_bf, b_conv_in.reshape(1, -1), b_gate.reshape(1, -1), conv_w,
      conv_b.reshape(1, cc), ln_g.reshape(1, cc), ln_b.reshape(1, cc))


def _half_rmsnorm(x, gain_row, lo_mask):
    x2 = x * x
    s_lo = jnp.sum(jnp.where(lo_mask, x2, 0.0), axis=-1, keepdims=True)
    s_hi = jnp.sum(jnp.where(lo_mask, 0.0, x2), axis=-1, keepdims=True)
    r = jnp.where(lo_mask, lax.rsqrt(s_lo * (1.0 / HEAD_DIM) + EPS), lax.rsqrt(s_hi * (1.0 / HEAD_DIM) + EPS))
    return x * r * gain_row


def _attn_kernel(lam_ref, q_ref, k_ref, v_ref, qn_ref, kn_ref, sub_ref, o_ref, kn_s, va_s, qs_s, m_s, acc_s,
                 sa_s, sb_s, sc_s):
    s_len = q_ref.shape[1]
    tk = ATTN_TK
    tq = 2 * tk
    nq = s_len // tq
    lane = lax.broadcasted_iota(jnp.int32, (1, V_DIM), 1)
    lo_mask = lane < HEAD_DIM

    lv = lam_ref[...]
    d1 = jnp.sum(lv[0:1, :] * lv[1:2, :], axis=-1, keepdims=True)
    d2 = jnp.sum(lv[2:3, :] * lv[3:4, :], axis=-1, keepdims=True)
    lam = jnp.exp(d1) - jnp.exp(d2) + LAMBDA_INIT

    kgain = kn_ref[...]
    qgain = qn_ref[...] * (math.log2(math.e) / math.sqrt(HEAD_DIM))

    def prep_kv(i, carry):
        r0 = pl.multiple_of(i * tq, tq)
        kk = k_ref[0, pl.ds(r0, tq), :].astype(_F32)
        kn_s[pl.ds(r0, tq), :] = _half_rmsnorm(kk, kgain, lo_mask).astype(_BF)
        va_s[pl.ds(r0, tq), 0:V_DIM] = v_ref[0, pl.ds(r0, tq), :]
        va_s[pl.ds(r0, tq), V_DIM:2 * V_DIM] = jnp.ones((tq, V_DIM), _BF)
        return carry

    lax.fori_loop(0, nq, prep_kv, 0)

    def scores(c0, ra, nr):
        kc = kn_s[pl.ds(c0, tk), :]
        return lax.dot_general(qs_s[ra:ra + nr, :], kc, (((1,), (1,)), ((), ())), preferred_element_type=_F32)

    def consume(s, c0, ra, nr, n_masked):
        vc = va_s[pl.ds(c0, tk), :]
        if n_masked:
            keep = (lax.broadcasted_iota(jnp.int32, (tk, tk), 1) <= lax.broadcasted_iota(jnp.int32, (tk, tk), 0))
            groups = [s[g * tk:(g + 1) * tk, :] for g in range(nr // tk)]
            groups = [jnp.where(keep, sg, NEG_BIG) if g < n_masked else sg for g, sg in enumerate(groups)]
            s = jnp.concatenate(groups, axis=0)
        m_old = m_s[ra:ra + nr, :]
        m_new = jnp.maximum(m_old, jnp.max(s, axis=-1, keepdims=True))
        alpha = jnp.exp2(m_old - m_new)
        p = jnp.concatenate(
            [jnp.exp2(s[:, c * LANES:(c + 1) * LANES] - m_new) for c in range(tk // LANES)], axis=1)
        pv = jnp.dot(p.astype(_BF), vc, preferred_element_type=_F32)
        acc_s[ra:ra + nr, :] = jnp.concatenate([alpha, alpha], axis=1) * acc_s[ra:ra + nr, :] + pv
        m_s[ra:ra + nr, :] = m_new

    def prep_q(r0):
        for hf in range(2):
            qq = _half_rmsnorm(q_ref[0, pl.ds(r0 + hf * tk, tk), :].astype(_F32), qgain, lo_mask)
            qs_s[(2 * hf) * tk:(2 * hf + 1) * tk, :] = jnp.where(lo_mask, qq, 0.0).astype(_BF)
            qs_s[(2 * hf + 1) * tk:(2 * hf + 2) * tk, :] = jnp.where(lo_mask, 0.0, qq).astype(_BF)

    bufs = (sa_s, sb_s, sc_s)
    prep_q(0)
    sa_s[...] = scores(0, 0, 4 * tk)

    for qi in range(nq):
        r0 = qi * tq
        cur = bufs[(2 * qi) % 3]
        nxt = bufs[(2 * qi + 1) % 3]
        ahead = bufs[(2 * qi + 2) % 3]
        m_s[...] = jnp.full(m_s.shape, NEG_BIG, _F32)
        acc_s[...] = jnp.zeros(acc_s.shape, _F32)

        def two_chunks(t, c, cur=cur, nxt=nxt):
            c0 = pl.multiple_of(2 * t * tk, tk)
            c1 = pl.multiple_of(c0 + tk, tk)
            c2 = pl.multiple_of(c0 + 2 * tk, tk)
            nxt[...] = scores(c1, 0, 4 * tk)
            consume(cur[...], c0, 0, 4 * tk, 0)
            cur[...] = scores(c2, 0, 4 * tk)
            consume(nxt[...], c1, 0, 4 * tk, 0)
            return c

        lax.fori_loop(0, qi, two_chunks, 0)
        nxt[0:2 * tk, :] = scores(r0 + tk, 2 * tk, 2 * tk)
        if qi + 1 < nq:
            prep_q(r0 + tq)
            ahead[...] = scores(0, 0, 4 * tk)
        consume(cur[...], r0, 0, 4 * tk, 2)
        consume(nxt[0:2 * tk, :], r0 + tk, 2 * tk, 2 * tk, 2)

        acc = acc_s[...]
        o = acc[:, 0:V_DIM] / acc[:, V_DIM:2 * V_DIM]
        for hf in range(2):
            a = o[(2 * hf) * tk:(2 * hf + 1) * tk, :] - lam * o[(2 * hf + 1) * tk:(2 * hf + 2) * tk, :]
            a = a * lax.rsqrt(jnp.mean(a * a, axis=-1, keepdims=True) + EPS) * sub_ref[...] * (1.0 - LAMBDA_INIT)
            o_ref[0, r0 + hf * tk:r0 + (hf + 1) * tk, :] = a.astype(_BF)


def _attention(lam_vecs, q3, k3, v3, q_norm, k_norm, subln):
    b, s, _ = q3.shape
    tq = 2 * ATTN_TK
    slab = lambda bi, hi: (bi, 0, hi)
    return pl.pallas_call(
        _attn_kernel,
        grid=(b, ATTN_HEADS),
        in_specs=[
            _const_spec((4, HEAD_DIM)),
            pl.BlockSpec((1, s, V_DIM), slab),
            pl.BlockSpec((1, s, V_DIM), slab),
            pl.BlockSpec((1, s, V_DIM), slab),
            _const_spec((1, V_DIM)),
            _const_spec((1, V_DIM)),
            _const_spec((1, V_DIM)),
        ],
        out_specs=pl.BlockSpec((1, s, V_DIM), slab),
        out_shape=jax.ShapeDtypeStruct((b, s, ATTN_HEADS * V_DIM), _BF),
        scratch_shapes=[
            pltpu.VMEM((s, V_DIM), _BF),
            pltpu.VMEM((s, 2 * V_DIM), _BF),
            pltpu.VMEM((2 * tq, V_DIM), _BF),
            pltpu.VMEM((2 * tq, V_DIM), _F32),
            pltpu.VMEM((2 * tq, 2 * V_DIM), _F32),
            pltpu.VMEM((2 * tq, ATTN_TK), _F32),
            pltpu.VMEM((2 * tq, ATTN_TK), _F32),
            pltpu.VMEM((2 * tq, ATTN_TK), _F32),
        ],
        compiler_params=pltpu.CompilerParams(
            dimension_semantics=("arbitrary", "arbitrary"), vmem_limit_bytes=VMEM_LIMIT),
        name="diffattn",
    )(lam_vecs, q3, k3, v3, q_norm.reshape(1, V_DIM), k_norm.reshape(1, V_DIM), subln.reshape(1, V_DIM))


def _router_logits(t, wr_ref, wrh_ref, br_ref):
    t_hi = t.astype(_BF)
    t_lo = (t - t_hi.astype(_F32)).astype(_BF)
    l_hi = jnp.dot(t_hi, wr_ref[...], preferred_element_type=_F32)
    l_lo = jnp.dot(t_lo, wrh_ref[...], preferred_element_type=_F32)
    return l_hi[:, 0:LANES] + l_hi[:, LANES:2 * LANES] + l_lo + br_ref[...]


def _mix_kernel(x_ref, a_ref, c_ref, g_ref, wa_ref, wc_ref, wo_ref, bco_ref, nf_ref, wr_ref, wrh_ref, br_ref,
                h_ref, t_ref, key_ref, cnt_ref, run_s, lg_s, tw_s):
    i = pl.program_id(0)
    tm, d = x_ref.shape
    half_groups = d // LANES // 2

    @pl.when(i == 0)
    def _():
        run_s[...] = jnp.zeros(run_s.shape, _F32)
        lg_s[1] = jnp.zeros(lg_s.shape[1:], _F32)
        tw_s[1] = jnp.zeros(tw_s.shape[1:], jnp.int32)

    def tile_matmuls(sl):
        y_b = jnp.dot(c_ref[...], wc_ref[...], preferred_element_type=_F32) + bco_ref[...]
        y_a = jnp.dot(a_ref[...], wa_ref[...], preferred_element_type=_F32)
        merged = g_ref[:, 0:d].astype(_F32) * y_a + g_ref[:, d:2 * d].astype(_F32) * y_b
        h = x_ref[...] + jnp.dot(merged.astype(_BF), wo_ref[...], preferred_element_type=_F32)
        h_ref[...] = h
        t = h * lax.rsqrt(jnp.mean(h * h, axis=-1, keepdims=True) + EPS) * nf_ref[...]
        for sg in range(half_groups):
            hi_bits = lax.bitcast_convert_type(
                t[:, sg * LANES:(sg + 1) * LANES].astype(_BF).astype(_F32), jnp.uint32)
            lo_bits = lax.bitcast_convert_type(
                t[:, (sg + half_groups) * LANES:(sg + half_groups + 1) * LANES].astype(_BF).astype(_F32),
                jnp.uint32)
            tw_s[sl, sg] = lax.bitcast_convert_type(hi_bits | (lo_bits >> 16), jnp.int32)
        lg_s[sl] = _router_logits(t, wr_ref, wrh_ref, br_ref)

    for sl in range(2):
        @pl.when(i % 2 == sl)
        def _(sl=sl):
            tile_matmuls(sl)
            _route_tile(i - 1, lg_s[1 - sl], tw_s.at[1 - sl], t_ref, key_ref, cnt_ref, run_s)


def _route_tile(tile, lg, words_ref, t_ref, key_ref, cnt_ref, run_s):
    tm = lg.shape[0]
    for sg in range(META_SUBLANE):
        t_ref[pl.ds(sg, tm, stride=SUBLANES), :] = words_ref[sg]
    lane = lax.broadcasted_iota(jnp.int32, (tm, LANES), 1)
    gl = jnp.where(lane < N_GROUPS, lg, NEG_BIG)
    gmax = jnp.max(gl, axis=-1, keepdims=True)
    gidx = jnp.min(jnp.where(gl == gmax, lane, LANES), axis=-1, keepdims=True)
    gsum = jnp.sum(jnp.where(lane < N_GROUPS, jnp.exp(gl - gmax), 0.0), axis=-1, keepdims=True)
    g_gate = 1.0 / gsum
    e_lo = N_GROUPS + EXPERTS_PER_GROUP * gidx
    el = jnp.where((lane >= e_lo) & (lane < e_lo + EXPERTS_PER_GROUP), lg, NEG_BIG)
    m1 = jnp.max(el, axis=-1, keepdims=True)
    i1 = jnp.min(jnp.where(el == m1, lane, LANES), axis=-1, keepdims=True)
    el2 = jnp.where(lane == i1, NEG_BIG, el)
    m2 = jnp.max(el2, axis=-1, keepdims=True)
    i2 = jnp.min(jnp.where(el2 == m2, lane, LANES), axis=-1, keepdims=True)
    e21 = jnp.exp(m2 - m1)
    w_top1 = g_gate * (1.0 / (1.0 + e21))
    w_top2 = g_gate * (e21 / (1.0 + e21))
    a1 = i1 - e_lo
    a2 = i2 - e_lo
    lo = jnp.minimum(a1, a2)
    hi = jnp.maximum(a1, a2)
    w_lo = jnp.where(a1 < a2, w_top1, w_top2)
    w_hi = jnp.where(a1 < a2, w_top2, w_top1)
    pidx = ((lo * (2 * EXPERTS_PER_GROUP - 1 - lo)) >> 1) + (hi - lo - 1)
    cls = gidx * PAIRS_PER_GROUP + pidx
    token = tile * tm + lax.broadcasted_iota(jnp.int32, (tm, LANES), 0)
    meta = jnp.where(lane == META_W_LO, lax.bitcast_convert_type(w_lo, jnp.int32),
                     jnp.where(lane == META_W_HI, lax.bitcast_convert_type(w_hi, jnp.int32),
                               jnp.where(lane == META_TOKEN, token, 0)))
    t_ref[pl.ds(META_SUBLANE, tm, stride=SUBLANES), :] = meta
    for sg in range(META_SUBLANE + 1, SUBLANES):
        t_ref[pl.ds(sg, tm, stride=SUBLANES), :] = jnp.zeros((tm, LANES), jnp.int32)

    onehot = lane == cls
    oh_bf = jnp.where(onehot, 1.0, 0.0).astype(_BF)
    rr = lax.broadcasted_iota(jnp.int32, (tm, tm), 0)
    ccol = lax.broadcasted_iota(jnp.int32, (tm, tm), 1)
    tri = jnp.where(ccol < rr, 1.0, 0.0).astype(_BF)
    before = jnp.dot(tri, oh_bf, preferred_element_type=_F32) + run_s[...]
    rank = jnp.sum(jnp.where(onehot, before, 0.0), axis=-1, keepdims=True)
    counted = jnp.where(tile >= 0, 1.0, 0.0)
    run_new = run_s[...] + counted * jnp.sum(jnp.where(onehot, 1.0, 0.0), axis=0, keepdims=True)
    run_s[...] = run_new
    cnt_ref[...] = run_new
    keyf = jnp.broadcast_to(cls.astype(_F32) * float(1 << RANK_BITS) + rank, (tm, LANES))
    for g in range(tm // LANES):
        kt = keyf[g * LANES:(g + 1) * LANES, :].T
        key_ref[0, g:g + 1, :] = kt[0:1, :].astype(jnp.int32)


def _mix(x2, attn2, cact2, gates2, wa_bf, wc_bf, wo_bf, b_conv_out, norm_ffn, w_router_cat, w_router_hi, b_router):
    n, d = x2.shape
    tm = TM_MIX
    aw = attn2.shape[1]
    cc = cact2.shape[1]
    nt = n // tm
    row = lambda i: (jnp.minimum(i, nt - 1), 0)
    prev = lambda i: (jnp.maximum(i - 1, 0), 0)
    return pl.pallas_call(
        _mix_kernel,
        grid=(nt + 1,),
        in_specs=[
            pl.BlockSpec((tm, d), row),
            pl.BlockSpec((tm, aw), row),
            pl.BlockSpec((tm, cc), row),
            pl.BlockSpec((tm, 2 * d), row),
            _const_spec(wa_bf.shape),
            _const_spec(wc_bf.shape),
            _const_spec(wo_bf.shape),
            _const_spec((1, d)),
            _const_spec((1, d)),
            _const_spec((d, 2 * LANES)),
            _const_spec((d, LANES)),
            _const_spec((1, LANES)),
        ],
        out_specs=[
            pl.BlockSpec((tm, d), row),
            pl.BlockSpec((tm * SUBLANES, LANES), prev),
            pl.BlockSpec((1, tm // LANES, LANES), lambda i: (jnp.maximum(i - 1, 0), 0, 0)),
            _const_spec((1, LANES)),
        ],
        out_shape=[
            jax.ShapeDtypeStruct((n, d), _F32),
            jax.ShapeDtypeStruct((n * SUBLANES, LANES), jnp.int32),
            jax.ShapeDtypeStruct((n // tm, tm // LANES, LANES), jnp.int32),
            jax.ShapeDtypeStruct((1, LANES), _F32),
        ],
        scratch_shapes=[
            pltpu.VMEM((1, LANES), _F32),
            pltpu.VMEM((2, tm, LANES), _F32),
            pltpu.VMEM((2, META_SUBLANE, tm, LANES), jnp.int32),
        ],
        compiler_params=pltpu.CompilerParams(
            dimension_semantics=("arbitrary",), vmem_limit_bytes=VMEM_LIMIT),
        name="mix_router",
    )(x2, attn2, cact2, gates2, wa_bf, wc_bf, wo_bf, b_conv_out.reshape(1, d), norm_ffn.reshape(1, d),
      w_router_cat, w_router_hi, b_router)


def _sorted_rows_kernel(pstart_ref, key_ref, o_ref):
    key = key_ref[...]
    cls = key >> RANK_BITS
    row = key & ((1 << RANK_BITS) - 1)
    for c in range(N_CLASSES):
        row = row + jnp.where(cls == c, pstart_ref[c], 0)
    o_ref[...] = row


def _sorted_rows(pstart, keys2):
    grid_spec = pltpu.PrefetchScalarGridSpec(
        num_scalar_prefetch=1,
        grid=(1,),
        in_specs=[pl.BlockSpec(keys2.shape, lambda i, *_: (0, 0))],
        out_specs=pl.BlockSpec(keys2.shape, lambda i, *_: (0, 0)),
    )
    return pl.pallas_call(
        _sorted_rows_kernel,
        grid_spec=grid_spec,
        out_shape=jax.ShapeDtypeStruct(keys2.shape, jnp.int32),
        name="sorted_rows",
    )(pstart, keys2)


def _token_rows(idx, count=1):
    return pl.ds(pl.multiple_of(idx * SUBLANES, SUBLANES), count * SUBLANES)


def _scatter_kernel(padstart_ref, padlen_ref, nvalid_ref, dest_ref, t_ref, xs_ref, zero_s, sem, zsem):
    i = pl.program_id(0)
    ts = t_ref.shape[0] // SUBLANES
    n_tokens = ts * pl.num_programs(0)
    bm = zero_s.shape[0] // SUBLANES
    nblk = xs_ref.shape[0] // (bm * SUBLANES)

    def start_rows(g, c):
        for u in range(DMA_UNROLL):
            r = g * DMA_UNROLL + u
            pltpu.make_async_copy(t_ref.at[_token_rows(r)], xs_ref.at[_token_rows(dest_ref[0, 0, r])], sem).start(
                priority=u % 2)
        return c

    def wait_rows(g, c):
        for _ in range(DMA_UNROLL):
            pltpu.make_async_copy(t_ref.at[_token_rows(0)], xs_ref.at[_token_rows(0)], sem).wait()
        return c

    lax.fori_loop(0, ts // DMA_UNROLL, start_rows, 0)

    @pl.when(i == 0)
    def _():
        rows = lax.broadcasted_iota(jnp.int32, zero_s.shape, 0)
        lanes = lax.broadcasted_iota(jnp.int32, zero_s.shape, 1)
        is_token = ((rows & (SUBLANES - 1)) == META_SUBLANE) & (lanes == META_TOKEN)
        zero_s[...] = jnp.where(is_token, n_tokens + (rows >> (SUBLANES.bit_length() - 1)), 0)

        def pad_copies(c, act):
            ln = padlen_ref[c]
            st = padstart_ref[c]
            for k in range(bm.bit_length() - 1):
                sz = 1 << k

                @pl.when(((ln >> k) & 1) == 1)
                def _():
                    dst = st + (ln & (sz - 1))
                    src = dst & (bm - 1)
                    act(pltpu.make_async_copy(zero_s.at[_token_rows(src, sz)], xs_ref.at[_token_rows(dst, sz)], zsem))

        def tail_copy(j, act):
            @pl.when(j >= nvalid_ref[0])
            def _():
                act(pltpu.make_async_copy(zero_s, xs_ref.at[_token_rows(j * bm, bm)], zsem))

        for act in (lambda cp: cp.start(), lambda cp: cp.wait()):
            lax.fori_loop(0, N_CLASSES, lambda c, carry, act=act: (pad_copies(c, act), carry)[1], 0)
            lax.fori_loop(0, nblk, lambda j, carry, act=act: (tail_copy(j, act), carry)[1], 0)

    lax.fori_loop(0, ts // DMA_UNROLL, wait_rows, 0)


def _scatter_rows(padstart, padlen, nvalid, dest, t_rows, n_rows):
    n = t_rows.shape[0] // SUBLANES
    ts = TS_SCATTER
    grid_spec = pltpu.PrefetchScalarGridSpec(
        num_scalar_prefetch=3,
        grid=(n // ts,),
        in_specs=[
            pl.BlockSpec((1, 1, ts), lambda i, *_: (i, 0, 0), memory_space=pltpu.SMEM),
            pl.BlockSpec((ts * SUBLANES, LANES), lambda i, *_: (i, 0)),
        ],
        out_specs=pl.BlockSpec(memory_space=pl.ANY),
        scratch_shapes=[pltpu.VMEM((BM_MOE * SUBLANES, LANES), jnp.int32), pltpu.SemaphoreType.DMA,
                        pltpu.SemaphoreType.DMA],
    )
    return pl.pallas_call(
        _scatter_kernel,
        grid_spec=grid_spec,
        out_shape=jax.ShapeDtypeStruct((n_rows * SUBLANES, LANES), jnp.int32),
        compiler_params=pltpu.CompilerParams(
            dimension_semantics=("arbitrary",), vmem_limit_bytes=VMEM_LIMIT),
        name="scatter_rows",
    )(padstart, padlen, nvalid, dest.reshape(n // ts, 1, ts), t_rows)


def _moe_kernel(ea_ref, eb_ref, valid_ref, xs_ref, w1a, w3a, w2a, w1b, w3b, w2b, out_ref,
                y_s, ids_v, ids_sm, rsem, isem):
    j = pl.program_id(0)
    nb = pl.num_programs(0)
    bm = xs_ref.shape[0] // SUBLANES
    ng = bm // LANES
    n_tokens = out_ref.shape[0] // SUBLANES - bm

    def ids_copy(sl):
        return pltpu.make_async_copy(ids_v.at[sl], ids_sm.at[sl], isem.at[sl])

    def row_copy(sl, r, tok):
        return pltpu.make_async_copy(y_s.at[sl, _token_rows(r)], out_ref.at[_token_rows(tok)], rsem.at[sl])

    def issue_rows(sl):
        for r in range(bm):
            row_copy(sl, r, ids_sm[sl, r // LANES, r % LANES]).start(priority=r % 2)

    def wait_rows(sl):
        def body(g, c):
            for _ in range(DMA_UNROLL):
                row_copy(sl, 0, 0).wait()
            return c

        lax.fori_loop(0, bm // DMA_UNROLL, body, 0)

    @pl.when(j == 0)
    def _():
        y_s[1] = jnp.zeros(y_s.shape[1:], _F32)
        pos = (lax.broadcasted_iota(jnp.int32, (ng, LANES), 0) * LANES
               + lax.broadcasted_iota(jnp.int32, (ng, LANES), 1))
        ids_v[1] = n_tokens + pos
        ids_copy(1).start()

    def send_previous(sl):
        ids_copy(1 - sl).wait()

        @pl.when(j >= 1)
        def _():
            wait_rows(sl)

        issue_rows(1 - sl)

    def step(sl):
        other = 1 - sl
        send_previous(sl)

        meta = xs_ref[pl.ds(META_SUBLANE, bm, stride=SUBLANES), :]
        lane = lax.broadcasted_iota(jnp.int32, (bm, LANES), 1)
        tok = jnp.sum(jnp.where(lane == META_TOKEN, meta.astype(_F32), 0.0), axis=-1, keepdims=True)
        tokb = jnp.broadcast_to(tok, (bm, LANES))
        for g in range(ng):
            ids_v[sl, g:g + 1, :] = tokb[g * LANES:(g + 1) * LANES, :].T[0:1, :].astype(jnp.int32)
        ids_copy(sl).start()

        words = [xs_ref[pl.ds(sg, bm, stride=SUBLANES), :] for sg in range(META_SUBLANE)]
        hi = [lax.bitcast_convert_type(w & jnp.int32(-65536), _F32).astype(_BF) for w in words]
        lo = [lax.bitcast_convert_type(w << 16, _F32).astype(_BF) for w in words]
        xb = jnp.concatenate(hi + lo, axis=1)
        metaf = lax.bitcast_convert_type(meta, _F32)
        w_a = jnp.sum(jnp.where(lane == META_W_LO, metaf, 0.0), axis=-1, keepdims=True)
        w_b = jnp.sum(jnp.where(lane == META_W_HI, metaf, 0.0), axis=-1, keepdims=True)

        def expert(w1, w3, w2):
            h1 = jnp.dot(xb, w1[0], preferred_element_type=_F32)
            h3 = jnp.dot(xb, w3[0], preferred_element_type=_F32)
            hdn = h1 * _sigmoid(h1) * h3
            return jnp.dot(hdn.astype(_BF), w2[0], preferred_element_type=_F32)

        y = w_a * expert(w1a, w3a, w2a) + w_b * expert(w1b, w3b, w2b)
        for sg in range(SUBLANES):
            y_s[sl, pl.ds(sg, bm, stride=SUBLANES), :] = y[:, sg * LANES:(sg + 1) * LANES]

        @pl.when(j == nb - 1)
        def _():
            wait_rows(other)
            ids_copy(sl).wait()
            issue_rows(sl)
            wait_rows(sl)

    is_valid = valid_ref[j] != 0
    prev_valid = valid_ref[jnp.maximum(j - 1, 0)] != 0
    for sl in range(2):
        @pl.when((j % 2 == sl) & is_valid)
        def _(sl=sl):
            step(sl)

        @pl.when((j % 2 == sl) & jnp.logical_not(is_valid) & prev_valid)
        def _(sl=sl):
            send_previous(sl)
            wait_rows(1 - sl)


def _moe(ea, eb, valid, xs, w1_bf, w3_bf, w2_bf, n_tokens):
    d, ff = w1_bf.shape[1], w1_bf.shape[2]
    assert d == SUBLANES * LANES
    bm = BM_MOE
    nb = xs.shape[0] // (bm * SUBLANES)
    grid_spec = pltpu.PrefetchScalarGridSpec(
        num_scalar_prefetch=3,
        grid=(nb,),
        in_specs=[
            pl.BlockSpec((bm * SUBLANES, LANES), lambda j, ea, eb, va: (j, 0)),
            pl.BlockSpec((1, d, ff), lambda j, ea, eb, va: (ea[j], 0, 0)),
            pl.BlockSpec((1, d, ff), lambda j, ea, eb, va: (ea[j], 0, 0)),
            pl.BlockSpec((1, ff, d), lambda j, ea, eb, va: (ea[j], 0, 0)),
            pl.BlockSpec((1, d, ff), lambda j, ea, eb, va: (eb[j], 0, 0)),
            pl.BlockSpec((1, d, ff), lambda j, ea, eb, va: (eb[j], 0, 0)),
            pl.BlockSpec((1, ff, d), lambda j, ea, eb, va: (eb[j], 0, 0)),
        ],
        out_specs=pl.BlockSpec(memory_space=pl.ANY),
        scratch_shapes=[
            pltpu.VMEM((2, bm * SUBLANES, LANES), _F32),
            pltpu.VMEM((2, bm // LANES, LANES), jnp.int32),
            pltpu.SMEM((2, bm // LANES, LANES), jnp.int32),
            pltpu.SemaphoreType.DMA((2,)),
            pltpu.SemaphoreType.DMA((2,)),
        ],
    )
    return pl.pallas_call(
        _moe_kernel,
        grid_spec=grid_spec,
        out_shape=jax.ShapeDtypeStruct(((n_tokens + bm) * SUBLANES, LANES), _F32),
        compiler_params=pltpu.CompilerParams(
            dimension_semantics=("arbitrary",), vmem_limit_bytes=VMEM_LIMIT),
        name="moe_experts",
    )(ea, eb, valid, xs, w1_bf, w3_bf, w2_bf, w1_bf, w3_bf, w2_bf)


def _final_kernel(h_ref, p_ref, y_ref, np_ref, wg_ref, bg_ref, wp_ref, o_ref):
    tm = h_ref.shape[0]
    proj = jnp.dot(p_ref[...].astype(_BF), wp_ref[...], preferred_element_type=_F32)
    y = jnp.concatenate([y_ref[pl.ds(sg, tm, stride=SUBLANES), :] for sg in range(SUBLANES)], axis=1)
    h = h_ref[...] + y
    u = h * lax.rsqrt(jnp.mean(h * h, axis=-1, keepdims=True) + EPS) * np_ref[...]
    gate = _sigmoid(jnp.dot(u.astype(_BF), wg_ref[...], preferred_element_type=_F32) + bg_ref[...])
    o_ref[...] = h + gate * proj


def _final(h1, p2, y_rows, norm_ple, wg_bf, b_ple_gate, wp_bf):
    n, d = h1.shape
    assert d == SUBLANES * LANES
    tm = TM_FINAL
    pd = p2.shape[1]
    row = lambda i: (i, 0)
    return pl.pallas_call(
        _final_kernel,
        grid=(n // tm,),
        in_specs=[
            pl.BlockSpec((tm, d), row),
            pl.BlockSpec((tm, pd), row),
            pl.BlockSpec((tm * SUBLANES, LANES), row),
            _const_spec((1, d)),
            _const_spec(wg_bf.shape),
            _const_spec((1, d)),
            _const_spec(wp_bf.shape),
        ],
        out_specs=pl.BlockSpec((tm, d), row),
        out_shape=jax.ShapeDtypeStruct((n, d), _F32),
        compiler_params=pltpu.CompilerParams(
            dimension_semantics=("arbitrary",), vmem_limit_bytes=VMEM_LIMIT),
        name="final_ple",
    )(h1, p2, y_rows, norm_ple.reshape(1, d), wg_bf, b_ple_gate.reshape(1, d), wp_bf)


def _class_tables():
    ea, eb = [], []
    for g in range(N_GROUPS):
        for lo in range(EXPERTS_PER_GROUP):
            for hi in range(lo + 1, EXPERTS_PER_GROUP):
                ea.append(g * EXPERTS_PER_GROUP + lo)
                eb.append(g * EXPERTS_PER_GROUP + hi)
    return np.asarray(ea, np.int32), np.asarray(eb, np.int32)


_CLASS_EA, _CLASS_EB = _class_tables()


def kernel(x, p, norm_mix, w_in, b_conv_in, b_gate, q_norm, k_norm, lambda_q1, lambda_k1, lambda_q2, lambda_k2, subln, w_attn_out, conv_w, conv_b, conv_ln_g, conv_ln_b, w_conv_out, b_conv_out, w_o, norm_ffn, w_router_group, b_router_group, w_router_expert, b_router_expert, w1, w3, w2, norm_ple, w_ple_gate, b_ple_gate, w_ple_proj):
    b, s, d = x.shape
    n = b * s
    depth = w_in.shape[0]
    qk = ATTN_HEADS * 2 * HEAD_DIM
    aw = ATTN_HEADS * V_DIM
    cc = conv_w.shape[-1]
    assert depth == 1 and n % TM_MIX == 0 and s % (2 * ATTN_TK) == 0 and s % TM_INPROJ == 0
    assert n % TS_SCATTER == 0 and n % TM_FINAL == 0 and n % BM_MOE == 0
    i = 0
    h = x.reshape(n, d)

    conv_w_pad = jnp.pad(conv_w[i], ((0, CONV_HALO - CONV_WIDTH), (0, 0)))
    q2, k2, v2, cact2, gates2 = _inproj(h, norm_mix[i], w_in[i].astype(_BF), b_conv_in[i], b_gate[i], conv_w_pad,
                                        conv_b[i], conv_ln_g[i], conv_ln_b[i], qk, aw, cc, s)

    lam_vecs = jnp.stack([lambda_q1[i], lambda_k1[i], lambda_q2[i], lambda_k2[i]]).astype(_F32)
    attn = _attention(lam_vecs, q2.reshape(b, s, qk), k2.reshape(b, s, qk), v2.reshape(b, s, aw),
                      q_norm[i], k_norm[i], subln[i])

    w_router = jnp.concatenate([w_router_group[i], w_router_expert[i]], axis=1)
    w_router = jnp.pad(w_router, ((0, 0), (0, LANES - w_router.shape[1])))
    b_router = jnp.concatenate([b_router_group[i], b_router_expert[i]])
    b_router = jnp.pad(b_router, (0, LANES - b_router.shape[0])).reshape(1, LANES)
    w_router_hi = w_router.astype(_BF)
    w_router_lo = (w_router - w_router_hi.astype(_F32)).astype(_BF)
    w_router_cat = jnp.concatenate([w_router_hi, w_router_lo], axis=1)
    h1, t_rows, keys, cnt = _mix(h, attn.reshape(n, aw), cact2, gates2, w_attn_out[i].astype(_BF),
                                 w_conv_out[i].astype(_BF), w_o[i].astype(_BF), b_conv_out[i], norm_ffn[i],
                                 w_router_cat, w_router_hi, b_router)

    bm = BM_MOE
    nb = n // bm + N_CLASSES
    counts = cnt[0].astype(jnp.int32)
    padded = ((counts + bm - 1) // bm) * bm
    pend = jnp.cumsum(padded)
    pstart = (pend - padded).astype(jnp.int32)
    total = pend[-1]
    blk = jnp.arange(nb, dtype=jnp.int32) * bm
    valid = blk < total
    first_row = jnp.where(valid, blk, total - 1)
    bcls = jnp.sum((pend[None, :] <= first_row[:, None]).astype(jnp.int32), axis=1)
    bcls = jnp.minimum(bcls, N_CLASSES - 1)
    cls_onehot = (bcls[:, None] == jnp.arange(N_CLASSES, dtype=jnp.int32)[None, :]).astype(jnp.int32)
    ea = jnp.sum(cls_onehot * jnp.asarray(_CLASS_EA)[None, :], axis=1)
    eb = jnp.sum(cls_onehot * jnp.asarray(_CLASS_EB)[None, :], axis=1)
    nvalid = (total // bm).astype(jnp.int32).reshape(1)

    dest = _sorted_rows(pstart, keys.reshape(n // LANES, LANES))
    xs = _scatter_rows((pstart + counts).astype(jnp.int32), (padded - counts).astype(jnp.int32), nvalid,
                       dest, t_rows, nb * bm)
    y_rows = _moe(ea, eb, valid.astype(jnp.int32), xs, w1[i].astype(_BF), w3[i].astype(_BF), w2[i].astype(_BF), n)

    out = _final(h1, p[i].reshape(n, -1), y_rows, norm_ple[i], w_ple_gate[i].astype(_BF), b_ple_gate[i],
                 w_ple_proj[i].astype(_BF))
    return out.reshape(b, s, d)
```

```python
import functools
import math

import numpy as np
import jax
import jax.numpy as jnp
from jax import lax
from jax.experimental import pallas as pl
from jax.experimental.pallas import tpu as pltpu

ATTN_HEADS = 4
HEAD_DIM = 64
V_DIM = 2 * HEAD_DIM
CONV_WIDTH = 31
N_GROUPS = 4
EXPERTS_PER_GROUP = 8
N_EXPERTS = N_GROUPS * EXPERTS_PER_GROUP
PAIRS_PER_GROUP = EXPERTS_PER_GROUP * (EXPERTS_PER_GROUP - 1) // 2
N_CLASSES = N_GROUPS * PAIRS_PER_GROUP
EPS = 1e-6
LAMBDA_INIT = 0.8 - 0.6 * math.exp(-0.3 * 0)

LANES = 128
SUBLANES = 8
CONV_HALO = 32
NEG_BIG = -1e30
RANK_BITS = 17
META_SUBLANE = 4
META_W_LO, META_W_HI, META_TOKEN = 0, 1, 2
DMA_UNROLL = 8

TM_INPROJ = 512
ATTN_TK = 512
TM_MIX = 512
CONV_ROWS = 64
TS_SCATTER = 512
BM_MOE = 256
TM_FINAL = 512
VMEM_LIMIT = 56 * 1024 * 1024

_BF = jnp.bfloat16
_F32 = jnp.float32


def _const_spec(shape):
    nd = len(shape)
    return pl.BlockSpec(shape, lambda *_: (0,) * nd)


def _sigmoid(x):
    return 1.0 / (1.0 + jnp.exp(-x))


def _inproj_kernel(x_ref, nm_ref, w_ref, bci_ref, bg_ref, cw_ref, cb_ref, lng_ref, lnb_ref,
                   q_ref, k_ref, v_ref, c_ref, g_ref, ext_s, xsh_s, cv_s, *, tiles_per_seq):
    i = pl.program_id(0)
    tm = x_ref.shape[0]
    qk = q_ref.shape[1]
    aw = v_ref.shape[1]
    cc = c_ref.shape[1]
    ext_rows = CONV_HALO + tm
    o0, o1, o2, o3 = qk, 2 * qk, 2 * qk + aw, 2 * qk + aw + 2 * cc

    @pl.when(i == 0)
    def _():
        ext_s[ext_rows:ext_rows + SUBLANES, :] = jnp.zeros((SUBLANES, cc), _F32)

    @pl.when((i % tiles_per_seq) == 0)
    def _():
        ext_s[0:CONV_HALO, :] = jnp.zeros((CONV_HALO, cc), _F32)

    x = x_ref[...]
    u = x * lax.rsqrt(jnp.mean(x * x, axis=-1, keepdims=True) + EPS) * nm_ref[...]
    ub = u.astype(_BF)
    c = jnp.dot(ub, w_ref[:, o2:o3], preferred_element_type=_F32) + bci_ref[...]
    ext_s[CONV_HALO:ext_rows, :] = c[:, :cc] * _sigmoid(c[:, cc:])
    q_ref[...] = jnp.dot(ub, w_ref[:, 0:o0], preferred_element_type=_F32).astype(_BF)
    k_ref[...] = jnp.dot(ub, w_ref[:, o0:o1], preferred_element_type=_F32).astype(_BF)
    v_ref[...] = jnp.dot(ub, w_ref[:, o1:o2], preferred_element_type=_F32).astype(_BF)
    g = jnp.dot(ub, w_ref[:, o3:], preferred_element_type=_F32) + bg_ref[...]
    g_ref[...] = _sigmoid(g).astype(_BF)
    off = CONV_HALO - (CONV_WIDTH - 1)

    for rho in range(1, SUBLANES):
        xsh_s[rho - 1] = ext_s[rho:rho + ext_rows, :]

    for r0 in range(0, tm, CONV_ROWS):
        acc = jnp.zeros((CONV_ROWS, cc), _F32)
        for j in range(CONV_WIDTH):
            a, rho = divmod(off + j, SUBLANES)
            lo_row = r0 + a * SUBLANES
            if rho == 0:
                xv = ext_s[lo_row:lo_row + CONV_ROWS, :]
            else:
                xv = xsh_s[rho - 1, lo_row:lo_row + CONV_ROWS, :]
            acc = acc + cw_ref[j:j + 1, :] * xv
        cv_s[r0:r0 + CONV_ROWS, :] = acc

    ext_s[0:CONV_HALO, :] = ext_s[tm:ext_rows, :]

    cv = cv_s[...] + cb_ref[...]
    mu = jnp.mean(cv, axis=-1, keepdims=True)
    xc = cv - mu
    var = jnp.mean(xc * xc, axis=-1, keepdims=True)
    ln = xc * lax.rsqrt(var + EPS) * lng_ref[...] + lnb_ref[...]
    c_ref[...] = (ln * _sigmoid(ln)).astype(_BF)


def _inproj(x2, norm_mix, w_in_bf, b_conv_in, b_gate, conv_w, conv_b, ln_g, ln_b, qk, aw, cc, seq_len):
    n, d = x2.shape
    tm = TM_INPROJ
    ng = b_gate.shape[-1]
    kern = functools.partial(_inproj_kernel, tiles_per_seq=seq_len // tm)
    return pl.pallas_call(
        kern,
        grid=(n // tm,),
        in_specs=[
            pl.BlockSpec((tm, d), lambda i: (i, 0)),
            _const_spec((1, d)),
            pl.BlockSpec(w_in_bf.shape, lambda i: (0, 0), pipeline_mode=pl.Buffered(1)),
            _const_spec((1, 2 * cc)),
            _const_spec((1, ng)),
            _const_spec((CONV_HALO, cc)),
            _const_spec((1, cc)),
            _const_spec((1, cc)),
            _const_spec((1, cc)),
        ],
        out_specs=[
            pl.BlockSpec((tm, qk), lambda i: (i, 0)),
            pl.BlockSpec((tm, qk), lambda i: (i, 0)),
            pl.BlockSpec((tm, aw), lambda i: (i, 0)),
            pl.BlockSpec((tm, cc), lambda i: (i, 0)),
            pl.BlockSpec((tm, ng), lambda i: (i, 0)),
        ],
        out_shape=[
            jax.ShapeDtypeStruct((n, qk), _BF),
            jax.ShapeDtypeStruct((n, qk), _BF),
            jax.ShapeDtypeStruct((n, aw), _BF),
            jax.ShapeDtypeStruct((n, cc), _BF),
            jax.ShapeDtypeStruct((n, ng), _BF),
        ],
        scratch_shapes=[
            pltpu.VMEM((CONV_HALO + tm + SUBLANES, cc), _F32),
            pltpu.VMEM((SUBLANES - 1, CONV_HALO + tm, cc), _F32),
            pltpu.VMEM((tm, cc), _F32),
        ],
        compiler_params=pltpu.CompilerParams(
            dimension_semantics=("arbitrary",), vmem_limit_bytes=VMEM_LIMIT),
        name="inproj",
    )(x2, norm_mix.reshape(1, d), w_in_bf, b_conv_in.reshape(1, -1), b_gate.reshape(1, -1), conv_w,
      conv_b.reshape(1, cc), ln_g.reshape(1, cc), ln_b.reshape(1, cc))


def _half_rmsnorm(x, gain_row, lo_mask):
    x2 = x * x
    s_lo = jnp.sum(jnp.where(lo_mask, x2, 0.0), axis=-1, keepdims=True)
    s_hi = jnp.sum(jnp.where(lo_mask, 0.0, x2), axis=-1, keepdims=True)
    r = jnp.where(lo_mask, lax.rsqrt(s_lo * (1.0 / HEAD_DIM) + EPS), lax.rsqrt(s_hi * (1.0 / HEAD_DIM) + EPS))
    return x * r * gain_row


def _attn_kernel(lam_ref, q_ref, k_ref, v_ref, qn_ref, kn_ref, sub_ref, o_ref, kn_s, va_s, qs_s, m_s, acc_s,
                 sa_s, sb_s, sc_s):
    s_len = q_ref.shape[1]
    tk = ATTN_TK
    tq = 2 * tk
    nq = s_len // tq
    lane = lax.broadcasted_iota(jnp.int32, (1, V_DIM), 1)
    lo_mask = lane < HEAD_DIM

    lv = lam_ref[...]
    d1 = jnp.sum(lv[0:1, :] * lv[1:2, :], axis=-1, keepdims=True)
    d2 = jnp.sum(lv[2:3, :] * lv[3:4, :], axis=-1, keepdims=True)
    lam = jnp.exp(d1) - jnp.exp(d2) + LAMBDA_INIT

    kgain = kn_ref[...]
    qgain = qn_ref[...] * (math.log2(math.e) / math.sqrt(HEAD_DIM))

    def prep_kv(i, carry):
        r0 = pl.multiple_of(i * tq, tq)
        kk = k_ref[0, pl.ds(r0, tq), :].astype(_F32)
        kn_s[pl.ds(r0, tq), :] = _half_rmsnorm(kk, kgain, lo_mask).astype(_BF)
        va_s[pl.ds(r0, tq), 0:V_DIM] = v_ref[0, pl.ds(r0, tq), :]
        va_s[pl.ds(r0, tq), V_DIM:2 * V_DIM] = jnp.ones((tq, V_DIM), _BF)
        return carry

    lax.fori_loop(0, nq, prep_kv, 0)

    def scores(c0, ra, nr):
        kc = kn_s[pl.ds(c0, tk), :]
        return lax.dot_general(qs_s[ra:ra + nr, :], kc, (((1,), (1,)), ((), ())), preferred_element_type=_F32)

    def consume(s, c0, ra, nr, n_masked):
        vc = va_s[pl.ds(c0, tk), :]
        if n_masked:
            keep = (lax.broadcasted_iota(jnp.int32, (tk, tk), 1) <= lax.broadcasted_iota(jnp.int32, (tk, tk), 0))
            groups = [s[g * tk:(g + 1) * tk, :] for g in range(nr // tk)]
            groups = [jnp.where(keep, sg, NEG_BIG) if g < n_masked else sg for g, sg in enumerate(groups)]
            s = jnp.concatenate(groups, axis=0)
        m_old = m_s[ra:ra + nr, :]
        m_new = jnp.maximum(m_old, jnp.max(s, axis=-1, keepdims=True))
        alpha = jnp.exp2(m_old - m_new)
        p = jnp.concatenate(
            [jnp.exp2(s[:, c * LANES:(c + 1) * LANES] - m_new) for c in range(tk // LANES)], axis=1)
        pv = jnp.dot(p.astype(_BF), vc, preferred_element_type=_F32)
        acc_s[ra:ra + nr, :] = jnp.concatenate([alpha, alpha], axis=1) * acc_s[ra:ra + nr, :] + pv
        m_s[ra:ra + nr, :] = m_new

    def prep_q(r0):
        for hf in range(2):
            qq = _half_rmsnorm(q_ref[0, pl.ds(r0 + hf * tk, tk), :].astype(_F32), qgain, lo_mask)
            qs_s[(2 * hf) * tk:(2 * hf + 1) * tk, :] = jnp.where(lo_mask, qq, 0.0).astype(_BF)
            qs_s[(2 * hf + 1) * tk:(2 * hf + 2) * tk, :] = jnp.where(lo_mask, 0.0, qq).astype(_BF)

    bufs = (sa_s, sb_s, sc_s)
    prep_q(0)
    sa_s[...] = scores(0, 0, 4 * tk)

    for qi in range(nq):
        r0 = qi * tq
        cur = bufs[(2 * qi) % 3]
        nxt = bufs[(2 * qi + 1) % 3]
        ahead = bufs[(2 * qi + 2) % 3]
        m_s[...] = jnp.full(m_s.shape, NEG_BIG, _F32)
        acc_s[...] = jnp.zeros(acc_s.shape, _F32)

        def two_chunks(t, c, cur=cur, nxt=nxt):
            c0 = pl.multiple_of(2 * t * tk, tk)
            c1 = pl.multiple_of(c0 + tk, tk)
            c2 = pl.multiple_of(c0 + 2 * tk, tk)
            nxt[...] = scores(c1, 0, 4 * tk)
            consume(cur[...], c0, 0, 4 * tk, 0)
            cur[...] = scores(c2, 0, 4 * tk)
            consume(nxt[...], c1, 0, 4 * tk, 0)
            return c

        lax.fori_loop(0, qi, two_chunks, 0)
        nxt[0:2 * tk, :] = scores(r0 + tk, 2 * tk, 2 * tk)
        if qi + 1 < nq:
            prep_q(r0 + tq)
            ahead[...] = scores(0, 0, 4 * tk)
        consume(cur[...], r0, 0, 4 * tk, 2)
        consume(nxt[0:2 * tk, :], r0 + tk, 2 * tk, 2 * tk, 2)

        acc = acc_s[...]
        o = acc[:, 0:V_DIM] / acc[:, V_DIM:2 * V_DIM]
        for hf in range(2):
            a = o[(2 * hf) * tk:(2 * hf + 1) * tk, :] - lam * o[(2 * hf + 1) * tk:(2 * hf + 2) * tk, :]
            a = a * lax.rsqrt(jnp.mean(a * a, axis=-1, keepdims=True) + EPS) * sub_ref[...] * (1.0 - LAMBDA_INIT)
            o_ref[0, r0 + hf * tk:r0 + (hf + 1) * tk, :] = a.astype(_BF)


def _attention(lam_vecs, q3, k3, v3, q_norm, k_norm, subln):
    b, s, _ = q3.shape
    tq = 2 * ATTN_TK
    slab = lambda bi, hi: (bi, 0, hi)
    return pl.pallas_call(
        _attn_kernel,
        grid=(b, ATTN_HEADS),
        in_specs=[
            _const_spec((4, HEAD_DIM)),
            pl.BlockSpec((1, s, V_DIM), slab),
            pl.BlockSpec((1, s, V_DIM), slab),
            pl.BlockSpec((1, s, V_DIM), slab),
            _const_spec((1, V_DIM)),
            _const_spec((1, V_DIM)),
            _const_spec((1, V_DIM)),
        ],
        out_specs=pl.BlockSpec((1, s, V_DIM), slab),
        out_shape=jax.ShapeDtypeStruct((b, s, ATTN_HEADS * V_DIM), _BF),
        scratch_shapes=[
            pltpu.VMEM((s, V_DIM), _BF),
            pltpu.VMEM((s, 2 * V_DIM), _BF),
            pltpu.VMEM((2 * tq, V_DIM), _BF),
            pltpu.VMEM((2 * tq, V_DIM), _F32),
            pltpu.VMEM((2 * tq, 2 * V_DIM), _F32),
            pltpu.VMEM((2 * tq, ATTN_TK), _F32),
            pltpu.VMEM((2 * tq, ATTN_TK), _F32),
            pltpu.VMEM((2 * tq, ATTN_TK), _F32),
        ],
        compiler_params=pltpu.CompilerParams(
            dimension_semantics=("arbitrary", "arbitrary"), vmem_limit_bytes=VMEM_LIMIT),
        name="diffattn",
    )(lam_vecs, q3, k3, v3, q_norm.reshape(1, V_DIM), k_norm.reshape(1, V_DIM), subln.reshape(1, V_DIM))


def _router_logits(t, wr_ref, wrh_ref, br_ref):
    t_hi = t.astype(_BF)
    t_lo = (t - t_hi.astype(_F32)).astype(_BF)
    l_hi = jnp.dot(t_hi, wr_ref[...], preferred_element_type=_F32)
    l_lo = jnp.dot(t_lo, wrh_ref[...], preferred_element_type=_F32)
    return l_hi[:, 0:LANES] + l_hi[:, LANES:2 * LANES] + l_lo + br_ref[...]


def _mix_kernel(x_ref, a_ref, c_ref, g_ref, wa_ref, wc_ref, wo_ref, bco_ref, nf_ref, wr_ref, wrh_ref, br_ref,
                h_ref, t_ref, key_ref, cnt_ref, run_s, lg_s, tw_s):
    i = pl.program_id(0)
    tm, d = x_ref.shape
    half_groups = d // LANES // 2

    @pl.when(i == 0)
    def _():
        run_s[...] = jnp.zeros(run_s.shape, _F32)
        lg_s[1] = jnp.zeros(lg_s.shape[1:], _F32)
        tw_s[1] = jnp.zeros(tw_s.shape[1:], jnp.int32)

    def tile_matmuls(sl):
        y_b = jnp.dot(c_ref[...], wc_ref[...], preferred_element_type=_F32) + bco_ref[...]
        y_a = jnp.dot(a_ref[...], wa_ref[...], preferred_element_type=_F32)
        merged = g_ref[:, 0:d].astype(_F32) * y_a + g_ref[:, d:2 * d].astype(_F32) * y_b
        h = x_ref[...] + jnp.dot(merged.astype(_BF), wo_ref[...], preferred_element_type=_F32)
        h_ref[...] = h
        t = h * lax.rsqrt(jnp.mean(h * h, axis=-1, keepdims=True) + EPS) * nf_ref[...]
        for sg in range(half_groups):
            hi_bits = lax.bitcast_convert_type(
                t[:, sg * LANES:(sg + 1) * LANES].astype(_BF).astype(_F32), jnp.uint32)
            lo_bits = lax.bitcast_convert_type(
                t[:, (sg + half_groups) * LANES:(sg + half_groups + 1) * LANES].astype(_BF).astype(_F32),
                jnp.uint32)
            tw_s[sl, sg] = lax.bitcast_convert_type(hi_bits | (lo_bits >> 16), jnp.int32)
        lg_s[sl] = _router_logits(t, wr_ref, wrh_ref, br_ref)

    for sl in range(2):
        @pl.when(i % 2 == sl)
        def _(sl=sl):
            tile_matmuls(sl)
            _route_tile(i - 1, lg_s[1 - sl], tw_s.at[1 - sl], t_ref, key_ref, cnt_ref, run_s)


def _route_tile(tile, lg, words_ref, t_ref, key_ref, cnt_ref, run_s):
    tm = lg.shape[0]
    for sg in range(META_SUBLANE):
        t_ref[pl.ds(sg, tm, stride=SUBLANES), :] = words_ref[sg]
    lane = lax.broadcasted_iota(jnp.int32, (tm, LANES), 1)
    gl = jnp.where(lane < N_GROUPS, lg, NEG_BIG)
    gmax = jnp.max(gl, axis=-1, keepdims=True)
    gidx = jnp.min(jnp.where(gl == gmax, lane, LANES), axis=-1, keepdims=True)
    gsum = jnp.sum(jnp.where(lane < N_GROUPS, jnp.exp(gl - gmax), 0.0), axis=-1, keepdims=True)
    g_gate = 1.0 / gsum
    e_lo = N_GROUPS + EXPERTS_PER_GROUP * gidx
    el = jnp.where((lane >= e_lo) & (lane < e_lo + EXPERTS_PER_GROUP), lg, NEG_BIG)
    m1 = jnp.max(el, axis=-1, keepdims=True)
    i1 = jnp.min(jnp.where(el == m1, lane, LANES), axis=-1, keepdims=True)
    el2 = jnp.where(lane == i1, NEG_BIG, el)
    m2 = jnp.max(el2, axis=-1, keepdims=True)
    i2 = jnp.min(jnp.where(el2 == m2, lane, LANES), axis=-1, keepdims=True)
    e21 = jnp.exp(m2 - m1)
    w_top1 = g_gate * (1.0 / (1.0 + e21))
    w_top2 = g_gate * (e21 / (1.0 + e21))
    a1 = i1 - e_lo
    a2 = i2 - e_lo
    lo = jnp.minimum(a1, a2)
    hi = jnp.maximum(a1, a2)
    w_lo = jnp.where(a1 < a2, w_top1, w_top2)
    w_hi = jnp.where(a1 < a2, w_top2, w_top1)
    pidx = ((lo * (2 * EXPERTS_PER_GROUP - 1 - lo)) >> 1) + (hi - lo - 1)
    cls = gidx * PAIRS_PER_GROUP + pidx
    token = tile * tm + lax.broadcasted_iota(jnp.int32, (tm, LANES), 0)
    meta = jnp.where(lane == META_W_LO, lax.bitcast_convert_type(w_lo, jnp.int32),
                     jnp.where(lane == META_W_HI, lax.bitcast_convert_type(w_hi, jnp.int32),
                               jnp.where(lane == META_TOKEN, token, 0)))
    t_ref[pl.ds(META_SUBLANE, tm, stride=SUBLANES), :] = meta
    for sg in range(META_SUBLANE + 1, SUBLANES):
        t_ref[pl.ds(sg, tm, stride=SUBLANES), :] = jnp.zeros((tm, LANES), jnp.int32)

    onehot = lane == cls
    oh_bf = jnp.where(onehot, 1.0, 0.0).astype(_BF)
    rr = lax.broadcasted_iota(jnp.int32, (tm, tm), 0)
    ccol = lax.broadcasted_iota(jnp.int32, (tm, tm), 1)
    tri = jnp.where(ccol < rr, 1.0, 0.0).astype(_BF)
    before = jnp.dot(tri, oh_bf, preferred_element_type=_F32) + run_s[...]
    rank = jnp.sum(jnp.where(onehot, before, 0.0), axis=-1, keepdims=True)
    counted = jnp.where(tile >= 0, 1.0, 0.0)
    run_new = run_s[...] + counted * jnp.sum(jnp.where(onehot, 1.0, 0.0), axis=0, keepdims=True)
    run_s[...] = run_new
    cnt_ref[...] = run_new
    keyf = jnp.broadcast_to(cls.astype(_F32) * float(1 << RANK_BITS) + rank, (tm, LANES))
    for g in range(tm // LANES):
        kt = keyf[g * LANES:(g + 1) * LANES, :].T
        key_ref[0, g:g + 1, :] = kt[0:1, :].astype(jnp.int32)


def _mix(x2, attn2, cact2, gates2, wa_bf, wc_bf, wo_bf, b_conv_out, norm_ffn, w_router_cat, w_router_hi, b_router):
    n, d = x2.shape
    tm = TM_MIX
    aw = attn2.shape[1]
    cc = cact2.shape[1]
    nt = n // tm
    row = lambda i: (jnp.minimum(i, nt - 1), 0)
    prev = lambda i: (jnp.maximum(i - 1, 0), 0)
    return pl.pallas_call(
        _mix_kernel,
        grid=(nt + 1,),
        in_specs=[
            pl.BlockSpec((tm, d), row),
            pl.BlockSpec((tm, aw), row),
            pl.BlockSpec((tm, cc), row),
            pl.BlockSpec((tm, 2 * d), row),
            _const_spec(wa_bf.shape),
            _const_spec(wc_bf.shape),
            _const_spec(wo_bf.shape),
            _const_spec((1, d)),
            _const_spec((1, d)),
            _const_spec((d, 2 * LANES)),
            _const_spec((d, LANES)),
            _const_spec((1, LANES)),
        ],
        out_specs=[
            pl.BlockSpec((tm, d), row),
            pl.BlockSpec((tm * SUBLANES, LANES), prev),
            pl.BlockSpec((1, tm // LANES, LANES), lambda i: (jnp.maximum(i - 1, 0), 0, 0)),
            _const_spec((1, LANES)),
        ],
        out_shape=[
            jax.ShapeDtypeStruct((n, d), _F32),
            jax.ShapeDtypeStruct((n * SUBLANES, LANES), jnp.int32),
            jax.ShapeDtypeStruct((n // tm, tm // LANES, LANES), jnp.int32),
            jax.ShapeDtypeStruct((1, LANES), _F32),
        ],
        scratch_shapes=[
            pltpu.VMEM((1, LANES), _F32),
            pltpu.VMEM((2, tm, LANES), _F32),
            pltpu.VMEM((2, META_SUBLANE, tm, LANES), jnp.int32),
        ],
        compiler_params=pltpu.CompilerParams(
            dimension_semantics=("arbitrary",), vmem_limit_bytes=VMEM_LIMIT),
        name="mix_router",
    )(x2, attn2, cact2, gates2, wa_bf, wc_bf, wo_bf, b_conv_out.reshape(1, d), norm_ffn.reshape(1, d),
      w_router_cat, w_router_hi, b_router)


def _sorted_rows_kernel(pstart_ref, key_ref, o_ref):
    key = key_ref[...]
    cls = key >> RANK_BITS
    row = key & ((1 << RANK_BITS) - 1)
    for c in range(N_CLASSES):
        row = row + jnp.where(cls == c, pstart_ref[c], 0)
    o_ref[...] = row


def _sorted_rows(pstart, keys2):
    grid_spec = pltpu.PrefetchScalarGridSpec(
        num_scalar_prefetch=1,
        grid=(1,),
        in_specs=[pl.BlockSpec(keys2.shape, lambda i, *_: (0, 0))],
        out_specs=pl.BlockSpec(keys2.shape, lambda i, *_: (0, 0)),
    )
    return pl.pallas_call(
        _sorted_rows_kernel,
        grid_spec=grid_spec,
        out_shape=jax.ShapeDtypeStruct(keys2.shape, jnp.int32),
        name="sorted_rows",
    )(pstart, keys2)


def _token_rows(idx, count=1):
    return pl.ds(pl.multiple_of(idx * SUBLANES, SUBLANES), count * SUBLANES)


def _scatter_kernel(padstart_ref, padlen_ref, nvalid_ref, dest_ref, t_ref, xs_ref, zero_s, sem, zsem):
    i = pl.program_id(0)
    ts = t_ref.shape[0] // SUBLANES
    n_tokens = ts * pl.num_programs(0)
    bm = zero_s.shape[0] // SUBLANES // 2
    nblk = xs_ref.shape[0] // (bm * SUBLANES)

    def start_rows(g, c):
        for u in range(DMA_UNROLL):
            r = g * DMA_UNROLL + u
            pltpu.make_async_copy(t_ref.at[_token_rows(r)], xs_ref.at[_token_rows(dest_ref[0, 0, r])], sem).start(
                priority=u % 2)
        return c

    def wait_rows(g, c):
        for _ in range(DMA_UNROLL):
            pltpu.make_async_copy(t_ref.at[_token_rows(0)], xs_ref.at[_token_rows(0)], sem).wait()
        return c

    lax.fori_loop(0, ts // DMA_UNROLL, start_rows, 0)

    @pl.when(i == 0)
    def _():
        rows = lax.broadcasted_iota(jnp.int32, zero_s.shape, 0)
        lanes = lax.broadcasted_iota(jnp.int32, zero_s.shape, 1)
        is_token = ((rows & (SUBLANES - 1)) == META_SUBLANE) & (lanes == META_TOKEN)
        zero_s[...] = jnp.where(is_token, n_tokens + (rows >> (SUBLANES.bit_length() - 1)), 0)

        def pad_copies(c, act):
            ln = padlen_ref[c]
            st = padstart_ref[c]
            for k in range(bm.bit_length() - 1):
                sz = 1 << k

                @pl.when(((ln >> k) & 1) == 1)
                def _():
                    dst = st + (ln & (sz - 1))
                    src = dst & (2 * bm - 1)
                    act(pltpu.make_async_copy(zero_s.at[_token_rows(src, sz)], xs_ref.at[_token_rows(dst, sz)], zsem))

        def tail_copy(j, act):
            @pl.when(j >= nvalid_ref[0])
            def _():
                src = (j & 1) * bm
                act(pltpu.make_async_copy(zero_s.at[_token_rows(src, bm)], xs_ref.at[_token_rows(j * bm, bm)], zsem))

        for act in (lambda cp: cp.start(), lambda cp: cp.wait()):
            lax.fori_loop(0, N_CLASSES, lambda c, carry, act=act: (pad_copies(c, act), carry)[1], 0)
            lax.fori_loop(0, nblk, lambda j, carry, act=act: (tail_copy(j, act), carry)[1], 0)

    lax.fori_loop(0, ts // DMA_UNROLL, wait_rows, 0)


def _scatter_rows(padstart, padlen, nvalid, dest, t_rows, n_rows):
    n = t_rows.shape[0] // SUBLANES
    ts = TS_SCATTER
    grid_spec = pltpu.PrefetchScalarGridSpec(
        num_scalar_prefetch=3,
        grid=(n // ts,),
        in_specs=[
            pl.BlockSpec((1, 1, ts), lambda i, *_: (i, 0, 0), memory_space=pltpu.SMEM),
            pl.BlockSpec((ts * SUBLANES, LANES), lambda i, *_: (i, 0)),
        ],
        out_specs=pl.BlockSpec(memory_space=pl.ANY),
        scratch_shapes=[pltpu.VMEM((2 * BM_MOE * SUBLANES, LANES), jnp.int32), pltpu.SemaphoreType.DMA,
                        pltpu.SemaphoreType.DMA],
    )
    return pl.pallas_call(
        _scatter_kernel,
        grid_spec=grid_spec,
        out_shape=jax.ShapeDtypeStruct((n_rows * SUBLANES, LANES), jnp.int32),
        compiler_params=pltpu.CompilerParams(
            dimension_semantics=("arbitrary",), vmem_limit_bytes=VMEM_LIMIT),
        name="scatter_rows",
    )(padstart, padlen, nvalid, dest.reshape(n // ts, 1, ts), t_rows)


def _moe_kernel(ea_ref, eb_ref, valid_ref, xs_ref, w1a, w3a, w2a, w1b, w3b, w2b, out_ref,
                y_s, ids_v, ids_sm, rsem, isem):
    j = pl.program_id(0)
    nb = pl.num_programs(0)
    bm = xs_ref.shape[0] // SUBLANES
    ng = bm // LANES
    n_tokens = out_ref.shape[0] // SUBLANES - 2 * bm

    def ids_copy(sl):
        return pltpu.make_async_copy(ids_v.at[sl], ids_sm.at[sl], isem.at[sl])

    def row_copy(sl, r, tok):
        return pltpu.make_async_copy(y_s.at[sl, _token_rows(r)], out_ref.at[_token_rows(tok)], rsem.at[sl])

    def issue_rows(sl):
        for r in range(bm):
            row_copy(sl, r, ids_sm[sl, r // LANES, r % LANES]).start(priority=r % 2)

    def wait_rows(sl):
        def body(g, c):
            for _ in range(DMA_UNROLL):
                row_copy(sl, 0, 0).wait()
            return c

        lax.fori_loop(0, bm // DMA_UNROLL, body, 0)

    @pl.when(j == 0)
    def _():
        y_s[1] = jnp.zeros(y_s.shape[1:], _F32)
        pos = (lax.broadcasted_iota(jnp.int32, (ng, LANES), 0) * LANES
               + lax.broadcasted_iota(jnp.int32, (ng, LANES), 1))
        ids_v[1] = n_tokens + bm + pos
        ids_copy(1).start()
        even_spares = pltpu.make_async_copy(y_s.at[1], out_ref.at[_token_rows(n_tokens, bm)], rsem.at[0])
        even_spares.start()
        even_spares.wait()

    def send_previous(sl):
        ids_copy(1 - sl).wait()
        issue_rows(1 - sl)

    def wait_older(sl):
        @pl.when(j >= 1)
        def _():
            wait_rows(sl)

    def step(sl):
        other = 1 - sl
        send_previous(sl)

        meta = xs_ref[pl.ds(META_SUBLANE, bm, stride=SUBLANES), :]
        lane = lax.broadcasted_iota(jnp.int32, (bm, LANES), 1)
        tok = jnp.sum(jnp.where(lane == META_TOKEN, meta.astype(_F32), 0.0), axis=-1, keepdims=True)
        tokb = jnp.broadcast_to(tok, (bm, LANES))
        for g in range(ng):
            ids_v[sl, g:g + 1, :] = tokb[g * LANES:(g + 1) * LANES, :].T[0:1, :].astype(jnp.int32)
        ids_copy(sl).start()

        words = [xs_ref[pl.ds(sg, bm, stride=SUBLANES), :] for sg in range(META_SUBLANE)]
        hi = [lax.bitcast_convert_type(w & jnp.int32(-65536), _F32).astype(_BF) for w in words]
        lo = [lax.bitcast_convert_type(w << 16, _F32).astype(_BF) for w in words]
        xb = jnp.concatenate(hi + lo, axis=1)
        metaf = lax.bitcast_convert_type(meta, _F32)
        w_a = jnp.sum(jnp.where(lane == META_W_LO, metaf, 0.0), axis=-1, keepdims=True)
        w_b = jnp.sum(jnp.where(lane == META_W_HI, metaf, 0.0), axis=-1, keepdims=True)

        def expert(w1, w3, w2):
            h1 = jnp.dot(xb, w1[0], preferred_element_type=_F32)
            h3 = jnp.dot(xb, w3[0], preferred_element_type=_F32)
            hdn = h1 * _sigmoid(h1) * h3
            return jnp.dot(hdn.astype(_BF), w2[0], preferred_element_type=_F32)

        y = w_a * expert(w1a, w3a, w2a) + w_b * expert(w1b, w3b, w2b)
        wait_older(sl)
        for sg in range(SUBLANES):
            y_s[sl, pl.ds(sg, bm, stride=SUBLANES), :] = y[:, sg * LANES:(sg + 1) * LANES]

        @pl.when(j == nb - 1)
        def _():
            wait_rows(other)
            ids_copy(sl).wait()
            issue_rows(sl)
            wait_rows(sl)

    is_valid = valid_ref[j] != 0
    prev_valid = valid_ref[jnp.maximum(j - 1, 0)] != 0
    for sl in range(2):
        @pl.when((j % 2 == sl) & is_valid)
        def _(sl=sl):
            step(sl)

        @pl.when((j % 2 == sl) & jnp.logical_not(is_valid) & prev_valid)
        def _(sl=sl):
            send_previous(sl)
            wait_older(sl)
            wait_rows(1 - sl)


def _moe(ea, eb, valid, xs, w1_bf, w3_bf, w2_bf, n_tokens):
    d, ff = w1_bf.shape[1], w1_bf.shape[2]
    assert d == SUBLANES * LANES
    bm = BM_MOE
    nb = xs.shape[0] // (bm * SUBLANES)
    grid_spec = pltpu.PrefetchScalarGridSpec(
        num_scalar_prefetch=3,
        grid=(nb,),
        in_specs=[
            pl.BlockSpec((bm * SUBLANES, LANES), lambda j, ea, eb, va: (j, 0)),
            pl.BlockSpec((1, d, ff), lambda j, ea, eb, va: (ea[j], 0, 0)),
            pl.BlockSpec((1, d, ff), lambda j, ea, eb, va: (ea[j], 0, 0)),
            pl.BlockSpec((1, ff, d), lambda j, ea, eb, va: (ea[j], 0, 0)),
            pl.BlockSpec((1, d, ff), lambda j, ea, eb, va: (eb[j], 0, 0)),
            pl.BlockSpec((1, d, ff), lambda j, ea, eb, va: (eb[j], 0, 0)),
            pl.BlockSpec((1, ff, d), lambda j, ea, eb, va: (eb[j], 0, 0)),
        ],
        out_specs=pl.BlockSpec(memory_space=pl.ANY),
        scratch_shapes=[
            pltpu.VMEM((2, bm * SUBLANES, LANES), _F32),
            pltpu.VMEM((2, bm // LANES, LANES), jnp.int32),
            pltpu.SMEM((2, bm // LANES, LANES), jnp.int32),
            pltpu.SemaphoreType.DMA((2,)),
            pltpu.SemaphoreType.DMA((2,)),
        ],
    )
    return pl.pallas_call(
        _moe_kernel,
        grid_spec=grid_spec,
        out_shape=jax.ShapeDtypeStruct(((n_tokens + 2 * bm) * SUBLANES, LANES), _F32),
        compiler_params=pltpu.CompilerParams(
            dimension_semantics=("arbitrary",), vmem_limit_bytes=VMEM_LIMIT),
        name="moe_experts",
    )(ea, eb, valid, xs, w1_bf, w3_bf, w2_bf, w1_bf, w3_bf, w2_bf)


def _final_kernel(h_ref, p_ref, y_ref, np_ref, wg_ref, bg_ref, wp_ref, o_ref):
    tm = h_ref.shape[0]
    proj = jnp.dot(p_ref[...].astype(_BF), wp_ref[...], preferred_element_type=_F32)
    y = jnp.concatenate([y_ref[pl.ds(sg, tm, stride=SUBLANES), :] for sg in range(SUBLANES)], axis=1)
    h = h_ref[...] + y
    u = h * lax.rsqrt(jnp.mean(h * h, axis=-1, keepdims=True) + EPS) * np_ref[...]
    gate = _sigmoid(jnp.dot(u.astype(_BF), wg_ref[...], preferred_element_type=_F32) + bg_ref[...])
    o_ref[...] = h + gate * proj


def _final(h1, p2, y_rows, norm_ple, wg_bf, b_ple_gate, wp_bf):
    n, d = h1.shape
    assert d == SUBLANES * LANES
    tm = TM_FINAL
    pd = p2.shape[1]
    row = lambda i: (i, 0)
    return pl.pallas_call(
        _final_kernel,
        grid=(n // tm,),
        in_specs=[
            pl.BlockSpec((tm, d), row),
            pl.BlockSpec((tm, pd), row),
            pl.BlockSpec((tm * SUBLANES, LANES), row),
            _const_spec((1, d)),
            _const_spec(wg_bf.shape),
            _const_spec((1, d)),
            _const_spec(wp_bf.shape),
        ],
        out_specs=pl.BlockSpec((tm, d), row),
        out_shape=jax.ShapeDtypeStruct((n, d), _F32),
        compiler_params=pltpu.CompilerParams(
            dimension_semantics=("arbitrary",), vmem_limit_bytes=VMEM_LIMIT),
        name="final_ple",
    )(h1, p2, y_rows, norm_ple.reshape(1, d), wg_bf, b_ple_gate.reshape(1, d), wp_bf)


def _class_tables():
    ea, eb = [], []
    for g in range(N_GROUPS):
        for lo in range(EXPERTS_PER_GROUP):
            for hi in range(lo + 1, EXPERTS_PER_GROUP):
                ea.append(g * EXPERTS_PER_GROUP + lo)
                eb.append(g * EXPERTS_PER_GROUP + hi)
    return np.asarray(ea, np.int32), np.asarray(eb, np.int32)


_CLASS_EA, _CLASS_EB = _class_tables()


def kernel(x, p, norm_mix, w_in, b_conv_in, b_gate, q_norm, k_norm, lambda_q1, lambda_k1, lambda_q2, lambda_k2, subln, w_attn_out, conv_w, conv_b, conv_ln_g, conv_ln_b, w_conv_out, b_conv_out, w_o, norm_ffn, w_router_group, b_router_group, w_router_expert, b_router_expert, w1, w3, w2, norm_ple, w_ple_gate, b_ple_gate, w_ple_proj):
    b, s, d = x.shape
    n = b * s
    depth = w_in.shape[0]
    qk = ATTN_HEADS * 2 * HEAD_DIM
    aw = ATTN_HEADS * V_DIM
    cc = conv_w.shape[-1]
    assert depth == 1 and n % TM_MIX == 0 and s % (2 * ATTN_TK) == 0 and s % TM_INPROJ == 0
    assert n % TS_SCATTER == 0 and n % TM_FINAL == 0 and n % BM_MOE == 0
    i = 0
    h = x.reshape(n, d)

    conv_w_pad = jnp.pad(conv_w[i], ((0, CONV_HALO - CONV_WIDTH), (0, 0)))
    q2, k2, v2, cact2, gates2 = _inproj(h, norm_mix[i], w_in[i].astype(_BF), b_conv_in[i], b_gate[i], conv_w_pad,
                                        conv_b[i], conv_ln_g[i], conv_ln_b[i], qk, aw, cc, s)

    lam_vecs = jnp.stack([lambda_q1[i], lambda_k1[i], lambda_q2[i], lambda_k2[i]]).astype(_F32)
    attn = _attention(lam_vecs, q2.reshape(b, s, qk), k2.reshape(b, s, qk), v2.reshape(b, s, aw),
                      q_norm[i], k_norm[i], subln[i])

    w_router = jnp.concatenate([w_router_group[i], w_router_expert[i]], axis=1)
    w_router = jnp.pad(w_router, ((0, 0), (0, LANES - w_router.shape[1])))
    b_router = jnp.concatenate([b_router_group[i], b_router_expert[i]])
    b_router = jnp.pad(b_router, (0, LANES - b_router.shape[0])).reshape(1, LANES)
    w_router_hi = w_router.astype(_BF)
    w_router_lo = (w_router - w_router_hi.astype(_F32)).astype(_BF)
    w_router_cat = jnp.concatenate([w_router_hi, w_router_lo], axis=1)
    h1, t_rows, keys, cnt = _mix(h, attn.reshape(n, aw), cact2, gates2, w_attn_out[i].astype(_BF),
                                 w_conv_out[i].astype(_BF), w_o[i].astype(_BF), b_conv_out[i], norm_ffn[i],
                                 w_router_cat, w_router_hi, b_router)

    bm = BM_MOE
    nb = n // bm + N_CLASSES
    counts = cnt[0].astype(jnp.int32)
    padded = ((counts + bm - 1) // bm) * bm
    pend = jnp.cumsum(padded)
    pstart = (pend - padded).astype(jnp.int32)
    total = pend[-1]
    blk = jnp.arange(nb, dtype=jnp.int32) * bm
    valid = blk < total
    first_row = jnp.where(valid, blk, total - 1)
    bcls = jnp.sum((pend[None, :] <= first_row[:, None]).astype(jnp.int32), axis=1)
    bcls = jnp.minimum(bcls, N_CLASSES - 1)
    cls_onehot = (bcls[:, None] == jnp.arange(N_CLASSES, dtype=jnp.int32)[None, :]).astype(jnp.int32)
    ea = jnp.sum(cls_onehot * jnp.asarray(_CLASS_EA)[None, :], axis=1)
    eb = jnp.sum(cls_onehot * jnp.asarray(_CLASS_EB)[None, :], axis=1)
    nvalid = (total // bm).astype(jnp.int32).reshape(1)

    dest = _sorted_rows(pstart, keys.reshape(n // LANES, LANES))
    xs = _scatter_rows((pstart + counts).astype(jnp.int32), (padded - counts).astype(jnp.int32), nvalid,
                       dest, t_rows, nb * bm)
    y_rows = _moe(ea, eb, valid.astype(jnp.int32), xs, w1[i].astype(_BF), w3[i].astype(_BF), w2[i].astype(_BF), n)

    out = _final(h1, p[i].reshape(n, -1), y_rows, norm_ple[i], w_ple_gate[i].astype(_BF), b_ple_gate[i],
                 w_ple_proj[i].astype(_BF))
    return out.reshape(b, s, d)
```

```python
import functools
import math

import numpy as np
import jax
import jax.numpy as jnp
from jax import lax
from jax.experimental import pallas as pl
from jax.experimental.pallas import tpu as pltpu

ATTN_HEADS = 4
HEAD_DIM = 64
V_DIM = 2 * HEAD_DIM
CONV_WIDTH = 31
N_GROUPS = 4
EXPERTS_PER_GROUP = 8
PAIRS_PER_GROUP = EXPERTS_PER_GROUP * (EXPERTS_PER_GROUP - 1) // 2
N_CLASSES = N_GROUPS * PAIRS_PER_GROUP
EPS = 1e-6
LAMBDA_INIT = 0.8 - 0.6 * math.exp(-0.3 * 0)

LANES = 128
SUBLANES = 8
CONV_HALO = 32
NEG_BIG = -1e30
RANK_BITS = 17
BF16_BITS = 16
HIGH_HALF = -(1 << BF16_BITS)
META_SUBLANE = 4
META_W_LO, META_W_HI, META_TOKEN = 0, 1, 2
DMA_UNROLL = 8

TM_INPROJ = 512
ATTN_TK = 512
TM_MIX = 512
CONV_ROWS = 64
TS_SCATTER = 512
BM_MOE = 256
TM_FINAL = 512
VMEM_LIMIT = 56 * 1024 * 1024

_BF = jnp.bfloat16
_F32 = jnp.float32


def _const_spec(shape):
    nd = len(shape)
    return pl.BlockSpec(shape, lambda *_: (0,) * nd)


def _sigmoid(x):
    return 1.0 / (1.0 + jnp.exp(-x))


def _inproj_kernel(x_ref, nm_ref, w_ref, bci_ref, bg_ref, cw_ref, cb_ref, lng_ref, lnb_ref,
                   q_ref, k_ref, v_ref, c_ref, g_ref, ext_s, xsh_s, cv_s, *, tiles_per_seq):
    i = pl.program_id(0)
    tm = x_ref.shape[0]
    qk = q_ref.shape[1]
    aw = v_ref.shape[1]
    cc = c_ref.shape[1]
    ext_rows = CONV_HALO + tm
    o0, o1, o2, o3 = qk, 2 * qk, 2 * qk + aw, 2 * qk + aw + 2 * cc

    @pl.when(i == 0)
    def _():
        ext_s[ext_rows:ext_rows + SUBLANES, :] = jnp.zeros((SUBLANES, cc), _F32)

    @pl.when((i % tiles_per_seq) == 0)
    def _():
        ext_s[0:CONV_HALO, :] = jnp.zeros((CONV_HALO, cc), _F32)

    x = x_ref[...]
    u = x * lax.rsqrt(jnp.mean(x * x, axis=-1, keepdims=True) + EPS) * nm_ref[...]
    ub = u.astype(_BF)
    c = jnp.dot(ub, w_ref[:, o2:o3], preferred_element_type=_F32) + bci_ref[...]
    ext_s[CONV_HALO:ext_rows, :] = c[:, :cc] * _sigmoid(c[:, cc:])
    q_ref[...] = jnp.dot(ub, w_ref[:, 0:o0], preferred_element_type=_F32).astype(_BF)
    k_ref[...] = jnp.dot(ub, w_ref[:, o0:o1], preferred_element_type=_F32).astype(_BF)
    v_ref[...] = jnp.dot(ub, w_ref[:, o1:o2], preferred_element_type=_F32).astype(_BF)
    g = jnp.dot(ub, w_ref[:, o3:], preferred_element_type=_F32) + bg_ref[...]
    g_ref[...] = _sigmoid(g).astype(_BF)
    off = CONV_HALO - (CONV_WIDTH - 1)

    for rho in range(1, SUBLANES):
        xsh_s[rho - 1] = ext_s[rho:rho + ext_rows, :]

    for r0 in range(0, tm, CONV_ROWS):
        acc = jnp.zeros((CONV_ROWS, cc), _F32)
        for j in range(CONV_WIDTH):
            a, rho = divmod(off + j, SUBLANES)
            lo_row = r0 + a * SUBLANES
            if rho == 0:
                xv = ext_s[lo_row:lo_row + CONV_ROWS, :]
            else:
                xv = xsh_s[rho - 1, lo_row:lo_row + CONV_ROWS, :]
            acc = acc + cw_ref[j:j + 1, :] * xv
        cv_s[r0:r0 + CONV_ROWS, :] = acc

    ext_s[0:CONV_HALO, :] = ext_s[tm:ext_rows, :]

    cv = cv_s[...] + cb_ref[...]
    mu = jnp.mean(cv, axis=-1, keepdims=True)
    xc = cv - mu
    var = jnp.mean(xc * xc, axis=-1, keepdims=True)
    ln = xc * lax.rsqrt(var + EPS) * lng_ref[...] + lnb_ref[...]
    c_ref[...] = (ln * _sigmoid(ln)).astype(_BF)


def _inproj(x2, norm_mix, w_in_bf, b_conv_in, b_gate, conv_w, conv_b, ln_g, ln_b, qk, aw, cc, seq_len):
    n, d = x2.shape
    tm = TM_INPROJ
    ng = b_gate.shape[-1]
    kern = functools.partial(_inproj_kernel, tiles_per_seq=seq_len // tm)
    return pl.pallas_call(
        kern,
        grid=(n // tm,),
        in_specs=[
            pl.BlockSpec((tm, d), lambda i: (i, 0)),
            _const_spec((1, d)),
            pl.BlockSpec(w_in_bf.shape, lambda i: (0, 0), pipeline_mode=pl.Buffered(1)),
            _const_spec((1, 2 * cc)),
            _const_spec((1, ng)),
            _const_spec((CONV_HALO, cc)),
            _const_spec((1, cc)),
            _const_spec((1, cc)),
            _const_spec((1, cc)),
        ],
        out_specs=[
            pl.BlockSpec((tm, qk), lambda i: (i, 0)),
            pl.BlockSpec((tm, qk), lambda i: (i, 0)),
            pl.BlockSpec((tm, aw), lambda i: (i, 0)),
            pl.BlockSpec((tm, cc), lambda i: (i, 0)),
            pl.BlockSpec((tm, ng), lambda i: (i, 0)),
        ],
        out_shape=[
            jax.ShapeDtypeStruct((n, qk), _BF),
            jax.ShapeDtypeStruct((n, qk), _BF),
            jax.ShapeDtypeStruct((n, aw), _BF),
            jax.ShapeDtypeStruct((n, cc), _BF),
            jax.ShapeDtypeStruct((n, ng), _BF),
        ],
        scratch_shapes=[
            pltpu.VMEM((CONV_HALO + tm + SUBLANES, cc), _F32),
            pltpu.VMEM((SUBLANES - 1, CONV_HALO + tm, cc), _F32),
            pltpu.VMEM((tm, cc), _F32),
        ],
        compiler_params=pltpu.CompilerParams(
            dimension_semantics=("arbitrary",), vmem_limit_bytes=VMEM_LIMIT),
        name="inproj",
    )(x2, norm_mix.reshape(1, d), w_in_bf, b_conv_in.reshape(1, -1), b_gate.reshape(1, -1), conv_w,
      conv_b.reshape(1, cc), ln_g.reshape(1, cc), ln_b.reshape(1, cc))


def _half_rmsnorm(x, gain_row, lo_mask):
    x2 = x * x
    s_lo = jnp.sum(jnp.where(lo_mask, x2, 0.0), axis=-1, keepdims=True)
    s_hi = jnp.sum(jnp.where(lo_mask, 0.0, x2), axis=-1, keepdims=True)
    r = jnp.where(lo_mask, lax.rsqrt(s_lo * (1.0 / HEAD_DIM) + EPS), lax.rsqrt(s_hi * (1.0 / HEAD_DIM) + EPS))
    return x * r * gain_row


def _attn_kernel(lam_ref, q_ref, k_ref, v_ref, qn_ref, kn_ref, sub_ref, o_ref, kn_s, va_s, qs_s, m_s, acc_s,
                 sa_s, sb_s, sc_s):
    s_len = q_ref.shape[1]
    tk = ATTN_TK
    tq = 2 * tk
    nq = s_len // tq
    lane = lax.broadcasted_iota(jnp.int32, (1, V_DIM), 1)
    lo_mask = lane < HEAD_DIM

    lv = lam_ref[...]
    d1 = jnp.sum(lv[0:1, :] * lv[1:2, :], axis=-1, keepdims=True)
    d2 = jnp.sum(lv[2:3, :] * lv[3:4, :], axis=-1, keepdims=True)
    lam = jnp.exp(d1) - jnp.exp(d2) + LAMBDA_INIT

    kgain = kn_ref[...]
    qgain = qn_ref[...] * (math.log2(math.e) / math.sqrt(HEAD_DIM))

    def prep_kv(i, carry):
        r0 = pl.multiple_of(i * tq, tq)
        kk = k_ref[0, pl.ds(r0, tq), :].astype(_F32)
        kn_s[pl.ds(r0, tq), :] = _half_rmsnorm(kk, kgain, lo_mask).astype(_BF)
        va_s[pl.ds(r0, tq), 0:V_DIM] = v_ref[0, pl.ds(r0, tq), :]
        va_s[pl.ds(r0, tq), V_DIM:2 * V_DIM] = jnp.ones((tq, V_DIM), _BF)
        return carry

    lax.fori_loop(0, nq, prep_kv, 0)

    def scores(c0, ra, nr):
        kc = kn_s[pl.ds(c0, tk), :]
        return lax.dot_general(qs_s[ra:ra + nr, :], kc, (((1,), (1,)), ((), ())), preferred_element_type=_F32)

    def consume(s, c0, ra, nr, n_masked):
        vc = va_s[pl.ds(c0, tk), :]
        if n_masked:
            keep = (lax.broadcasted_iota(jnp.int32, (tk, tk), 1) <= lax.broadcasted_iota(jnp.int32, (tk, tk), 0))
            groups = [s[g * tk:(g + 1) * tk, :] for g in range(nr // tk)]
            groups = [jnp.where(keep, sg, NEG_BIG) if g < n_masked else sg for g, sg in enumerate(groups)]
            s = jnp.concatenate(groups, axis=0)
        m_old = m_s[ra:ra + nr, :]
        m_new = jnp.maximum(m_old, jnp.max(s, axis=-1, keepdims=True))
        alpha = jnp.exp2(m_old - m_new)
        p = jnp.concatenate(
            [jnp.exp2(s[:, c * LANES:(c + 1) * LANES] - m_new) for c in range(tk // LANES)], axis=1)
        pv = jnp.dot(p.astype(_BF), vc, preferred_element_type=_F32)
        acc_s[ra:ra + nr, :] = jnp.concatenate([alpha, alpha], axis=1) * acc_s[ra:ra + nr, :] + pv
        m_s[ra:ra + nr, :] = m_new

    def prep_q(r0):
        for hf in range(2):
            qq = _half_rmsnorm(q_ref[0, pl.ds(r0 + hf * tk, tk), :].astype(_F32), qgain, lo_mask)
            qs_s[(2 * hf) * tk:(2 * hf + 1) * tk, :] = jnp.where(lo_mask, qq, 0.0).astype(_BF)
            qs_s[(2 * hf + 1) * tk:(2 * hf + 2) * tk, :] = jnp.where(lo_mask, 0.0, qq).astype(_BF)

    bufs = (sa_s, sb_s, sc_s)
    prep_q(0)
    sa_s[...] = scores(0, 0, 4 * tk)

    for qi in range(nq):
        r0 = qi * tq
        cur = bufs[(2 * qi) % 3]
        nxt = bufs[(2 * qi + 1) % 3]
        ahead = bufs[(2 * qi + 2) % 3]
        m_s[...] = jnp.full(m_s.shape, NEG_BIG, _F32)
        acc_s[...] = jnp.zeros(acc_s.shape, _F32)

        def two_chunks(t, c, cur=cur, nxt=nxt):
            c0 = pl.multiple_of(2 * t * tk, tk)
            c1 = pl.multiple_of(c0 + tk, tk)
            c2 = pl.multiple_of(c0 + 2 * tk, tk)
            nxt[...] = scores(c1, 0, 4 * tk)
            consume(cur[...], c0, 0, 4 * tk, 0)
            cur[...] = scores(c2, 0, 4 * tk)
            consume(nxt[...], c1, 0, 4 * tk, 0)
            return c

        lax.fori_loop(0, qi, two_chunks, 0)
        nxt[0:2 * tk, :] = scores(r0 + tk, 2 * tk, 2 * tk)
        if qi + 1 < nq:
            prep_q(r0 + tq)
            ahead[...] = scores(0, 0, 4 * tk)
        consume(cur[...], r0, 0, 4 * tk, 2)
        consume(nxt[0:2 * tk, :], r0 + tk, 2 * tk, 2 * tk, 2)

        acc = acc_s[...]
        o = acc[:, 0:V_DIM] / acc[:, V_DIM:2 * V_DIM]
        for hf in range(2):
            a = o[(2 * hf) * tk:(2 * hf + 1) * tk, :] - lam * o[(2 * hf + 1) * tk:(2 * hf + 2) * tk, :]
            a = a * lax.rsqrt(jnp.mean(a * a, axis=-1, keepdims=True) + EPS) * sub_ref[...] * (1.0 - LAMBDA_INIT)
            o_ref[0, r0 + hf * tk:r0 + (hf + 1) * tk, :] = a.astype(_BF)


def _attention(lam_vecs, q3, k3, v3, q_norm, k_norm, subln):
    b, s, _ = q3.shape
    tq = 2 * ATTN_TK
    slab = lambda bi, hi: (bi, 0, hi)
    return pl.pallas_call(
        _attn_kernel,
        grid=(b, ATTN_HEADS),
        in_specs=[
            _const_spec((4, HEAD_DIM)),
            pl.BlockSpec((1, s, V_DIM), slab),
            pl.BlockSpec((1, s, V_DIM), slab),
            pl.BlockSpec((1, s, V_DIM), slab),
            _const_spec((1, V_DIM)),
            _const_spec((1, V_DIM)),
            _const_spec((1, V_DIM)),
        ],
        out_specs=pl.BlockSpec((1, s, V_DIM), slab),
        out_shape=jax.ShapeDtypeStruct((b, s, ATTN_HEADS * V_DIM), _BF),
        scratch_shapes=[
            pltpu.VMEM((s, V_DIM), _BF),
            pltpu.VMEM((s, 2 * V_DIM), _BF),
            pltpu.VMEM((2 * tq, V_DIM), _BF),
            pltpu.VMEM((2 * tq, V_DIM), _F32),
            pltpu.VMEM((2 * tq, 2 * V_DIM), _F32),
            pltpu.VMEM((2 * tq, ATTN_TK), _F32),
            pltpu.VMEM((2 * tq, ATTN_TK), _F32),
            pltpu.VMEM((2 * tq, ATTN_TK), _F32),
        ],
        compiler_params=pltpu.CompilerParams(
            dimension_semantics=("arbitrary", "arbitrary"), vmem_limit_bytes=VMEM_LIMIT),
        name="diffattn",
    )(lam_vecs, q3, k3, v3, q_norm.reshape(1, V_DIM), k_norm.reshape(1, V_DIM), subln.reshape(1, V_DIM))


def _router_logits(t, wr_ref, wrh_ref, br_ref):
    t_hi = t.astype(_BF)
    t_lo = (t - t_hi.astype(_F32)).astype(_BF)
    l_hi = jnp.dot(t_hi, wr_ref[...], preferred_element_type=_F32)
    l_lo = jnp.dot(t_lo, wrh_ref[...], preferred_element_type=_F32)
    return l_hi[:, 0:LANES] + l_hi[:, LANES:2 * LANES] + l_lo + br_ref[...]


def _mix_kernel(x_ref, a_ref, c_ref, g_ref, wa_ref, wc_ref, wo_ref, bco_ref, nf_ref, wr_ref, wrh_ref, br_ref,
                h_ref, t_ref, key_ref, cnt_ref, run_s, lg_s, tw_s, tri_s):
    i = pl.program_id(0)
    tm, d = x_ref.shape
    half_groups = d // LANES // 2

    @pl.when(i == 0)
    def _():
        run_s[...] = jnp.zeros(run_s.shape, _F32)
        lg_s[1] = jnp.zeros(lg_s.shape[1:], _F32)
        tw_s[1] = jnp.zeros(tw_s.shape[1:], jnp.int32)
        rr = lax.broadcasted_iota(jnp.int32, (tm, tm), 0)
        ccol = lax.broadcasted_iota(jnp.int32, (tm, tm), 1)
        tri_s[...] = jnp.where(ccol < rr, 1.0, 0.0).astype(_BF)

    def tile_matmuls(sl):
        y_b = jnp.dot(c_ref[...], wc_ref[...], preferred_element_type=_F32) + bco_ref[...]
        y_a = jnp.dot(a_ref[...], wa_ref[...], preferred_element_type=_F32)
        merged = g_ref[:, 0:d].astype(_F32) * y_a + g_ref[:, d:2 * d].astype(_F32) * y_b
        h = x_ref[...] + jnp.dot(merged.astype(_BF), wo_ref[...], preferred_element_type=_F32)
        h_ref[...] = h
        t = h * lax.rsqrt(jnp.mean(h * h, axis=-1, keepdims=True) + EPS) * nf_ref[...]
        for sg in range(half_groups):
            hi_bits = lax.bitcast_convert_type(
                t[:, sg * LANES:(sg + 1) * LANES].astype(_BF).astype(_F32), jnp.uint32)
            lo_bits = lax.bitcast_convert_type(
                t[:, (sg + half_groups) * LANES:(sg + half_groups + 1) * LANES].astype(_BF).astype(_F32),
                jnp.uint32)
            tw_s[sl, sg] = lax.bitcast_convert_type(hi_bits | (lo_bits >> BF16_BITS), jnp.int32)
        lg_s[sl] = _router_logits(t, wr_ref, wrh_ref, br_ref)

    for sl in range(2):
        @pl.when(i % 2 == sl)
        def _(sl=sl):
            tile_matmuls(sl)
            _route_tile(i - 1, lg_s[1 - sl], tw_s.at[1 - sl], t_ref, key_ref, cnt_ref, run_s, tri_s)


def _route_tile(tile, lg, words_ref, t_ref, key_ref, cnt_ref, run_s, tri_s):
    tm = lg.shape[0]
    for sg in range(META_SUBLANE):
        t_ref[pl.ds(sg, tm, stride=SUBLANES), :] = words_ref[sg]
    lane = lax.broadcasted_iota(jnp.int32, (tm, LANES), 1)
    gl = jnp.where(lane < N_GROUPS, lg, NEG_BIG)
    gmax = jnp.max(gl, axis=-1, keepdims=True)
    gidx = jnp.min(jnp.where(gl == gmax, lane, LANES), axis=-1, keepdims=True)
    gsum = jnp.sum(jnp.where(lane < N_GROUPS, jnp.exp(gl - gmax), 0.0), axis=-1, keepdims=True)
    g_gate = 1.0 / gsum
    e_lo = N_GROUPS + EXPERTS_PER_GROUP * gidx
    el = jnp.where((lane >= e_lo) & (lane < e_lo + EXPERTS_PER_GROUP), lg, NEG_BIG)
    m1 = jnp.max(el, axis=-1, keepdims=True)
    i1 = jnp.min(jnp.where(el == m1, lane, LANES), axis=-1, keepdims=True)
    el2 = jnp.where(lane == i1, NEG_BIG, el)
    m2 = jnp.max(el2, axis=-1, keepdims=True)
    i2 = jnp.min(jnp.where(el2 == m2, lane, LANES), axis=-1, keepdims=True)
    e21 = jnp.exp(m2 - m1)
    w_top1 = g_gate * (1.0 / (1.0 + e21))
    w_top2 = g_gate * (e21 / (1.0 + e21))
    a1 = i1 - e_lo
    a2 = i2 - e_lo
    lo = jnp.minimum(a1, a2)
    hi = jnp.maximum(a1, a2)
    w_lo = jnp.where(a1 < a2, w_top1, w_top2)
    w_hi = jnp.where(a1 < a2, w_top2, w_top1)
    pidx = ((lo * (2 * EXPERTS_PER_GROUP - 1 - lo)) >> 1) + (hi - lo - 1)
    cls = gidx * PAIRS_PER_GROUP + pidx
    token = tile * tm + lax.broadcasted_iota(jnp.int32, (tm, LANES), 0)
    meta = jnp.where(lane == META_W_LO, lax.bitcast_convert_type(w_lo, jnp.int32),
                     jnp.where(lane == META_W_HI, lax.bitcast_convert_type(w_hi, jnp.int32),
                               jnp.where(lane == META_TOKEN, token, 0)))
    t_ref[pl.ds(META_SUBLANE, tm, stride=SUBLANES), :] = meta
    for sg in range(META_SUBLANE + 1, SUBLANES):
        t_ref[pl.ds(sg, tm, stride=SUBLANES), :] = jnp.zeros((tm, LANES), jnp.int32)

    onehot = lane == cls
    oh_bf = jnp.where(onehot, 1.0, 0.0).astype(_BF)
    before = jnp.dot(tri_s[...], oh_bf, preferred_element_type=_F32) + run_s[...]
    rank = jnp.sum(jnp.where(onehot, before, 0.0), axis=-1, keepdims=True)
    counted = jnp.where(tile >= 0, 1.0, 0.0)
    run_new = run_s[...] + counted * jnp.sum(jnp.where(onehot, 1.0, 0.0), axis=0, keepdims=True)
    run_s[...] = run_new
    cnt_ref[...] = run_new
    keyf = jnp.broadcast_to(cls.astype(_F32) * float(1 << RANK_BITS) + rank, (tm, LANES))
    for g in range(tm // LANES):
        kt = keyf[g * LANES:(g + 1) * LANES, :].T
        key_ref[0, g:g + 1, :] = kt[0:1, :].astype(jnp.int32)


def _mix(x2, attn2, cact2, gates2, wa_bf, wc_bf, wo_bf, b_conv_out, norm_ffn, w_router_cat, w_router_hi, b_router):
    n, d = x2.shape
    tm = TM_MIX
    aw = attn2.shape[1]
    cc = cact2.shape[1]
    nt = n // tm
    row = lambda i: (jnp.minimum(i, nt - 1), 0)
    prev = lambda i: (jnp.maximum(i - 1, 0), 0)
    return pl.pallas_call(
        _mix_kernel,
        grid=(nt + 1,),
        in_specs=[
            pl.BlockSpec((tm, d), row),
            pl.BlockSpec((tm, aw), row),
            pl.BlockSpec((tm, cc), row),
            pl.BlockSpec((tm, 2 * d), row),
            _const_spec(wa_bf.shape),
            _const_spec(wc_bf.shape),
            _const_spec(wo_bf.shape),
            _const_spec((1, d)),
            _const_spec((1, d)),
            _const_spec((d, 2 * LANES)),
            _const_spec((d, LANES)),
            _const_spec((1, LANES)),
        ],
        out_specs=[
            pl.BlockSpec((tm, d), row),
            pl.BlockSpec((tm * SUBLANES, LANES), prev),
            pl.BlockSpec((1, tm // LANES, LANES), lambda i: (jnp.maximum(i - 1, 0), 0, 0)),
            _const_spec((1, LANES)),
        ],
        out_shape=[
            jax.ShapeDtypeStruct((n, d), _F32),
            jax.ShapeDtypeStruct((n * SUBLANES, LANES), jnp.int32),
            jax.ShapeDtypeStruct((n // tm, tm // LANES, LANES), jnp.int32),
            jax.ShapeDtypeStruct((1, LANES), _F32),
        ],
        scratch_shapes=[
            pltpu.VMEM((1, LANES), _F32),
            pltpu.VMEM((2, tm, LANES), _F32),
            pltpu.VMEM((2, META_SUBLANE, tm, LANES), jnp.int32),
            pltpu.VMEM((tm, tm), _BF),
        ],
        compiler_params=pltpu.CompilerParams(
            dimension_semantics=("arbitrary",), vmem_limit_bytes=VMEM_LIMIT),
        name="mix_router",
    )(x2, attn2, cact2, gates2, wa_bf, wc_bf, wo_bf, b_conv_out.reshape(1, d), norm_ffn.reshape(1, d),
      w_router_cat, w_router_hi, b_router)


def _sorted_rows_kernel(pstart_ref, key_ref, o_ref):
    key = key_ref[...]
    cls = key >> RANK_BITS
    row = key & ((1 << RANK_BITS) - 1)
    for c in range(N_CLASSES):
        row = row + jnp.where(cls == c, pstart_ref[c], 0)
    o_ref[...] = row


def _sorted_rows(pstart, keys2):
    grid_spec = pltpu.PrefetchScalarGridSpec(
        num_scalar_prefetch=1,
        grid=(1,),
        in_specs=[pl.BlockSpec(keys2.shape, lambda i, *_: (0, 0))],
        out_specs=pl.BlockSpec(keys2.shape, lambda i, *_: (0, 0)),
    )
    return pl.pallas_call(
        _sorted_rows_kernel,
        grid_spec=grid_spec,
        out_shape=jax.ShapeDtypeStruct(keys2.shape, jnp.int32),
        name="sorted_rows",
    )(pstart, keys2)


def _token_rows(idx, count=1):
    return pl.ds(pl.multiple_of(idx * SUBLANES, SUBLANES), count * SUBLANES)


def _scatter_kernel(padstart_ref, padlen_ref, nvalid_ref, dest_ref, t_ref, xs_ref, zero_s, sem, zsem):
    i = pl.program_id(0)
    ts = t_ref.shape[0] // SUBLANES
    n_tokens = ts * pl.num_programs(0)
    bm = zero_s.shape[0] // SUBLANES
    nblk = xs_ref.shape[0] // (bm * SUBLANES)

    def start_rows(g, c):
        for u in range(DMA_UNROLL):
            r = g * DMA_UNROLL + u
            pltpu.make_async_copy(t_ref.at[_token_rows(r)], xs_ref.at[_token_rows(dest_ref[0, 0, r])], sem).start(
                priority=u % 2)
        return c

    def wait_rows(g, c):
        for _ in range(DMA_UNROLL):
            pltpu.make_async_copy(t_ref.at[_token_rows(0)], xs_ref.at[_token_rows(0)], sem).wait()
        return c

    lax.fori_loop(0, ts // DMA_UNROLL, start_rows, 0)

    @pl.when(i == 0)
    def _():
        rows = lax.broadcasted_iota(jnp.int32, zero_s.shape, 0)
        lanes = lax.broadcasted_iota(jnp.int32, zero_s.shape, 1)
        is_token = ((rows & (SUBLANES - 1)) == META_SUBLANE) & (lanes == META_TOKEN)
        zero_s[...] = jnp.where(is_token, n_tokens + (rows >> (SUBLANES.bit_length() - 1)), 0)

        def pad_copies(c, act):
            ln = padlen_ref[c]
            st = padstart_ref[c]
            for k in range(bm.bit_length() - 1):
                sz = 1 << k

                @pl.when(((ln >> k) & 1) == 1)
                def _():
                    dst = st + (ln & (sz - 1))
                    src = dst & (bm - 1)
                    act(pltpu.make_async_copy(zero_s.at[_token_rows(src, sz)], xs_ref.at[_token_rows(dst, sz)], zsem))

        def tail_copy(j, act):
            @pl.when(j >= nvalid_ref[0])
            def _():
                act(pltpu.make_async_copy(zero_s, xs_ref.at[_token_rows(j * bm, bm)], zsem))

        for act in (lambda cp: cp.start(), lambda cp: cp.wait()):
            lax.fori_loop(0, N_CLASSES, lambda c, carry, act=act: (pad_copies(c, act), carry)[1], 0)
            lax.fori_loop(0, nblk, lambda j, carry, act=act: (tail_copy(j, act), carry)[1], 0)

    lax.fori_loop(0, ts // DMA_UNROLL, wait_rows, 0)


def _scatter_rows(padstart, padlen, nvalid, dest, t_rows, n_rows):
    n = t_rows.shape[0] // SUBLANES
    ts = TS_SCATTER
    grid_spec = pltpu.PrefetchScalarGridSpec(
        num_scalar_prefetch=3,
        grid=(n // ts,),
        in_specs=[
            pl.BlockSpec((1, 1, ts), lambda i, *_: (i, 0, 0), memory_space=pltpu.SMEM),
            pl.BlockSpec((ts * SUBLANES, LANES), lambda i, *_: (i, 0)),
        ],
        out_specs=pl.BlockSpec(memory_space=pl.ANY),
        scratch_shapes=[pltpu.VMEM((BM_MOE * SUBLANES, LANES), jnp.int32), pltpu.SemaphoreType.DMA,
                        pltpu.SemaphoreType.DMA],
    )
    return pl.pallas_call(
        _scatter_kernel,
        grid_spec=grid_spec,
        out_shape=jax.ShapeDtypeStruct((n_rows * SUBLANES, LANES), jnp.int32),
        compiler_params=pltpu.CompilerParams(
            dimension_semantics=("arbitrary",), vmem_limit_bytes=VMEM_LIMIT),
        name="scatter_rows",
    )(padstart, padlen, nvalid, dest.reshape(n // ts, 1, ts), t_rows)


def _moe_kernel(ea_ref, eb_ref, valid_ref, xs_ref, w1a, w3a, w2a, w1b, w3b, w2b, out_ref,
                y_s, ids_v, ids_sm, rsem, isem):
    j = pl.program_id(0)
    nb = pl.num_programs(0)
    bm = xs_ref.shape[0] // SUBLANES
    ng = bm // LANES
    n_tokens = out_ref.shape[0] // SUBLANES - bm

    def ids_copy(sl):
        return pltpu.make_async_copy(ids_v.at[sl], ids_sm.at[sl], isem.at[sl])

    def row_copy(sl, r, tok):
        return pltpu.make_async_copy(y_s.at[sl, _token_rows(r)], out_ref.at[_token_rows(tok)], rsem.at[sl])

    def issue_rows(sl):
        for r in range(bm):
            row_copy(sl, r, ids_sm[sl, r // LANES, r % LANES]).start(priority=r % 2)

    def wait_rows(sl):
        def body(g, c):
            for _ in range(DMA_UNROLL):
                row_copy(sl, 0, 0).wait()
            return c

        lax.fori_loop(0, bm // DMA_UNROLL, body, 0)

    @pl.when(j == 0)
    def _():
        y_s[1] = jnp.zeros(y_s.shape[1:], _F32)
        pos = (lax.broadcasted_iota(jnp.int32, (ng, LANES), 0) * LANES
               + lax.broadcasted_iota(jnp.int32, (ng, LANES), 1))
        ids_v[1] = n_tokens + pos
        ids_copy(1).start()

    def send_previous(sl):
        ids_copy(1 - sl).wait()

        @pl.when(j >= 1)
        def _():
            wait_rows(sl)

        issue_rows(1 - sl)

    def step(sl):
        other = 1 - sl
        send_previous(sl)

        meta = xs_ref[pl.ds(META_SUBLANE, bm, stride=SUBLANES), :]
        lane = lax.broadcasted_iota(jnp.int32, (bm, LANES), 1)
        tok = jnp.sum(jnp.where(lane == META_TOKEN, meta.astype(_F32), 0.0), axis=-1, keepdims=True)
        tokb = jnp.broadcast_to(tok, (bm, LANES))
        for g in range(ng):
            ids_v[sl, g:g + 1, :] = tokb[g * LANES:(g + 1) * LANES, :].T[0:1, :].astype(jnp.int32)
        ids_copy(sl).start()

        words = [xs_ref[pl.ds(sg, bm, stride=SUBLANES), :] for sg in range(META_SUBLANE)]
        hi = [lax.bitcast_convert_type(w & jnp.int32(HIGH_HALF), _F32).astype(_BF) for w in words]
        lo = [lax.bitcast_convert_type(w << BF16_BITS, _F32).astype(_BF) for w in words]
        xb = jnp.concatenate(hi + lo, axis=1)
        metaf = lax.bitcast_convert_type(meta, _F32)
        w_a = jnp.sum(jnp.where(lane == META_W_LO, metaf, 0.0), axis=-1, keepdims=True)
        w_b = jnp.sum(jnp.where(lane == META_W_HI, metaf, 0.0), axis=-1, keepdims=True)

        def expert(w1, w3, w2):
            h1 = jnp.dot(xb, w1[0], preferred_element_type=_F32)
            h3 = jnp.dot(xb, w3[0], preferred_element_type=_F32)
            hdn = h1 * _sigmoid(h1) * h3
            return jnp.dot(hdn.astype(_BF), w2[0], preferred_element_type=_F32)

        y = w_a * expert(w1a, w3a, w2a) + w_b * expert(w1b, w3b, w2b)
        for sg in range(SUBLANES):
            y_s[sl, pl.ds(sg, bm, stride=SUBLANES), :] = y[:, sg * LANES:(sg + 1) * LANES]

        @pl.when(j == nb - 1)
        def _():
            wait_rows(other)
            ids_copy(sl).wait()
            issue_rows(sl)
            wait_rows(sl)

    is_valid = valid_ref[j] != 0
    prev_valid = valid_ref[jnp.maximum(j - 1, 0)] != 0
    for sl in range(2):
        @pl.when((j % 2 == sl) & is_valid)
        def _(sl=sl):
            step(sl)

        @pl.when((j % 2 == sl) & jnp.logical_not(is_valid) & prev_valid)
        def _(sl=sl):
            send_previous(sl)
            wait_rows(1 - sl)


def _moe(ea, eb, valid, xs, w1_bf, w3_bf, w2_bf, n_tokens):
    d, ff = w1_bf.shape[1], w1_bf.shape[2]
    assert d == SUBLANES * LANES
    bm = BM_MOE
    nb = xs.shape[0] // (bm * SUBLANES)
    grid_spec = pltpu.PrefetchScalarGridSpec(
        num_scalar_prefetch=3,
        grid=(nb,),
        in_specs=[
            pl.BlockSpec((bm * SUBLANES, LANES), lambda j, ea, eb, va: (j, 0)),
            pl.BlockSpec((1, d, ff), lambda j, ea, eb, va: (ea[j], 0, 0)),
            pl.BlockSpec((1, d, ff), lambda j, ea, eb, va: (ea[j], 0, 0)),
            pl.BlockSpec((1, ff, d), lambda j, ea, eb, va: (ea[j], 0, 0)),
            pl.BlockSpec((1, d, ff), lambda j, ea, eb, va: (eb[j], 0, 0)),
            pl.BlockSpec((1, d, ff), lambda j, ea, eb, va: (eb[j], 0, 0)),
            pl.BlockSpec((1, ff, d), lambda j, ea, eb, va: (eb[j], 0, 0)),
        ],
        out_specs=pl.BlockSpec(memory_space=pl.ANY),
        scratch_shapes=[
            pltpu.VMEM((2, bm * SUBLANES, LANES), _F32),
            pltpu.VMEM((2, bm // LANES, LANES), jnp.int32),
            pltpu.SMEM((2, bm // LANES, LANES), jnp.int32),
            pltpu.SemaphoreType.DMA((2,)),
            pltpu.SemaphoreType.DMA((2,)),
        ],
    )
    return pl.pallas_call(
        _moe_kernel,
        grid_spec=grid_spec,
        out_shape=jax.ShapeDtypeStruct(((n_tokens + bm) * SUBLANES, LANES), _F32),
        compiler_params=pltpu.CompilerParams(
            dimension_semantics=("arbitrary",), vmem_limit_bytes=VMEM_LIMIT),
        name="moe_experts",
    )(ea, eb, valid, xs, w1_bf, w3_bf, w2_bf, w1_bf, w3_bf, w2_bf)


def _final_kernel(h_ref, p_ref, y_ref, np_ref, wg_ref, bg_ref, wp_ref, o_ref):
    tm = h_ref.shape[0]
    proj = jnp.dot(p_ref[...].astype(_BF), wp_ref[...], preferred_element_type=_F32)
    y = jnp.concatenate([y_ref[pl.ds(sg, tm, stride=SUBLANES), :] for sg in range(SUBLANES)], axis=1)
    h = h_ref[...] + y
    u = h * lax.rsqrt(jnp.mean(h * h, axis=-1, keepdims=True) + EPS) * np_ref[...]
    gate = _sigmoid(jnp.dot(u.astype(_BF), wg_ref[...], preferred_element_type=_F32) + bg_ref[...])
    o_ref[...] = h + gate * proj


def _final(h1, p2, y_rows, norm_ple, wg_bf, b_ple_gate, wp_bf):
    n, d = h1.shape
    assert d == SUBLANES * LANES
    tm = TM_FINAL
    pd = p2.shape[1]
    row = lambda i: (i, 0)
    return pl.pallas_call(
        _final_kernel,
        grid=(n // tm,),
        in_specs=[
            pl.BlockSpec((tm, d), row),
            pl.BlockSpec((tm, pd), row),
            pl.BlockSpec((tm * SUBLANES, LANES), row),
            _const_spec((1, d)),
            _const_spec(wg_bf.shape),
            _const_spec((1, d)),
            _const_spec(wp_bf.shape),
        ],
        out_specs=pl.BlockSpec((tm, d), row),
        out_shape=jax.ShapeDtypeStruct((n, d), _F32),
        compiler_params=pltpu.CompilerParams(
            dimension_semantics=("arbitrary",), vmem_limit_bytes=VMEM_LIMIT),
        name="final_ple",
    )(h1, p2, y_rows, norm_ple.reshape(1, d), wg_bf, b_ple_gate.reshape(1, d), wp_bf)


def _class_tables():
    ea, eb = [], []
    for g in range(N_GROUPS):
        for lo in range(EXPERTS_PER_GROUP):
            for hi in range(lo + 1, EXPERTS_PER_GROUP):
                ea.append(g * EXPERTS_PER_GROUP + lo)
                eb.append(g * EXPERTS_PER_GROUP + hi)
    return np.asarray(ea, np.int32), np.asarray(eb, np.int32)


_CLASS_EA, _CLASS_EB = _class_tables()


def kernel(x, p, norm_mix, w_in, b_conv_in, b_gate, q_norm, k_norm, lambda_q1, lambda_k1, lambda_q2, lambda_k2, subln, w_attn_out, conv_w, conv_b, conv_ln_g, conv_ln_b, w_conv_out, b_conv_out, w_o, norm_ffn, w_router_group, b_router_group, w_router_expert, b_router_expert, w1, w3, w2, norm_ple, w_ple_gate, b_ple_gate, w_ple_proj):
    b, s, d = x.shape
    n = b * s
    depth = w_in.shape[0]
    qk = ATTN_HEADS * 2 * HEAD_DIM
    aw = ATTN_HEADS * V_DIM
    cc = conv_w.shape[-1]
    assert depth == 1 and n % TM_MIX == 0 and s % (2 * ATTN_TK) == 0 and s % TM_INPROJ == 0
    assert n % TS_SCATTER == 0 and n % TM_FINAL == 0 and n % BM_MOE == 0
    i = 0
    h = x.reshape(n, d)

    conv_w_pad = jnp.pad(conv_w[i], ((0, CONV_HALO - CONV_WIDTH), (0, 0)))
    q2, k2, v2, cact2, gates2 = _inproj(h, norm_mix[i], w_in[i].astype(_BF), b_conv_in[i], b_gate[i], conv_w_pad,
                                        conv_b[i], conv_ln_g[i], conv_ln_b[i], qk, aw, cc, s)

    lam_vecs = jnp.stack([lambda_q1[i], lambda_k1[i], lambda_q2[i], lambda_k2[i]]).astype(_F32)
    attn = _attention(lam_vecs, q2.reshape(b, s, qk), k2.reshape(b, s, qk), v2.reshape(b, s, aw),
                      q_norm[i], k_norm[i], subln[i])

    w_router = jnp.concatenate([w_router_group[i], w_router_expert[i]], axis=1)
    w_router = jnp.pad(w_router, ((0, 0), (0, LANES - w_router.shape[1])))
    b_router = jnp.concatenate([b_router_group[i], b_router_expert[i]])
    b_router = jnp.pad(b_router, (0, LANES - b_router.shape[0])).reshape(1, LANES)
    w_router_hi = w_router.astype(_BF)
    w_router_lo = (w_router - w_router_hi.astype(_F32)).astype(_BF)
    w_router_cat = jnp.concatenate([w_router_hi, w_router_lo], axis=1)
    h1, t_rows, keys, cnt = _mix(h, attn.reshape(n, aw), cact2, gates2, w_attn_out[i].astype(_BF),
                                 w_conv_out[i].astype(_BF), w_o[i].astype(_BF), b_conv_out[i], norm_ffn[i],
                                 w_router_cat, w_router_hi, b_router)

    bm = BM_MOE
    nb = n // bm + N_CLASSES
    counts = cnt[0].astype(jnp.int32)
    padded = ((counts + bm - 1) // bm) * bm
    pend = jnp.cumsum(padded)
    pstart = (pend - padded).astype(jnp.int32)
    total = pend[-1]
    blk = jnp.arange(nb, dtype=jnp.int32) * bm
    valid = blk < total
    first_row = jnp.where(valid, blk, total - 1)
    bcls = jnp.sum((pend[None, :] <= first_row[:, None]).astype(jnp.int32), axis=1)
    bcls = jnp.minimum(bcls, N_CLASSES - 1)
    cls_onehot = (bcls[:, None] == jnp.arange(N_CLASSES, dtype=jnp.int32)[None, :]).astype(jnp.int32)
    ea = jnp.sum(cls_onehot * jnp.asarray(_CLASS_EA)[None, :], axis=1)
    eb = jnp.sum(cls_onehot * jnp.asarray(_CLASS_EB)[None, :], axis=1)
    nvalid = (total // bm).astype(jnp.int32).reshape(1)

    dest = _sorted_rows(pstart, keys.reshape(n // LANES, LANES))
    xs = _scatter_rows((pstart + counts).astype(jnp.int32), (padded - counts).astype(jnp.int32), nvalid,
                       dest, t_rows, nb * bm)
    y_rows = _moe(ea, eb, valid.astype(jnp.int32), xs, w1[i].astype(_BF), w3[i].astype(_BF), w2[i].astype(_BF), n)

    out = _final(h1, p[i].reshape(n, -1), y_rows, norm_ple[i], w_ple_gate[i].astype(_BF), b_ple_gate[i],
                 w_ple_proj[i].astype(_BF))
    return out.reshape(b, s, d)
```

```python
import functools
import math

import numpy as np
import jax
import jax.numpy as jnp
from jax import lax
from jax.experimental import pallas as pl
from jax.experimental.pallas import tpu as pltpu

ATTN_HEADS = 4
HEAD_DIM = 64
V_DIM = 2 * HEAD_DIM
CONV_WIDTH = 31
N_GROUPS = 4
EXPERTS_PER_GROUP = 8
PAIRS_PER_GROUP = EXPERTS_PER_GROUP * (EXPERTS_PER_GROUP - 1) // 2
N_CLASSES = N_GROUPS * PAIRS_PER_GROUP
EPS = 1e-6
LAMBDA_INIT = 0.8 - 0.6 * math.exp(-0.3 * 0)

LANES = 128
SUBLANES = 8
CONV_HALO = 32
NEG_BIG = -1e30
RANK_BITS = 17
BF16_BITS = 16
HIGH_HALF = -(1 << BF16_BITS)
META_SUBLANE = 4
META_W_LO, META_W_HI, META_TOKEN = 0, 1, 2
DMA_UNROLL = 8

TM_INPROJ = 512
ATTN_TK = 512
TM_MIX = 512
CONV_ROWS = 64
TS_SCATTER = 512
BM_MOE = 256
TM_FINAL = 512
VMEM_LIMIT = 56 * 1024 * 1024

_BF = jnp.bfloat16
_F32 = jnp.float32


def _const_spec(shape):
    nd = len(shape)
    return pl.BlockSpec(shape, lambda *_: (0,) * nd)


def _sigmoid(x):
    return 1.0 / (1.0 + jnp.exp(-x))


def _inproj_kernel(x_ref, nm_ref, w_ref, bci_ref, bg_ref, cw_ref, cb_ref, lng_ref, lnb_ref,
                   q_ref, k_ref, v_ref, c_ref, g_ref, ext_s, xsh_s, cv_s, *, tiles_per_seq):
    i = pl.program_id(0)
    tm = x_ref.shape[0]
    qk = q_ref.shape[1]
    aw = v_ref.shape[1]
    cc = c_ref.shape[1]
    ext_rows = CONV_HALO + tm
    o0, o1, o2, o3 = qk, 2 * qk, 2 * qk + aw, 2 * qk + aw + 2 * cc

    @pl.when(i == 0)
    def _():
        ext_s[ext_rows:ext_rows + SUBLANES, :] = jnp.zeros((SUBLANES, cc), _F32)

    @pl.when((i % tiles_per_seq) == 0)
    def _():
        ext_s[0:CONV_HALO, :] = jnp.zeros((CONV_HALO, cc), _F32)

    x = x_ref[...]
    u = x * lax.rsqrt(jnp.mean(x * x, axis=-1, keepdims=True) + EPS) * nm_ref[...]
    ub = u.astype(_BF)
    c = jnp.dot(ub, w_ref[:, o2:o3], preferred_element_type=_F32) + bci_ref[...]
    ext_s[CONV_HALO:ext_rows, :] = c[:, :cc] * _sigmoid(c[:, cc:])
    q_ref[...] = jnp.dot(ub, w_ref[:, 0:o0], preferred_element_type=_F32).astype(_BF)
    k_ref[...] = jnp.dot(ub, w_ref[:, o0:o1], preferred_element_type=_F32).astype(_BF)
    v_ref[...] = jnp.dot(ub, w_ref[:, o1:o2], preferred_element_type=_F32).astype(_BF)
    g = jnp.dot(ub, w_ref[:, o3:], preferred_element_type=_F32) + bg_ref[...]
    g_ref[...] = _sigmoid(g).astype(_BF)
    off = CONV_HALO - (CONV_WIDTH - 1)

    for rho in range(1, SUBLANES):
        xsh_s[rho - 1] = ext_s[rho:rho + ext_rows, :]

    for r0 in range(0, tm, CONV_ROWS):
        acc = jnp.zeros((CONV_ROWS, cc), _F32)
        for j in range(CONV_WIDTH):
            a, rho = divmod(off + j, SUBLANES)
            lo_row = r0 + a * SUBLANES
            if rho == 0:
                xv = ext_s[lo_row:lo_row + CONV_ROWS, :]
            else:
                xv = xsh_s[rho - 1, lo_row:lo_row + CONV_ROWS, :]
            acc = acc + cw_ref[j:j + 1, :] * xv
        cv_s[r0:r0 + CONV_ROWS, :] = acc

    ext_s[0:CONV_HALO, :] = ext_s[tm:ext_rows, :]

    cv = cv_s[...] + cb_ref[...]
    mu = jnp.mean(cv, axis=-1, keepdims=True)
    xc = cv - mu
    var = jnp.mean(xc * xc, axis=-1, keepdims=True)
    ln = xc * lax.rsqrt(var + EPS) * lng_ref[...] + lnb_ref[...]
    c_ref[...] = (ln * _sigmoid(ln)).astype(_BF)


def _inproj(x2, norm_mix, w_in_bf, b_conv_in, b_gate, conv_w, conv_b, ln_g, ln_b, qk, aw, cc, seq_len):
    n, d = x2.shape
    tm = TM_INPROJ
    ng = b_gate.shape[-1]
    kern = functools.partial(_inproj_kernel, tiles_per_seq=seq_len // tm)
    return pl.pallas_call(
        kern,
        grid=(n // tm,),
        in_specs=[
            pl.BlockSpec((tm, d), lambda i: (i, 0)),
            _const_spec((1, d)),
            pl.BlockSpec(w_in_bf.shape, lambda i: (0, 0), pipeline_mode=pl.Buffered(1)),
            _const_spec((1, 2 * cc)),
            _const_spec((1, ng)),
            _const_spec((CONV_HALO, cc)),
            _const_spec((1, cc)),
            _const_spec((1, cc)),
            _const_spec((1, cc)),
        ],
        out_specs=[
            pl.BlockSpec((tm, qk), lambda i: (i, 0)),
            pl.BlockSpec((tm, qk), lambda i: (i, 0)),
            pl.BlockSpec((tm, aw), lambda i: (i, 0)),
            pl.BlockSpec((tm, cc), lambda i: (i, 0)),
            pl.BlockSpec((tm, ng), lambda i: (i, 0)),
        ],
        out_shape=[
            jax.ShapeDtypeStruct((n, qk), _BF),
            jax.ShapeDtypeStruct((n, qk), _BF),
            jax.ShapeDtypeStruct((n, aw), _BF),
            jax.ShapeDtypeStruct((n, cc), _BF),
            jax.ShapeDtypeStruct((n, ng), _BF),
        ],
        scratch_shapes=[
            pltpu.VMEM((CONV_HALO + tm + SUBLANES, cc), _F32),
            pltpu.VMEM((SUBLANES - 1, CONV_HALO + tm, cc), _F32),
            pltpu.VMEM((tm, cc), _F32),
        ],
        compiler_params=pltpu.CompilerParams(
            dimension_semantics=("arbitrary",), vmem_limit_bytes=VMEM_LIMIT),
        name="inproj",
    )(x2, norm_mix.reshape(1, d), w_in_bf, b_conv_in.reshape(1, -1), b_gate.reshape(1, -1), conv_w,
      conv_b.reshape(1, cc), ln_g.reshape(1, cc), ln_b.reshape(1, cc))


def _half_rmsnorm(x, gain_row, lo_mask):
    x2 = x * x
    s_lo = jnp.sum(jnp.where(lo_mask, x2, 0.0), axis=-1, keepdims=True)
    s_hi = jnp.sum(jnp.where(lo_mask, 0.0, x2), axis=-1, keepdims=True)
    r = jnp.where(lo_mask, lax.rsqrt(s_lo * (1.0 / HEAD_DIM) + EPS), lax.rsqrt(s_hi * (1.0 / HEAD_DIM) + EPS))
    return x * r * gain_row


def _attn_kernel(lam_ref, q_ref, k_ref, v_ref, qn_ref, kn_ref, sub_ref, o_ref, kn_s, va_s, qs_s, m_s, acc_s,
                 sa_s, sb_s, sc_s):
    s_len = q_ref.shape[1]
    tk = ATTN_TK
    tq = 2 * tk
    nq = s_len // tq
    lane = lax.broadcasted_iota(jnp.int32, (1, V_DIM), 1)
    lo_mask = lane < HEAD_DIM

    lv = lam_ref[...]
    d1 = jnp.sum(lv[0:1, :] * lv[1:2, :], axis=-1, keepdims=True)
    d2 = jnp.sum(lv[2:3, :] * lv[3:4, :], axis=-1, keepdims=True)
    lam = jnp.exp(d1) - jnp.exp(d2) + LAMBDA_INIT

    kgain = kn_ref[...]
    qgain = qn_ref[...] * (math.log2(math.e) / math.sqrt(HEAD_DIM))

    def prep_kv(i, carry):
        r0 = pl.multiple_of(i * tq, tq)
        kk = k_ref[0, pl.ds(r0, tq), :].astype(_F32)
        kn_s[pl.ds(r0, tq), :] = _half_rmsnorm(kk, kgain, lo_mask).astype(_BF)
        va_s[pl.ds(r0, tq), 0:V_DIM] = v_ref[0, pl.ds(r0, tq), :]
        va_s[pl.ds(r0, tq), V_DIM:2 * V_DIM] = jnp.ones((tq, V_DIM), _BF)
        return carry

    lax.fori_loop(0, nq, prep_kv, 0)

    def scores(c0, ra, nr):
        kc = kn_s[pl.ds(c0, tk), :]
        return lax.dot_general(qs_s[ra:ra + nr, :], kc, (((1,), (1,)), ((), ())), preferred_element_type=_F32)

    def consume(s, c0, ra, nr, n_masked):
        vc = va_s[pl.ds(c0, tk), :]
        if n_masked:
            keep = (lax.broadcasted_iota(jnp.int32, (tk, tk), 1) <= lax.broadcasted_iota(jnp.int32, (tk, tk), 0))
            groups = [s[g * tk:(g + 1) * tk, :] for g in range(nr // tk)]
            groups = [jnp.where(keep, sg, NEG_BIG) if g < n_masked else sg for g, sg in enumerate(groups)]
            s = jnp.concatenate(groups, axis=0)
        m_old = m_s[ra:ra + nr, :]
        m_new = jnp.maximum(m_old, jnp.max(s, axis=-1, keepdims=True))
        alpha = jnp.exp2(m_old - m_new)
        p = jnp.concatenate(
            [jnp.exp2(s[:, c * LANES:(c + 1) * LANES] - m_new) for c in range(tk // LANES)], axis=1)
        pv = jnp.dot(p.astype(_BF), vc, preferred_element_type=_F32)
        acc_s[ra:ra + nr, :] = jnp.concatenate([alpha, alpha], axis=1) * acc_s[ra:ra + nr, :] + pv
        m_s[ra:ra + nr, :] = m_new

    def prep_q(r0):
        for hf in range(2):
            qq = _half_rmsnorm(q_ref[0, pl.ds(r0 + hf * tk, tk), :].astype(_F32), qgain, lo_mask)
            qs_s[(2 * hf) * tk:(2 * hf + 1) * tk, :] = jnp.where(lo_mask, qq, 0.0).astype(_BF)
            qs_s[(2 * hf + 1) * tk:(2 * hf + 2) * tk, :] = jnp.where(lo_mask, 0.0, qq).astype(_BF)

    bufs = (sa_s, sb_s, sc_s)
    prep_q(0)
    sa_s[...] = scores(0, 0, 4 * tk)

    for qi in range(nq):
        r0 = qi * tq
        cur = bufs[(2 * qi) % 3]
        nxt = bufs[(2 * qi + 1) % 3]
        ahead = bufs[(2 * qi + 2) % 3]
        m_s[...] = jnp.full(m_s.shape, NEG_BIG, _F32)
        acc_s[...] = jnp.zeros(acc_s.shape, _F32)

        def two_chunks(t, c, cur=cur, nxt=nxt):
            c0 = pl.multiple_of(2 * t * tk, tk)
            c1 = pl.multiple_of(c0 + tk, tk)
            c2 = pl.multiple_of(c0 + 2 * tk, tk)
            nxt[...] = scores(c1, 0, 4 * tk)
            consume(cur[...], c0, 0, 4 * tk, 0)
            cur[...] = scores(c2, 0, 4 * tk)
            consume(nxt[...], c1, 0, 4 * tk, 0)
            return c

        lax.fori_loop(0, qi, two_chunks, 0)
        nxt[0:2 * tk, :] = scores(r0 + tk, 2 * tk, 2 * tk)
        if qi + 1 < nq:
            prep_q(r0 + tq)
            ahead[...] = scores(0, 0, 4 * tk)
        consume(cur[...], r0, 0, 4 * tk, 2)
        consume(nxt[0:2 * tk, :], r0 + tk, 2 * tk, 2 * tk, 2)

        acc = acc_s[...]
        o = acc[:, 0:V_DIM] / acc[:, V_DIM:2 * V_DIM]
        for hf in range(2):
            a = o[(2 * hf) * tk:(2 * hf + 1) * tk, :] - lam * o[(2 * hf + 1) * tk:(2 * hf + 2) * tk, :]
            a = a * lax.rsqrt(jnp.mean(a * a, axis=-1, keepdims=True) + EPS) * sub_ref[...] * (1.0 - LAMBDA_INIT)
            o_ref[0, r0 + hf * tk:r0 + (hf + 1) * tk, :] = a.astype(_BF)


def _attention(lam_vecs, q3, k3, v3, q_norm, k_norm, subln):
    b, s, _ = q3.shape
    tq = 2 * ATTN_TK
    slab = lambda bi, hi: (bi, 0, hi)
    return pl.pallas_call(
        _attn_kernel,
        grid=(b, ATTN_HEADS),
        in_specs=[
            _const_spec((4, HEAD_DIM)),
            pl.BlockSpec((1, s, V_DIM), slab),
            pl.BlockSpec((1, s, V_DIM), slab),
            pl.BlockSpec((1, s, V_DIM), slab),
            _const_spec((1, V_DIM)),
            _const_spec((1, V_DIM)),
            _const_spec((1, V_DIM)),
        ],
        out_specs=pl.BlockSpec((1, s, V_DIM), slab),
        out_shape=jax.ShapeDtypeStruct((b, s, ATTN_HEADS * V_DIM), _BF),
        scratch_shapes=[
            pltpu.VMEM((s, V_DIM), _BF),
            pltpu.VMEM((s, 2 * V_DIM), _BF),
            pltpu.VMEM((2 * tq, V_DIM), _BF),
            pltpu.VMEM((2 * tq, V_DIM), _F32),
            pltpu.VMEM((2 * tq, 2 * V_DIM), _F32),
            pltpu.VMEM((2 * tq, ATTN_TK), _F32),
            pltpu.VMEM((2 * tq, ATTN_TK), _F32),
            pltpu.VMEM((2 * tq, ATTN_TK), _F32),
        ],
        compiler_params=pltpu.CompilerParams(
            dimension_semantics=("arbitrary", "arbitrary"), vmem_limit_bytes=VMEM_LIMIT),
        name="diffattn",
    )(lam_vecs, q3, k3, v3, q_norm.reshape(1, V_DIM), k_norm.reshape(1, V_DIM), subln.reshape(1, V_DIM))


def _router_logits(t, wr_ref, wrh_ref, br_ref):
    t_hi = t.astype(_BF)
    t_lo = (t - t_hi.astype(_F32)).astype(_BF)
    l_hi = jnp.dot(t_hi, wr_ref[...], preferred_element_type=_F32)
    l_lo = jnp.dot(t_lo, wrh_ref[...], preferred_element_type=_F32)
    return l_hi[:, 0:LANES] + l_hi[:, LANES:2 * LANES] + l_lo + br_ref[...]


def _mix_kernel(x_ref, a_ref, c_ref, g_ref, wa_ref, wc_ref, wo_ref, bco_ref, nf_ref, wr_ref, wrh_ref, br_ref,
                h_ref, t_ref, key_ref, cnt_ref, run_s, lg_s, tw_s, tri_s):
    i = pl.program_id(0)
    tm, d = x_ref.shape
    half_groups = d // LANES // 2

    @pl.when(i == 0)
    def _():
        run_s[...] = jnp.zeros(run_s.shape, _F32)
        lg_s[1] = jnp.zeros(lg_s.shape[1:], _F32)
        tw_s[1] = jnp.zeros(tw_s.shape[1:], jnp.int32)
        rr = lax.broadcasted_iota(jnp.int32, (tm, tm), 0)
        ccol = lax.broadcasted_iota(jnp.int32, (tm, tm), 1)
        tri_s[...] = jnp.where(ccol < rr, 1.0, 0.0).astype(_BF)

    def tile_matmuls(sl):
        y_b = jnp.dot(c_ref[...], wc_ref[...], preferred_element_type=_F32) + bco_ref[...]
        y_a = jnp.dot(a_ref[...], wa_ref[...], preferred_element_type=_F32)
        merged = g_ref[:, 0:d].astype(_F32) * y_a + g_ref[:, d:2 * d].astype(_F32) * y_b
        h = x_ref[...] + jnp.dot(merged.astype(_BF), wo_ref[...], preferred_element_type=_F32)
        h_ref[...] = h
        t = h * lax.rsqrt(jnp.mean(h * h, axis=-1, keepdims=True) + EPS) * nf_ref[...]
        for sg in range(half_groups):
            hi_bits = lax.bitcast_convert_type(
                t[:, sg * LANES:(sg + 1) * LANES].astype(_BF).astype(_F32), jnp.uint32)
            lo_bits = lax.bitcast_convert_type(
                t[:, (sg + half_groups) * LANES:(sg + half_groups + 1) * LANES].astype(_BF).astype(_F32),
                jnp.uint32)
            tw_s[sl, sg] = lax.bitcast_convert_type(hi_bits | (lo_bits >> BF16_BITS), jnp.int32)
        lg_s[sl] = _router_logits(t, wr_ref, wrh_ref, br_ref)

    for sl in range(2):
        @pl.when(i % 2 == sl)
        def _(sl=sl):
            tile_matmuls(sl)
            _route_tile(i - 1, lg_s[1 - sl], tw_s.at[1 - sl], t_ref, key_ref, cnt_ref, run_s, tri_s)


def _route_tile(tile, lg, words_ref, t_ref, key_ref, cnt_ref, run_s, tri_s):
    tm = lg.shape[0]
    for sg in range(META_SUBLANE):
        t_ref[pl.ds(sg, tm, stride=SUBLANES), :] = words_ref[sg]
    lane = lax.broadcasted_iota(jnp.int32, (tm, LANES), 1)
    gl = jnp.where(lane < N_GROUPS, lg, NEG_BIG)
    gmax = jnp.max(gl, axis=-1, keepdims=True)
    gidx = jnp.min(jnp.where(gl == gmax, lane, LANES), axis=-1, keepdims=True)
    gsum = jnp.sum(jnp.where(lane < N_GROUPS, jnp.exp(gl - gmax), 0.0), axis=-1, keepdims=True)
    g_gate = 1.0 / gsum
    e_lo = N_GROUPS + EXPERTS_PER_GROUP * gidx
    el = jnp.where((lane >= e_lo) & (lane < e_lo + EXPERTS_PER_GROUP), lg, NEG_BIG)
    m1 = jnp.max(el, axis=-1, keepdims=True)
    i1 = jnp.min(jnp.where(el == m1, lane, LANES), axis=-1, keepdims=True)
    el2 = jnp.where(lane == i1, NEG_BIG, el)
    m2 = jnp.max(el2, axis=-1, keepdims=True)
    i2 = jnp.min(jnp.where(el2 == m2, lane, LANES), axis=-1, keepdims=True)
    e21 = jnp.exp(m2 - m1)
    w_top1 = g_gate * (1.0 / (1.0 + e21))
    w_top2 = g_gate * (e21 / (1.0 + e21))
    a1 = i1 - e_lo
    a2 = i2 - e_lo
    lo = jnp.minimum(a1, a2)
    hi = jnp.maximum(a1, a2)
    w_lo = jnp.where(a1 < a2, w_top1, w_top2)
    w_hi = jnp.where(a1 < a2, w_top2, w_top1)
    pidx = ((lo * (2 * EXPERTS_PER_GROUP - 1 - lo)) >> 1) + (hi - lo - 1)
    cls = gidx * PAIRS_PER_GROUP + pidx
    token = tile * tm + lax.broadcasted_iota(jnp.int32, (tm, LANES), 0)
    meta = jnp.where(lane == META_W_LO, lax.bitcast_convert_type(w_lo, jnp.int32),
                     jnp.where(lane == META_W_HI, lax.bitcast_convert_type(w_hi, jnp.int32),
                               jnp.where(lane == META_TOKEN, token, 0)))
    t_ref[pl.ds(META_SUBLANE, tm, stride=SUBLANES), :] = meta
    for sg in range(META_SUBLANE + 1, SUBLANES):
        t_ref[pl.ds(sg, tm, stride=SUBLANES), :] = jnp.zeros((tm, LANES), jnp.int32)

    onehot = lane == cls
    oh_bf = jnp.where(onehot, 1.0, 0.0).astype(_BF)
    before = jnp.dot(tri_s[...], oh_bf, preferred_element_type=_F32) + run_s[...]
    rank = jnp.sum(jnp.where(onehot, before, 0.0), axis=-1, keepdims=True)
    counted = jnp.where(tile >= 0, 1.0, 0.0)
    run_new = run_s[...] + counted * jnp.sum(jnp.where(onehot, 1.0, 0.0), axis=0, keepdims=True)
    run_s[...] = run_new
    cnt_ref[...] = run_new
    keyf = jnp.broadcast_to(cls.astype(_F32) * float(1 << RANK_BITS) + rank, (tm, LANES))
    for g in range(tm // LANES):
        kt = keyf[g * LANES:(g + 1) * LANES, :].T
        key_ref[0, g:g + 1, :] = kt[0:1, :].astype(jnp.int32)


def _mix(x2, attn2, cact2, gates2, wa_bf, wc_bf, wo_bf, b_conv_out, norm_ffn, w_router_cat, w_router_hi, b_router):
    n, d = x2.shape
    tm = TM_MIX
    aw = attn2.shape[1]
    cc = cact2.shape[1]
    nt = n // tm
    row = lambda i: (jnp.minimum(i, nt - 1), 0)
    prev = lambda i: (jnp.maximum(i - 1, 0), 0)
    return pl.pallas_call(
        _mix_kernel,
        grid=(nt + 1,),
        in_specs=[
            pl.BlockSpec((tm, d), row),
            pl.BlockSpec((tm, aw), row),
            pl.BlockSpec((tm, cc), row),
            pl.BlockSpec((tm, 2 * d), row),
            _const_spec(wa_bf.shape),
            _const_spec(wc_bf.shape),
            _const_spec(wo_bf.shape),
            _const_spec((1, d)),
            _const_spec((1, d)),
            _const_spec((d, 2 * LANES)),
            _const_spec((d, LANES)),
            _const_spec((1, LANES)),
        ],
        out_specs=[
            pl.BlockSpec((tm, d), row),
            pl.BlockSpec((tm * SUBLANES, LANES), prev),
            pl.BlockSpec((1, tm // LANES, LANES), lambda i: (jnp.maximum(i - 1, 0), 0, 0)),
            _const_spec((1, LANES)),
        ],
        out_shape=[
            jax.ShapeDtypeStruct((n, d), _F32),
            jax.ShapeDtypeStruct((n * SUBLANES, LANES), jnp.int32),
            jax.ShapeDtypeStruct((n // tm, tm // LANES, LANES), jnp.int32),
            jax.ShapeDtypeStruct((1, LANES), _F32),
        ],
        scratch_shapes=[
            pltpu.VMEM((1, LANES), _F32),
            pltpu.VMEM((2, tm, LANES), _F32),
            pltpu.VMEM((2, META_SUBLANE, tm, LANES), jnp.int32),
            pltpu.VMEM((tm, tm), _BF),
        ],
        compiler_params=pltpu.CompilerParams(
            dimension_semantics=("arbitrary",), vmem_limit_bytes=VMEM_LIMIT),
        name="mix_router",
    )(x2, attn2, cact2, gates2, wa_bf, wc_bf, wo_bf, b_conv_out.reshape(1, d), norm_ffn.reshape(1, d),
      w_router_cat, w_router_hi, b_router)


def _sorted_rows_kernel(pstart_ref, key_ref, o_ref):
    key = key_ref[...]
    cls = key >> RANK_BITS
    row = key & ((1 << RANK_BITS) - 1)
    for c in range(N_CLASSES):
        row = row + jnp.where(cls == c, pstart_ref[c], 0)
    o_ref[...] = row


def _sorted_rows(pstart, keys2):
    grid_spec = pltpu.PrefetchScalarGridSpec(
        num_scalar_prefetch=1,
        grid=(1,),
        in_specs=[pl.BlockSpec(keys2.shape, lambda i, *_: (0, 0))],
        out_specs=pl.BlockSpec(keys2.shape, lambda i, *_: (0, 0)),
    )
    return pl.pallas_call(
        _sorted_rows_kernel,
        grid_spec=grid_spec,
        out_shape=jax.ShapeDtypeStruct(keys2.shape, jnp.int32),
        name="sorted_rows",
    )(pstart, keys2)


def _token_rows(idx, count=1):
    return pl.ds(pl.multiple_of(idx * SUBLANES, SUBLANES), count * SUBLANES)


def _scatter_kernel(padstart_ref, padlen_ref, nvalid_ref, dest_ref, t_ref, xs_ref, zero_s, sem, zsem):
    i = pl.program_id(0)
    ts = dest_ref.shape[2]
    n_tokens = t_ref.shape[0] // SUBLANES
    bm = zero_s.shape[0] // SUBLANES
    nblk = xs_ref.shape[0] // (bm * SUBLANES)

    def start_rows(g, c):
        for u in range(DMA_UNROLL):
            r = g * DMA_UNROLL + u
            pltpu.make_async_copy(t_ref.at[_token_rows(i * ts + r)], xs_ref.at[_token_rows(dest_ref[0, 0, r])],
                                  sem).start(priority=u % 2)
        return c

    def wait_rows(g, c):
        for _ in range(DMA_UNROLL):
            pltpu.make_async_copy(t_ref.at[_token_rows(0)], xs_ref.at[_token_rows(0)], sem).wait()
        return c

    lax.fori_loop(0, ts // DMA_UNROLL, start_rows, 0)

    @pl.when(i == 0)
    def _():
        rows = lax.broadcasted_iota(jnp.int32, zero_s.shape, 0)
        lanes = lax.broadcasted_iota(jnp.int32, zero_s.shape, 1)
        is_token = ((rows & (SUBLANES - 1)) == META_SUBLANE) & (lanes == META_TOKEN)
        zero_s[...] = jnp.where(is_token, n_tokens + (rows >> (SUBLANES.bit_length() - 1)), 0)

        def pad_copies(c, act):
            ln = padlen_ref[c]
            st = padstart_ref[c]
            for k in range(bm.bit_length() - 1):
                sz = 1 << k

                @pl.when(((ln >> k) & 1) == 1)
                def _():
                    dst = st + (ln & (sz - 1))
                    src = dst & (bm - 1)
                    act(pltpu.make_async_copy(zero_s.at[_token_rows(src, sz)], xs_ref.at[_token_rows(dst, sz)], zsem))

        def tail_copy(j, act):
            @pl.when(j >= nvalid_ref[0])
            def _():
                act(pltpu.make_async_copy(zero_s, xs_ref.at[_token_rows(j * bm, bm)], zsem))

        for act in (lambda cp: cp.start(), lambda cp: cp.wait()):
            lax.fori_loop(0, N_CLASSES, lambda c, carry, act=act: (pad_copies(c, act), carry)[1], 0)
            lax.fori_loop(0, nblk, lambda j, carry, act=act: (tail_copy(j, act), carry)[1], 0)

    lax.fori_loop(0, ts // DMA_UNROLL, wait_rows, 0)


def _scatter_rows(padstart, padlen, nvalid, dest, t_rows, n_rows):
    n = t_rows.shape[0] // SUBLANES
    ts = TS_SCATTER
    grid_spec = pltpu.PrefetchScalarGridSpec(
        num_scalar_prefetch=3,
        grid=(n // ts,),
        in_specs=[
            pl.BlockSpec((1, 1, ts), lambda i, *_: (i, 0, 0), memory_space=pltpu.SMEM),
            pl.BlockSpec(memory_space=pl.ANY),
        ],
        out_specs=pl.BlockSpec(memory_space=pl.ANY),
        scratch_shapes=[pltpu.VMEM((BM_MOE * SUBLANES, LANES), jnp.int32), pltpu.SemaphoreType.DMA,
                        pltpu.SemaphoreType.DMA],
    )
    return pl.pallas_call(
        _scatter_kernel,
        grid_spec=grid_spec,
        out_shape=jax.ShapeDtypeStruct((n_rows * SUBLANES, LANES), jnp.int32),
        compiler_params=pltpu.CompilerParams(
            dimension_semantics=("arbitrary",), vmem_limit_bytes=VMEM_LIMIT),
        name="scatter_rows",
    )(padstart, padlen, nvalid, dest.reshape(n // ts, 1, ts), t_rows)


def _moe_kernel(ea_ref, eb_ref, valid_ref, xs_ref, w1a, w3a, w2a, w1b, w3b, w2b, out_ref,
                y_s, ids_v, ids_sm, rsem, isem):
    j = pl.program_id(0)
    nb = pl.num_programs(0)
    bm = xs_ref.shape[0] // SUBLANES
    ng = bm // LANES
    n_tokens = out_ref.shape[0] // SUBLANES - bm

    def ids_copy(sl):
        return pltpu.make_async_copy(ids_v.at[sl], ids_sm.at[sl], isem.at[sl])

    def row_copy(sl, r, tok):
        return pltpu.make_async_copy(y_s.at[sl, _token_rows(r)], out_ref.at[_token_rows(tok)], rsem.at[sl])

    def issue_rows(sl):
        for r in range(bm):
            row_copy(sl, r, ids_sm[sl, r // LANES, r % LANES]).start(priority=r % 2)

    def wait_rows(sl):
        def body(g, c):
            for _ in range(DMA_UNROLL):
                row_copy(sl, 0, 0).wait()
            return c

        lax.fori_loop(0, bm // DMA_UNROLL, body, 0)

    @pl.when(j == 0)
    def _():
        y_s[1] = jnp.zeros(y_s.shape[1:], _F32)
        pos = (lax.broadcasted_iota(jnp.int32, (ng, LANES), 0) * LANES
               + lax.broadcasted_iota(jnp.int32, (ng, LANES), 1))
        ids_v[1] = n_tokens + pos
        ids_copy(1).start()

    def send_previous(sl):
        ids_copy(1 - sl).wait()

        @pl.when(j >= 1)
        def _():
            wait_rows(sl)

        issue_rows(1 - sl)

    def step(sl):
        other = 1 - sl
        send_previous(sl)

        meta = xs_ref[pl.ds(META_SUBLANE, bm, stride=SUBLANES), :]
        lane = lax.broadcasted_iota(jnp.int32, (bm, LANES), 1)
        tok = jnp.sum(jnp.where(lane == META_TOKEN, meta.astype(_F32), 0.0), axis=-1, keepdims=True)
        tokb = jnp.broadcast_to(tok, (bm, LANES))
        for g in range(ng):
            ids_v[sl, g:g + 1, :] = tokb[g * LANES:(g + 1) * LANES, :].T[0:1, :].astype(jnp.int32)
        ids_copy(sl).start()

        words = [xs_ref[pl.ds(sg, bm, stride=SUBLANES), :] for sg in range(META_SUBLANE)]
        hi = [lax.bitcast_convert_type(w & jnp.int32(HIGH_HALF), _F32).astype(_BF) for w in words]
        lo = [lax.bitcast_convert_type(w << BF16_BITS, _F32).astype(_BF) for w in words]
        xb = jnp.concatenate(hi + lo, axis=1)
        metaf = lax.bitcast_convert_type(meta, _F32)
        w_a = jnp.sum(jnp.where(lane == META_W_LO, metaf, 0.0), axis=-1, keepdims=True)
        w_b = jnp.sum(jnp.where(lane == META_W_HI, metaf, 0.0), axis=-1, keepdims=True)

        def expert(w1, w3, w2):
            h1 = jnp.dot(xb, w1[0], preferred_element_type=_F32)
            h3 = jnp.dot(xb, w3[0], preferred_element_type=_F32)
            hdn = h1 * _sigmoid(h1) * h3
            return jnp.dot(hdn.astype(_BF), w2[0], preferred_element_type=_F32)

        y = w_a * expert(w1a, w3a, w2a) + w_b * expert(w1b, w3b, w2b)
        for sg in range(SUBLANES):
            y_s[sl, pl.ds(sg, bm, stride=SUBLANES), :] = y[:, sg * LANES:(sg + 1) * LANES]

        @pl.when(j == nb - 1)
        def _():
            wait_rows(other)
            ids_copy(sl).wait()
            issue_rows(sl)
            wait_rows(sl)

    is_valid = valid_ref[j] != 0
    prev_valid = valid_ref[jnp.maximum(j - 1, 0)] != 0
    for sl in range(2):
        @pl.when((j % 2 == sl) & is_valid)
        def _(sl=sl):
            step(sl)

        @pl.when((j % 2 == sl) & jnp.logical_not(is_valid) & prev_valid)
        def _(sl=sl):
            send_previous(sl)
            wait_rows(1 - sl)


def _moe(ea, eb, valid, xs, w1_bf, w3_bf, w2_bf, n_tokens):
    d, ff = w1_bf.shape[1], w1_bf.shape[2]
    assert d == SUBLANES * LANES
    bm = BM_MOE
    nb = xs.shape[0] // (bm * SUBLANES)
    grid_spec = pltpu.PrefetchScalarGridSpec(
        num_scalar_prefetch=3,
        grid=(nb,),
        in_specs=[
            pl.BlockSpec((bm * SUBLANES, LANES), lambda j, ea, eb, va: (j, 0)),
            pl.BlockSpec((1, d, ff), lambda j, ea, eb, va: (ea[j], 0, 0)),
            pl.BlockSpec((1, d, ff), lambda j, ea, eb, va: (ea[j], 0, 0)),
            pl.BlockSpec((1, ff, d), lambda j, ea, eb, va: (ea[j], 0, 0)),
            pl.BlockSpec((1, d, ff), lambda j, ea, eb, va: (eb[j], 0, 0)),
            pl.BlockSpec((1, d, ff), lambda j, ea, eb, va: (eb[j], 0, 0)),
            pl.BlockSpec((1, ff, d), lambda j, ea, eb, va: (eb[j], 0, 0)),
        ],
        out_specs=pl.BlockSpec(memory_space=pl.ANY),
        scratch_shapes=[
            pltpu.VMEM((2, bm * SUBLANES, LANES), _F32),
            pltpu.VMEM((2, bm // LANES, LANES), jnp.int32),
            pltpu.SMEM((2, bm // LANES, LANES), jnp.int32),
            pltpu.SemaphoreType.DMA((2,)),
            pltpu.SemaphoreType.DMA((2,)),
        ],
    )
    return pl.pallas_call(
        _moe_kernel,
        grid_spec=grid_spec,
        out_shape=jax.ShapeDtypeStruct(((n_tokens + bm) * SUBLANES, LANES), _F32),
        compiler_params=pltpu.CompilerParams(
            dimension_semantics=("arbitrary",), vmem_limit_bytes=VMEM_LIMIT),
        name="moe_experts",
    )(ea, eb, valid, xs, w1_bf, w3_bf, w2_bf, w1_bf, w3_bf, w2_bf)


def _final_kernel(h_ref, p_ref, y_ref, np_ref, wg_ref, bg_ref, wp_ref, o_ref):
    tm = h_ref.shape[0]
    proj = jnp.dot(p_ref[...].astype(_BF), wp_ref[...], preferred_element_type=_F32)
    y = jnp.concatenate([y_ref[pl.ds(sg, tm, stride=SUBLANES), :] for sg in range(SUBLANES)], axis=1)
    h = h_ref[...] + y
    u = h * lax.rsqrt(jnp.mean(h * h, axis=-1, keepdims=True) + EPS) * np_ref[...]
    gate = _sigmoid(jnp.dot(u.astype(_BF), wg_ref[...], preferred_element_type=_F32) + bg_ref[...])
    o_ref[...] = h + gate * proj


def _final(h1, p2, y_rows, norm_ple, wg_bf, b_ple_gate, wp_bf):
    n, d = h1.shape
    assert d == SUBLANES * LANES
    tm = TM_FINAL
    pd = p2.shape[1]
    row = lambda i: (i, 0)
    return pl.pallas_call(
        _final_kernel,
        grid=(n // tm,),
        in_specs=[
            pl.BlockSpec((tm, d), row),
            pl.BlockSpec((tm, pd), row),
            pl.BlockSpec((tm * SUBLANES, LANES), row),
            _const_spec((1, d)),
            _const_spec(wg_bf.shape),
            _const_spec((1, d)),
            _const_spec(wp_bf.shape),
        ],
        out_specs=pl.BlockSpec((tm, d), row),
        out_shape=jax.ShapeDtypeStruct((n, d), _F32),
        compiler_params=pltpu.CompilerParams(
            dimension_semantics=("arbitrary",), vmem_limit_bytes=VMEM_LIMIT),
        name="final_ple",
    )(h1, p2, y_rows, norm_ple.reshape(1, d), wg_bf, b_ple_gate.reshape(1, d), wp_bf)


def _class_tables():
    ea, eb = [], []
    for g in range(N_GROUPS):
        for lo in range(EXPERTS_PER_GROUP):
            for hi in range(lo + 1, EXPERTS_PER_GROUP):
                ea.append(g * EXPERTS_PER_GROUP + lo)
                eb.append(g * EXPERTS_PER_GROUP + hi)
    return np.asarray(ea, np.int32), np.asarray(eb, np.int32)


_CLASS_EA, _CLASS_EB = _class_tables()


def kernel(x, p, norm_mix, w_in, b_conv_in, b_gate, q_norm, k_norm, lambda_q1, lambda_k1, lambda_q2, lambda_k2, subln, w_attn_out, conv_w, conv_b, conv_ln_g, conv_ln_b, w_conv_out, b_conv_out, w_o, norm_ffn, w_router_group, b_router_group, w_router_expert, b_router_expert, w1, w3, w2, norm_ple, w_ple_gate, b_ple_gate, w_ple_proj):
    b, s, d = x.shape
    n = b * s
    depth = w_in.shape[0]
    qk = ATTN_HEADS * 2 * HEAD_DIM
    aw = ATTN_HEADS * V_DIM
    cc = conv_w.shape[-1]
    assert depth == 1 and n % TM_MIX == 0 and s % (2 * ATTN_TK) == 0 and s % TM_INPROJ == 0
    assert n % TS_SCATTER == 0 and n % TM_FINAL == 0 and n % BM_MOE == 0
    i = 0
    h = x.reshape(n, d)

    conv_w_pad = jnp.pad(conv_w[i], ((0, CONV_HALO - CONV_WIDTH), (0, 0)))
    q2, k2, v2, cact2, gates2 = _inproj(h, norm_mix[i], w_in[i].astype(_BF), b_conv_in[i], b_gate[i], conv_w_pad,
                                        conv_b[i], conv_ln_g[i], conv_ln_b[i], qk, aw, cc, s)

    lam_vecs = jnp.stack([lambda_q1[i], lambda_k1[i], lambda_q2[i], lambda_k2[i]]).astype(_F32)
    attn = _attention(lam_vecs, q2.reshape(b, s, qk), k2.reshape(b, s, qk), v2.reshape(b, s, aw),
                      q_norm[i], k_norm[i], subln[i])

    w_router = jnp.concatenate([w_router_group[i], w_router_expert[i]], axis=1)
    w_router = jnp.pad(w_router, ((0, 0), (0, LANES - w_router.shape[1])))
    b_router = jnp.concatenate([b_router_group[i], b_router_expert[i]])
    b_router = jnp.pad(b_router, (0, LANES - b_router.shape[0])).reshape(1, LANES)
    w_router_hi = w_router.astype(_BF)
    w_router_lo = (w_router - w_router_hi.astype(_F32)).astype(_BF)
    w_router_cat = jnp.concatenate([w_router_hi, w_router_lo], axis=1)
    h1, t_rows, keys, cnt = _mix(h, attn.reshape(n, aw), cact2, gates2, w_attn_out[i].astype(_BF),
                                 w_conv_out[i].astype(_BF), w_o[i].astype(_BF), b_conv_out[i], norm_ffn[i],
                                 w_router_cat, w_router_hi, b_router)

    bm = BM_MOE
    nb = n // bm + N_CLASSES
    counts = cnt[0].astype(jnp.int32)
    padded = ((counts + bm - 1) // bm) * bm
    pend = jnp.cumsum(padded)
    pstart = (pend - padded).astype(jnp.int32)
    total = pend[-1]
    blk = jnp.arange(nb, dtype=jnp.int32) * bm
    valid = blk < total
    first_row = jnp.where(valid, blk, total - 1)
    bcls = jnp.sum((pend[None, :] <= first_row[:, None]).astype(jnp.int32), axis=1)
    bcls = jnp.minimum(bcls, N_CLASSES - 1)
    cls_onehot = (bcls[:, None] == jnp.arange(N_CLASSES, dtype=jnp.int32)[None, :]).astype(jnp.int32)
    ea = jnp.sum(cls_onehot * jnp.asarray(_CLASS_EA)[None, :], axis=1)
    eb = jnp.sum(cls_onehot * jnp.asarray(_CLASS_EB)[None, :], axis=1)
    nvalid = (total // bm).astype(jnp.int32).reshape(1)

    dest = _sorted_rows(pstart, keys.reshape(n // LANES, LANES))
    xs = _scatter_rows((pstart + counts).astype(jnp.int32), (padded - counts).astype(jnp.int32), nvalid,
                       dest, t_rows, nb * bm)
    y_rows = _moe(ea, eb, valid.astype(jnp.int32), xs, w1[i].astype(_BF), w3[i].astype(_BF), w2[i].astype(_BF), n)

    out = _final(h1, p[i].reshape(n, -1), y_rows, norm_ple[i], w_ple_gate[i].astype(_BF), b_ple_gate[i],
                 w_ple_proj[i].astype(_BF))
    return out.reshape(b, s, d)
```

```python
import functools
import math

import numpy as np
import jax
import jax.numpy as jnp
from jax import lax
from jax.experimental import pallas as pl
from jax.experimental.pallas import tpu as pltpu

ATTN_HEADS = 4
HEAD_DIM = 64
V_DIM = 2 * HEAD_DIM
CONV_WIDTH = 31
N_GROUPS = 4
EXPERTS_PER_GROUP = 8
PAIRS_PER_GROUP = EXPERTS_PER_GROUP * (EXPERTS_PER_GROUP - 1) // 2
N_CLASSES = N_GROUPS * PAIRS_PER_GROUP
EPS = 1e-6
LAMBDA_INIT = 0.8 - 0.6 * math.exp(-0.3 * 0)

LANES = 128
SUBLANES = 8
CONV_HALO = 32
NEG_BIG = -1e30
RANK_BITS = 17
BF16_BITS = 16
HIGH_HALF = -(1 << BF16_BITS)
META_SUBLANE = 4
META_W_LO, META_W_HI, META_TOKEN = 0, 1, 2
DMA_UNROLL = 8

TM_INPROJ = 512
ATTN_TK = 512
TM_MIX = 512
CONV_ROWS = 64
TS_SCATTER = 512
BM_MOE = 256
TM_FINAL = 512
VMEM_LIMIT = 56 * 1024 * 1024

_BF = jnp.bfloat16
_F32 = jnp.float32


def _const_spec(shape):
    nd = len(shape)
    return pl.BlockSpec(shape, lambda *_: (0,) * nd)


def _sigmoid(x):
    return 1.0 / (1.0 + jnp.exp(-x))


def _inproj_kernel(x_ref, nm_ref, w_ref, bci_ref, bg_ref, cw_ref, cb_ref, lng_ref, lnb_ref,
                   q_ref, k_ref, v_ref, c_ref, g_ref, ext_s, xsh_s, cv_s, *, tiles_per_seq):
    i = pl.program_id(0)
    tm = x_ref.shape[0]
    qk = q_ref.shape[1]
    aw = v_ref.shape[1]
    cc = c_ref.shape[1]
    ext_rows = CONV_HALO + tm
    o0, o1, o2, o3 = qk, 2 * qk, 2 * qk + aw, 2 * qk + aw + 2 * cc

    @pl.when(i == 0)
    def _():
        ext_s[ext_rows:ext_rows + SUBLANES, :] = jnp.zeros((SUBLANES, cc), _F32)

    @pl.when((i % tiles_per_seq) == 0)
    def _():
        ext_s[0:CONV_HALO, :] = jnp.zeros((CONV_HALO, cc), _F32)

    x = x_ref[...]
    u = x * lax.rsqrt(jnp.mean(x * x, axis=-1, keepdims=True) + EPS) * nm_ref[...]
    ub = u.astype(_BF)
    c = jnp.dot(ub, w_ref[:, o2:o3], preferred_element_type=_F32) + bci_ref[...]
    ext_s[CONV_HALO:ext_rows, :] = c[:, :cc] * _sigmoid(c[:, cc:])
    q_ref[...] = jnp.dot(ub, w_ref[:, 0:o0], preferred_element_type=_F32).astype(_BF)
    k_ref[...] = jnp.dot(ub, w_ref[:, o0:o1], preferred_element_type=_F32).astype(_BF)
    v_ref[...] = jnp.dot(ub, w_ref[:, o1:o2], preferred_element_type=_F32).astype(_BF)
    g = jnp.dot(ub, w_ref[:, o3:], preferred_element_type=_F32) + bg_ref[...]
    g_ref[...] = _sigmoid(g).astype(_BF)
    off = CONV_HALO - (CONV_WIDTH - 1)

    for rho in range(1, SUBLANES):
        xsh_s[rho - 1] = ext_s[rho:rho + ext_rows, :]

    for r0 in range(0, tm, CONV_ROWS):
        acc = jnp.zeros((CONV_ROWS, cc), _F32)
        for j in range(CONV_WIDTH):
            a, rho = divmod(off + j, SUBLANES)
            lo_row = r0 + a * SUBLANES
            if rho == 0:
                xv = ext_s[lo_row:lo_row + CONV_ROWS, :]
            else:
                xv = xsh_s[rho - 1, lo_row:lo_row + CONV_ROWS, :]
            acc = acc + cw_ref[j:j + 1, :] * xv
        cv_s[r0:r0 + CONV_ROWS, :] = acc

    ext_s[0:CONV_HALO, :] = ext_s[tm:ext_rows, :]

    cv = cv_s[...] + cb_ref[...]
    mu = jnp.mean(cv, axis=-1, keepdims=True)
    xc = cv - mu
    var = jnp.mean(xc * xc, axis=-1, keepdims=True)
    ln = xc * lax.rsqrt(var + EPS) * lng_ref[...] + lnb_ref[...]
    c_ref[...] = (ln * _sigmoid(ln)).astype(_BF)


def _inproj(x2, norm_mix, w_in_bf, b_conv_in, b_gate, conv_w, conv_b, ln_g, ln_b, qk, aw, cc, seq_len):
    n, d = x2.shape
    tm = TM_INPROJ
    ng = b_gate.shape[-1]
    kern = functools.partial(_inproj_kernel, tiles_per_seq=seq_len // tm)
    return pl.pallas_call(
        kern,
        grid=(n // tm,),
        in_specs=[
            pl.BlockSpec((tm, d), lambda i: (i, 0)),
            _const_spec((1, d)),
            pl.BlockSpec(w_in_bf.shape, lambda i: (0, 0), pipeline_mode=pl.Buffered(1)),
            _const_spec((1, 2 * cc)),
            _const_spec((1, ng)),
            _const_spec((CONV_HALO, cc)),
            _const_spec((1, cc)),
            _const_spec((1, cc)),
            _const_spec((1, cc)),
        ],
        out_specs=[
            pl.BlockSpec((tm, qk), lambda i: (i, 0)),
            pl.BlockSpec((tm, qk), lambda i: (i, 0)),
            pl.BlockSpec((tm, aw), lambda i: (i, 0)),
            pl.BlockSpec((tm, cc), lambda i: (i, 0)),
            pl.BlockSpec((tm, ng), lambda i: (i, 0)),
        ],
        out_shape=[
            jax.ShapeDtypeStruct((n, qk), _BF),
            jax.ShapeDtypeStruct((n, qk), _BF),
            jax.ShapeDtypeStruct((n, aw), _BF),
            jax.ShapeDtypeStruct((n, cc), _BF),
            jax.ShapeDtypeStruct((n, ng), _BF),
        ],
        scratch_shapes=[
            pltpu.VMEM((CONV_HALO + tm + SUBLANES, cc), _F32),
            pltpu.VMEM((SUBLANES - 1, CONV_HALO + tm, cc), _F32),
            pltpu.VMEM((tm, cc), _F32),
        ],
        compiler_params=pltpu.CompilerParams(
            dimension_semantics=("arbitrary",), vmem_limit_bytes=VMEM_LIMIT),
        name="inproj",
    )(x2, norm_mix.reshape(1, d), w_in_bf, b_conv_in.reshape(1, -1), b_gate.reshape(1, -1), conv_w,
      conv_b.reshape(1, cc), ln_g.reshape(1, cc), ln_b.reshape(1, cc))


def _half_rmsnorm(x, gain_row, lo_mask):
    x2 = x * x
    s_lo = jnp.sum(jnp.where(lo_mask, x2, 0.0), axis=-1, keepdims=True)
    s_hi = jnp.sum(jnp.where(lo_mask, 0.0, x2), axis=-1, keepdims=True)
    r = jnp.where(lo_mask, lax.rsqrt(s_lo * (1.0 / HEAD_DIM) + EPS), lax.rsqrt(s_hi * (1.0 / HEAD_DIM) + EPS))
    return x * r * gain_row


def _attn_kernel(lam_ref, q_ref, k_ref, v_ref, qn_ref, kn_ref, sub_ref, w1f_ref, w3f_ref, w2f_ref,
                 o_ref, w1b_ref, w3b_ref, w2b_ref, kn_s, va_s, qs_s, m_s, acc_s, sa_s, sb_s, sc_s):
    for wf_ref, wb_ref in ((w1f_ref, w1b_ref), (w3f_ref, w3b_ref), (w2f_ref, w2b_ref)):
        wb_ref[...] = wf_ref[...].astype(_BF)

    s_len = q_ref.shape[1]
    tk = ATTN_TK
    tq = 2 * tk
    nq = s_len // tq
    lane = lax.broadcasted_iota(jnp.int32, (1, V_DIM), 1)
    lo_mask = lane < HEAD_DIM

    lv = lam_ref[...]
    d1 = jnp.sum(lv[0:1, :] * lv[1:2, :], axis=-1, keepdims=True)
    d2 = jnp.sum(lv[2:3, :] * lv[3:4, :], axis=-1, keepdims=True)
    lam = jnp.exp(d1) - jnp.exp(d2) + LAMBDA_INIT

    kgain = kn_ref[...]
    qgain = qn_ref[...] * (math.log2(math.e) / math.sqrt(HEAD_DIM))

    def prep_kv(i, carry):
        r0 = pl.multiple_of(i * tq, tq)
        kk = k_ref[0, pl.ds(r0, tq), :].astype(_F32)
        kn_s[pl.ds(r0, tq), :] = _half_rmsnorm(kk, kgain, lo_mask).astype(_BF)
        va_s[pl.ds(r0, tq), 0:V_DIM] = v_ref[0, pl.ds(r0, tq), :]
        va_s[pl.ds(r0, tq), V_DIM:2 * V_DIM] = jnp.ones((tq, V_DIM), _BF)
        return carry

    lax.fori_loop(0, nq, prep_kv, 0)

    def scores(c0, ra, nr):
        kc = kn_s[pl.ds(c0, tk), :]
        return lax.dot_general(qs_s[ra:ra + nr, :], kc, (((1,), (1,)), ((), ())), preferred_element_type=_F32)

    def consume(s, c0, ra, nr, n_masked):
        vc = va_s[pl.ds(c0, tk), :]
        if n_masked:
            keep = (lax.broadcasted_iota(jnp.int32, (tk, tk), 1) <= lax.broadcasted_iota(jnp.int32, (tk, tk), 0))
            groups = [s[g * tk:(g + 1) * tk, :] for g in range(nr // tk)]
            groups = [jnp.where(keep, sg, NEG_BIG) if g < n_masked else sg for g, sg in enumerate(groups)]
            s = jnp.concatenate(groups, axis=0)
        m_old = m_s[ra:ra + nr, :]
        m_new = jnp.maximum(m_old, jnp.max(s, axis=-1, keepdims=True))
        alpha = jnp.exp2(m_old - m_new)
        p = jnp.concatenate(
            [jnp.exp2(s[:, c * LANES:(c + 1) * LANES] - m_new) for c in range(tk // LANES)], axis=1)
        pv = jnp.dot(p.astype(_BF), vc, preferred_element_type=_F32)
        acc_s[ra:ra + nr, :] = jnp.concatenate([alpha, alpha], axis=1) * acc_s[ra:ra + nr, :] + pv
        m_s[ra:ra + nr, :] = m_new

    def prep_q(r0):
        for hf in range(2):
            qq = _half_rmsnorm(q_ref[0, pl.ds(r0 + hf * tk, tk), :].astype(_F32), qgain, lo_mask)
            qs_s[(2 * hf) * tk:(2 * hf + 1) * tk, :] = jnp.where(lo_mask, qq, 0.0).astype(_BF)
            qs_s[(2 * hf + 1) * tk:(2 * hf + 2) * tk, :] = jnp.where(lo_mask, 0.0, qq).astype(_BF)

    bufs = (sa_s, sb_s, sc_s)
    prep_q(0)
    sa_s[...] = scores(0, 0, 4 * tk)

    for qi in range(nq):
        r0 = qi * tq
        cur = bufs[(2 * qi) % 3]
        nxt = bufs[(2 * qi + 1) % 3]
        ahead = bufs[(2 * qi + 2) % 3]
        m_s[...] = jnp.full(m_s.shape, NEG_BIG, _F32)
        acc_s[...] = jnp.zeros(acc_s.shape, _F32)

        def two_chunks(t, c, cur=cur, nxt=nxt):
            c0 = pl.multiple_of(2 * t * tk, tk)
            c1 = pl.multiple_of(c0 + tk, tk)
            c2 = pl.multiple_of(c0 + 2 * tk, tk)
            nxt[...] = scores(c1, 0, 4 * tk)
            consume(cur[...], c0, 0, 4 * tk, 0)
            cur[...] = scores(c2, 0, 4 * tk)
            consume(nxt[...], c1, 0, 4 * tk, 0)
            return c

        lax.fori_loop(0, qi, two_chunks, 0)
        nxt[0:2 * tk, :] = scores(r0 + tk, 2 * tk, 2 * tk)
        if qi + 1 < nq:
            prep_q(r0 + tq)
            ahead[...] = scores(0, 0, 4 * tk)
        consume(cur[...], r0, 0, 4 * tk, 2)
        consume(nxt[0:2 * tk, :], r0 + tk, 2 * tk, 2 * tk, 2)

        acc = acc_s[...]
        o = acc[:, 0:V_DIM] / acc[:, V_DIM:2 * V_DIM]
        for hf in range(2):
            a = o[(2 * hf) * tk:(2 * hf + 1) * tk, :] - lam * o[(2 * hf + 1) * tk:(2 * hf + 2) * tk, :]
            a = a * lax.rsqrt(jnp.mean(a * a, axis=-1, keepdims=True) + EPS) * sub_ref[...] * (1.0 - LAMBDA_INIT)
            o_ref[0, r0 + hf * tk:r0 + (hf + 1) * tk, :] = a.astype(_BF)


def _attention(lam_vecs, q3, k3, v3, q_norm, k_norm, subln, expert_weights):
    b, s, _ = q3.shape
    tq = 2 * ATTN_TK
    steps = b * ATTN_HEADS
    slab = lambda bi, hi: (bi, 0, hi)
    step_rows = lambda bi, hi: (bi * ATTN_HEADS + hi, 0)
    flat = [w.reshape(-1, w.shape[-1]) for w in expert_weights]
    assert all(w.shape[0] % (steps * 2 * SUBLANES) == 0 for w in flat)
    w_specs = [pl.BlockSpec((w.shape[0] // steps, w.shape[1]), step_rows) for w in flat]
    outs = pl.pallas_call(
        _attn_kernel,
        grid=(b, ATTN_HEADS),
        in_specs=[
            _const_spec((4, HEAD_DIM)),
            pl.BlockSpec((1, s, V_DIM), slab),
            pl.BlockSpec((1, s, V_DIM), slab),
            pl.BlockSpec((1, s, V_DIM), slab),
            _const_spec((1, V_DIM)),
            _const_spec((1, V_DIM)),
            _const_spec((1, V_DIM)),
        ] + w_specs,
        out_specs=[pl.BlockSpec((1, s, V_DIM), slab)] + w_specs,
        out_shape=[jax.ShapeDtypeStruct((b, s, ATTN_HEADS * V_DIM), _BF)] + [
            jax.ShapeDtypeStruct(w.shape, _BF) for w in flat],
        scratch_shapes=[
            pltpu.VMEM((s, V_DIM), _BF),
            pltpu.VMEM((s, 2 * V_DIM), _BF),
            pltpu.VMEM((2 * tq, V_DIM), _BF),
            pltpu.VMEM((2 * tq, V_DIM), _F32),
            pltpu.VMEM((2 * tq, 2 * V_DIM), _F32),
            pltpu.VMEM((2 * tq, ATTN_TK), _F32),
            pltpu.VMEM((2 * tq, ATTN_TK), _F32),
            pltpu.VMEM((2 * tq, ATTN_TK), _F32),
        ],
        compiler_params=pltpu.CompilerParams(
            dimension_semantics=("arbitrary", "arbitrary"), vmem_limit_bytes=VMEM_LIMIT),
        name="diffattn",
    )(lam_vecs, q3, k3, v3, q_norm.reshape(1, V_DIM), k_norm.reshape(1, V_DIM), subln.reshape(1, V_DIM), *flat)
    return outs[0], [wb.reshape(w.shape) for wb, w in zip(outs[1:], expert_weights)]


def _router_logits(t, wr_ref, wrh_ref, br_ref):
    t_hi = t.astype(_BF)
    t_lo = (t - t_hi.astype(_F32)).astype(_BF)
    l_hi = jnp.dot(t_hi, wr_ref[...], preferred_element_type=_F32)
    l_lo = jnp.dot(t_lo, wrh_ref[...], preferred_element_type=_F32)
    return l_hi[:, 0:LANES] + l_hi[:, LANES:2 * LANES] + l_lo + br_ref[...]


def _mix_kernel(x_ref, a_ref, c_ref, g_ref, wa_ref, wc_ref, wo_ref, bco_ref, nf_ref, wr_ref, wrh_ref, br_ref,
                h_ref, t_ref, key_ref, cnt_ref, run_s, lg_s, tw_s, tri_s):
    i = pl.program_id(0)
    tm, d = x_ref.shape
    half_groups = d // LANES // 2

    @pl.when(i == 0)
    def _():
        run_s[...] = jnp.zeros(run_s.shape, _F32)
        lg_s[1] = jnp.zeros(lg_s.shape[1:], _F32)
        tw_s[1] = jnp.zeros(tw_s.shape[1:], jnp.int32)
        rr = lax.broadcasted_iota(jnp.int32, (tm, tm), 0)
        ccol = lax.broadcasted_iota(jnp.int32, (tm, tm), 1)
        tri_s[...] = jnp.where(ccol < rr, 1.0, 0.0).astype(_BF)

    def tile_matmuls(sl):
        y_b = jnp.dot(c_ref[...], wc_ref[...], preferred_element_type=_F32) + bco_ref[...]
        y_a = jnp.dot(a_ref[...], wa_ref[...], preferred_element_type=_F32)
        merged = g_ref[:, 0:d].astype(_F32) * y_a + g_ref[:, d:2 * d].astype(_F32) * y_b
        h = x_ref[...] + jnp.dot(merged.astype(_BF), wo_ref[...], preferred_element_type=_F32)
        h_ref[...] = h
        t = h * lax.rsqrt(jnp.mean(h * h, axis=-1, keepdims=True) + EPS) * nf_ref[...]
        for sg in range(half_groups):
            hi_bits = lax.bitcast_convert_type(
                t[:, sg * LANES:(sg + 1) * LANES].astype(_BF).astype(_F32), jnp.uint32)
            lo_bits = lax.bitcast_convert_type(
                t[:, (sg + half_groups) * LANES:(sg + half_groups + 1) * LANES].astype(_BF).astype(_F32),
                jnp.uint32)
            tw_s[sl, sg] = lax.bitcast_convert_type(hi_bits | (lo_bits >> BF16_BITS), jnp.int32)
        lg_s[sl] = _router_logits(t, wr_ref, wrh_ref, br_ref)

    for sl in range(2):
        @pl.when(i % 2 == sl)
        def _(sl=sl):
            tile_matmuls(sl)
            _route_tile(i - 1, lg_s[1 - sl], tw_s.at[1 - sl], t_ref, key_ref, cnt_ref, run_s, tri_s)


def _route_tile(tile, lg, words_ref, t_ref, key_ref, cnt_ref, run_s, tri_s):
    tm = lg.shape[0]
    for sg in range(META_SUBLANE):
        t_ref[pl.ds(sg, tm, stride=SUBLANES), :] = words_ref[sg]
    lane = lax.broadcasted_iota(jnp.int32, (tm, LANES), 1)
    gl = jnp.where(lane < N_GROUPS, lg, NEG_BIG)
    gmax = jnp.max(gl, axis=-1, keepdims=True)
    gidx = jnp.min(jnp.where(gl == gmax, lane, LANES), axis=-1, keepdims=True)
    gsum = jnp.sum(jnp.where(lane < N_GROUPS, jnp.exp(gl - gmax), 0.0), axis=-1, keepdims=True)
    g_gate = 1.0 / gsum
    e_lo = N_GROUPS + EXPERTS_PER_GROUP * gidx
    el = jnp.where((lane >= e_lo) & (lane < e_lo + EXPERTS_PER_GROUP), lg, NEG_BIG)
    m1 = jnp.max(el, axis=-1, keepdims=True)
    i1 = jnp.min(jnp.where(el == m1, lane, LANES), axis=-1, keepdims=True)
    el2 = jnp.where(lane == i1, NEG_BIG, el)
    m2 = jnp.max(el2, axis=-1, keepdims=True)
    i2 = jnp.min(jnp.where(el2 == m2, lane, LANES), axis=-1, keepdims=True)
    e21 = jnp.exp(m2 - m1)
    w_top1 = g_gate * (1.0 / (1.0 + e21))
    w_top2 = g_gate * (e21 / (1.0 + e21))
    a1 = i1 - e_lo
    a2 = i2 - e_lo
    lo = jnp.minimum(a1, a2)
    hi = jnp.maximum(a1, a2)
    w_lo = jnp.where(a1 < a2, w_top1, w_top2)
    w_hi = jnp.where(a1 < a2, w_top2, w_top1)
    pidx = ((lo * (2 * EXPERTS_PER_GROUP - 1 - lo)) >> 1) + (hi - lo - 1)
    cls = gidx * PAIRS_PER_GROUP + pidx
    token = tile * tm + lax.broadcasted_iota(jnp.int32, (tm, LANES), 0)
    meta = jnp.where(lane == META_W_LO, lax.bitcast_convert_type(w_lo, jnp.int32),
                     jnp.where(lane == META_W_HI, lax.bitcast_convert_type(w_hi, jnp.int32),
                               jnp.where(lane == META_TOKEN, token, 0)))
    t_ref[pl.ds(META_SUBLANE, tm, stride=SUBLANES), :] = meta
    for sg in range(META_SUBLANE + 1, SUBLANES):
        t_ref[pl.ds(sg, tm, stride=SUBLANES), :] = jnp.zeros((tm, LANES), jnp.int32)

    onehot = lane == cls
    oh_bf = jnp.where(onehot, 1.0, 0.0).astype(_BF)
    before = jnp.dot(tri_s[...], oh_bf, preferred_element_type=_F32) + run_s[...]
    rank = jnp.sum(jnp.where(onehot, before, 0.0), axis=-1, keepdims=True)
    counted = jnp.where(tile >= 0, 1.0, 0.0)
    run_new = run_s[...] + counted * jnp.sum(jnp.where(onehot, 1.0, 0.0), axis=0, keepdims=True)
    run_s[...] = run_new
    cnt_ref[...] = run_new
    keyf = jnp.broadcast_to(cls.astype(_F32) * float(1 << RANK_BITS) + rank, (tm, LANES))
    for g in range(tm // LANES):
        kt = keyf[g * LANES:(g + 1) * LANES, :].T
        key_ref[0, g:g + 1, :] = kt[0:1, :].astype(jnp.int32)


def _mix(x2, attn2, cact2, gates2, wa_bf, wc_bf, wo_bf, b_conv_out, norm_ffn, w_router_cat, w_router_hi, b_router):
    n, d = x2.shape
    tm = TM_MIX
    aw = attn2.shape[1]
    cc = cact2.shape[1]
    nt = n // tm
    row = lambda i: (jnp.minimum(i, nt - 1), 0)
    prev = lambda i: (jnp.maximum(i - 1, 0), 0)
    return pl.pallas_call(
        _mix_kernel,
        grid=(nt + 1,),
        in_specs=[
            pl.BlockSpec((tm, d), row),
            pl.BlockSpec((tm, aw), row),
            pl.BlockSpec((tm, cc), row),
            pl.BlockSpec((tm, 2 * d), row),
            _const_spec(wa_bf.shape),
            _const_spec(wc_bf.shape),
            _const_spec(wo_bf.shape),
            _const_spec((1, d)),
            _const_spec((1, d)),
            _const_spec((d, 2 * LANES)),
            _const_spec((d, LANES)),
            _const_spec((1, LANES)),
        ],
        out_specs=[
            pl.BlockSpec((tm, d), row),
            pl.BlockSpec((tm * SUBLANES, LANES), prev),
            pl.BlockSpec((1, tm // LANES, LANES), lambda i: (jnp.maximum(i - 1, 0), 0, 0)),
            _const_spec((1, LANES)),
        ],
        out_shape=[
            jax.ShapeDtypeStruct((n, d), _F32),
            jax.ShapeDtypeStruct((n * SUBLANES, LANES), jnp.int32),
            jax.ShapeDtypeStruct((n // tm, tm // LANES, LANES), jnp.int32),
            jax.ShapeDtypeStruct((1, LANES), _F32),
        ],
        scratch_shapes=[
            pltpu.VMEM((1, LANES), _F32),
            pltpu.VMEM((2, tm, LANES), _F32),
            pltpu.VMEM((2, META_SUBLANE, tm, LANES), jnp.int32),
            pltpu.VMEM((tm, tm), _BF),
        ],
        compiler_params=pltpu.CompilerParams(
            dimension_semantics=("arbitrary",), vmem_limit_bytes=VMEM_LIMIT),
        name="mix_router",
    )(x2, attn2, cact2, gates2, wa_bf, wc_bf, wo_bf, b_conv_out.reshape(1, d), norm_ffn.reshape(1, d),
      w_router_cat, w_router_hi, b_router)


def _sorted_rows_kernel(pstart_ref, key_ref, o_ref):
    key = key_ref[...]
    cls = key >> RANK_BITS
    row = key & ((1 << RANK_BITS) - 1)
    for c in range(N_CLASSES):
        row = row + jnp.where(cls == c, pstart_ref[c], 0)
    o_ref[...] = row


def _sorted_rows(pstart, keys2):
    grid_spec = pltpu.PrefetchScalarGridSpec(
        num_scalar_prefetch=1,
        grid=(1,),
        in_specs=[pl.BlockSpec(keys2.shape, lambda i, *_: (0, 0))],
        out_specs=pl.BlockSpec(keys2.shape, lambda i, *_: (0, 0)),
    )
    return pl.pallas_call(
        _sorted_rows_kernel,
        grid_spec=grid_spec,
        out_shape=jax.ShapeDtypeStruct(keys2.shape, jnp.int32),
        name="sorted_rows",
    )(pstart, keys2)


def _token_rows(idx, count=1):
    return pl.ds(pl.multiple_of(idx * SUBLANES, SUBLANES), count * SUBLANES)


def _scatter_kernel(padstart_ref, padlen_ref, nvalid_ref, dest_ref, t_ref, xs_ref, zero_s, sem, zsem):
    i = pl.program_id(0)
    ts = t_ref.shape[0] // SUBLANES
    n_tokens = ts * pl.num_programs(0)
    bm = zero_s.shape[0] // SUBLANES
    nblk = xs_ref.shape[0] // (bm * SUBLANES)

    def start_rows(g, c):
        for u in range(DMA_UNROLL):
            r = g * DMA_UNROLL + u
            pltpu.make_async_copy(t_ref.at[_token_rows(r)], xs_ref.at[_token_rows(dest_ref[0, 0, r])], sem).start(
                priority=u % 2)
        return c

    def wait_rows(g, c):
        for _ in range(DMA_UNROLL):
            pltpu.make_async_copy(t_ref.at[_token_rows(0)], xs_ref.at[_token_rows(0)], sem).wait()
        return c

    lax.fori_loop(0, ts // DMA_UNROLL, start_rows, 0)

    @pl.when(i == 0)
    def _():
        rows = lax.broadcasted_iota(jnp.int32, zero_s.shape, 0)
        lanes = lax.broadcasted_iota(jnp.int32, zero_s.shape, 1)
        is_token = ((rows & (SUBLANES - 1)) == META_SUBLANE) & (lanes == META_TOKEN)
        zero_s[...] = jnp.where(is_token, n_tokens + (rows >> (SUBLANES.bit_length() - 1)), 0)

        def pad_copies(c, act):
            ln = padlen_ref[c]
            st = padstart_ref[c]
            for k in range(bm.bit_length() - 1):
                sz = 1 << k

                @pl.when(((ln >> k) & 1) == 1)
                def _():
                    dst = st + (ln & (sz - 1))
                    src = dst & (bm - 1)
                    act(pltpu.make_async_copy(zero_s.at[_token_rows(src, sz)], xs_ref.at[_token_rows(dst, sz)], zsem))

        def tail_copy(j, act):
            @pl.when(j >= nvalid_ref[0])
            def _():
                act(pltpu.make_async_copy(zero_s, xs_ref.at[_token_rows(j * bm, bm)], zsem))

        for act in (lambda cp: cp.start(), lambda cp: cp.wait()):
            lax.fori_loop(0, N_CLASSES, lambda c, carry, act=act: (pad_copies(c, act), carry)[1], 0)
            lax.fori_loop(0, nblk, lambda j, carry, act=act: (tail_copy(j, act), carry)[1], 0)

    lax.fori_loop(0, ts // DMA_UNROLL, wait_rows, 0)


def _scatter_rows(padstart, padlen, nvalid, dest, t_rows, n_rows):
    n = t_rows.shape[0] // SUBLANES
    ts = TS_SCATTER
    grid_spec = pltpu.PrefetchScalarGridSpec(
        num_scalar_prefetch=3,
        grid=(n // ts,),
        in_specs=[
            pl.BlockSpec((1, 1, ts), lambda i, *_: (i, 0, 0), memory_space=pltpu.SMEM),
            pl.BlockSpec((ts * SUBLANES, LANES), lambda i, *_: (i, 0)),
        ],
        out_specs=pl.BlockSpec(memory_space=pl.ANY),
        scratch_shapes=[pltpu.VMEM((BM_MOE * SUBLANES, LANES), jnp.int32), pltpu.SemaphoreType.DMA,
                        pltpu.SemaphoreType.DMA],
    )
    return pl.pallas_call(
        _scatter_kernel,
        grid_spec=grid_spec,
        out_shape=jax.ShapeDtypeStruct((n_rows * SUBLANES, LANES), jnp.int32),
        compiler_params=pltpu.CompilerParams(
            dimension_semantics=("arbitrary",), vmem_limit_bytes=VMEM_LIMIT),
        name="scatter_rows",
    )(padstart, padlen, nvalid, dest.reshape(n // ts, 1, ts), t_rows)


def _moe_kernel(ea_ref, eb_ref, valid_ref, xs_ref, w1a, w3a, w2a, w1b, w3b, w2b, out_ref,
                y_s, ids_v, ids_sm, rsem, isem):
    j = pl.program_id(0)
    nb = pl.num_programs(0)
    bm = xs_ref.shape[0] // SUBLANES
    ng = bm // LANES
    n_tokens = out_ref.shape[0] // SUBLANES - bm

    def ids_copy(sl):
        return pltpu.make_async_copy(ids_v.at[sl], ids_sm.at[sl], isem.at[sl])

    def row_copy(sl, r, tok):
        return pltpu.make_async_copy(y_s.at[sl, _token_rows(r)], out_ref.at[_token_rows(tok)], rsem.at[sl])

    def issue_rows(sl):
        for r in range(bm):
            row_copy(sl, r, ids_sm[sl, r // LANES, r % LANES]).start(priority=r % 2)

    def wait_rows(sl):
        def body(g, c):
            for _ in range(DMA_UNROLL):
                row_copy(sl, 0, 0).wait()
            return c

        lax.fori_loop(0, bm // DMA_UNROLL, body, 0)

    @pl.when(j == 0)
    def _():
        y_s[1] = jnp.zeros(y_s.shape[1:], _F32)
        pos = (lax.broadcasted_iota(jnp.int32, (ng, LANES), 0) * LANES
               + lax.broadcasted_iota(jnp.int32, (ng, LANES), 1))
        ids_v[1] = n_tokens + pos
        ids_copy(1).start()

    def send_previous(sl):
        ids_copy(1 - sl).wait()

        @pl.when(j >= 1)
        def _():
            wait_rows(sl)

        issue_rows(1 - sl)

    def step(sl):
        other = 1 - sl
        send_previous(sl)

        meta = xs_ref[pl.ds(META_SUBLANE, bm, stride=SUBLANES), :]
        lane = lax.broadcasted_iota(jnp.int32, (bm, LANES), 1)
        tok = jnp.sum(jnp.where(lane == META_TOKEN, meta.astype(_F32), 0.0), axis=-1, keepdims=True)
        tokb = jnp.broadcast_to(tok, (bm, LANES))
        for g in range(ng):
            ids_v[sl, g:g + 1, :] = tokb[g * LANES:(g + 1) * LANES, :].T[0:1, :].astype(jnp.int32)
        ids_copy(sl).start()

        words = [xs_ref[pl.ds(sg, bm, stride=SUBLANES), :] for sg in range(META_SUBLANE)]
        hi = [lax.bitcast_convert_type(w & jnp.int32(HIGH_HALF), _F32).astype(_BF) for w in words]
        lo = [lax.bitcast_convert_type(w << BF16_BITS, _F32).astype(_BF) for w in words]
        xb = jnp.concatenate(hi + lo, axis=1)
        metaf = lax.bitcast_convert_type(meta, _F32)
        w_a = jnp.sum(jnp.where(lane == META_W_LO, metaf, 0.0), axis=-1, keepdims=True)
        w_b = jnp.sum(jnp.where(lane == META_W_HI, metaf, 0.0), axis=-1, keepdims=True)

        def expert(w1, w3, w2):
            h1 = jnp.dot(xb, w1[0], preferred_element_type=_F32)
            h3 = jnp.dot(xb, w3[0], preferred_element_type=_F32)
            hdn = h1 * _sigmoid(h1) * h3
            return jnp.dot(hdn.astype(_BF), w2[0], preferred_element_type=_F32)

        y = w_a * expert(w1a, w3a, w2a) + w_b * expert(w1b, w3b, w2b)
        for sg in range(SUBLANES):
            y_s[sl, pl.ds(sg, bm, stride=SUBLANES), :] = y[:, sg * LANES:(sg + 1) * LANES]

        @pl.when(j == nb - 1)
        def _():
            wait_rows(other)
            ids_copy(sl).wait()
            issue_rows(sl)
            wait_rows(sl)

    is_valid = valid_ref[j] != 0
    prev_valid = valid_ref[jnp.maximum(j - 1, 0)] != 0
    for sl in range(2):
        @pl.when((j % 2 == sl) & is_valid)
        def _(sl=sl):
            step(sl)

        @pl.when((j % 2 == sl) & jnp.logical_not(is_valid) & prev_valid)
        def _(sl=sl):
            send_previous(sl)
            wait_rows(1 - sl)


def _moe(ea, eb, valid, xs, w1_bf, w3_bf, w2_bf, n_tokens):
    d, ff = w1_bf.shape[1], w1_bf.shape[2]
    assert d == SUBLANES * LANES
    bm = BM_MOE
    nb = xs.shape[0] // (bm * SUBLANES)
    grid_spec = pltpu.PrefetchScalarGridSpec(
        num_scalar_prefetch=3,
        grid=(nb,),
        in_specs=[
            pl.BlockSpec((bm * SUBLANES, LANES), lambda j, ea, eb, va: (j, 0)),
            pl.BlockSpec((1, d, ff), lambda j, ea, eb, va: (ea[j], 0, 0)),
            pl.BlockSpec((1, d, ff), lambda j, ea, eb, va: (ea[j], 0, 0)),
            pl.BlockSpec((1, ff, d), lambda j, ea, eb, va: (ea[j], 0, 0)),
            pl.BlockSpec((1, d, ff), lambda j, ea, eb, va: (eb[j], 0, 0)),
            pl.BlockSpec((1, d, ff), lambda j, ea, eb, va: (eb[j], 0, 0)),
            pl.BlockSpec((1, ff, d), lambda j, ea, eb, va: (eb[j], 0, 0)),
        ],
        out_specs=pl.BlockSpec(memory_space=pl.ANY),
        scratch_shapes=[
            pltpu.VMEM((2, bm * SUBLANES, LANES), _F32),
            pltpu.VMEM((2, bm // LANES, LANES), jnp.int32),
            pltpu.SMEM((2, bm // LANES, LANES), jnp.int32),
            pltpu.SemaphoreType.DMA((2,)),
            pltpu.SemaphoreType.DMA((2,)),
        ],
    )
    return pl.pallas_call(
        _moe_kernel,
        grid_spec=grid_spec,
        out_shape=jax.ShapeDtypeStruct(((n_tokens + bm) * SUBLANES, LANES), _F32),
        compiler_params=pltpu.CompilerParams(
            dimension_semantics=("arbitrary",), vmem_limit_bytes=VMEM_LIMIT),
        name="moe_experts",
    )(ea, eb, valid, xs, w1_bf, w3_bf, w2_bf, w1_bf, w3_bf, w2_bf)


def _final_kernel(h_ref, p_ref, y_ref, np_ref, wg_ref, bg_ref, wp_ref, o_ref):
    tm = h_ref.shape[0]
    proj = jnp.dot(p_ref[...].astype(_BF), wp_ref[...], preferred_element_type=_F32)
    y = jnp.concatenate([y_ref[pl.ds(sg, tm, stride=SUBLANES), :] for sg in range(SUBLANES)], axis=1)
    h = h_ref[...] + y
    u = h * lax.rsqrt(jnp.mean(h * h, axis=-1, keepdims=True) + EPS) * np_ref[...]
    gate = _sigmoid(jnp.dot(u.astype(_BF), wg_ref[...], preferred_element_type=_F32) + bg_ref[...])
    o_ref[...] = h + gate * proj


def _final(h1, p2, y_rows, norm_ple, wg_bf, b_ple_gate, wp_bf):
    n, d = h1.shape
    assert d == SUBLANES * LANES
    tm = TM_FINAL
    pd = p2.shape[1]
    row = lambda i: (i, 0)
    return pl.pallas_call(
        _final_kernel,
        grid=(n // tm,),
        in_specs=[
            pl.BlockSpec((tm, d), row),
            pl.BlockSpec((tm, pd), row),
            pl.BlockSpec((tm * SUBLANES, LANES), row),
            _const_spec((1, d)),
            _const_spec(wg_bf.shape),
            _const_spec((1, d)),
            _const_spec(wp_bf.shape),
        ],
        out_specs=pl.BlockSpec((tm, d), row),
        out_shape=jax.ShapeDtypeStruct((n, d), _F32),
        compiler_params=pltpu.CompilerParams(
            dimension_semantics=("arbitrary",), vmem_limit_bytes=VMEM_LIMIT),
        name="final_ple",
    )(h1, p2, y_rows, norm_ple.reshape(1, d), wg_bf, b_ple_gate.reshape(1, d), wp_bf)


def _class_tables():
    ea, eb = [], []
    for g in range(N_GROUPS):
        for lo in range(EXPERTS_PER_GROUP):
            for hi in range(lo + 1, EXPERTS_PER_GROUP):
                ea.append(g * EXPERTS_PER_GROUP + lo)
                eb.append(g * EXPERTS_PER_GROUP + hi)
    return np.asarray(ea, np.int32), np.asarray(eb, np.int32)


_CLASS_EA, _CLASS_EB = _class_tables()


def kernel(x, p, norm_mix, w_in, b_conv_in, b_gate, q_norm, k_norm, lambda_q1, lambda_k1, lambda_q2, lambda_k2, subln, w_attn_out, conv_w, conv_b, conv_ln_g, conv_ln_b, w_conv_out, b_conv_out, w_o, norm_ffn, w_router_group, b_router_group, w_router_expert, b_router_expert, w1, w3, w2, norm_ple, w_ple_gate, b_ple_gate, w_ple_proj):
    b, s, d = x.shape
    n = b * s
    depth = w_in.shape[0]
    qk = ATTN_HEADS * 2 * HEAD_DIM
    aw = ATTN_HEADS * V_DIM
    cc = conv_w.shape[-1]
    assert depth == 1 and n % TM_MIX == 0 and s % (2 * ATTN_TK) == 0 and s % TM_INPROJ == 0
    assert n % TS_SCATTER == 0 and n % TM_FINAL == 0 and n % BM_MOE == 0
    i = 0
    h = x.reshape(n, d)

    conv_w_pad = jnp.pad(conv_w[i], ((0, CONV_HALO - CONV_WIDTH), (0, 0)))
    q2, k2, v2, cact2, gates2 = _inproj(h, norm_mix[i], w_in[i].astype(_BF), b_conv_in[i], b_gate[i], conv_w_pad,
                                        conv_b[i], conv_ln_g[i], conv_ln_b[i], qk, aw, cc, s)

    lam_vecs = jnp.stack([lambda_q1[i], lambda_k1[i], lambda_q2[i], lambda_k2[i]]).astype(_F32)
    attn, (w1_bf, w3_bf, w2_bf) = _attention(lam_vecs, q2.reshape(b, s, qk), k2.reshape(b, s, qk),
                                             v2.reshape(b, s, aw), q_norm[i], k_norm[i], subln[i],
                                             (w1[i], w3[i], w2[i]))

    w_router = jnp.concatenate([w_router_group[i], w_router_expert[i]], axis=1)
    w_router = jnp.pad(w_router, ((0, 0), (0, LANES - w_router.shape[1])))
    b_router = jnp.concatenate([b_router_group[i], b_router_expert[i]])
    b_router = jnp.pad(b_router, (0, LANES - b_router.shape[0])).reshape(1, LANES)
    w_router_hi = w_router.astype(_BF)
    w_router_lo = (w_router - w_router_hi.astype(_F32)).astype(_BF)
    w_router_cat = jnp.concatenate([w_router_hi, w_router_lo], axis=1)
    h1, t_rows, keys, cnt = _mix(h, attn.reshape(n, aw), cact2, gates2, w_attn_out[i].astype(_BF),
                                 w_conv_out[i].astype(_BF), w_o[i].astype(_BF), b_conv_out[i], norm_ffn[i],
                                 w_router_cat, w_router_hi, b_router)

    bm = BM_MOE
    nb = n // bm + N_CLASSES
    counts = cnt[0].astype(jnp.int32)
    padded = ((counts + bm - 1) // bm) * bm
    pend = jnp.cumsum(padded)
    pstart = (pend - padded).astype(jnp.int32)
    total = pend[-1]
    blk = jnp.arange(nb, dtype=jnp.int32) * bm
    valid = blk < total
    first_row = jnp.where(valid, blk, total - 1)
    bcls = jnp.sum((pend[None, :] <= first_row[:, None]).astype(jnp.int32), axis=1)
    bcls = jnp.minimum(bcls, N_CLASSES - 1)
    cls_onehot = (bcls[:, None] == jnp.arange(N_CLASSES, dtype=jnp.int32)[None, :]).astype(jnp.int32)
    ea = jnp.sum(cls_onehot * jnp.asarray(_CLASS_EA)[None, :], axis=1)
    eb = jnp.sum(cls_onehot * jnp.asarray(_CLASS_EB)[None, :], axis=1)
    nvalid = (total // bm).astype(jnp.int32).reshape(1)

    dest = _sorted_rows(pstart, keys.reshape(n // LANES, LANES))
    xs = _scatter_rows((pstart + counts).astype(jnp.int32), (padded - counts).astype(jnp.int32), nvalid,
                       dest, t_rows, nb * bm)
    y_rows = _moe(ea, eb, valid.astype(jnp.int32), xs, w1_bf, w3_bf, w2_bf, n)

    out = _final(h1, p[i].reshape(n, -1), y_rows, norm_ple[i], w_ple_gate[i].astype(_BF), b_ple_gate[i],
                 w_ple_proj[i].astype(_BF))
    return out.reshape(b, s, d)
```

```python
import functools
import math

import numpy as np
import jax
import jax.numpy as jnp
from jax import lax
from jax.experimental import pallas as pl
from jax.experimental.pallas import tpu as pltpu

ATTN_HEADS = 4
HEAD_DIM = 64
V_DIM = 2 * HEAD_DIM
CONV_WIDTH = 31
N_GROUPS = 4
EXPERTS_PER_GROUP = 8
PAIRS_PER_GROUP = EXPERTS_PER_GROUP * (EXPERTS_PER_GROUP - 1) // 2
N_CLASSES = N_GROUPS * PAIRS_PER_GROUP
EPS = 1e-6
LAMBDA_INIT = 0.8 - 0.6 * math.exp(-0.3 * 0)

LANES = 128
SUBLANES = 8
CONV_HALO = 32
NEG_BIG = -1e30
RANK_BITS = 17
BF16_BITS = 16
HIGH_HALF = -(1 << BF16_BITS)
META_SUBLANE = 4
META_W_LO, META_W_HI, META_TOKEN = 0, 1, 2
DMA_UNROLL = 8

TM_INPROJ = 512
ATTN_TK = 512
TM_MIX = 512
CONV_ROWS = 64
TS_SCATTER = 512
BM_MOE = 256
TM_FINAL = 1024
VMEM_LIMIT = 56 * 1024 * 1024

_BF = jnp.bfloat16
_F32 = jnp.float32


def _const_spec(shape):
    nd = len(shape)
    return pl.BlockSpec(shape, lambda *_: (0,) * nd)


def _sigmoid(x):
    return 1.0 / (1.0 + jnp.exp(-x))


def _inproj_kernel(x_ref, nm_ref, w_ref, bci_ref, bg_ref, cw_ref, cb_ref, lng_ref, lnb_ref,
                   q_ref, k_ref, v_ref, c_ref, g_ref, ext_s, xsh_s, cv_s, *, tiles_per_seq):
    i = pl.program_id(0)
    tm = x_ref.shape[0]
    qk = q_ref.shape[1]
    aw = v_ref.shape[1]
    cc = c_ref.shape[1]
    ext_rows = CONV_HALO + tm
    o0, o1, o2, o3 = qk, 2 * qk, 2 * qk + aw, 2 * qk + aw + 2 * cc

    @pl.when(i == 0)
    def _():
        ext_s[ext_rows:ext_rows + SUBLANES, :] = jnp.zeros((SUBLANES, cc), _F32)

    @pl.when((i % tiles_per_seq) == 0)
    def _():
        ext_s[0:CONV_HALO, :] = jnp.zeros((CONV_HALO, cc), _F32)

    x = x_ref[...]
    u = x * lax.rsqrt(jnp.mean(x * x, axis=-1, keepdims=True) + EPS) * nm_ref[...]
    ub = u.astype(_BF)
    c = jnp.dot(ub, w_ref[:, o2:o3], preferred_element_type=_F32) + bci_ref[...]
    ext_s[CONV_HALO:ext_rows, :] = c[:, :cc] * _sigmoid(c[:, cc:])
    q_ref[...] = jnp.dot(ub, w_ref[:, 0:o0], preferred_element_type=_F32).astype(_BF)
    k_ref[...] = jnp.dot(ub, w_ref[:, o0:o1], preferred_element_type=_F32).astype(_BF)
    v_ref[...] = jnp.dot(ub, w_ref[:, o1:o2], preferred_element_type=_F32).astype(_BF)
    g = jnp.dot(ub, w_ref[:, o3:], preferred_element_type=_F32) + bg_ref[...]
    g_ref[...] = _sigmoid(g).astype(_BF)
    off = CONV_HALO - (CONV_WIDTH - 1)

    for rho in range(1, SUBLANES):
        xsh_s[rho - 1] = ext_s[rho:rho + ext_rows, :]

    for r0 in range(0, tm, CONV_ROWS):
        acc = jnp.zeros((CONV_ROWS, cc), _F32)
        for j in range(CONV_WIDTH):
            a, rho = divmod(off + j, SUBLANES)
            lo_row = r0 + a * SUBLANES
            if rho == 0:
                xv = ext_s[lo_row:lo_row + CONV_ROWS, :]
            else:
                xv = xsh_s[rho - 1, lo_row:lo_row + CONV_ROWS, :]
            acc = acc + cw_ref[j:j + 1, :] * xv
        cv_s[r0:r0 + CONV_ROWS, :] = acc

    ext_s[0:CONV_HALO, :] = ext_s[tm:ext_rows, :]

    cv = cv_s[...] + cb_ref[...]
    mu = jnp.mean(cv, axis=-1, keepdims=True)
    xc = cv - mu
    var = jnp.mean(xc * xc, axis=-1, keepdims=True)
    ln = xc * lax.rsqrt(var + EPS) * lng_ref[...] + lnb_ref[...]
    c_ref[...] = (ln * _sigmoid(ln)).astype(_BF)


def _inproj(x2, norm_mix, w_in_bf, b_conv_in, b_gate, conv_w, conv_b, ln_g, ln_b, qk, aw, cc, seq_len):
    n, d = x2.shape
    tm = TM_INPROJ
    ng = b_gate.shape[-1]
    kern = functools.partial(_inproj_kernel, tiles_per_seq=seq_len // tm)
    return pl.pallas_call(
        kern,
        grid=(n // tm,),
        in_specs=[
            pl.BlockSpec((tm, d), lambda i: (i, 0)),
            _const_spec((1, d)),
            pl.BlockSpec(w_in_bf.shape, lambda i: (0, 0), pipeline_mode=pl.Buffered(1)),
            _const_spec((1, 2 * cc)),
            _const_spec((1, ng)),
            _const_spec((CONV_HALO, cc)),
            _const_spec((1, cc)),
            _const_spec((1, cc)),
            _const_spec((1, cc)),
        ],
        out_specs=[
            pl.BlockSpec((tm, qk), lambda i: (i, 0)),
            pl.BlockSpec((tm, qk), lambda i: (i, 0)),
            pl.BlockSpec((tm, aw), lambda i: (i, 0)),
            pl.BlockSpec((tm, cc), lambda i: (i, 0)),
            pl.BlockSpec((tm, ng), lambda i: (i, 0)),
        ],
        out_shape=[
            jax.ShapeDtypeStruct((n, qk), _BF),
            jax.ShapeDtypeStruct((n, qk), _BF),
            jax.ShapeDtypeStruct((n, aw), _BF),
            jax.ShapeDtypeStruct((n, cc), _BF),
            jax.ShapeDtypeStruct((n, ng), _BF),
        ],
        scratch_shapes=[
            pltpu.VMEM((CONV_HALO + tm + SUBLANES, cc), _F32),
            pltpu.VMEM((SUBLANES - 1, CONV_HALO + tm, cc), _F32),
            pltpu.VMEM((tm, cc), _F32),
        ],
        compiler_params=pltpu.CompilerParams(
            dimension_semantics=("arbitrary",), vmem_limit_bytes=VMEM_LIMIT),
        name="inproj",
    )(x2, norm_mix.reshape(1, d), w_in_bf, b_conv_in.reshape(1, -1), b_gate.reshape(1, -1), conv_w,
      conv_b.reshape(1, cc), ln_g.reshape(1, cc), ln_b.reshape(1, cc))


def _half_rmsnorm(x, gain_row, lo_mask):
    x2 = x * x
    s_lo = jnp.sum(jnp.where(lo_mask, x2, 0.0), axis=-1, keepdims=True)
    s_hi = jnp.sum(jnp.where(lo_mask, 0.0, x2), axis=-1, keepdims=True)
    r = jnp.where(lo_mask, lax.rsqrt(s_lo * (1.0 / HEAD_DIM) + EPS), lax.rsqrt(s_hi * (1.0 / HEAD_DIM) + EPS))
    return x * r * gain_row


def _attn_kernel(lam_ref, q_ref, k_ref, v_ref, qn_ref, kn_ref, sub_ref, w1f_ref, w3f_ref, w2f_ref,
                 o_ref, w1b_ref, w3b_ref, w2b_ref, kn_s, va_s, qs_s, m_s, acc_s, sa_s, sb_s, sc_s):
    for wf_ref, wb_ref in ((w1f_ref, w1b_ref), (w3f_ref, w3b_ref), (w2f_ref, w2b_ref)):
        wb_ref[...] = wf_ref[...].astype(_BF)

    s_len = q_ref.shape[1]
    tk = ATTN_TK
    tq = 2 * tk
    nq = s_len // tq
    lane = lax.broadcasted_iota(jnp.int32, (1, V_DIM), 1)
    lo_mask = lane < HEAD_DIM

    lv = lam_ref[...]
    d1 = jnp.sum(lv[0:1, :] * lv[1:2, :], axis=-1, keepdims=True)
    d2 = jnp.sum(lv[2:3, :] * lv[3:4, :], axis=-1, keepdims=True)
    lam = jnp.exp(d1) - jnp.exp(d2) + LAMBDA_INIT

    kgain = kn_ref[...]
    qgain = qn_ref[...] * (math.log2(math.e) / math.sqrt(HEAD_DIM))

    def prep_kv(i, carry):
        r0 = pl.multiple_of(i * tq, tq)
        kk = k_ref[0, pl.ds(r0, tq), :].astype(_F32)
        kn_s[pl.ds(r0, tq), :] = _half_rmsnorm(kk, kgain, lo_mask).astype(_BF)
        va_s[pl.ds(r0, tq), 0:V_DIM] = v_ref[0, pl.ds(r0, tq), :]
        va_s[pl.ds(r0, tq), V_DIM:2 * V_DIM] = jnp.ones((tq, V_DIM), _BF)
        return carry

    lax.fori_loop(0, nq, prep_kv, 0)

    def scores(c0, ra, nr):
        kc = kn_s[pl.ds(c0, tk), :]
        return lax.dot_general(qs_s[ra:ra + nr, :], kc, (((1,), (1,)), ((), ())), preferred_element_type=_F32)

    def consume(s, c0, ra, nr, n_masked):
        vc = va_s[pl.ds(c0, tk), :]
        if n_masked:
            keep = (lax.broadcasted_iota(jnp.int32, (tk, tk), 1) <= lax.broadcasted_iota(jnp.int32, (tk, tk), 0))
            groups = [s[g * tk:(g + 1) * tk, :] for g in range(nr // tk)]
            groups = [jnp.where(keep, sg, NEG_BIG) if g < n_masked else sg for g, sg in enumerate(groups)]
            s = jnp.concatenate(groups, axis=0)
        m_old = m_s[ra:ra + nr, :]
        m_new = jnp.maximum(m_old, jnp.max(s, axis=-1, keepdims=True))
        alpha = jnp.exp2(m_old - m_new)
        p = jnp.concatenate(
            [jnp.exp2(s[:, c * LANES:(c + 1) * LANES] - m_new) for c in range(tk // LANES)], axis=1)
        pv = jnp.dot(p.astype(_BF), vc, preferred_element_type=_F32)
        acc_s[ra:ra + nr, :] = jnp.concatenate([alpha, alpha], axis=1) * acc_s[ra:ra + nr, :] + pv
        m_s[ra:ra + nr, :] = m_new

    def prep_q(r0):
        for hf in range(2):
            qq = _half_rmsnorm(q_ref[0, pl.ds(r0 + hf * tk, tk), :].astype(_F32), qgain, lo_mask)
            qs_s[(2 * hf) * tk:(2 * hf + 1) * tk, :] = jnp.where(lo_mask, qq, 0.0).astype(_BF)
            qs_s[(2 * hf + 1) * tk:(2 * hf + 2) * tk, :] = jnp.where(lo_mask, 0.0, qq).astype(_BF)

    bufs = (sa_s, sb_s, sc_s)
    prep_q(0)
    sa_s[...] = scores(0, 0, 4 * tk)

    for qi in range(nq):
        r0 = qi * tq
        cur = bufs[(2 * qi) % 3]
        nxt = bufs[(2 * qi + 1) % 3]
        ahead = bufs[(2 * qi + 2) % 3]
        m_s[...] = jnp.full(m_s.shape, NEG_BIG, _F32)
        acc_s[...] = jnp.zeros(acc_s.shape, _F32)

        def two_chunks(t, c, cur=cur, nxt=nxt):
            c0 = pl.multiple_of(2 * t * tk, tk)
            c1 = pl.multiple_of(c0 + tk, tk)
            c2 = pl.multiple_of(c0 + 2 * tk, tk)
            nxt[...] = scores(c1, 0, 4 * tk)
            consume(cur[...], c0, 0, 4 * tk, 0)
            cur[...] = scores(c2, 0, 4 * tk)
            consume(nxt[...], c1, 0, 4 * tk, 0)
            return c

        lax.fori_loop(0, qi, two_chunks, 0)
        nxt[0:2 * tk, :] = scores(r0 + tk, 2 * tk, 2 * tk)
        if qi + 1 < nq:
            prep_q(r0 + tq)
            ahead[...] = scores(0, 0, 4 * tk)
        consume(cur[...], r0, 0, 4 * tk, 2)
        consume(nxt[0:2 * tk, :], r0 + tk, 2 * tk, 2 * tk, 2)

        acc = acc_s[...]
        o = acc[:, 0:V_DIM] / acc[:, V_DIM:2 * V_DIM]
        for hf in range(2):
            a = o[(2 * hf) * tk:(2 * hf + 1) * tk, :] - lam * o[(2 * hf + 1) * tk:(2 * hf + 2) * tk, :]
            a = a * lax.rsqrt(jnp.mean(a * a, axis=-1, keepdims=True) + EPS) * sub_ref[...] * (1.0 - LAMBDA_INIT)
            o_ref[0, r0 + hf * tk:r0 + (hf + 1) * tk, :] = a.astype(_BF)


def _attention(lam_vecs, q3, k3, v3, q_norm, k_norm, subln, expert_weights):
    b, s, _ = q3.shape
    tq = 2 * ATTN_TK
    steps = b * ATTN_HEADS
    slab = lambda bi, hi: (bi, 0, hi)
    step_rows = lambda bi, hi: (bi * ATTN_HEADS + hi, 0)
    flat = [w.reshape(-1, w.shape[-1]) for w in expert_weights]
    assert all(w.shape[0] % (steps * 2 * SUBLANES) == 0 for w in flat)
    w_specs = [pl.BlockSpec((w.shape[0] // steps, w.shape[1]), step_rows) for w in flat]
    outs = pl.pallas_call(
        _attn_kernel,
        grid=(b, ATTN_HEADS),
        in_specs=[
            _const_spec((4, HEAD_DIM)),
            pl.BlockSpec((1, s, V_DIM), slab),
            pl.BlockSpec((1, s, V_DIM), slab),
            pl.BlockSpec((1, s, V_DIM), slab),
            _const_spec((1, V_DIM)),
            _const_spec((1, V_DIM)),
            _const_spec((1, V_DIM)),
        ] + w_specs,
        out_specs=[pl.BlockSpec((1, s, V_DIM), slab)] + w_specs,
        out_shape=[jax.ShapeDtypeStruct((b, s, ATTN_HEADS * V_DIM), _BF)] + [
            jax.ShapeDtypeStruct(w.shape, _BF) for w in flat],
        scratch_shapes=[
            pltpu.VMEM((s, V_DIM), _BF),
            pltpu.VMEM((s, 2 * V_DIM), _BF),
            pltpu.VMEM((2 * tq, V_DIM), _BF),
            pltpu.VMEM((2 * tq, V_DIM), _F32),
            pltpu.VMEM((2 * tq, 2 * V_DIM), _F32),
            pltpu.VMEM((2 * tq, ATTN_TK), _F32),
            pltpu.VMEM((2 * tq, ATTN_TK), _F32),
            pltpu.VMEM((2 * tq, ATTN_TK), _F32),
        ],
        compiler_params=pltpu.CompilerParams(
            dimension_semantics=("arbitrary", "arbitrary"), vmem_limit_bytes=VMEM_LIMIT),
        name="diffattn",
    )(lam_vecs, q3, k3, v3, q_norm.reshape(1, V_DIM), k_norm.reshape(1, V_DIM), subln.reshape(1, V_DIM), *flat)
    return outs[0], [wb.reshape(w.shape) for wb, w in zip(outs[1:], expert_weights)]


def _router_logits(t, wr_ref, wrh_ref, br_ref):
    t_hi = t.astype(_BF)
    t_lo = (t - t_hi.astype(_F32)).astype(_BF)
    l_hi = jnp.dot(t_hi, wr_ref[...], preferred_element_type=_F32)
    l_lo = jnp.dot(t_lo, wrh_ref[...], preferred_element_type=_F32)
    return l_hi[:, 0:LANES] + l_hi[:, LANES:2 * LANES] + l_lo + br_ref[...]


def _mix_kernel(x_ref, a_ref, c_ref, g_ref, wa_ref, wc_ref, wo_ref, bco_ref, nf_ref, wr_ref, wrh_ref, br_ref,
                h_ref, t_ref, key_ref, cnt_ref, run_s, lg_s, tw_s, tri_s):
    i = pl.program_id(0)
    tm, d = x_ref.shape
    half_groups = d // LANES // 2

    @pl.when(i == 0)
    def _():
        run_s[...] = jnp.zeros(run_s.shape, _F32)
        lg_s[1] = jnp.zeros(lg_s.shape[1:], _F32)
        tw_s[1] = jnp.zeros(tw_s.shape[1:], jnp.int32)
        rr = lax.broadcasted_iota(jnp.int32, (tm, tm), 0)
        ccol = lax.broadcasted_iota(jnp.int32, (tm, tm), 1)
        tri_s[...] = jnp.where(ccol < rr, 1.0, 0.0).astype(_BF)

    def tile_matmuls(sl):
        y_b = jnp.dot(c_ref[...], wc_ref[...], preferred_element_type=_F32) + bco_ref[...]
        y_a = jnp.dot(a_ref[...], wa_ref[...], preferred_element_type=_F32)
        merged = g_ref[:, 0:d].astype(_F32) * y_a + g_ref[:, d:2 * d].astype(_F32) * y_b
        h = x_ref[...] + jnp.dot(merged.astype(_BF), wo_ref[...], preferred_element_type=_F32)
        h_ref[...] = h
        t = h * lax.rsqrt(jnp.mean(h * h, axis=-1, keepdims=True) + EPS) * nf_ref[...]
        for sg in range(half_groups):
            hi_bits = lax.bitcast_convert_type(
                t[:, sg * LANES:(sg + 1) * LANES].astype(_BF).astype(_F32), jnp.uint32)
            lo_bits = lax.bitcast_convert_type(
                t[:, (sg + half_groups) * LANES:(sg + half_groups + 1) * LANES].astype(_BF).astype(_F32),
                jnp.uint32)
            tw_s[sl, sg] = lax.bitcast_convert_type(hi_bits | (lo_bits >> BF16_BITS), jnp.int32)
        lg_s[sl] = _router_logits(t, wr_ref, wrh_ref, br_ref)

    for sl in range(2):
        @pl.when(i % 2 == sl)
        def _(sl=sl):
            tile_matmuls(sl)
            _route_tile(i - 1, lg_s[1 - sl], tw_s.at[1 - sl], t_ref, key_ref, cnt_ref, run_s, tri_s)


def _route_tile(tile, lg, words_ref, t_ref, key_ref, cnt_ref, run_s, tri_s):
    tm = lg.shape[0]
    for sg in range(META_SUBLANE):
        t_ref[pl.ds(sg, tm, stride=SUBLANES), :] = words_ref[sg]
    lane = lax.broadcasted_iota(jnp.int32, (tm, LANES), 1)
    gl = jnp.where(lane < N_GROUPS, lg, NEG_BIG)
    gmax = jnp.max(gl, axis=-1, keepdims=True)
    gidx = jnp.min(jnp.where(gl == gmax, lane, LANES), axis=-1, keepdims=True)
    gsum = jnp.sum(jnp.where(lane < N_GROUPS, jnp.exp(gl - gmax), 0.0), axis=-1, keepdims=True)
    g_gate = 1.0 / gsum
    e_lo = N_GROUPS + EXPERTS_PER_GROUP * gidx
    el = jnp.where((lane >= e_lo) & (lane < e_lo + EXPERTS_PER_GROUP), lg, NEG_BIG)
    m1 = jnp.max(el, axis=-1, keepdims=True)
    i1 = jnp.min(jnp.where(el == m1, lane, LANES), axis=-1, keepdims=True)
    el2 = jnp.where(lane == i1, NEG_BIG, el)
    m2 = jnp.max(el2, axis=-1, keepdims=True)
    i2 = jnp.min(jnp.where(el2 == m2, lane, LANES), axis=-1, keepdims=True)
    e21 = jnp.exp(m2 - m1)
    w_top1 = g_gate * (1.0 / (1.0 + e21))
    w_top2 = g_gate * (e21 / (1.0 + e21))
    a1 = i1 - e_lo
    a2 = i2 - e_lo
    lo = jnp.minimum(a1, a2)
    hi = jnp.maximum(a1, a2)
    w_lo = jnp.where(a1 < a2, w_top1, w_top2)
    w_hi = jnp.where(a1 < a2, w_top2, w_top1)
    pidx = ((lo * (2 * EXPERTS_PER_GROUP - 1 - lo)) >> 1) + (hi - lo - 1)
    cls = gidx * PAIRS_PER_GROUP + pidx
    token = tile * tm + lax.broadcasted_iota(jnp.int32, (tm, LANES), 0)
    meta = jnp.where(lane == META_W_LO, lax.bitcast_convert_type(w_lo, jnp.int32),
                     jnp.where(lane == META_W_HI, lax.bitcast_convert_type(w_hi, jnp.int32),
                               jnp.where(lane == META_TOKEN, token, 0)))
    t_ref[pl.ds(META_SUBLANE, tm, stride=SUBLANES), :] = meta
    for sg in range(META_SUBLANE + 1, SUBLANES):
        t_ref[pl.ds(sg, tm, stride=SUBLANES), :] = jnp.zeros((tm, LANES), jnp.int32)

    onehot = lane == cls
    oh_bf = jnp.where(onehot, 1.0, 0.0).astype(_BF)
    before = jnp.dot(tri_s[...], oh_bf, preferred_element_type=_F32) + run_s[...]
    rank = jnp.sum(jnp.where(onehot, before, 0.0), axis=-1, keepdims=True)
    counted = jnp.where(tile >= 0, 1.0, 0.0)
    run_new = run_s[...] + counted * jnp.sum(jnp.where(onehot, 1.0, 0.0), axis=0, keepdims=True)
    run_s[...] = run_new
    cnt_ref[...] = run_new
    keyf = jnp.broadcast_to(cls.astype(_F32) * float(1 << RANK_BITS) + rank, (tm, LANES))
    for g in range(tm // LANES):
        kt = keyf[g * LANES:(g + 1) * LANES, :].T
        key_ref[0, g:g + 1, :] = kt[0:1, :].astype(jnp.int32)


def _mix(x2, attn2, cact2, gates2, wa_bf, wc_bf, wo_bf, b_conv_out, norm_ffn, w_router_cat, w_router_hi, b_router):
    n, d = x2.shape
    tm = TM_MIX
    aw = attn2.shape[1]
    cc = cact2.shape[1]
    nt = n // tm
    row = lambda i: (jnp.minimum(i, nt - 1), 0)
    prev = lambda i: (jnp.maximum(i - 1, 0), 0)
    return pl.pallas_call(
        _mix_kernel,
        grid=(nt + 1,),
        in_specs=[
            pl.BlockSpec((tm, d), row),
            pl.BlockSpec((tm, aw), row),
            pl.BlockSpec((tm, cc), row),
            pl.BlockSpec((tm, 2 * d), row),
            _const_spec(wa_bf.shape),
            _const_spec(wc_bf.shape),
            _const_spec(wo_bf.shape),
            _const_spec((1, d)),
            _const_spec((1, d)),
            _const_spec((d, 2 * LANES)),
            _const_spec((d, LANES)),
            _const_spec((1, LANES)),
        ],
        out_specs=[
            pl.BlockSpec((tm, d), row),
            pl.BlockSpec((tm * SUBLANES, LANES), prev),
            pl.BlockSpec((1, tm // LANES, LANES), lambda i: (jnp.maximum(i - 1, 0), 0, 0)),
            _const_spec((1, LANES)),
        ],
        out_shape=[
            jax.ShapeDtypeStruct((n, d), _F32),
            jax.ShapeDtypeStruct((n * SUBLANES, LANES), jnp.int32),
            jax.ShapeDtypeStruct((n // tm, tm // LANES, LANES), jnp.int32),
            jax.ShapeDtypeStruct((1, LANES), _F32),
        ],
        scratch_shapes=[
            pltpu.VMEM((1, LANES), _F32),
            pltpu.VMEM((2, tm, LANES), _F32),
            pltpu.VMEM((2, META_SUBLANE, tm, LANES), jnp.int32),
            pltpu.VMEM((tm, tm), _BF),
        ],
        compiler_params=pltpu.CompilerParams(
            dimension_semantics=("arbitrary",), vmem_limit_bytes=VMEM_LIMIT),
        name="mix_router",
    )(x2, attn2, cact2, gates2, wa_bf, wc_bf, wo_bf, b_conv_out.reshape(1, d), norm_ffn.reshape(1, d),
      w_router_cat, w_router_hi, b_router)


def _sorted_rows_kernel(pstart_ref, key_ref, o_ref):
    key = key_ref[...]
    cls = key >> RANK_BITS
    row = key & ((1 << RANK_BITS) - 1)
    for c in range(N_CLASSES):
        row = row + jnp.where(cls == c, pstart_ref[c], 0)
    o_ref[...] = row


def _sorted_rows(pstart, keys2):
    grid_spec = pltpu.PrefetchScalarGridSpec(
        num_scalar_prefetch=1,
        grid=(1,),
        in_specs=[pl.BlockSpec(keys2.shape, lambda i, *_: (0, 0))],
        out_specs=pl.BlockSpec(keys2.shape, lambda i, *_: (0, 0)),
    )
    return pl.pallas_call(
        _sorted_rows_kernel,
        grid_spec=grid_spec,
        out_shape=jax.ShapeDtypeStruct(keys2.shape, jnp.int32),
        name="sorted_rows",
    )(pstart, keys2)


def _token_rows(idx, count=1):
    return pl.ds(pl.multiple_of(idx * SUBLANES, SUBLANES), count * SUBLANES)


def _scatter_kernel(padstart_ref, padlen_ref, nvalid_ref, dest_ref, t_ref, xs_ref, zero_s, sem, zsem):
    i = pl.program_id(0)
    ts = t_ref.shape[0] // SUBLANES
    n_tokens = ts * pl.num_programs(0)
    bm = zero_s.shape[0] // SUBLANES
    nblk = xs_ref.shape[0] // (bm * SUBLANES)

    def start_rows(g, c):
        for u in range(DMA_UNROLL):
            r = g * DMA_UNROLL + u
            pltpu.make_async_copy(t_ref.at[_token_rows(r)], xs_ref.at[_token_rows(dest_ref[0, 0, r])], sem).start(
                priority=u % 2)
        return c

    def wait_rows(g, c):
        for _ in range(DMA_UNROLL):
            pltpu.make_async_copy(t_ref.at[_token_rows(0)], xs_ref.at[_token_rows(0)], sem).wait()
        return c

    lax.fori_loop(0, ts // DMA_UNROLL, start_rows, 0)

    @pl.when(i == 0)
    def _():
        rows = lax.broadcasted_iota(jnp.int32, zero_s.shape, 0)
        lanes = lax.broadcasted_iota(jnp.int32, zero_s.shape, 1)
        is_token = ((rows & (SUBLANES - 1)) == META_SUBLANE) & (lanes == META_TOKEN)
        zero_s[...] = jnp.where(is_token, n_tokens + (rows >> (SUBLANES.bit_length() - 1)), 0)

        def pad_copies(c, act):
            ln = padlen_ref[c]
            st = padstart_ref[c]
            for k in range(bm.bit_length() - 1):
                sz = 1 << k

                @pl.when(((ln >> k) & 1) == 1)
                def _():
                    dst = st + (ln & (sz - 1))
                    src = dst & (bm - 1)
                    act(pltpu.make_async_copy(zero_s.at[_token_rows(src, sz)], xs_ref.at[_token_rows(dst, sz)], zsem))

        def tail_copy(j, act):
            @pl.when(j >= nvalid_ref[0])
            def _():
                act(pltpu.make_async_copy(zero_s, xs_ref.at[_token_rows(j * bm, bm)], zsem))

        for act in (lambda cp: cp.start(), lambda cp: cp.wait()):
            lax.fori_loop(0, N_CLASSES, lambda c, carry, act=act: (pad_copies(c, act), carry)[1], 0)
            lax.fori_loop(0, nblk, lambda j, carry, act=act: (tail_copy(j, act), carry)[1], 0)

    lax.fori_loop(0, ts // DMA_UNROLL, wait_rows, 0)


def _scatter_rows(padstart, padlen, nvalid, dest, t_rows, n_rows):
    n = t_rows.shape[0] // SUBLANES
    ts = TS_SCATTER
    grid_spec = pltpu.PrefetchScalarGridSpec(
        num_scalar_prefetch=3,
        grid=(n // ts,),
        in_specs=[
            pl.BlockSpec((1, 1, ts), lambda i, *_: (i, 0, 0), memory_space=pltpu.SMEM),
            pl.BlockSpec((ts * SUBLANES, LANES), lambda i, *_: (i, 0)),
        ],
        out_specs=pl.BlockSpec(memory_space=pl.ANY),
        scratch_shapes=[pltpu.VMEM((BM_MOE * SUBLANES, LANES), jnp.int32), pltpu.SemaphoreType.DMA,
                        pltpu.SemaphoreType.DMA],
    )
    return pl.pallas_call(
        _scatter_kernel,
        grid_spec=grid_spec,
        out_shape=jax.ShapeDtypeStruct((n_rows * SUBLANES, LANES), jnp.int32),
        compiler_params=pltpu.CompilerParams(
            dimension_semantics=("arbitrary",), vmem_limit_bytes=VMEM_LIMIT),
        name="scatter_rows",
    )(padstart, padlen, nvalid, dest.reshape(n // ts, 1, ts), t_rows)


def _moe_kernel(ea_ref, eb_ref, valid_ref, xs_ref, w1a, w3a, w2a, w1b, w3b, w2b, out_ref,
                y_s, ids_v, ids_sm, rsem, isem):
    j = pl.program_id(0)
    nb = pl.num_programs(0)
    bm = xs_ref.shape[0] // SUBLANES
    ng = bm // LANES
    n_tokens = out_ref.shape[0] // SUBLANES - bm

    def ids_copy(sl):
        return pltpu.make_async_copy(ids_v.at[sl], ids_sm.at[sl], isem.at[sl])

    def row_copy(sl, r, tok):
        return pltpu.make_async_copy(y_s.at[sl, _token_rows(r)], out_ref.at[_token_rows(tok)], rsem.at[sl])

    def issue_rows(sl):
        for r in range(bm):
            row_copy(sl, r, ids_sm[sl, r // LANES, r % LANES]).start(priority=r % 2)

    def wait_rows(sl):
        def body(g, c):
            for _ in range(DMA_UNROLL):
                row_copy(sl, 0, 0).wait()
            return c

        lax.fori_loop(0, bm // DMA_UNROLL, body, 0)

    @pl.when(j == 0)
    def _():
        y_s[1] = jnp.zeros(y_s.shape[1:], _F32)
        pos = (lax.broadcasted_iota(jnp.int32, (ng, LANES), 0) * LANES
               + lax.broadcasted_iota(jnp.int32, (ng, LANES), 1))
        ids_v[1] = n_tokens + pos
        ids_copy(1).start()

    def send_previous(sl):
        ids_copy(1 - sl).wait()

        @pl.when(j >= 1)
        def _():
            wait_rows(sl)

        issue_rows(1 - sl)

    def step(sl):
        other = 1 - sl
        send_previous(sl)

        meta = xs_ref[pl.ds(META_SUBLANE, bm, stride=SUBLANES), :]
        lane = lax.broadcasted_iota(jnp.int32, (bm, LANES), 1)
        tok = jnp.sum(jnp.where(lane == META_TOKEN, meta.astype(_F32), 0.0), axis=-1, keepdims=True)
        tokb = jnp.broadcast_to(tok, (bm, LANES))
        for g in range(ng):
            ids_v[sl, g:g + 1, :] = tokb[g * LANES:(g + 1) * LANES, :].T[0:1, :].astype(jnp.int32)
        ids_copy(sl).start()

        words = [xs_ref[pl.ds(sg, bm, stride=SUBLANES), :] for sg in range(META_SUBLANE)]
        hi = [lax.bitcast_convert_type(w & jnp.int32(HIGH_HALF), _F32).astype(_BF) for w in words]
        lo = [lax.bitcast_convert_type(w << BF16_BITS, _F32).astype(_BF) for w in words]
        xb = jnp.concatenate(hi + lo, axis=1)
        metaf = lax.bitcast_convert_type(meta, _F32)
        w_a = jnp.sum(jnp.where(lane == META_W_LO, metaf, 0.0), axis=-1, keepdims=True)
        w_b = jnp.sum(jnp.where(lane == META_W_HI, metaf, 0.0), axis=-1, keepdims=True)

        def expert(w1, w3, w2):
            h1 = jnp.dot(xb, w1[0], preferred_element_type=_F32)
            h3 = jnp.dot(xb, w3[0], preferred_element_type=_F32)
            hdn = h1 * _sigmoid(h1) * h3
            return jnp.dot(hdn.astype(_BF), w2[0], preferred_element_type=_F32)

        y = w_a * expert(w1a, w3a, w2a) + w_b * expert(w1b, w3b, w2b)
        for sg in range(SUBLANES):
            y_s[sl, pl.ds(sg, bm, stride=SUBLANES), :] = y[:, sg * LANES:(sg + 1) * LANES]

        @pl.when(j == nb - 1)
        def _():
            wait_rows(other)
            ids_copy(sl).wait()
            issue_rows(sl)
            wait_rows(sl)

    is_valid = valid_ref[j] != 0
    prev_valid = valid_ref[jnp.maximum(j - 1, 0)] != 0
    for sl in range(2):
        @pl.when((j % 2 == sl) & is_valid)
        def _(sl=sl):
            step(sl)

        @pl.when((j % 2 == sl) & jnp.logical_not(is_valid) & prev_valid)
        def _(sl=sl):
            send_previous(sl)
            wait_rows(1 - sl)


def _moe(ea, eb, valid, xs, w1_bf, w3_bf, w2_bf, n_tokens):
    d, ff = w1_bf.shape[1], w1_bf.shape[2]
    assert d == SUBLANES * LANES
    bm = BM_MOE
    nb = xs.shape[0] // (bm * SUBLANES)
    grid_spec = pltpu.PrefetchScalarGridSpec(
        num_scalar_prefetch=3,
        grid=(nb,),
        in_specs=[
            pl.BlockSpec((bm * SUBLANES, LANES), lambda j, ea, eb, va: (j, 0)),
            pl.BlockSpec((1, d, ff), lambda j, ea, eb, va: (ea[j], 0, 0)),
            pl.BlockSpec((1, d, ff), lambda j, ea, eb, va: (ea[j], 0, 0)),
            pl.BlockSpec((1, ff, d), lambda j, ea, eb, va: (ea[j], 0, 0)),
            pl.BlockSpec((1, d, ff), lambda j, ea, eb, va: (eb[j], 0, 0)),
            pl.BlockSpec((1, d, ff), lambda j, ea, eb, va: (eb[j], 0, 0)),
            pl.BlockSpec((1, ff, d), lambda j, ea, eb, va: (eb[j], 0, 0)),
        ],
        out_specs=pl.BlockSpec(memory_space=pl.ANY),
        scratch_shapes=[
            pltpu.VMEM((2, bm * SUBLANES, LANES), _F32),
            pltpu.VMEM((2, bm // LANES, LANES), jnp.int32),
            pltpu.SMEM((2, bm // LANES, LANES), jnp.int32),
            pltpu.SemaphoreType.DMA((2,)),
            pltpu.SemaphoreType.DMA((2,)),
        ],
    )
    return pl.pallas_call(
        _moe_kernel,
        grid_spec=grid_spec,
        out_shape=jax.ShapeDtypeStruct(((n_tokens + bm) * SUBLANES, LANES), _F32),
        compiler_params=pltpu.CompilerParams(
            dimension_semantics=("arbitrary",), vmem_limit_bytes=VMEM_LIMIT),
        name="moe_experts",
    )(ea, eb, valid, xs, w1_bf, w3_bf, w2_bf, w1_bf, w3_bf, w2_bf)


def _final_kernel(h_ref, p_ref, y_ref, np_ref, wg_ref, bg_ref, wp_ref, o_ref):
    tm = h_ref.shape[0]
    proj = jnp.dot(p_ref[...].astype(_BF), wp_ref[...], preferred_element_type=_F32)
    y = jnp.concatenate([y_ref[pl.ds(sg, tm, stride=SUBLANES), :] for sg in range(SUBLANES)], axis=1)
    h = h_ref[...] + y
    u = h * lax.rsqrt(jnp.mean(h * h, axis=-1, keepdims=True) + EPS) * np_ref[...]
    gate = _sigmoid(jnp.dot(u.astype(_BF), wg_ref[...], preferred_element_type=_F32) + bg_ref[...])
    o_ref[...] = h + gate * proj


def _final(h1, p2, y_rows, norm_ple, wg_bf, b_ple_gate, wp_bf):
    n, d = h1.shape
    assert d == SUBLANES * LANES
    tm = TM_FINAL
    pd = p2.shape[1]
    row = lambda i: (i, 0)
    return pl.pallas_call(
        _final_kernel,
        grid=(n // tm,),
        in_specs=[
            pl.BlockSpec((tm, d), row),
            pl.BlockSpec((tm, pd), row),
            pl.BlockSpec((tm * SUBLANES, LANES), row),
            _const_spec((1, d)),
            _const_spec(wg_bf.shape),
            _const_spec((1, d)),
            _const_spec(wp_bf.shape),
        ],
        out_specs=pl.BlockSpec((tm, d), row),
        out_shape=jax.ShapeDtypeStruct((n, d), _F32),
        compiler_params=pltpu.CompilerParams(
            dimension_semantics=("arbitrary",), vmem_limit_bytes=VMEM_LIMIT),
        name="final_ple",
    )(h1, p2, y_rows, norm_ple.reshape(1, d), wg_bf, b_ple_gate.reshape(1, d), wp_bf)


def _class_tables():
    ea, eb = [], []
    for g in range(N_GROUPS):
        for lo in range(EXPERTS_PER_GROUP):
            for hi in range(lo + 1, EXPERTS_PER_GROUP):
                ea.append(g * EXPERTS_PER_GROUP + lo)
                eb.append(g * EXPERTS_PER_GROUP + hi)
    return np.asarray(ea, np.int32), np.asarray(eb, np.int32)


_CLASS_EA, _CLASS_EB = _class_tables()


def kernel(x, p, norm_mix, w_in, b_conv_in, b_gate, q_norm, k_norm, lambda_q1, lambda_k1, lambda_q2, lambda_k2, subln, w_attn_out, conv_w, conv_b, conv_ln_g, conv_ln_b, w_conv_out, b_conv_out, w_o, norm_ffn, w_router_group, b_router_group, w_router_expert, b_router_expert, w1, w3, w2, norm_ple, w_ple_gate, b_ple_gate, w_ple_proj):
    b, s, d = x.shape
    n = b * s
    depth = w_in.shape[0]
    qk = ATTN_HEADS * 2 * HEAD_DIM
    aw = ATTN_HEADS * V_DIM
    cc = conv_w.shape[-1]
    assert depth == 1 and n % TM_MIX == 0 and s % (2 * ATTN_TK) == 0 and s % TM_INPROJ == 0
    assert n % TS_SCATTER == 0 and n % TM_FINAL == 0 and n % BM_MOE == 0
    i = 0
    h = x.reshape(n, d)

    conv_w_pad = jnp.pad(conv_w[i], ((0, CONV_HALO - CONV_WIDTH), (0, 0)))
    q2, k2, v2, cact2, gates2 = _inproj(h, norm_mix[i], w_in[i].astype(_BF), b_conv_in[i], b_gate[i], conv_w_pad,
                                        conv_b[i], conv_ln_g[i], conv_ln_b[i], qk, aw, cc, s)

    lam_vecs = jnp.stack([lambda_q1[i], lambda_k1[i], lambda_q2[i], lambda_k2[i]]).astype(_F32)
    attn, (w1_bf, w3_bf, w2_bf) = _attention(lam_vecs, q2.reshape(b, s, qk), k2.reshape(b, s, qk),
                                             v2.reshape(b, s, aw), q_norm[i], k_norm[i], subln[i],
                                             (w1[i], w3[i], w2[i]))

    w_router = jnp.concatenate([w_router_group[i], w_router_expert[i]], axis=1)
    w_router = jnp.pad(w_router, ((0, 0), (0, LANES - w_router.shape[1])))
    b_router = jnp.concatenate([b_router_group[i], b_router_expert[i]])
    b_router = jnp.pad(b_router, (0, LANES - b_router.shape[0])).reshape(1, LANES)
    w_router_hi = w_router.astype(_BF)
    w_router_lo = (w_router - w_router_hi.astype(_F32)).astype(_BF)
    w_router_cat = jnp.concatenate([w_router_hi, w_router_lo], axis=1)
    h1, t_rows, keys, cnt = _mix(h, attn.reshape(n, aw), cact2, gates2, w_attn_out[i].astype(_BF),
                                 w_conv_out[i].astype(_BF), w_o[i].astype(_BF), b_conv_out[i], norm_ffn[i],
                                 w_router_cat, w_router_hi, b_router)

    bm = BM_MOE
    nb = n // bm + N_CLASSES
    counts = cnt[0].astype(jnp.int32)
    padded = ((counts + bm - 1) // bm) * bm
    pend = jnp.cumsum(padded)
    pstart = (pend - padded).astype(jnp.int32)
    total = pend[-1]
    blk = jnp.arange(nb, dtype=jnp.int32) * bm
    valid = blk < total
    first_row = jnp.where(valid, blk, total - 1)
    bcls = jnp.sum((pend[None, :] <= first_row[:, None]).astype(jnp.int32), axis=1)
    bcls = jnp.minimum(bcls, N_CLASSES - 1)
    cls_onehot = (bcls[:, None] == jnp.arange(N_CLASSES, dtype=jnp.int32)[None, :]).astype(jnp.int32)
    ea = jnp.sum(cls_onehot * jnp.asarray(_CLASS_EA)[None, :], axis=1)
    eb = jnp.sum(cls_onehot * jnp.asarray(_CLASS_EB)[None, :], axis=1)
    nvalid = (total // bm).astype(jnp.int32).reshape(1)

    dest = _sorted_rows(pstart, keys.reshape(n // LANES, LANES))
    xs = _scatter_rows((pstart + counts).astype(jnp.int32), (padded - counts).astype(jnp.int32), nvalid,
                       dest, t_rows, nb * bm)
    y_rows = _moe(ea, eb, valid.astype(jnp.int32), xs, w1_bf, w3_bf, w2_bf, n)

    out = _final(h1, p[i].reshape(n, -1), y_rows, norm_ple[i], w_ple_gate[i].astype(_BF), b_ple_gate[i],
                 w_ple_proj[i].astype(_BF))
    return out.reshape(b, s, d)
```
